```python
import math
import jax
import jax.numpy as jnp
from jax import lax
import numpy as np

D_MODEL = 1024
BATCH = 8
SEQ = 2048
DEPTH = 4
DEC_BATCH = 128
DEC_SEQ = 1
PAST_LEN = 16384
PAGE_SIZE = 128

N_META = 16
N_MIXERS = 4
CHUNK = 16
NORM_EPS = 1e-6
LN_EPS = 1e-5

A_DK = 128
A_DV = 128
A_HEADS = D_MODEL // A_DK
A_W = A_HEADS * A_DK
A_VW = A_HEADS * A_DV
B_DK = 128
B_DV = 128
B_HEADS = D_MODEL // B_DK
B_W = B_HEADS * B_DK
B_VW = B_HEADS * B_DV
B_CONV = 2 * B_W + B_VW
CONV_W = 4
C_DK = 128
C_DV = 128
C_HEADS = D_MODEL // C_DV
C_W = C_HEADS * C_DK
C_VW = C_HEADS * C_DV
D_HEADS = 4
D_DK = D_MODEL // (2 * D_HEADS)
D_DV = D_MODEL // D_HEADS
D_W = D_HEADS * D_DK
D_VW = D_HEADS * D_DV
D_GATE_RANK = 16
D_GATE_NORM = 16.0
PEER_HEADS = 8
PEER_NKEYS = 128
PEER_EXPERTS = PEER_NKEYS * PEER_NKEYS
PEER_DQ = 256
PEER_TOPK = 16
PEER_BLOCK = 128
DN_ALPHA = (2 * DEPTH) ** 0.25
DN_BETA = (8 * DEPTH) ** -0.25
N_OCC_A = (DEPTH + 3) // 4
N_OCC_B = (DEPTH + 2) // 4
N_OCC_C = (DEPTH + 1) // 4
N_OCC_D = DEPTH // 4

kernel_name = 'hybrid_hgrn2_gdn_mlstm_gla_peer_decode_step'


def _split(x, sizes):
    return jnp.split(x, [int(c) for c in np.cumsum(sizes)[:-1]], axis=-1)


def _layernorm(x, g, b):
    xf = x.astype(jnp.float32)
    xc = xf - jnp.mean(xf, -1, keepdims=True)
    var = jnp.mean(xc * xc, -1, keepdims=True)
    return (xc * lax.rsqrt(var + LN_EPS) * g.astype(jnp.float32) + b.astype(jnp.float32)).astype(x.dtype)


def _rmsnorm(x, g):
    return x * lax.rsqrt(jnp.mean(x * x, -1, keepdims=True) + NORM_EPS) * g


def _headnorm(x, g):
    xc = x - jnp.mean(x, -1, keepdims=True)
    return xc * lax.rsqrt(jnp.mean(xc * xc, -1, keepdims=True) + NORM_EPS) * g


def _l2norm(x):
    return x * lax.rsqrt(jnp.sum(x * x, -1, keepdims=True) + NORM_EPS)


def _masks():
    idx = jnp.arange(CHUNK)
    return idx[:, None] >= idx[None, :], idx[:, None] > idx[None, :]


def _chunk(t, pad_val=0.0):
    T = t.shape[1]
    pad = (-T) % CHUNK
    t = jnp.pad(t, [(0, 0), (0, pad)] + [(0, 0)] * (t.ndim - 2), constant_values=pad_val)
    t = t.reshape((t.shape[0], -1, CHUNK) + t.shape[2:])
    return jnp.moveaxis(t, 1, 0)


def _unchunk(t, T):
    t = jnp.moveaxis(t, 0, 1)
    return t.reshape((t.shape[0], -1) + t.shape[3:])[:, :T]


def _gla_scan(q, k, v, logf, s0):
    T = q.shape[1]
    incl, _ = _masks()

    def step(s, xs):
        q_, k_, v_, g_ = xs
        b = jnp.cumsum(g_, axis=1)
        diff = b[:, :, None] - b[:, None, :]
        decay = jnp.exp(jnp.where(incl[None, :, :, None, None], diff, -jnp.inf))
        att = jnp.einsum('bthk,bshk,btshk->btsh', q_, k_, decay)
        o = jnp.einsum('btsh,bshv->bthv', att, v_) + jnp.einsum('bthk,bhkv->bthv', q_ * jnp.exp(b), s)
        b_end = b[:, -1]
        s = s * jnp.exp(b_end)[..., None] + jnp.einsum('bshk,bshv->bhkv', k_ * jnp.exp(b_end[:, None] - b), v_)
        return s, o

    s, o = lax.scan(step, s0, (_chunk(q), _chunk(k), _chunk(v), _chunk(logf)))
    return _unchunk(o, T), s


def _gdn_scan(q, k, v, log_a, beta, s0):
    T = q.shape[1]
    incl, strict = _masks()
    eye = jnp.eye(CHUNK, dtype=jnp.float32)

    def step(s, xs):
        q_, k_, v_, g_, beta_ = xs
        b = jnp.cumsum(g_, axis=1)
        diff = b[:, :, None] - b[:, None, :]
        decay = jnp.moveaxis(jnp.exp(jnp.where(incl[None, :, :, None], diff, -jnp.inf)), 3, 1)
        beta_h = jnp.moveaxis(beta_, 2, 1)
        b_h = jnp.moveaxis(b, 2, 1)
        kk = jnp.einsum('bthk,bshk->bhts', k_, k_)
        lhs = eye + jnp.where(strict, kk * decay * beta_h[..., :, None], 0.0)
        rhs = jnp.concatenate([beta_h[..., None] * jnp.moveaxis(v_, 2, 1),
                               (beta_h * jnp.exp(b_h))[..., None] * jnp.moveaxis(k_, 2, 1)], axis=-1)
        sol = lax.linalg.triangular_solve(lhs, rhs, left_side=True, lower=True, unit_diagonal=True)
        u, wk = sol[..., :v_.shape[-1]], sol[..., v_.shape[-1]:]
        v_new = u - jnp.einsum('bhck,bhkv->bhcv', wk, s)
        att = jnp.einsum('bthk,bshk->bhts', q_, k_) * decay
        o = jnp.einsum('bhts,bhsv->bthv', att, v_new) + jnp.einsum('bthk,bhkv->bthv', q_ * jnp.exp(b)[..., None], s)
        b_end = b[:, -1]
        s = s * jnp.exp(b_end)[..., None, None] + jnp.einsum(
            'bshk,bhsv->bhkv', k_ * jnp.exp(b_end[:, None] - b)[..., None], v_new)
        return s, o

    s, o = lax.scan(step, s0, (_chunk(q), _chunk(k), _chunk(v), _chunk(log_a), _chunk(beta)))
    return _unchunk(o, T), s


def _mlstm_scan(q, k, v, log_i, log_f, c0, n0, m0):
    T = q.shape[1]
    incl, _ = _masks()

    def step(carry, xs):
        c, n, m = carry
        q_, k_, v_, li, lf = xs
        b = jnp.cumsum(lf, axis=1)
        dmat = jnp.where(incl[None, :, :, None], b[:, :, None] - b[:, None, :] + li[:, None, :], -jnp.inf)
        m_inter = b + m[:, None]
        m_t = jnp.maximum(m_inter, jnp.max(dmat, axis=2))
        wts = jnp.exp(dmat - m_t[:, :, None])
        a_int = jnp.exp(m_inter - m_t)
        qk = jnp.einsum('bthk,bshk->btsh', q_, k_) * wts
        num = jnp.einsum('btsh,bshv->bthv', qk, v_) + a_int[..., None] * jnp.einsum('bthk,bhkv->bthv', q_, c)
        den = jnp.sum(qk, axis=2) + a_int * jnp.einsum('bthk,bhk->bth', q_, n)
        h = num / jnp.maximum(jnp.abs(den), jnp.exp(-m_t))[..., None]
        w_end = wts[:, -1]
        a_end = a_int[:, -1]
        c = a_end[..., None, None] * c + jnp.einsum('bsh,bshk,bshv->bhkv', w_end, k_, v_)
        n = a_end[..., None] * n + jnp.einsum('bsh,bshk->bhk', w_end, k_)
        return (c, n, m_t[:, -1]), h

    (c, n, m), h = lax.scan(step, (c0, n0, m0),
                            (_chunk(q), _chunk(k), _chunk(v), _chunk(log_i, -jnp.inf), _chunk(log_f)))
    return _unchunk(h, T), c, n, m


def _causal_conv(x, buf, w):
    T = x.shape[1]
    full = jnp.concatenate([buf.astype(x.dtype), x], axis=1)
    y = full[:, 0:T] * w[0]
    for j in range(1, CONV_W):
        y = y + full[:, j:j + T] * w[j]
    return y, full[:, T:]


def _hgrn2(x, w_in, lb, norm_g, w_out, s0):
    bsz, T, _ = x.shape
    q, f, i, g = _split(x @ w_in, [A_W, A_W, A_VW, A_VW])
    q = jax.nn.silu(q.astype(jnp.float32)).reshape(bsz, T, A_HEADS, A_DK)
    fg = lb + (1.0 - lb) * jax.nn.sigmoid(f.astype(jnp.float32))
    k = (1.0 - fg).reshape(bsz, T, A_HEADS, A_DK)
    logf = jnp.log(fg).reshape(bsz, T, A_HEADS, A_DK)
    v = i.astype(jnp.float32).reshape(bsz, T, A_HEADS, A_DV)
    o, s = _gla_scan(q, k, v, logf, s0.astype(jnp.float32))
    gate = jax.nn.silu(g.astype(jnp.float32)).reshape(bsz, T, A_HEADS, A_DV)
    o = (_rmsnorm(o, norm_g.astype(jnp.float32)) * gate).reshape(bsz, T, A_VW).astype(x.dtype)
    return o @ w_out, s.astype(s0.dtype)


def _gated_deltanet(x, w_in, conv_w, a_log, dt_bias, norm_g, w_out, s0, conv0):
    bsz, T, _ = x.shape
    qkv, a, b, g = _split(x @ w_in, [B_CONV, B_HEADS, B_HEADS, B_VW])
    qkv, conv_new = _causal_conv(qkv, conv0, conv_w)
    q, k, v = _split(jax.nn.silu(qkv.astype(jnp.float32)), [B_W, B_W, B_VW])
    q = _l2norm(q.reshape(bsz, T, B_HEADS, B_DK)) * B_DK ** -0.5
    k = _l2norm(k.reshape(bsz, T, B_HEADS, B_DK))
    v = v.reshape(bsz, T, B_HEADS, B_DV)
    log_a = -jnp.exp(a_log.astype(jnp.float32)) * jax.nn.softplus(a.astype(jnp.float32) + dt_bias.astype(jnp.float32))
    beta = jax.nn.sigmoid(b.astype(jnp.float32))
    o, s = _gdn_scan(q, k, v, log_a, beta, s0.astype(jnp.float32))
    gate = jax.nn.silu(g.astype(jnp.float32)).reshape(bsz, T, B_HEADS, B_DV)
    o = (_rmsnorm(o, norm_g.astype(jnp.float32)) * gate).reshape(bsz, T, B_VW).astype(x.dtype)
    return o @ w_out, s.astype(s0.dtype), conv_new.astype(conv0.dtype)


def _mlstm(x, w_in, b_i, b_f, norm_g, w_out, c0, n0, m0):
    bsz, T, _ = x.shape
    q, k, v, i, f, o = _split(x @ w_in, [C_W, C_W, C_VW, C_HEADS, C_HEADS, C_VW])
    q = q.astype(jnp.float32).reshape(bsz, T, C_HEADS, C_DK)
    k = k.astype(jnp.float32).reshape(bsz, T, C_HEADS, C_DK) * C_DK ** -0.5
    v = v.astype(jnp.float32).reshape(bsz, T, C_HEADS, C_DV)
    log_i = i.astype(jnp.float32) + b_i.astype(jnp.float32)
    log_f = jax.nn.log_sigmoid(f.astype(jnp.float32) + b_f.astype(jnp.float32))
    h, c, n, m = _mlstm_scan(q, k, v, log_i, log_f, c0.astype(jnp.float32), n0.astype(jnp.float32),
                             m0.astype(jnp.float32))
    h = _headnorm(h, norm_g.astype(jnp.float32).reshape(C_HEADS, C_DV))
    og = jax.nn.sigmoid(o.astype(jnp.float32)).reshape(bsz, T, C_HEADS, C_DV)
    y = (og * h).reshape(bsz, T, C_VW).astype(x.dtype) @ w_out
    return y, c.astype(c0.dtype), n.astype(n0.dtype), m.astype(m0.dtype)


def _gla(x, w_in, w_gate, b_gate, norm_g, w_out, s0):
    bsz, T, _ = x.shape
    q, k, v, g, r = _split(x @ w_in, [D_W, D_W, D_VW, D_VW, D_GATE_RANK])
    q = q.astype(jnp.float32).reshape(bsz, T, D_HEADS, D_DK) * D_DK ** -0.5
    k = k.astype(jnp.float32).reshape(bsz, T, D_HEADS, D_DK)
    v = v.astype(jnp.float32).reshape(bsz, T, D_HEADS, D_DV)
    logf = (jax.nn.log_sigmoid((r @ w_gate + b_gate).astype(jnp.float32)) / D_GATE_NORM).reshape(bsz, T, D_HEADS, D_DK)
    o, s = _gla_scan(q, k, v, logf, s0.astype(jnp.float32))
    gate = jax.nn.silu(g.astype(jnp.float32)).reshape(bsz, T, D_HEADS, D_DV)
    o = (_rmsnorm(o, norm_g.astype(jnp.float32)) * gate).reshape(bsz, T, D_VW).astype(x.dtype)
    return o @ w_out, s.astype(s0.dtype)


def _peer(x, w_q, keys1, keys2, u, v):
    n, d = x.shape
    pad = (-n) % PEER_BLOCK
    xb = jnp.pad(x, ((0, pad), (0, 0))).reshape(-1, PEER_BLOCK, d)
    half = PEER_DQ // 2
    k1 = keys1.astype(jnp.float32)
    k2 = keys2.astype(jnp.float32)

    def block(xt):
        q = (xt @ w_q).astype(jnp.float32).reshape(PEER_BLOCK, PEER_HEADS, 2, half)
        s1 = jnp.einsum('thd,kd->thk', q[:, :, 0], k1)
        s2 = jnp.einsum('thd,kd->thk', q[:, :, 1], k2)
        v1, i1 = lax.top_k(s1, PEER_TOPK)
        v2, i2 = lax.top_k(s2, PEER_TOPK)
        cand = (v1[..., :, None] + v2[..., None, :]).reshape(PEER_BLOCK, PEER_HEADS, PEER_TOPK * PEER_TOPK)
        cidx = (i1[..., :, None] * PEER_NKEYS + i2[..., None, :]).reshape(PEER_BLOCK, PEER_HEADS, PEER_TOPK * PEER_TOPK)
        top, pos = lax.top_k(cand, PEER_TOPK)
        eidx = jnp.take_along_axis(cidx, pos, axis=-1)
        gsm = jax.nn.softmax(top, axis=-1)
        act = jax.nn.gelu(jnp.einsum('thkd,td->thk', u[eidx], xt).astype(jnp.float32), approximate=False)
        return jnp.einsum('thk,thkd->td', (gsm * act).astype(xt.dtype), v[eidx])

    return lax.map(block, xb).reshape(-1, d)[:n]


def _trunk(h, st, w):
    bsz, T, _ = h.shape
    lb_all = jnp.cumsum(jax.nn.softmax(w['hgrn_lb'].astype(jnp.float32), axis=0), axis=0)
    new = {'hgrn_S': [], 'gdn_S': [], 'gdn_conv': [], 'mlstm_C': [], 'mlstm_n': [], 'mlstm_m': [], 'gla_S': []}
    for i in range(DEPTH):
        mix, j = i % N_MIXERS, i // N_MIXERS
        if mix == 0:
            y, s = _hgrn2(h, w['hgrn_w_in'][j], lb_all[i], w['hgrn_norm_g'][j], w['hgrn_w_out'][j], st['hgrn_S'][j])
            new['hgrn_S'].append(s)
        elif mix == 1:
            y, s, cv = _gated_deltanet(h, w['gdn_w_in'][j], w['gdn_conv_w'][j], w['gdn_a_log'][j], w['gdn_dt_bias'][j],
                                       w['gdn_norm_g'][j], w['gdn_w_out'][j], st['gdn_S'][j], st['gdn_conv'][j])
            new['gdn_S'].append(s)
            new['gdn_conv'].append(cv)
        elif mix == 2:
            y, c, nn_, m = _mlstm(h, w['mlstm_w_in'][j], w['mlstm_b_i'][j], w['mlstm_b_f'][j], w['mlstm_norm_g'][j],
                                  w['mlstm_w_out'][j], st['mlstm_C'][j], st['mlstm_n'][j], st['mlstm_m'][j])
            new['mlstm_C'].append(c)
            new['mlstm_n'].append(nn_)
            new['mlstm_m'].append(m)
        else:
            y, s = _gla(h, w['gla_w_in'][j], w['gla_w_gate'][j], w['gla_b_gate'][j], w['gla_norm_g'][j],
                        w['gla_w_out'][j], st['gla_S'][j])
            new['gla_S'].append(s)
        h = _layernorm(DN_ALPHA * h + y, w['ln_g'][i, 0], w['ln_b'][i, 0])
        f = _peer(h.reshape(bsz * T, D_MODEL), w['peer_w_q'][i], w['peer_keys1'][i], w['peer_keys2'][i],
                  w['peer_u'][i], w['peer_v'][i]).reshape(bsz, T, D_MODEL)
        h = _layernorm(DN_ALPHA * h + f, w['ln_g'][i, 1], w['ln_b'][i, 1])
    return h, {name: jnp.stack(vals) for name, vals in new.items()}


def setup_inputs(seed: int = 0) -> dict:
    key = jax.random.key(seed)
    ks = jax.random.split(key, 48)

    def nrm(i, shape, scale):
        return scale * jax.random.normal(ks[i], shape, jnp.float32)

    def gain(i, shape):
        return 1.0 + nrm(i, shape, 0.01)

    dt = jnp.exp(jax.random.uniform(ks[13], (N_OCC_B, B_HEADS), jnp.float32,
                                    math.log(1e-3), math.log(1e-1)))
    return {
        'x_prompt': nrm(0, (BATCH, SEQ, D_MODEL), 1.0),
        'x_sample': nrm(1, (DEC_BATCH, DEC_SEQ, D_MODEL), 1.0),
        'state_hgrn_S': nrm(2, (N_OCC_A, DEC_BATCH, A_HEADS, A_DK, A_DV), 0.5),
        'state_gdn_S': nrm(3, (N_OCC_B, DEC_BATCH, B_HEADS, B_DK, B_DV), 0.1),
        'state_gdn_conv': nrm(4, (N_OCC_B, DEC_BATCH, CONV_W - 1, B_CONV), 1.0),
        'state_mlstm_C': nrm(5, (N_OCC_C, DEC_BATCH, C_HEADS, C_DK, C_DV), 0.3),
        'state_mlstm_n': nrm(6, (N_OCC_C, DEC_BATCH, C_HEADS, C_DK), 0.3),
        'state_mlstm_m': nrm(7, (N_OCC_C, DEC_BATCH, C_HEADS), 1.0),
        'state_gla_S': nrm(8, (N_OCC_D, DEC_BATCH, D_HEADS, D_DK, D_DV), 0.3),
        'meta_tokens': nrm(9, (N_META, D_MODEL), 1.0),
        'hgrn_w_in': nrm(10, (N_OCC_A, D_MODEL, 2 * A_W + 2 * A_VW), D_MODEL ** -0.5),
        'hgrn_lb': nrm(11, (DEPTH + 1, A_W), 0.1),
        'hgrn_norm_g': gain(12, (N_OCC_A, A_DV)),
        'hgrn_w_out': nrm(14, (N_OCC_A, A_VW, D_MODEL), DN_BETA * A_VW ** -0.5),
        'gdn_w_in': nrm(15, (N_OCC_B, D_MODEL, B_CONV + 2 * B_HEADS + B_VW), D_MODEL ** -0.5),
        'gdn_conv_w': nrm(16, (N_OCC_B, CONV_W, B_CONV), CONV_W ** -0.5),
        'gdn_a_log': jnp.log(jax.random.uniform(ks[17], (N_OCC_B, B_HEADS), jnp.float32, 1.0, 16.0)),
        'gdn_dt_bias': dt + jnp.log(-jnp.expm1(-dt)),
        'gdn_norm_g': gain(18, (N_OCC_B, B_DV)),
        'gdn_w_out': nrm(19, (N_OCC_B, B_VW, D_MODEL), DN_BETA * B_VW ** -0.5),
        'mlstm_w_in': nrm(20, (N_OCC_C, D_MODEL, 2 * C_W + 2 * C_VW + 2 * C_HEADS), D_MODEL ** -0.5),
        'mlstm_b_i': nrm(21, (N_OCC_C, C_HEADS), 0.1),
        'mlstm_b_f': jnp.linspace(3.0, 6.0, C_HEADS, dtype=jnp.float32)[None] + nrm(22, (N_OCC_C, C_HEADS), 0.1),
        'mlstm_norm_g': gain(23, (N_OCC_C, C_VW)),
        'mlstm_w_out': nrm(24, (N_OCC_C, C_VW, D_MODEL), DN_BETA * C_VW ** -0.5),
        'gla_w_in': nrm(25, (N_OCC_D, D_MODEL, 2 * D_W + 2 * D_VW + D_GATE_RANK), D_MODEL ** -0.5),
        'gla_w_gate': nrm(26, (N_OCC_D, D_GATE_RANK, D_W), D_GATE_RANK ** -0.5),
        'gla_b_gate': nrm(27, (N_OCC_D, D_W), 0.01),
        'gla_norm_g': gain(28, (N_OCC_D, D_DV)),
        'gla_w_out': nrm(29, (N_OCC_D, D_VW, D_MODEL), DN_BETA * D_VW ** -0.5),
        'peer_w_q': nrm(30, (DEPTH, D_MODEL, PEER_HEADS * PEER_DQ), D_MODEL ** -0.5),
        'peer_keys1': nrm(31, (DEPTH, PEER_NKEYS, PEER_DQ // 2), (PEER_DQ // 2) ** -0.5),
        'peer_keys2': nrm(32, (DEPTH, PEER_NKEYS, PEER_DQ // 2), (PEER_DQ // 2) ** -0.5),
        'peer_u': nrm(33, (DEPTH, PEER_EXPERTS, D_MODEL), D_MODEL ** -0.5),
        'peer_v': nrm(34, (DEPTH, PEER_EXPERTS, D_MODEL), DN_BETA * PEER_HEADS ** -0.5),
        'ln_g': gain(35, (DEPTH, 2, D_MODEL)),
        'ln_b': nrm(36, (DEPTH, 2, D_MODEL), 0.01),
    }


def reference(x_prompt, x_sample, state_hgrn_S, state_gdn_S, state_gdn_conv, state_mlstm_C, state_mlstm_n,
              state_mlstm_m, state_gla_S, meta_tokens, hgrn_w_in, hgrn_lb, hgrn_norm_g, hgrn_w_out,
              gdn_w_in, gdn_conv_w, gdn_a_log, gdn_dt_bias, gdn_norm_g, gdn_w_out,
              mlstm_w_in, mlstm_b_i, mlstm_b_f, mlstm_norm_g, mlstm_w_out,
              gla_w_in, gla_w_gate, gla_b_gate, gla_norm_g, gla_w_out,
              peer_w_q, peer_keys1, peer_keys2, peer_u, peer_v, ln_g, ln_b):
    w = dict(hgrn_w_in=hgrn_w_in, hgrn_lb=hgrn_lb, hgrn_norm_g=hgrn_norm_g, hgrn_w_out=hgrn_w_out,
             gdn_w_in=gdn_w_in, gdn_conv_w=gdn_conv_w, gdn_a_log=gdn_a_log, gdn_dt_bias=gdn_dt_bias,
             gdn_norm_g=gdn_norm_g, gdn_w_out=gdn_w_out,
             mlstm_w_in=mlstm_w_in, mlstm_b_i=mlstm_b_i, mlstm_b_f=mlstm_b_f, mlstm_norm_g=mlstm_norm_g,
             mlstm_w_out=mlstm_w_out,
             gla_w_in=gla_w_in, gla_w_gate=gla_w_gate, gla_b_gate=gla_b_gate, gla_norm_g=gla_norm_g,
             gla_w_out=gla_w_out,
             peer_w_q=peer_w_q, peer_keys1=peer_keys1, peer_keys2=peer_keys2, peer_u=peer_u, peer_v=peer_v,
             ln_g=ln_g, ln_b=ln_b)

    bp = x_prompt.shape[0]
    dt = x_prompt.dtype
    meta = jnp.broadcast_to(meta_tokens.astype(dt)[None], (bp, N_META, D_MODEL))
    h_p = jnp.concatenate([meta, x_prompt], axis=1)
    zero_st = dict(
        hgrn_S=jnp.zeros((N_OCC_A, bp, A_HEADS, A_DK, A_DV), dt),
        gdn_S=jnp.zeros((N_OCC_B, bp, B_HEADS, B_DK, B_DV), dt),
        gdn_conv=jnp.zeros((N_OCC_B, bp, CONV_W - 1, B_CONV), dt),
        mlstm_C=jnp.zeros((N_OCC_C, bp, C_HEADS, C_DK, C_DV), dt),
        mlstm_n=jnp.zeros((N_OCC_C, bp, C_HEADS, C_DK), dt),
        mlstm_m=jnp.zeros((N_OCC_C, bp, C_HEADS), dt),
        gla_S=jnp.zeros((N_OCC_D, bp, D_HEADS, D_DK, D_DV), dt))
    h_p, p_st = _trunk(h_p, zero_st, w)
    y_prompt = h_p[:, N_META:]

    sample_st = dict(hgrn_S=state_hgrn_S, gdn_S=state_gdn_S, gdn_conv=state_gdn_conv, mlstm_C=state_mlstm_C,
                     mlstm_n=state_mlstm_n, mlstm_m=state_mlstm_m, gla_S=state_gla_S)
    y_sample, s_st = _trunk(x_sample, sample_st, w)

    return (y_prompt, y_sample,
            p_st['hgrn_S'], p_st['gdn_S'], p_st['gdn_conv'], p_st['mlstm_C'], p_st['mlstm_n'], p_st['mlstm_m'],
            p_st['gla_S'],
            s_st['hgrn_S'], s_st['gdn_S'], s_st['gdn_conv'], s_st['mlstm_C'], s_st['mlstm_n'], s_st['mlstm_m'],
            s_st['gla_S'])
```

```python
import functools

import jax
import jax.numpy as jnp
from jax import lax
from jax.experimental import pallas as pl
from jax.experimental.pallas import tpu as pltpu

F32 = jnp.float32
BF16 = jnp.bfloat16
HI = lax.Precision.HIGHEST

D_MODEL = 1024
DEPTH = 4
N_META = 16
C = 16
CONV_W = 4
NORM_EPS = 1e-6
LN_EPS = 1e-5
DN_ALPHA = (2 * DEPTH) ** 0.25
GLA_GATE_NORM = 16.0
LANE = 128
PEER_HEADS = 8
PEER_NKEYS = 128
PEER_TOPK = 16
PEER_HALF = 128
PEER_I1_PER_STEP = 8
TOK_BLOCK = 256
MM_TOK_BLOCK = 640
VMEM_LIMIT = 48 * 1024 * 1024


def _cparams(sem):
    return pltpu.CompilerParams(dimension_semantics=sem, vmem_limit_bytes=VMEM_LIMIT)


def _iota(shape, dim):
    return lax.broadcasted_iota(jnp.int32, shape, dim)


def _dot_nt(a, b, precision=None):
    return lax.dot_general(a, b, (((1,), (1,)), ((), ())), precision=precision, preferred_element_type=F32)


def _dot_tn(a, b, precision=None):
    return lax.dot_general(a, b, (((0,), (0,)), ((), ())), precision=precision, preferred_element_type=F32)


def _dot(a, b, precision=None):
    return jnp.dot(a, b, precision=precision, preferred_element_type=F32)


def _cumsum_rows(x):
    tri = (_iota((C, C), 0) >= _iota((C, C), 1)).astype(F32)
    return _dot(tri, x, HI)


def _rows_to_cols(x):
    eye = (_iota((LANE, LANE), 0) == _iota((LANE, LANE), 1)).astype(F32)
    return _dot_nt(eye, x, HI)


def _incl_mask():
    return _iota((C, C), 0) >= _iota((C, C), 1)


def _mm_body(x_ref, w_ref, o_ref):
    o_ref[...] = _dot(x_ref[...].astype(BF16), w_ref[...])


def _mm(x, w, tm, tn):
    m, k = x.shape
    n = w.shape[1]
    return pl.pallas_call(
        _mm_body, grid=(m // tm, n // tn),
        in_specs=[pl.BlockSpec((tm, k), lambda i, j: (i, 0)), pl.BlockSpec((k, tn), lambda i, j: (0, j))],
        out_specs=pl.BlockSpec((tm, tn), lambda i, j: (i, j)),
        out_shape=jax.ShapeDtypeStruct((m, n), F32),
        compiler_params=_cparams(("parallel", "parallel")), name="mm")(x, w)


def _mm_ln_body(o_ref, w_ref, h_ref, g_ref, b_ref, out_ref):
    y = _dot(o_ref[...].astype(BF16), w_ref[...])
    z = DN_ALPHA * h_ref[...] + y
    zc = z - jnp.mean(z, -1, keepdims=True)
    var = jnp.mean(zc * zc, -1, keepdims=True)
    out_ref[...] = (zc * lax.rsqrt(var + LN_EPS) * g_ref[...] + b_ref[...]).T


def _mm_res_ln_t(o, w, h, g, b, tm):
    n, k = o.shape
    d = w.shape[1]
    return pl.pallas_call(
        _mm_ln_body, grid=(n // tm,),
        in_specs=[pl.BlockSpec((tm, k), lambda i: (i, 0)), pl.BlockSpec((k, d), lambda i: (0, 0)),
                  pl.BlockSpec((tm, d), lambda i: (i, 0)), pl.BlockSpec((1, d), lambda i: (0, 0)),
                  pl.BlockSpec((1, d), lambda i: (0, 0))],
        out_specs=pl.BlockSpec((d, tm), lambda i: (0, i)),
        out_shape=jax.ShapeDtypeStruct((d, n), F32),
        compiler_params=_cparams(("parallel",)), name="mm_res_ln")(o, w, h, g.reshape(1, d), b.reshape(1, d))


def _valid_rows(t, r0, tb, t_valid, nt):
    if t_valid == nt * tb:
        return None
    return (t * tb + r0 + _iota((C, 1), 0)) < t_valid


def _gla_chunk(q, k, v, g, st):
    b = _cumsum_rows(g)
    o = _dot_nt(q * jnp.exp(b), st)
    row = _iota((C, 1), 0)
    for s in range(C):
        d = jnp.exp(jnp.where(row >= s, b - b[s:s + 1, :], -jnp.inf))
        col = jnp.sum(q * (k[s:s + 1, :] * d), axis=-1, keepdims=True)
        o = o + col * v[s:s + 1, :]
    b_end = b[C - 1:C, :]
    st_new = st * jnp.exp(b_end) + _dot_tn(v, k * jnp.exp(b_end - b))
    return o, st_new


def _gated_rmsnorm(o, ng, gate):
    return o * lax.rsqrt(jnp.mean(o * o, -1, keepdims=True) + NORM_EPS) * ng * jax.nn.silu(gate)


def _gla_body(proj_ref, s0_ref, ng_ref, p1_ref, p2_ref, o_ref, sout_ref, st_scr, *, hgrn, nh, dk, dv, tb, t_valid,
              nt):
    t = pl.program_id(1)

    @pl.when(t == 0)
    def _():
        for h in range(nh):
            st_scr[h] = s0_ref[0, h].T

    kw = nh * dk
    vw = nh * dv

    def chunk(ci, carry):
        r0 = pl.multiple_of(ci * C, C)
        rows = pl.ds(r0, C)
        valid = _valid_rows(t, r0, tb, t_valid, nt)
        if not hgrn:
            r = proj_ref[rows, 2 * kw + 2 * vw:2 * kw + 2 * vw + LANE]
            logf_all = jax.nn.log_sigmoid(_dot(r, p1_ref[...], HI) + p2_ref[...]) / GLA_GATE_NORM
        for h in range(nh):
            ks = slice(h * dk, (h + 1) * dk)
            if hgrn:
                q = jax.nn.silu(proj_ref[rows, ks])
                lb = p1_ref[:, ks]
                fg = lb + (1.0 - lb) * jax.nn.sigmoid(proj_ref[rows, kw + h * dk:kw + (h + 1) * dk])
                k = 1.0 - fg
                g = jnp.log(fg)
            else:
                q = proj_ref[rows, ks] * dk ** -0.5
                k = proj_ref[rows, kw + h * dk:kw + (h + 1) * dk]
                g = logf_all[:, ks]
            v = proj_ref[rows, 2 * kw + h * dv:2 * kw + (h + 1) * dv]
            gate = proj_ref[rows, 2 * kw + vw + h * dv:2 * kw + vw + (h + 1) * dv]
            if valid is not None:
                k = jnp.where(valid, k, 0.0)
                g = jnp.where(valid, g, 0.0)
            o, st_new = _gla_chunk(q, k, v, g, st_scr[h])
            st_scr[h] = st_new
            o_ref[rows, h * dv:(h + 1) * dv] = _gated_rmsnorm(o, ng_ref[...], gate)
        return carry

    lax.fori_loop(0, tb // C, chunk, 0)

    @pl.when(t == nt - 1)
    def _():
        for h in range(nh):
            sout_ref[0, h] = st_scr[h].T


def _gla_scan(proj, s0, ng, p1, p2, *, hgrn, nb, nt, tb, t_valid):
    _, nh, dk, dv = s0.shape
    wp = proj.shape[1]
    body = functools.partial(_gla_body, hgrn=hgrn, nh=nh, dk=dk, dv=dv, tb=tb, t_valid=t_valid, nt=nt)
    return pl.pallas_call(
        body, grid=(nb, nt),
        in_specs=[pl.BlockSpec((tb, wp), lambda b, t: (b * nt + t, 0)),
                  pl.BlockSpec((1, nh, dk, dv), lambda b, t: (b, 0, 0, 0)),
                  pl.BlockSpec(ng.shape, lambda b, t: (0, 0)),
                  pl.BlockSpec(p1.shape, lambda b, t: (0, 0)),
                  pl.BlockSpec(p2.shape, lambda b, t: (0, 0))],
        out_specs=[pl.BlockSpec((tb, nh * dv), lambda b, t: (b * nt + t, 0)),
                   pl.BlockSpec((1, nh, dk, dv), lambda b, t: (b, 0, 0, 0))],
        out_shape=[jax.ShapeDtypeStruct((nb * nt * tb, nh * dv), F32), jax.ShapeDtypeStruct(s0.shape, F32)],
        scratch_shapes=[pltpu.VMEM((nh, dv, dk), F32)],
        compiler_params=_cparams(("parallel", "arbitrary")), name="hgrn_scan" if hgrn else "gla_scan",
    )(proj, s0, ng, p1, p2)


def _gdn_body(proj_ref, s0_ref, conv0_ref, cw_ref, ng_ref, alog_ref, dtb_ref, o_ref, sout_ref, convout_ref,
              st_scr, cbuf, ybuf, *, nh, dk, dv, tb, t_valid, nt):
    t = pl.program_id(1)
    cwid = 2 * nh * dk + nh * dv
    pre = 8 - (CONV_W - 1)

    @pl.when(t == 0)
    def _():
        for h in range(nh):
            st_scr[h] = s0_ref[0, h].T
        cbuf[0:8, :] = jnp.zeros((8, cwid), F32)
        cbuf[pre:8, :] = conv0_ref[0]

    cbuf[8:8 + tb, :] = proj_ref[:, 0:cwid]
    y = cbuf[pre:pre + tb, :] * cw_ref[0:1, :]
    for j in range(1, CONV_W):
        y = y + cbuf[pre + j:pre + j + tb, :] * cw_ref[j:j + 1, :]
    ybuf[...] = jax.nn.silu(y)

    incl = _incl_mask()
    strict = _iota((C, C), 0) > _iota((C, C), 1)
    eye = (_iota((C, C), 0) == _iota((C, C), 1)).astype(F32)

    def chunk(ci, carry):
        r0 = pl.multiple_of(ci * C, C)
        rows = pl.ds(r0, C)
        valid = _valid_rows(t, r0, tb, t_valid, nt)
        small = proj_ref[rows, cwid + nh * dv:cwid + nh * dv + LANE]
        g_all = -jnp.exp(alog_ref[...]) * jax.nn.softplus(small + dtb_ref[...])
        beta_all = jax.nn.sigmoid(small)
        if valid is not None:
            g_all = jnp.where(valid, g_all, 0.0)
            beta_all = jnp.where(valid, beta_all, 0.0)
        b_all = _cumsum_rows(g_all)
        b_t = _rows_to_cols(b_all)
        for h in range(nh):
            q = ybuf[rows, h * dk:(h + 1) * dk]
            k = ybuf[rows, nh * dk + h * dk:nh * dk + (h + 1) * dk]
            v = ybuf[rows, 2 * nh * dk + h * dv:2 * nh * dk + (h + 1) * dv]
            q = q * lax.rsqrt(jnp.sum(q * q, -1, keepdims=True) + NORM_EPS) * dk ** -0.5
            k = k * lax.rsqrt(jnp.sum(k * k, -1, keepdims=True) + NORM_EPS)
            if valid is not None:
                q = jnp.where(valid, q, 0.0)
                k = jnp.where(valid, k, 0.0)
                v = jnp.where(valid, v, 0.0)
            bcol = b_all[:, h:h + 1]
            brow = b_t[h:h + 1, :]
            beta = beta_all[:, nh + h:nh + h + 1]
            decay = jnp.exp(jnp.where(incl, bcol - brow, -jnp.inf))
            a = jnp.where(strict, _dot_nt(k, k, HI) * decay * beta, 0.0)
            a2 = _dot(a, a, HI)
            a4 = _dot(a2, a2, HI)
            a8 = _dot(a4, a4, HI)
            tinv = _dot(_dot(eye - a, eye + a2, HI), _dot(eye + a4, eye + a8, HI), HI)
            st = st_scr[h]
            u = _dot(tinv, beta * v, HI)
            wk = _dot(tinv, (beta * jnp.exp(bcol)) * k, HI)
            v_new = u - _dot_nt(wk, st)
            att = _dot_nt(q, k) * decay
            o = _dot(att, v_new) + _dot_nt(q * jnp.exp(bcol), st)
            b_end = bcol[C - 1:C, :]
            st_scr[h] = st * jnp.exp(b_end) + _dot_tn(v_new, k * jnp.exp(b_end - bcol))
            gate = proj_ref[rows, cwid + h * dv:cwid + (h + 1) * dv]
            o_ref[rows, h * dv:(h + 1) * dv] = _gated_rmsnorm(o, ng_ref[...], gate)
        return carry

    lax.fori_loop(0, tb // C, chunk, 0)

    @pl.when(t < nt - 1)
    def _():
        cbuf[pre:8, :] = cbuf[pre + tb:8 + tb, :]

    @pl.when(t == nt - 1)
    def _():
        for h in range(nh):
            sout_ref[0, h] = st_scr[h].T
        last = t_valid - (nt - 1) * tb
        convout_ref[0] = cbuf[pre + last:8 + last, :]


def _gdn_scan(proj, s0, conv0, cw, ng, alog, dtb, *, nb, nt, tb, t_valid):
    _, nh, dk, dv = s0.shape
    wp = proj.shape[1]
    cwid = 2 * nh * dk + nh * dv
    body = functools.partial(_gdn_body, nh=nh, dk=dk, dv=dv, tb=tb, t_valid=t_valid, nt=nt)
    const = lambda b, t: (0, 0)
    return pl.pallas_call(
        body, grid=(nb, nt),
        in_specs=[pl.BlockSpec((tb, wp), lambda b, t: (b * nt + t, 0)),
                  pl.BlockSpec((1, nh, dk, dv), lambda b, t: (b, 0, 0, 0)),
                  pl.BlockSpec((1, CONV_W - 1, cwid), lambda b, t: (b, 0, 0)),
                  pl.BlockSpec(cw.shape, const), pl.BlockSpec(ng.shape, const),
                  pl.BlockSpec(alog.shape, const), pl.BlockSpec(dtb.shape, const)],
        out_specs=[pl.BlockSpec((tb, nh * dv), lambda b, t: (b * nt + t, 0)),
                   pl.BlockSpec((1, nh, dk, dv), lambda b, t: (b, 0, 0, 0)),
                   pl.BlockSpec((1, CONV_W - 1, cwid), lambda b, t: (b, 0, 0))],
        out_shape=[jax.ShapeDtypeStruct((nb * nt * tb, nh * dv), F32), jax.ShapeDtypeStruct(s0.shape, F32),
                   jax.ShapeDtypeStruct(conv0.shape, F32)],
        scratch_shapes=[pltpu.VMEM((nh, dv, dk), F32), pltpu.VMEM((tb + 8, cwid), F32), pltpu.VMEM((tb, cwid), F32)],
        compiler_params=_cparams(("parallel", "arbitrary")), name="gdn_scan",
    )(proj, s0, conv0, cw, ng, alog, dtb)


def _mlstm_body(proj_ref, c0_ref, n0_ref, m0_ref, ng_ref, bi_ref, bf_ref, o_ref, cout_ref, nout_ref, mout_ref,
                ct_scr, n_scr, m_scr, *, nh, dk, dv, tb, t_valid, nt):
    t = pl.program_id(1)

    @pl.when(t == 0)
    def _():
        for h in range(nh):
            ct_scr[h] = c0_ref[0, h].T
            n_scr[h] = n0_ref[0, h:h + 1, :]
            m_scr[h] = jnp.broadcast_to(m0_ref[0, :, h:h + 1], (1, LANE))

    incl = _incl_mask()
    base = 2 * nh * dk + 2 * nh * dv

    def chunk(ci, carry):
        r0 = pl.multiple_of(ci * C, C)
        rows = pl.ds(r0, C)
        valid = _valid_rows(t, r0, tb, t_valid, nt)
        small = proj_ref[rows, base:base + LANE]
        li_all = small + bi_ref[...]
        lf_all = jax.nn.log_sigmoid(small + bf_ref[...])
        if valid is not None:
            lf_all = jnp.where(valid, lf_all, 0.0)
        b_all = _cumsum_rows(lf_all)
        b_t = _rows_to_cols(b_all)
        li_t = _rows_to_cols(li_all)
        if valid is not None:
            valid_row = (t * tb + r0 + _iota((1, C), 1)) < t_valid
        for h in range(nh):
            q = proj_ref[rows, h * dk:(h + 1) * dk]
            k = proj_ref[rows, nh * dk + h * dk:nh * dk + (h + 1) * dk] * dk ** -0.5
            v = proj_ref[rows, 2 * nh * dk + h * dv:2 * nh * dk + (h + 1) * dv]
            bcol = b_all[:, nh + h:nh + h + 1]
            licol = li_all[:, h:h + 1]
            xrow = b_t[nh + h:nh + h + 1, :] - li_t[h:h + 1, :]
            if valid is not None:
                xrow = jnp.where(valid_row, xrow, jnp.inf)
                licol = jnp.where(valid, licol, -jnp.inf)
            m_prev = m_scr[h][:, 0:1]
            dmat = jnp.where(incl, bcol - xrow, -jnp.inf)
            m_inter = bcol + m_prev
            m_t = jnp.maximum(m_inter, jnp.max(dmat, axis=1, keepdims=True))
            wts = jnp.exp(dmat - m_t)
            a_int = jnp.exp(m_inter - m_t)
            qk = _dot_nt(q, k) * wts
            ct = ct_scr[h]
            n_row = n_scr[h]
            num = _dot(qk, v) + a_int * _dot_nt(q, ct)
            den = jnp.sum(qk, axis=1, keepdims=True) + a_int * jnp.sum(q * n_row, axis=1, keepdims=True)
            hid = num / jnp.maximum(jnp.abs(den), jnp.exp(-m_t))
            b_end = bcol[C - 1:C, :]
            m_end = m_t[C - 1:C, :]
            a_end = a_int[C - 1:C, :]
            wk = jnp.exp(b_end - bcol + licol - m_end) * k
            ct_scr[h] = a_end * ct + _dot_tn(v, wk)
            n_scr[h] = a_end * n_row + jnp.sum(wk, axis=0, keepdims=True)
            m_scr[h] = jnp.broadcast_to(m_end, (1, LANE))
            hc = hid - jnp.mean(hid, -1, keepdims=True)
            hn = hc * lax.rsqrt(jnp.mean(hc * hc, -1, keepdims=True) + NORM_EPS) * ng_ref[:, h * dv:(h + 1) * dv]
            og = jax.nn.sigmoid(proj_ref[rows, 2 * nh * dk + nh * dv + h * dv:2 * nh * dk + nh * dv + (h + 1) * dv])
            o_ref[rows, h * dv:(h + 1) * dv] = og * hn
        return carry

    lax.fori_loop(0, tb // C, chunk, 0)

    @pl.when(t == nt - 1)
    def _():
        lane = _iota((1, nh), 1)
        mrow = jnp.zeros((1, nh), F32)
        for h in range(nh):
            cout_ref[0, h] = ct_scr[h].T
            nout_ref[0, h:h + 1, :] = n_scr[h]
            mrow = jnp.where(lane == h, m_scr[h][:, 0:1], mrow)
        mout_ref[0] = mrow


def _mlstm_scan(proj, c0, n0, m0, ng, bi, bf, *, nb, nt, tb, t_valid):
    _, nh, dk, dv = c0.shape
    wp = proj.shape[1]
    body = functools.partial(_mlstm_body, nh=nh, dk=dk, dv=dv, tb=tb, t_valid=t_valid, nt=nt)
    const = lambda b, t: (0, 0)
    m0 = m0.reshape(nb, 1, nh)
    return pl.pallas_call(
        body, grid=(nb, nt),
        in_specs=[pl.BlockSpec((tb, wp), lambda b, t: (b * nt + t, 0)),
                  pl.BlockSpec((1, nh, dk, dv), lambda b, t: (b, 0, 0, 0)),
                  pl.BlockSpec((1, nh, dk), lambda b, t: (b, 0, 0)),
                  pl.BlockSpec((1, 1, nh), lambda b, t: (b, 0, 0)),
                  pl.BlockSpec(ng.shape, const), pl.BlockSpec(bi.shape, const), pl.BlockSpec(bf.shape, const)],
        out_specs=[pl.BlockSpec((tb, nh * dv), lambda b, t: (b * nt + t, 0)),
                   pl.BlockSpec((1, nh, dk, dv), lambda b, t: (b, 0, 0, 0)),
                   pl.BlockSpec((1, nh, dk), lambda b, t: (b, 0, 0)),
                   pl.BlockSpec((1, 1, nh), lambda b, t: (b, 0, 0))],
        out_shape=[jax.ShapeDtypeStruct((nb * nt * tb, nh * dv), F32), jax.ShapeDtypeStruct(c0.shape, F32),
                   jax.ShapeDtypeStruct(n0.shape, F32), jax.ShapeDtypeStruct((nb, 1, nh), F32)],
        scratch_shapes=[pltpu.VMEM((nh, dv, dk), F32), pltpu.VMEM((nh, 1, dk), F32), pltpu.VMEM((nh, 1, LANE), F32)],
        compiler_params=_cparams(("parallel", "arbitrary")), name="mlstm_scan",
    )(proj, c0, n0, m0, ng, bi, bf)


def _top_values(s, v_scr, want_rank):
    rank = jnp.full(s.shape, float(PEER_TOPK), F32)
    for j in range(PEER_TOPK):
        m = jnp.max(s, axis=0, keepdims=True)
        hit = s == m
        if want_rank:
            rank = jnp.where(hit, float(j), rank)
        s = jnp.where(hit, -jnp.inf, s)
        v_scr[j:j + 1, :] = m
    return rank


def _route_body(xt_ref, wq_ref, k1_ref, k2_ref, r2_ref, c1_ref, a1_ref, a2_ref, q_scr, v1_scr, v2_scr, *, tm):
    q_scr[...] = _dot(wq_ref[...], xt_ref[...].astype(BF16))
    nsub = tm // LANE

    def unit(i, carry):
        h = i // nsub
        lanes = pl.ds(pl.multiple_of((i % nsub) * LANE, LANE), LANE)
        q1 = q_scr[pl.ds(pl.multiple_of(h * 2 * PEER_HALF, PEER_HALF), PEER_HALF), lanes]
        q2 = q_scr[pl.ds(pl.multiple_of(h * 2 * PEER_HALF + PEER_HALF, PEER_HALF), PEER_HALF), lanes]
        s1 = _dot(k1_ref[...], q1, HI)
        s2 = _dot(k2_ref[...], q2, HI)
        _top_values(s1, v1_scr, False)
        rank2 = _top_values(s2, v2_scr, True)
        v1 = v1_scr[...]
        v2 = v2_scr[...]
        cands = []
        for a in range(PEER_TOPK):
            nb_ok = PEER_TOPK // (a + 1)
            rows_b = C if nb_ok > 8 else 8
            cand = v1[a:a + 1, :] + v2[0:rows_b, :]
            cands.append(jnp.where(_iota((rows_b, 1), 0) < nb_ok, cand, -jnp.inf))
        work = list(cands)
        tau = None
        for _ in range(PEER_TOPK):
            tau = functools.reduce(jnp.maximum, [jnp.max(w, axis=0, keepdims=True) for w in work])
            work = [jnp.where(w == tau, -jnp.inf, w) for w in work]
        top = v1[0:1, :] + v2[0:1, :]
        z = functools.reduce(
            lambda x, y: x + y,
            [jnp.sum(jnp.where(cd >= tau, jnp.exp(cd - top), 0.0), axis=0, keepdims=True) for cd in cands])
        c1 = jnp.zeros(s1.shape, F32)
        for b in range(PEER_TOPK):
            c1 = c1 + jnp.where(s1 + v2[b:b + 1, :] >= tau, 1.0, 0.0)
        r2_ref[h, :, lanes] = rank2
        c1_ref[h, :, lanes] = c1
        a1_ref[h, :, lanes] = jnp.exp(s1 - v1[0:1, :])
        a2_ref[h, :, lanes] = jnp.exp(s2 - v2[0:1, :]) / z
        return carry

    lax.fori_loop(0, PEER_HEADS * nsub, unit, 0)


def _peer_route(xt, wq_t, k1, k2, tm):
    d, n = xt.shape
    shp = jax.ShapeDtypeStruct((PEER_HEADS, PEER_NKEYS, n), F32)
    ospec = pl.BlockSpec((PEER_HEADS, PEER_NKEYS, tm), lambda i: (0, 0, i))
    return pl.pallas_call(
        functools.partial(_route_body, tm=tm), grid=(n // tm,),
        in_specs=[pl.BlockSpec((d, tm), lambda i: (0, i)), pl.BlockSpec(wq_t.shape, lambda i: (0, 0)),
                  pl.BlockSpec(k1.shape, lambda i: (0, 0)), pl.BlockSpec(k2.shape, lambda i: (0, 0))],
        out_specs=[ospec] * 4, out_shape=[shp] * 4,
        scratch_shapes=[pltpu.VMEM((wq_t.shape[0], tm), F32), pltpu.VMEM((PEER_TOPK, LANE), F32),
                        pltpu.VMEM((PEER_TOPK, LANE), F32)],
        compiler_params=_cparams(("parallel",)), name="peer_route")(xt, wq_t, k1, k2)


def _peer_body(xt_ref, r2_ref, c1_ref, a1_ref, a2_ref, u_ref, vt_ref, g_ref, b_ref, out_ref, acc, xbf, p_scr):
    c = pl.program_id(1)

    @pl.when(c == 0)
    def _():
        xbf[...] = xt_ref[...].astype(BF16)
        acc[...] = jnp.zeros(acc.shape, F32)

    hid = _dot(u_ref[...], xbf[...])
    act = 0.5 * hid * (1.0 + lax.erf(hid * 0.5 ** 0.5))
    for j in range(PEER_I1_PER_STEP):
        i1 = c * PEER_I1_PER_STEP + j
        w = jnp.zeros((PEER_NKEYS, hid.shape[1]), F32)
        for h in range(PEER_HEADS):
            sel = r2_ref[h] < c1_ref[h, pl.ds(i1, 1), :]
            w = w + jnp.where(sel, a2_ref[h] * a1_ref[h, pl.ds(i1, 1), :], 0.0)
        p_scr[j * PEER_NKEYS:(j + 1) * PEER_NKEYS, :] = (w * act[j * PEER_NKEYS:(j + 1) * PEER_NKEYS, :]).astype(BF16)
    acc[...] += _dot(vt_ref[...], p_scr[...])

    @pl.when(c == pl.num_programs(1) - 1)
    def _():
        z = DN_ALPHA * xt_ref[...] + acc[...]
        zc = z - jnp.mean(z, 0, keepdims=True)
        var = jnp.mean(zc * zc, 0, keepdims=True)
        out_ref[...] = (zc * lax.rsqrt(var + LN_EPS) * g_ref[...] + b_ref[...]).T


def _peer_main(xt, r2, c1, a1, a2, u, vt, g, b, tm):
    d, n = xt.shape
    ec = PEER_I1_PER_STEP * PEER_NKEYS
    rspec = pl.BlockSpec((PEER_HEADS, PEER_NKEYS, tm), lambda i, c: (0, 0, i))
    return pl.pallas_call(
        _peer_body, grid=(n // tm, u.shape[0] // ec),
        in_specs=[pl.BlockSpec((d, tm), lambda i, c: (0, i)), rspec, rspec, rspec, rspec,
                  pl.BlockSpec((ec, d), lambda i, c: (c, 0)), pl.BlockSpec((d, ec), lambda i, c: (0, c)),
                  pl.BlockSpec((d, 1), lambda i, c: (0, 0)), pl.BlockSpec((d, 1), lambda i, c: (0, 0))],
        out_specs=pl.BlockSpec((tm, d), lambda i, c: (i, 0)),
        out_shape=jax.ShapeDtypeStruct((n, d), F32),
        scratch_shapes=[pltpu.VMEM((d, tm), F32), pltpu.VMEM((d, tm), BF16), pltpu.VMEM((ec, tm), BF16)],
        compiler_params=_cparams(("parallel", "arbitrary")), name="peer_main",
    )(xt, r2, c1, a1, a2, u, vt, g.reshape(d, 1), b.reshape(d, 1))


def _pad_cols(w, width):
    return jnp.pad(w, ((0, 0), (0, width - w.shape[1])))


def _lane_row(*parts):
    v = jnp.concatenate([p.astype(F32).reshape(-1) for p in parts])
    return jnp.pad(v, (0, LANE - v.shape[0])).reshape(1, LANE)


def _pick_tb(t_pad):
    for tb in (48, 32, 16):
        if t_pad % tb == 0:
            return tb
    raise ValueError(t_pad)


def kernel(x_prompt, x_sample, state_hgrn_S, state_gdn_S, state_gdn_conv, state_mlstm_C, state_mlstm_n, state_mlstm_m, state_gla_S, meta_tokens, hgrn_w_in, hgrn_lb, hgrn_norm_g, hgrn_w_out, gdn_w_in, gdn_conv_w, gdn_a_log, gdn_dt_bias, gdn_norm_g, gdn_w_out, mlstm_w_in, mlstm_b_i, mlstm_b_f, mlstm_norm_g, mlstm_w_out, gla_w_in, gla_w_gate, gla_b_gate, gla_norm_g, gla_w_out, peer_w_q, peer_keys1, peer_keys2, peer_u, peer_v, ln_g, ln_b):
    bp, seq, d = x_prompt.shape
    bs, seq_s, _ = x_sample.shape
    assert d == D_MODEL and len(state_hgrn_S) == 1 and len(state_gdn_S) == 1
    assert len(state_mlstm_C) == 1 and len(state_gla_S) == 1
    tp = N_META + seq
    tp_pad = -(-tp // C) * C
    ts_pad = -(-seq_s // C) * C
    tbp = _pick_tb(tp_pad)
    tbs = _pick_tb(ts_pad)
    np_rows = bp * tp_pad
    n_real = np_rows + bs * seq_s

    meta = jnp.broadcast_to(meta_tokens.astype(F32)[None], (bp, N_META, d))
    hp = jnp.concatenate([meta, x_prompt], axis=1)
    hp = jnp.pad(hp, ((0, 0), (0, tp_pad - tp), (0, 0))).reshape(np_rows, d)
    h = jnp.concatenate([hp, x_sample.reshape(bs * seq_s, d)], axis=0)
    n = -(-n_real // TOK_BLOCK) * TOK_BLOCK
    h = jnp.pad(h, ((0, n - n_real), (0, 0)))
    tm_mm = MM_TOK_BLOCK if n % MM_TOK_BLOCK == 0 else TOK_BLOCK

    def split_cols(w, sizes):
        out, o = [], 0
        for s in sizes:
            out.append(w[:, o:o + s])
            o += s
        return out

    def run_mixer(scan, proj, states_p, states_s, **kw):
        res_p = scan(proj, *states_p, nb=bp, nt=tp_pad // tbp, tb=tbp, t_valid=tp, **kw)
        ps = proj[np_rows:n_real].reshape(bs, seq_s, -1)
        ps = jnp.pad(ps, ((0, 0), (0, ts_pad - seq_s), (0, 0))).reshape(bs * ts_pad, -1)
        res_s = scan(ps, *states_s, nb=bs, nt=ts_pad // tbs, tb=tbs, t_valid=seq_s, **kw)
        o_s = res_s[0].reshape(bs, ts_pad, -1)[:, :seq_s].reshape(bs * seq_s, -1)
        o = jnp.concatenate([res_p[0], o_s], axis=0)
        o = jnp.pad(o, ((0, n - n_real), (0, 0)))
        return o, res_p[1:], res_s[1:]

    lb_all = jnp.cumsum(jax.nn.softmax(hgrn_lb.astype(F32), axis=0), axis=0)
    outs = {}
    for i in range(DEPTH):
        mix = i % 4
        if mix == 0:
            w_in = hgrn_w_in[0].astype(BF16)
            proj = _mm(h, w_in, tm_mm, 1024)
            nh, dk, dv = state_hgrn_S.shape[2:]
            zero = jnp.zeros((bp, nh, dk, dv), F32)
            scan = functools.partial(_gla_scan, ng=hgrn_norm_g[0].reshape(1, dv), p1=lb_all[i].reshape(1, nh * dk),
                                     p2=jnp.zeros((1, LANE), F32), hgrn=True)
            o, (sp,), (ss,) = run_mixer(scan, proj, (zero,), (state_hgrn_S[0],))
            outs['hgrn'] = (sp[None], ss[None])
            w_out = hgrn_w_out[0]
        elif mix == 1:
            nh, dk, dv = state_gdn_S.shape[2:]
            cwid = 2 * nh * dk + nh * dv
            qkv, a, b, g = split_cols(gdn_w_in[0], [cwid, nh, nh, nh * dv])
            w_in = jnp.concatenate([qkv, g, _pad_cols(jnp.concatenate([a, b], axis=1), LANE)], axis=1).astype(BF16)
            proj = _mm(h, w_in, tm_mm, w_in.shape[1] // 3)
            scan = functools.partial(_gdn_scan, cw=gdn_conv_w[0], ng=gdn_norm_g[0].reshape(1, dv),
                                     alog=_lane_row(gdn_a_log[0]), dtb=_lane_row(gdn_dt_bias[0]))
            o, (sp, cp), (ss, cs) = run_mixer(
                scan, proj,
                (jnp.zeros((bp, nh, dk, dv), F32), jnp.zeros((bp, CONV_W - 1, cwid), F32)),
                (state_gdn_S[0], state_gdn_conv[0]))
            outs['gdn'] = (sp[None], cp[None], ss[None], cs[None])
            w_out = gdn_w_out[0]
        elif mix == 2:
            nh, dk, dv = state_mlstm_C.shape[2:]
            q, k, v, ig, fg, og = split_cols(mlstm_w_in[0], [nh * dk, nh * dk, nh * dv, nh, nh, nh * dv])
            w_in = jnp.concatenate([q, k, v, og, _pad_cols(jnp.concatenate([ig, fg], axis=1), LANE)],
                                   axis=1).astype(BF16)
            proj = _mm(h, w_in, tm_mm, w_in.shape[1] // 3)
            zeros_nh = jnp.zeros((nh,), F32)
            scan = functools.partial(_mlstm_scan, ng=mlstm_norm_g[0].reshape(1, nh * dv),
                                     bi=_lane_row(mlstm_b_i[0]), bf=_lane_row(zeros_nh, mlstm_b_f[0]))
            o, (cp, npp, mp), (cs, ns, ms) = run_mixer(
                scan, proj,
                (jnp.zeros((bp, nh, dk, dv), F32), jnp.zeros((bp, nh, dk), F32), jnp.zeros((bp, nh), F32)),
                (state_mlstm_C[0], state_mlstm_n[0], state_mlstm_m[0]))
            outs['mlstm'] = (cp[None], npp[None], mp.reshape(1, bp, nh), cs[None], ns[None], ms.reshape(1, bs, nh))
            w_out = mlstm_w_out[0]
        else:
            nh, dk, dv = state_gla_S.shape[2:]
            rank = gla_w_gate.shape[1]
            q, k, v, g, r = split_cols(gla_w_in[0], [nh * dk, nh * dk, nh * dv, nh * dv, rank])
            w_in = jnp.concatenate([q, k, v, g, _pad_cols(r, LANE)], axis=1).astype(BF16)
            proj = _mm(h, w_in, tm_mm, w_in.shape[1] // 5)
            wg = jnp.pad(gla_w_gate[0].astype(F32), ((0, LANE - rank), (0, 0)))
            scan = functools.partial(_gla_scan, ng=gla_norm_g[0].reshape(1, dv), p1=wg,
                                     p2=gla_b_gate[0].reshape(1, nh * dk), hgrn=False)
            o, (sp,), (ss,) = run_mixer(scan, proj, (jnp.zeros((bp, nh, dk, dv), F32),), (state_gla_S[0],))
            outs['gla'] = (sp[None], ss[None])
            w_out = gla_w_out[0]

        h1t = _mm_res_ln_t(o, w_out.astype(BF16), h, ln_g[i, 0], ln_b[i, 0], TOK_BLOCK)
        r2, c1, a1, a2 = _peer_route(h1t, peer_w_q[i].T.astype(BF16), peer_keys1[i], peer_keys2[i], TOK_BLOCK)
        h = _peer_main(h1t, r2, c1, a1, a2, peer_u[i].astype(BF16), peer_v[i].T.astype(BF16), ln_g[i, 1],
                       ln_b[i, 1], TOK_BLOCK)

    y_prompt = h[:np_rows].reshape(bp, tp_pad, d)[:, N_META:tp]
    y_sample = h[np_rows:n_real].reshape(bs, seq_s, d)
    hg, gd, ml, gl = outs['hgrn'], outs['gdn'], outs['mlstm'], outs['gla']
    return (y_prompt, y_sample, hg[0], gd[0], gd[1], ml[0], ml[1], ml[2], gl[0],
            hg[1], gd[2], gd[3], ml[3], ml[4], ml[5], gl[1])
```

```python
import functools

import jax
import jax.numpy as jnp
from jax import lax
from jax.experimental import pallas as pl
from jax.experimental.pallas import tpu as pltpu

F32 = jnp.float32
BF16 = jnp.bfloat16
HI = lax.Precision.HIGHEST

D_MODEL = 1024
DEPTH = 4
N_META = 16
C = 16
CONV_W = 4
NORM_EPS = 1e-6
LN_EPS = 1e-5
DN_ALPHA = (2 * DEPTH) ** 0.25
GLA_GATE_NORM = 16.0
LANE = 128
PEER_HEADS = 8
PEER_NKEYS = 128
PEER_TOPK = 16
PEER_HALF = 128
PEER_I1_PER_STEP = 8
PEER_SUB_I1 = 2
TOK_BLOCK = 256
MM_TOK_BLOCK = 640
VMEM_LIMIT = 48 * 1024 * 1024


def _cparams(sem):
    return pltpu.CompilerParams(dimension_semantics=sem, vmem_limit_bytes=VMEM_LIMIT)


def _iota(shape, dim):
    return lax.broadcasted_iota(jnp.int32, shape, dim)


def _dot_nt(a, b, precision=None):
    return lax.dot_general(a, b, (((1,), (1,)), ((), ())), precision=precision, preferred_element_type=F32)


def _dot_tn(a, b, precision=None):
    return lax.dot_general(a, b, (((0,), (0,)), ((), ())), precision=precision, preferred_element_type=F32)


def _dot(a, b, precision=None):
    return jnp.dot(a, b, precision=precision, preferred_element_type=F32)


def _cumsum_rows(x):
    tri = (_iota((C, C), 0) >= _iota((C, C), 1)).astype(F32)
    return _dot(tri, x, HI)


def _rows_to_cols(x):
    eye = (_iota((LANE, LANE), 0) == _iota((LANE, LANE), 1)).astype(F32)
    return _dot_nt(eye, x, HI)


def _incl_mask():
    return _iota((C, C), 0) >= _iota((C, C), 1)


def _mm_body(x_ref, w_ref, o_ref):
    o_ref[...] = _dot(x_ref[...].astype(BF16), w_ref[...])


def _mm(x, w, tm, tn):
    m, k = x.shape
    n = w.shape[1]
    return pl.pallas_call(
        _mm_body, grid=(m // tm, n // tn),
        in_specs=[pl.BlockSpec((tm, k), lambda i, j: (i, 0)), pl.BlockSpec((k, tn), lambda i, j: (0, j))],
        out_specs=pl.BlockSpec((tm, tn), lambda i, j: (i, j)),
        out_shape=jax.ShapeDtypeStruct((m, n), F32),
        compiler_params=_cparams(("parallel", "parallel")), name="mm")(x, w)


def _mm_ln_body(o_ref, w_ref, h_ref, g_ref, b_ref, out_ref):
    y = _dot(o_ref[...].astype(BF16), w_ref[...])
    z = DN_ALPHA * h_ref[...] + y
    zc = z - jnp.mean(z, -1, keepdims=True)
    var = jnp.mean(zc * zc, -1, keepdims=True)
    out_ref[...] = (zc * lax.rsqrt(var + LN_EPS) * g_ref[...] + b_ref[...]).T


def _mm_res_ln_t(o, w, h, g, b, tm):
    n, k = o.shape
    d = w.shape[1]
    return pl.pallas_call(
        _mm_ln_body, grid=(n // tm,),
        in_specs=[pl.BlockSpec((tm, k), lambda i: (i, 0)), pl.BlockSpec((k, d), lambda i: (0, 0)),
                  pl.BlockSpec((tm, d), lambda i: (i, 0)), pl.BlockSpec((1, d), lambda i: (0, 0)),
                  pl.BlockSpec((1, d), lambda i: (0, 0))],
        out_specs=pl.BlockSpec((d, tm), lambda i: (0, i)),
        out_shape=jax.ShapeDtypeStruct((d, n), F32),
        compiler_params=_cparams(("parallel",)), name="mm_res_ln")(o, w, h, g.reshape(1, d), b.reshape(1, d))


def _valid_rows(t, r0, tb, t_valid, nt):
    if t_valid == nt * tb:
        return None
    return (t * tb + r0 + _iota((C, 1), 0)) < t_valid


def _gla_chunk(q, k, v, g, st):
    b = _cumsum_rows(g)
    o = _dot_nt(q * jnp.exp(b), st)
    row = _iota((C, 1), 0)
    for s in range(C):
        d = jnp.exp(jnp.where(row >= s, b - b[s:s + 1, :], -jnp.inf))
        col = jnp.sum(q * (k[s:s + 1, :] * d), axis=-1, keepdims=True)
        o = o + col * v[s:s + 1, :]
    b_end = b[C - 1:C, :]
    st_new = st * jnp.exp(b_end) + _dot_tn(v, k * jnp.exp(b_end - b))
    return o, st_new


def _gated_rmsnorm(o, ng, gate):
    return o * lax.rsqrt(jnp.mean(o * o, -1, keepdims=True) + NORM_EPS) * ng * jax.nn.silu(gate)


def _gla_body(proj_ref, s0_ref, ng_ref, p1_ref, p2_ref, o_ref, sout_ref, st_scr, *, hgrn, nh, dk, dv, tb, t_valid,
              nt):
    t = pl.program_id(1)

    @pl.when(t == 0)
    def _():
        for h in range(nh):
            st_scr[h] = s0_ref[0, h].T

    kw = nh * dk
    vw = nh * dv

    def chunk(ci, carry):
        r0 = pl.multiple_of(ci * C, C)
        rows = pl.ds(r0, C)
        valid = _valid_rows(t, r0, tb, t_valid, nt)
        if not hgrn:
            r = proj_ref[rows, 2 * kw + 2 * vw:2 * kw + 2 * vw + LANE]
            logf_all = jax.nn.log_sigmoid(_dot(r, p1_ref[...], HI) + p2_ref[...]) / GLA_GATE_NORM
        for h in range(nh):
            ks = slice(h * dk, (h + 1) * dk)
            if hgrn:
                q = jax.nn.silu(proj_ref[rows, ks])
                lb = p1_ref[:, ks]
                fg = lb + (1.0 - lb) * jax.nn.sigmoid(proj_ref[rows, kw + h * dk:kw + (h + 1) * dk])
                k = 1.0 - fg
                g = jnp.log(fg)
            else:
                q = proj_ref[rows, ks] * dk ** -0.5
                k = proj_ref[rows, kw + h * dk:kw + (h + 1) * dk]
                g = logf_all[:, ks]
            v = proj_ref[rows, 2 * kw + h * dv:2 * kw + (h + 1) * dv]
            gate = proj_ref[rows, 2 * kw + vw + h * dv:2 * kw + vw + (h + 1) * dv]
            if valid is not None:
                k = jnp.where(valid, k, 0.0)
                g = jnp.where(valid, g, 0.0)
            o, st_new = _gla_chunk(q, k, v, g, st_scr[h])
            st_scr[h] = st_new
            o_ref[rows, h * dv:(h + 1) * dv] = _gated_rmsnorm(o, ng_ref[...], gate)
        return carry

    lax.fori_loop(0, tb // C, chunk, 0)

    @pl.when(t == nt - 1)
    def _():
        for h in range(nh):
            sout_ref[0, h] = st_scr[h].T


def _gla_scan(proj, s0, ng, p1, p2, *, hgrn, nb, nt, tb, t_valid):
    _, nh, dk, dv = s0.shape
    wp = proj.shape[1]
    body = functools.partial(_gla_body, hgrn=hgrn, nh=nh, dk=dk, dv=dv, tb=tb, t_valid=t_valid, nt=nt)
    return pl.pallas_call(
        body, grid=(nb, nt),
        in_specs=[pl.BlockSpec((tb, wp), lambda b, t: (b * nt + t, 0)),
                  pl.BlockSpec((1, nh, dk, dv), lambda b, t: (b, 0, 0, 0)),
                  pl.BlockSpec(ng.shape, lambda b, t: (0, 0)),
                  pl.BlockSpec(p1.shape, lambda b, t: (0, 0)),
                  pl.BlockSpec(p2.shape, lambda b, t: (0, 0))],
        out_specs=[pl.BlockSpec((tb, nh * dv), lambda b, t: (b * nt + t, 0)),
                   pl.BlockSpec((1, nh, dk, dv), lambda b, t: (b, 0, 0, 0))],
        out_shape=[jax.ShapeDtypeStruct((nb * nt * tb, nh * dv), F32), jax.ShapeDtypeStruct(s0.shape, F32)],
        scratch_shapes=[pltpu.VMEM((nh, dv, dk), F32)],
        compiler_params=_cparams(("parallel", "arbitrary")), name="hgrn_scan" if hgrn else "gla_scan",
    )(proj, s0, ng, p1, p2)


def _gdn_body(proj_ref, s0_ref, conv0_ref, cw_ref, ng_ref, alog_ref, dtb_ref, o_ref, sout_ref, convout_ref,
              st_scr, cbuf, ybuf, *, nh, dk, dv, tb, t_valid, nt):
    t = pl.program_id(1)
    cwid = 2 * nh * dk + nh * dv
    pre = 8 - (CONV_W - 1)

    @pl.when(t == 0)
    def _():
        for h in range(nh):
            st_scr[h] = s0_ref[0, h].T
        cbuf[0:8, :] = jnp.zeros((8, cwid), F32)
        cbuf[pre:8, :] = conv0_ref[0]

    cbuf[8:8 + tb, :] = proj_ref[:, 0:cwid]
    y = cbuf[pre:pre + tb, :] * cw_ref[0:1, :]
    for j in range(1, CONV_W):
        y = y + cbuf[pre + j:pre + j + tb, :] * cw_ref[j:j + 1, :]
    ybuf[...] = jax.nn.silu(y)

    incl = _incl_mask()
    strict = _iota((C, C), 0) > _iota((C, C), 1)
    eye = (_iota((C, C), 0) == _iota((C, C), 1)).astype(F32)

    def chunk(ci, carry):
        r0 = pl.multiple_of(ci * C, C)
        rows = pl.ds(r0, C)
        valid = _valid_rows(t, r0, tb, t_valid, nt)
        small = proj_ref[rows, cwid + nh * dv:cwid + nh * dv + LANE]
        g_all = -jnp.exp(alog_ref[...]) * jax.nn.softplus(small + dtb_ref[...])
        beta_all = jax.nn.sigmoid(small)
        if valid is not None:
            g_all = jnp.where(valid, g_all, 0.0)
            beta_all = jnp.where(valid, beta_all, 0.0)
        b_all = _cumsum_rows(g_all)
        b_t = _rows_to_cols(b_all)
        for h in range(nh):
            q = ybuf[rows, h * dk:(h + 1) * dk]
            k = ybuf[rows, nh * dk + h * dk:nh * dk + (h + 1) * dk]
            v = ybuf[rows, 2 * nh * dk + h * dv:2 * nh * dk + (h + 1) * dv]
            q = q * lax.rsqrt(jnp.sum(q * q, -1, keepdims=True) + NORM_EPS) * dk ** -0.5
            k = k * lax.rsqrt(jnp.sum(k * k, -1, keepdims=True) + NORM_EPS)
            if valid is not None:
                q = jnp.where(valid, q, 0.0)
                k = jnp.where(valid, k, 0.0)
                v = jnp.where(valid, v, 0.0)
            bcol = b_all[:, h:h + 1]
            brow = b_t[h:h + 1, :]
            beta = beta_all[:, nh + h:nh + h + 1]
            decay = jnp.exp(jnp.where(incl, bcol - brow, -jnp.inf))
            a = jnp.where(strict, _dot_nt(k, k, HI) * decay * beta, 0.0)
            a2 = _dot(a, a, HI)
            a4 = _dot(a2, a2, HI)
            a8 = _dot(a4, a4, HI)
            tinv = _dot(_dot(eye - a, eye + a2, HI), _dot(eye + a4, eye + a8, HI), HI)
            st = st_scr[h]
            u = _dot(tinv, beta * v, HI)
            wk = _dot(tinv, (beta * jnp.exp(bcol)) * k, HI)
            v_new = u - _dot_nt(wk, st)
            att = _dot_nt(q, k) * decay
            o = _dot(att, v_new) + _dot_nt(q * jnp.exp(bcol), st)
            b_end = bcol[C - 1:C, :]
            st_scr[h] = st * jnp.exp(b_end) + _dot_tn(v_new, k * jnp.exp(b_end - bcol))
            gate = proj_ref[rows, cwid + h * dv:cwid + (h + 1) * dv]
            o_ref[rows, h * dv:(h + 1) * dv] = _gated_rmsnorm(o, ng_ref[...], gate)
        return carry

    lax.fori_loop(0, tb // C, chunk, 0)

    @pl.when(t < nt - 1)
    def _():
        cbuf[pre:8, :] = cbuf[pre + tb:8 + tb, :]

    @pl.when(t == nt - 1)
    def _():
        for h in range(nh):
            sout_ref[0, h] = st_scr[h].T
        last = t_valid - (nt - 1) * tb
        convout_ref[0] = cbuf[pre + last:8 + last, :]


def _gdn_scan(proj, s0, conv0, cw, ng, alog, dtb, *, nb, nt, tb, t_valid):
    _, nh, dk, dv = s0.shape
    wp = proj.shape[1]
    cwid = 2 * nh * dk + nh * dv
    body = functools.partial(_gdn_body, nh=nh, dk=dk, dv=dv, tb=tb, t_valid=t_valid, nt=nt)
    const = lambda b, t: (0, 0)
    return pl.pallas_call(
        body, grid=(nb, nt),
        in_specs=[pl.BlockSpec((tb, wp), lambda b, t: (b * nt + t, 0)),
                  pl.BlockSpec((1, nh, dk, dv), lambda b, t: (b, 0, 0, 0)),
                  pl.BlockSpec((1, CONV_W - 1, cwid), lambda b, t: (b, 0, 0)),
                  pl.BlockSpec(cw.shape, const), pl.BlockSpec(ng.shape, const),
                  pl.BlockSpec(alog.shape, const), pl.BlockSpec(dtb.shape, const)],
        out_specs=[pl.BlockSpec((tb, nh * dv), lambda b, t: (b * nt + t, 0)),
                   pl.BlockSpec((1, nh, dk, dv), lambda b, t: (b, 0, 0, 0)),
                   pl.BlockSpec((1, CONV_W - 1, cwid), lambda b, t: (b, 0, 0))],
        out_shape=[jax.ShapeDtypeStruct((nb * nt * tb, nh * dv), F32), jax.ShapeDtypeStruct(s0.shape, F32),
                   jax.ShapeDtypeStruct(conv0.shape, F32)],
        scratch_shapes=[pltpu.VMEM((nh, dv, dk), F32), pltpu.VMEM((tb + 8, cwid), F32), pltpu.VMEM((tb, cwid), F32)],
        compiler_params=_cparams(("parallel", "arbitrary")), name="gdn_scan",
    )(proj, s0, conv0, cw, ng, alog, dtb)


def _mlstm_body(proj_ref, c0_ref, n0_ref, m0_ref, ng_ref, bi_ref, bf_ref, o_ref, cout_ref, nout_ref, mout_ref,
                ct_scr, n_scr, m_scr, *, nh, dk, dv, tb, t_valid, nt):
    t = pl.program_id(1)

    @pl.when(t == 0)
    def _():
        for h in range(nh):
            ct_scr[h] = c0_ref[0, h].T
            n_scr[h] = n0_ref[0, h:h + 1, :]
            m_scr[h] = jnp.broadcast_to(m0_ref[0, :, h:h + 1], (1, LANE))

    incl = _incl_mask()
    base = 2 * nh * dk + 2 * nh * dv

    def chunk(ci, carry):
        r0 = pl.multiple_of(ci * C, C)
        rows = pl.ds(r0, C)
        valid = _valid_rows(t, r0, tb, t_valid, nt)
        small = proj_ref[rows, base:base + LANE]
        li_all = small + bi_ref[...]
        lf_all = jax.nn.log_sigmoid(small + bf_ref[...])
        if valid is not None:
            lf_all = jnp.where(valid, lf_all, 0.0)
        b_all = _cumsum_rows(lf_all)
        b_t = _rows_to_cols(b_all)
        li_t = _rows_to_cols(li_all)
        if valid is not None:
            valid_row = (t * tb + r0 + _iota((1, C), 1)) < t_valid
        for h in range(nh):
            q = proj_ref[rows, h * dk:(h + 1) * dk]
            k = proj_ref[rows, nh * dk + h * dk:nh * dk + (h + 1) * dk] * dk ** -0.5
            v = proj_ref[rows, 2 * nh * dk + h * dv:2 * nh * dk + (h + 1) * dv]
            bcol = b_all[:, nh + h:nh + h + 1]
            licol = li_all[:, h:h + 1]
            xrow = b_t[nh + h:nh + h + 1, :] - li_t[h:h + 1, :]
            if valid is not None:
                xrow = jnp.where(valid_row, xrow, jnp.inf)
                licol = jnp.where(valid, licol, -jnp.inf)
            m_prev = m_scr[h][:, 0:1]
            dmat = jnp.where(incl, bcol - xrow, -jnp.inf)
            m_inter = bcol + m_prev
            m_t = jnp.maximum(m_inter, jnp.max(dmat, axis=1, keepdims=True))
            wts = jnp.exp(dmat - m_t)
            a_int = jnp.exp(m_inter - m_t)
            qk = _dot_nt(q, k) * wts
            ct = ct_scr[h]
            n_row = n_scr[h]
            num = _dot(qk, v) + a_int * _dot_nt(q, ct)
            den = jnp.sum(qk, axis=1, keepdims=True) + a_int * jnp.sum(q * n_row, axis=1, keepdims=True)
            hid = num / jnp.maximum(jnp.abs(den), jnp.exp(-m_t))
            b_end = bcol[C - 1:C, :]
            m_end = m_t[C - 1:C, :]
            a_end = a_int[C - 1:C, :]
            wk = jnp.exp(b_end - bcol + licol - m_end) * k
            ct_scr[h] = a_end * ct + _dot_tn(v, wk)
            n_scr[h] = a_end * n_row + jnp.sum(wk, axis=0, keepdims=True)
            m_scr[h] = jnp.broadcast_to(m_end, (1, LANE))
            hc = hid - jnp.mean(hid, -1, keepdims=True)
            hn = hc * lax.rsqrt(jnp.mean(hc * hc, -1, keepdims=True) + NORM_EPS) * ng_ref[:, h * dv:(h + 1) * dv]
            og = jax.nn.sigmoid(proj_ref[rows, 2 * nh * dk + nh * dv + h * dv:2 * nh * dk + nh * dv + (h + 1) * dv])
            o_ref[rows, h * dv:(h + 1) * dv] = og * hn
        return carry

    lax.fori_loop(0, tb // C, chunk, 0)

    @pl.when(t == nt - 1)
    def _():
        lane = _iota((1, nh), 1)
        mrow = jnp.zeros((1, nh), F32)
        for h in range(nh):
            cout_ref[0, h] = ct_scr[h].T
            nout_ref[0, h:h + 1, :] = n_scr[h]
            mrow = jnp.where(lane == h, m_scr[h][:, 0:1], mrow)
        mout_ref[0] = mrow


def _mlstm_scan(proj, c0, n0, m0, ng, bi, bf, *, nb, nt, tb, t_valid):
    _, nh, dk, dv = c0.shape
    wp = proj.shape[1]
    body = functools.partial(_mlstm_body, nh=nh, dk=dk, dv=dv, tb=tb, t_valid=t_valid, nt=nt)
    const = lambda b, t: (0, 0)
    m0 = m0.reshape(nb, 1, nh)
    return pl.pallas_call(
        body, grid=(nb, nt),
        in_specs=[pl.BlockSpec((tb, wp), lambda b, t: (b * nt + t, 0)),
                  pl.BlockSpec((1, nh, dk, dv), lambda b, t: (b, 0, 0, 0)),
                  pl.BlockSpec((1, nh, dk), lambda b, t: (b, 0, 0)),
                  pl.BlockSpec((1, 1, nh), lambda b, t: (b, 0, 0)),
                  pl.BlockSpec(ng.shape, const), pl.BlockSpec(bi.shape, const), pl.BlockSpec(bf.shape, const)],
        out_specs=[pl.BlockSpec((tb, nh * dv), lambda b, t: (b * nt + t, 0)),
                   pl.BlockSpec((1, nh, dk, dv), lambda b, t: (b, 0, 0, 0)),
                   pl.BlockSpec((1, nh, dk), lambda b, t: (b, 0, 0)),
                   pl.BlockSpec((1, 1, nh), lambda b, t: (b, 0, 0))],
        out_shape=[jax.ShapeDtypeStruct((nb * nt * tb, nh * dv), F32), jax.ShapeDtypeStruct(c0.shape, F32),
                   jax.ShapeDtypeStruct(n0.shape, F32), jax.ShapeDtypeStruct((nb, 1, nh), F32)],
        scratch_shapes=[pltpu.VMEM((nh, dv, dk), F32), pltpu.VMEM((nh, 1, dk), F32), pltpu.VMEM((nh, 1, LANE), F32)],
        compiler_params=_cparams(("parallel", "arbitrary")), name="mlstm_scan",
    )(proj, c0, n0, m0, ng, bi, bf)


SUB = 8


def _tree(op, xs):
    xs = list(xs)
    while len(xs) > 1:
        xs = [op(xs[i], xs[i + 1]) if i + 1 < len(xs) else xs[i] for i in range(0, len(xs), 2)]
    return xs[0]


def _all_sublanes(op, m):
    for shift in (4, 2, 1):
        m = op(m, pltpu.roll(m, shift, axis=0))
    return m


def _top16(tiles, want_rank):
    tiles = list(tiles)
    ranks = [jnp.full((SUB, LANE), float(PEER_TOPK), F32) for _ in tiles] if want_rank else None
    vals = []
    for j in range(PEER_TOPK):
        m = _all_sublanes(jnp.maximum, _tree(jnp.maximum, tiles))
        vals.append(m)
        for i in range(len(tiles)):
            hit = tiles[i] == m
            if want_rank:
                ranks[i] = jnp.where(hit, float(j), ranks[i])
            tiles[i] = jnp.where(hit, -jnp.inf, tiles[i])
    return vals, ranks


def _route_body(xt_ref, wq_ref, k1_ref, k2_ref, r2_ref, c1_ref, a1_ref, a2_ref, q_scr, *, tm):
    q_scr[...] = _dot(wq_ref[...], xt_ref[...].astype(BF16))
    nsub = tm // LANE
    ntile = PEER_NKEYS // SUB
    sub = _iota((SUB, LANE), 0)

    def unit(i, carry):
        h = i // nsub
        lanes = pl.ds(pl.multiple_of((i % nsub) * LANE, LANE), LANE)
        q1 = q_scr[pl.ds(pl.multiple_of(h * 2 * PEER_HALF, PEER_HALF), PEER_HALF), lanes]
        q2 = q_scr[pl.ds(pl.multiple_of(h * 2 * PEER_HALF + PEER_HALF, PEER_HALF), PEER_HALF), lanes]
        s1 = _dot(k1_ref[...], q1, HI)
        s2 = _dot(k2_ref[...], q2, HI)
        t1 = [s1[k * SUB:(k + 1) * SUB, :] for k in range(ntile)]
        t2 = [s2[k * SUB:(k + 1) * SUB, :] for k in range(ntile)]
        v1, _ = _top16(t1, False)
        v2, rank2 = _top16(t2, True)
        v2lo, v2hi = v2[SUB - 1], v2[2 * SUB - 1]
        for b in range(SUB - 2, -1, -1):
            v2lo = jnp.where(sub == b, v2[b], v2lo)
            v2hi = jnp.where(sub == b, v2[SUB + b], v2hi)
        cands = []
        for a in range(PEER_TOPK):
            nb_ok = PEER_TOPK // (a + 1)
            cands.append(jnp.where(sub < min(nb_ok, SUB), v1[a] + v2lo, -jnp.inf))
            if nb_ok > SUB:
                cands.append(jnp.where(sub < nb_ok - SUB, v1[a] + v2hi, -jnp.inf))
        work = list(cands)
        tau = None
        for _ in range(PEER_TOPK):
            tau = _all_sublanes(jnp.maximum, _tree(jnp.maximum, work))
            work = [jnp.where(w == tau, -jnp.inf, w) for w in work]
        top = v1[0] + v2[0]
        z = _all_sublanes(jnp.add, _tree(jnp.add, [jnp.where(cd >= tau, jnp.exp(cd - top), 0.0) for cd in cands]))
        inv_z = 1.0 / z
        c1 = []
        for k in range(ntile):
            c1.append(_tree(jnp.add, [jnp.where(t1[k] + v2[b] >= tau, 1.0, 0.0) for b in range(PEER_TOPK)]))
        r2_ref[h, :, lanes] = jnp.concatenate(rank2, axis=0).astype(BF16)
        c1_ref[h, :, lanes] = jnp.concatenate(c1, axis=0)
        a1_ref[h, :, lanes] = jnp.concatenate([jnp.exp(t - v1[0]) for t in t1], axis=0)
        a2_ref[h, :, lanes] = jnp.concatenate([jnp.exp(t - v2[0]) * inv_z for t in t2], axis=0).astype(BF16)
        return carry

    lax.fori_loop(0, PEER_HEADS * nsub, unit, 0)


def _peer_route(xt, wq_t, k1, k2, tm):
    d, n = xt.shape
    ospec = pl.BlockSpec((PEER_HEADS, PEER_NKEYS, tm), lambda i: (0, 0, i))
    shp = lambda dt: jax.ShapeDtypeStruct((PEER_HEADS, PEER_NKEYS, n), dt)
    return pl.pallas_call(
        functools.partial(_route_body, tm=tm), grid=(n // tm,),
        in_specs=[pl.BlockSpec((d, tm), lambda i: (0, i)), pl.BlockSpec(wq_t.shape, lambda i: (0, 0)),
                  pl.BlockSpec(k1.shape, lambda i: (0, 0)), pl.BlockSpec(k2.shape, lambda i: (0, 0))],
        out_specs=[ospec] * 4, out_shape=[shp(BF16), shp(F32), shp(F32), shp(BF16)],
        scratch_shapes=[pltpu.VMEM((wq_t.shape[0], tm), F32)],
        compiler_params=_cparams(("parallel",)), name="peer_route")(xt, wq_t, k1, k2)


def _peer_body(xt_ref, r2_ref, c1_ref, a1_ref, a2_ref, u_ref, vt_ref, g_ref, b_ref, out_ref, acc, xbf):
    c = pl.program_id(1)

    @pl.when(c == 0)
    def _():
        xbf[...] = xt_ref[...].astype(BF16)
        acc[...] = jnp.zeros(acc.shape, F32)

    tm = xbf.shape[1]
    rows16 = 2 * SUB
    sub_w = PEER_SUB_I1 * PEER_NKEYS
    total = None
    for s in range(PEER_I1_PER_STEP // PEER_SUB_I1):
        hid = _dot(u_ref[s * sub_w:(s + 1) * sub_w, :], xbf[...])
        act = (0.5 * hid * (1.0 + lax.erf(hid * 0.5 ** 0.5))).astype(BF16)
        parts = []
        for jj in range(PEER_SUB_I1):
            j = s * PEER_SUB_I1 + jj
            c1b = [jnp.broadcast_to(c1_ref[h, j:j + 1, :], (rows16, tm)).astype(BF16) for h in range(PEER_HEADS)]
            a1b = [jnp.broadcast_to(a1_ref[h, j:j + 1, :], (rows16, tm)).astype(BF16) for h in range(PEER_HEADS)]
            for kb in range(PEER_NKEYS // rows16):
                ks = slice(kb * rows16, (kb + 1) * rows16)
                w = _tree(jnp.add, [jnp.where(r2_ref[h, ks, :] < c1b[h], a2_ref[h, ks, :] * a1b[h], 0.0)
                                    for h in range(PEER_HEADS)])
                es = slice(jj * PEER_NKEYS + kb * rows16, jj * PEER_NKEYS + (kb + 1) * rows16)
                parts.append(w * act[es, :])
        part = _dot(vt_ref[:, s * sub_w:(s + 1) * sub_w], jnp.concatenate(parts, axis=0))
        total = part if total is None else total + part
    acc[...] += total

    @pl.when(c == pl.num_programs(1) - 1)
    def _():
        z = DN_ALPHA * xt_ref[...] + acc[...]
        zc = z - jnp.mean(z, 0, keepdims=True)
        var = jnp.mean(zc * zc, 0, keepdims=True)
        out_ref[...] = (zc * lax.rsqrt(var + LN_EPS) * g_ref[...] + b_ref[...]).T


def _peer_main(xt, r2, c1, a1, a2, u, vt, g, b, tm):
    d, n = xt.shape
    ec = PEER_I1_PER_STEP * PEER_NKEYS
    k2spec = pl.BlockSpec((PEER_HEADS, PEER_NKEYS, tm), lambda i, c: (0, 0, i))
    k1spec = pl.BlockSpec((PEER_HEADS, PEER_I1_PER_STEP, tm), lambda i, c: (0, c, i))
    return pl.pallas_call(
        _peer_body, grid=(n // tm, u.shape[0] // ec),
        in_specs=[pl.BlockSpec((d, tm), lambda i, c: (0, i)), k2spec, k1spec, k1spec, k2spec,
                  pl.BlockSpec((ec, d), lambda i, c: (c, 0)), pl.BlockSpec((d, ec), lambda i, c: (0, c)),
                  pl.BlockSpec((d, 1), lambda i, c: (0, 0)), pl.BlockSpec((d, 1), lambda i, c: (0, 0))],
        out_specs=pl.BlockSpec((tm, d), lambda i, c: (i, 0)),
        out_shape=jax.ShapeDtypeStruct((n, d), F32),
        scratch_shapes=[pltpu.VMEM((d, tm), F32), pltpu.VMEM((d, tm), BF16)],
        compiler_params=_cparams(("parallel", "arbitrary")), name="peer_main",
    )(xt, r2, c1, a1, a2, u, vt, g.reshape(d, 1), b.reshape(d, 1))


def _pad_cols(w, width):
    return jnp.pad(w, ((0, 0), (0, width - w.shape[1])))


def _lane_row(*parts):
    v = jnp.concatenate([p.astype(F32).reshape(-1) for p in parts])
    return jnp.pad(v, (0, LANE - v.shape[0])).reshape(1, LANE)


def _pick_tb(t_pad):
    for tb in (48, 32, 16):
        if t_pad % tb == 0:
            return tb
    raise ValueError(t_pad)


def kernel(x_prompt, x_sample, state_hgrn_S, state_gdn_S, state_gdn_conv, state_mlstm_C, state_mlstm_n, state_mlstm_m, state_gla_S, meta_tokens, hgrn_w_in, hgrn_lb, hgrn_norm_g, hgrn_w_out, gdn_w_in, gdn_conv_w, gdn_a_log, gdn_dt_bias, gdn_norm_g, gdn_w_out, mlstm_w_in, mlstm_b_i, mlstm_b_f, mlstm_norm_g, mlstm_w_out, gla_w_in, gla_w_gate, gla_b_gate, gla_norm_g, gla_w_out, peer_w_q, peer_keys1, peer_keys2, peer_u, peer_v, ln_g, ln_b):
    bp, seq, d = x_prompt.shape
    bs, seq_s, _ = x_sample.shape
    assert d == D_MODEL and len(state_hgrn_S) == 1 and len(state_gdn_S) == 1
    assert len(state_mlstm_C) == 1 and len(state_gla_S) == 1
    tp = N_META + seq
    tp_pad = -(-tp // C) * C
    ts_pad = -(-seq_s // C) * C
    tbp = _pick_tb(tp_pad)
    tbs = _pick_tb(ts_pad)
    np_rows = bp * tp_pad
    n_real = np_rows + bs * seq_s

    meta = jnp.broadcast_to(meta_tokens.astype(F32)[None], (bp, N_META, d))
    hp = jnp.concatenate([meta, x_prompt], axis=1)
    hp = jnp.pad(hp, ((0, 0), (0, tp_pad - tp), (0, 0))).reshape(np_rows, d)
    h = jnp.concatenate([hp, x_sample.reshape(bs * seq_s, d)], axis=0)
    n = -(-n_real // TOK_BLOCK) * TOK_BLOCK
    h = jnp.pad(h, ((0, n - n_real), (0, 0)))
    tm_mm = MM_TOK_BLOCK if n % MM_TOK_BLOCK == 0 else TOK_BLOCK

    def split_cols(w, sizes):
        out, o = [], 0
        for s in sizes:
            out.append(w[:, o:o + s])
            o += s
        return out

    def run_mixer(scan, proj, states_p, states_s, **kw):
        res_p = scan(proj, *states_p, nb=bp, nt=tp_pad // tbp, tb=tbp, t_valid=tp, **kw)
        ps = proj[np_rows:n_real].reshape(bs, seq_s, -1)
        ps = jnp.pad(ps, ((0, 0), (0, ts_pad - seq_s), (0, 0))).reshape(bs * ts_pad, -1)
        res_s = scan(ps, *states_s, nb=bs, nt=ts_pad // tbs, tb=tbs, t_valid=seq_s, **kw)
        o_s = res_s[0].reshape(bs, ts_pad, -1)[:, :seq_s].reshape(bs * seq_s, -1)
        o = jnp.concatenate([res_p[0], o_s], axis=0)
        o = jnp.pad(o, ((0, n - n_real), (0, 0)))
        return o, res_p[1:], res_s[1:]

    lb_all = jnp.cumsum(jax.nn.softmax(hgrn_lb.astype(F32), axis=0), axis=0)
    outs = {}
    for i in range(DEPTH):
        mix = i % 4
        if mix == 0:
            w_in = hgrn_w_in[0].astype(BF16)
            proj = _mm(h, w_in, tm_mm, 1024)
            nh, dk, dv = state_hgrn_S.shape[2:]
            zero = jnp.zeros((bp, nh, dk, dv), F32)
            scan = functools.partial(_gla_scan, ng=hgrn_norm_g[0].reshape(1, dv), p1=lb_all[i].reshape(1, nh * dk),
                                     p2=jnp.zeros((1, LANE), F32), hgrn=True)
            o, (sp,), (ss,) = run_mixer(scan, proj, (zero,), (state_hgrn_S[0],))
            outs['hgrn'] = (sp[None], ss[None])
            w_out = hgrn_w_out[0]
        elif mix == 1:
            nh, dk, dv = state_gdn_S.shape[2:]
            cwid = 2 * nh * dk + nh * dv
            qkv, a, b, g = split_cols(gdn_w_in[0], [cwid, nh, nh, nh * dv])
            w_in = jnp.concatenate([qkv, g, _pad_cols(jnp.concatenate([a, b], axis=1), LANE)], axis=1).astype(BF16)
            proj = _mm(h, w_in, tm_mm, w_in.shape[1] // 3)
            scan = functools.partial(_gdn_scan, cw=gdn_conv_w[0], ng=gdn_norm_g[0].reshape(1, dv),
                                     alog=_lane_row(gdn_a_log[0]), dtb=_lane_row(gdn_dt_bias[0]))
            o, (sp, cp), (ss, cs) = run_mixer(
                scan, proj,
                (jnp.zeros((bp, nh, dk, dv), F32), jnp.zeros((bp, CONV_W - 1, cwid), F32)),
                (state_gdn_S[0], state_gdn_conv[0]))
            outs['gdn'] = (sp[None], cp[None], ss[None], cs[None])
            w_out = gdn_w_out[0]
        elif mix == 2:
            nh, dk, dv = state_mlstm_C.shape[2:]
            q, k, v, ig, fg, og = split_cols(mlstm_w_in[0], [nh * dk, nh * dk, nh * dv, nh, nh, nh * dv])
            w_in = jnp.concatenate([q, k, v, og, _pad_cols(jnp.concatenate([ig, fg], axis=1), LANE)],
                                   axis=1).astype(BF16)
            proj = _mm(h, w_in, tm_mm, w_in.shape[1] // 3)
            zeros_nh = jnp.zeros((nh,), F32)
            scan = functools.partial(_mlstm_scan, ng=mlstm_norm_g[0].reshape(1, nh * dv),
                                     bi=_lane_row(mlstm_b_i[0]), bf=_lane_row(zeros_nh, mlstm_b_f[0]))
            o, (cp, npp, mp), (cs, ns, ms) = run_mixer(
                scan, proj,
                (jnp.zeros((bp, nh, dk, dv), F32), jnp.zeros((bp, nh, dk), F32), jnp.zeros((bp, nh), F32)),
                (state_mlstm_C[0], state_mlstm_n[0], state_mlstm_m[0]))
            outs['mlstm'] = (cp[None], npp[None], mp.reshape(1, bp, nh), cs[None], ns[None], ms.reshape(1, bs, nh))
            w_out = mlstm_w_out[0]
        else:
            nh, dk, dv = state_gla_S.shape[2:]
            rank = gla_w_gate.shape[1]
            q, k, v, g, r = split_cols(gla_w_in[0], [nh * dk, nh * dk, nh * dv, nh * dv, rank])
            w_in = jnp.concatenate([q, k, v, g, _pad_cols(r, LANE)], axis=1).astype(BF16)
            proj = _mm(h, w_in, tm_mm, w_in.shape[1] // 5)
            wg = jnp.pad(gla_w_gate[0].astype(F32), ((0, LANE - rank), (0, 0)))
            scan = functools.partial(_gla_scan, ng=gla_norm_g[0].reshape(1, dv), p1=wg,
                                     p2=gla_b_gate[0].reshape(1, nh * dk), hgrn=False)
            o, (sp,), (ss,) = run_mixer(scan, proj, (jnp.zeros((bp, nh, dk, dv), F32),), (state_gla_S[0],))
            outs['gla'] = (sp[None], ss[None])
            w_out = gla_w_out[0]

        h1t = _mm_res_ln_t(o, w_out.astype(BF16), h, ln_g[i, 0], ln_b[i, 0], TOK_BLOCK)
        r2, c1, a1, a2 = _peer_route(h1t, peer_w_q[i].T.astype(BF16), peer_keys1[i], peer_keys2[i], TOK_BLOCK)
        h = _peer_main(h1t, r2, c1, a1, a2, peer_u[i].astype(BF16), peer_v[i].T.astype(BF16), ln_g[i, 1],
                       ln_b[i, 1], TOK_BLOCK)

    y_prompt = h[:np_rows].reshape(bp, tp_pad, d)[:, N_META:tp]
    y_sample = h[np_rows:n_real].reshape(bs, seq_s, d)
    hg, gd, ml, gl = outs['hgrn'], outs['gdn'], outs['mlstm'], outs['gla']
    return (y_prompt, y_sample, hg[0], gd[0], gd[1], ml[0], ml[1], ml[2], gl[0],
            hg[1], gd[2], gd[3], ml[3], ml[4], ml[5], gl[1])
```

```python
import functools

import jax
import jax.numpy as jnp
from jax import lax
from jax.experimental import pallas as pl
from jax.experimental.pallas import tpu as pltpu

F32 = jnp.float32
BF16 = jnp.bfloat16
HI = lax.Precision.HIGHEST

D_MODEL = 1024
DEPTH = 4
N_META = 16
C = 16
CONV_W = 4
NORM_EPS = 1e-6
LN_EPS = 1e-5
DN_ALPHA = (2 * DEPTH) ** 0.25
GLA_GATE_NORM = 16.0
LANE = 128
PEER_HEADS = 8
PEER_NKEYS = 128
PEER_TOPK = 16
PEER_HALF = 128
PEER_I1_PER_STEP = 8
PEER_SUB_I1 = 2
TOK_BLOCK = 256
MM_TOK_BLOCK = 640
VMEM_LIMIT = 48 * 1024 * 1024


def _cparams(sem):
    return pltpu.CompilerParams(dimension_semantics=sem, vmem_limit_bytes=VMEM_LIMIT)


def _iota(shape, dim):
    return lax.broadcasted_iota(jnp.int32, shape, dim)


def _dot_nt(a, b, precision=None):
    return lax.dot_general(a, b, (((1,), (1,)), ((), ())), precision=precision, preferred_element_type=F32)


def _dot_tn(a, b, precision=None):
    return lax.dot_general(a, b, (((0,), (0,)), ((), ())), precision=precision, preferred_element_type=F32)


def _dot(a, b, precision=None):
    return jnp.dot(a, b, precision=precision, preferred_element_type=F32)


def _cumsum_rows(x):
    tri = (_iota((C, C), 0) >= _iota((C, C), 1)).astype(F32)
    return _dot(tri, x, HI)


def _rows_to_cols(x):
    eye = (_iota((LANE, LANE), 0) == _iota((LANE, LANE), 1)).astype(F32)
    return _dot_nt(eye, x, HI)


def _incl_mask():
    return _iota((C, C), 0) >= _iota((C, C), 1)


def _mm_body(x_ref, w_ref, o_ref):
    o_ref[...] = _dot(x_ref[...].astype(BF16), w_ref[...])


def _mm(x, w, tm, tn):
    m, k = x.shape
    n = w.shape[1]
    return pl.pallas_call(
        _mm_body, grid=(m // tm, n // tn),
        in_specs=[pl.BlockSpec((tm, k), lambda i, j: (i, 0)), pl.BlockSpec((k, tn), lambda i, j: (0, j))],
        out_specs=pl.BlockSpec((tm, tn), lambda i, j: (i, j)),
        out_shape=jax.ShapeDtypeStruct((m, n), F32),
        compiler_params=_cparams(("parallel", "parallel")), name="mm")(x, w)


def _mm_ln_body(o_ref, w_ref, h_ref, g_ref, b_ref, out_ref):
    y = _dot(o_ref[...].astype(BF16), w_ref[...])
    z = DN_ALPHA * h_ref[...] + y
    zc = z - jnp.mean(z, -1, keepdims=True)
    var = jnp.mean(zc * zc, -1, keepdims=True)
    out_ref[...] = (zc * lax.rsqrt(var + LN_EPS) * g_ref[...] + b_ref[...]).T


def _mm_res_ln_t(o, w, h, g, b, tm):
    n, k = o.shape
    d = w.shape[1]
    return pl.pallas_call(
        _mm_ln_body, grid=(n // tm,),
        in_specs=[pl.BlockSpec((tm, k), lambda i: (i, 0)), pl.BlockSpec((k, d), lambda i: (0, 0)),
                  pl.BlockSpec((tm, d), lambda i: (i, 0)), pl.BlockSpec((1, d), lambda i: (0, 0)),
                  pl.BlockSpec((1, d), lambda i: (0, 0))],
        out_specs=pl.BlockSpec((d, tm), lambda i: (0, i)),
        out_shape=jax.ShapeDtypeStruct((d, n), F32),
        compiler_params=_cparams(("parallel",)), name="mm_res_ln")(o, w, h, g.reshape(1, d), b.reshape(1, d))


def _valid_rows(t, r0, tb, t_valid, nt):
    if t_valid == nt * tb:
        return None
    return (t * tb + r0 + _iota((C, 1), 0)) < t_valid


def _gla_chunk(q, k, v, g, st):
    b = _cumsum_rows(g)
    o = _dot_nt(q * jnp.exp(b), st)
    row = _iota((C, 1), 0)
    for s in range(C):
        d = jnp.exp(jnp.where(row >= s, b - b[s:s + 1, :], -jnp.inf))
        col = jnp.sum(q * (k[s:s + 1, :] * d), axis=-1, keepdims=True)
        o = o + col * v[s:s + 1, :]
    b_end = b[C - 1:C, :]
    st_new = st * jnp.exp(b_end) + _dot_tn(v, k * jnp.exp(b_end - b))
    return o, st_new


def _gated_rmsnorm(o, ng, gate):
    return o * lax.rsqrt(jnp.mean(o * o, -1, keepdims=True) + NORM_EPS) * ng * jax.nn.silu(gate)


def _gla_body(proj_ref, s0_ref, ng_ref, p1_ref, p2_ref, o_ref, sout_ref, st_scr, *, hgrn, nh, dk, dv, tb, t_valid,
              nt):
    t = pl.program_id(1)

    @pl.when(t == 0)
    def _():
        for h in range(nh):
            st_scr[h] = s0_ref[0, h].T

    kw = nh * dk
    vw = nh * dv

    def chunk(ci, carry):
        r0 = pl.multiple_of(ci * C, C)
        rows = pl.ds(r0, C)
        valid = _valid_rows(t, r0, tb, t_valid, nt)
        if not hgrn:
            r = proj_ref[rows, 2 * kw + 2 * vw:2 * kw + 2 * vw + LANE]
            logf_all = jax.nn.log_sigmoid(_dot(r, p1_ref[...], HI) + p2_ref[...]) / GLA_GATE_NORM
        for h in range(nh):
            ks = slice(h * dk, (h + 1) * dk)
            if hgrn:
                q = jax.nn.silu(proj_ref[rows, ks])
                lb = p1_ref[:, ks]
                fg = lb + (1.0 - lb) * jax.nn.sigmoid(proj_ref[rows, kw + h * dk:kw + (h + 1) * dk])
                k = 1.0 - fg
                g = jnp.log(fg)
            else:
                q = proj_ref[rows, ks] * dk ** -0.5
                k = proj_ref[rows, kw + h * dk:kw + (h + 1) * dk]
                g = logf_all[:, ks]
            v = proj_ref[rows, 2 * kw + h * dv:2 * kw + (h + 1) * dv]
            gate = proj_ref[rows, 2 * kw + vw + h * dv:2 * kw + vw + (h + 1) * dv]
            if valid is not None:
                k = jnp.where(valid, k, 0.0)
                g = jnp.where(valid, g, 0.0)
            o, st_new = _gla_chunk(q, k, v, g, st_scr[h])
            st_scr[h] = st_new
            o_ref[rows, h * dv:(h + 1) * dv] = _gated_rmsnorm(o, ng_ref[...], gate)
        return carry

    lax.fori_loop(0, tb // C, chunk, 0)

    @pl.when(t == nt - 1)
    def _():
        for h in range(nh):
            sout_ref[0, h] = st_scr[h].T


def _gla_scan(proj, s0, ng, p1, p2, *, hgrn, nb, nt, tb, t_valid):
    _, nh, dk, dv = s0.shape
    wp = proj.shape[1]
    body = functools.partial(_gla_body, hgrn=hgrn, nh=nh, dk=dk, dv=dv, tb=tb, t_valid=t_valid, nt=nt)
    return pl.pallas_call(
        body, grid=(nb, nt),
        in_specs=[pl.BlockSpec((tb, wp), lambda b, t: (b * nt + t, 0)),
                  pl.BlockSpec((1, nh, dk, dv), lambda b, t: (b, 0, 0, 0)),
                  pl.BlockSpec(ng.shape, lambda b, t: (0, 0)),
                  pl.BlockSpec(p1.shape, lambda b, t: (0, 0)),
                  pl.BlockSpec(p2.shape, lambda b, t: (0, 0))],
        out_specs=[pl.BlockSpec((tb, nh * dv), lambda b, t: (b * nt + t, 0)),
                   pl.BlockSpec((1, nh, dk, dv), lambda b, t: (b, 0, 0, 0))],
        out_shape=[jax.ShapeDtypeStruct((nb * nt * tb, nh * dv), F32), jax.ShapeDtypeStruct(s0.shape, F32)],
        scratch_shapes=[pltpu.VMEM((nh, dv, dk), F32)],
        compiler_params=_cparams(("parallel", "arbitrary")), name="hgrn_scan" if hgrn else "gla_scan",
    )(proj, s0, ng, p1, p2)


def _gdn_body(proj_ref, s0_ref, conv0_ref, cw_ref, ng_ref, alog_ref, dtb_ref, o_ref, sout_ref, convout_ref,
              st_scr, cbuf, ybuf, *, nh, dk, dv, tb, t_valid, nt):
    t = pl.program_id(1)
    cwid = 2 * nh * dk + nh * dv
    pre = 8 - (CONV_W - 1)

    @pl.when(t == 0)
    def _():
        for h in range(nh):
            st_scr[:, h * dv:(h + 1) * dv] = s0_ref[0, h]
        cbuf[0:8, :] = jnp.zeros((8, cwid), F32)
        cbuf[pre:8, :] = conv0_ref[0]

    cbuf[8:8 + tb, :] = proj_ref[:, 0:cwid]
    y = cbuf[pre:pre + tb, :] * cw_ref[0:1, :]
    for j in range(1, CONV_W):
        y = y + cbuf[pre + j:pre + j + tb, :] * cw_ref[j:j + 1, :]
    ybuf[...] = jax.nn.silu(y)

    nr = nh * C
    ri = _iota((nr, 1), 0)
    cj = _iota((1, nr), 1)
    rhead = ri // C
    same = rhead == cj // C
    incl = same & (ri % C >= cj % C)
    strict = same & (ri % C > cj % C)
    eye = (ri == cj).astype(F32)

    def stack(ref, rows, base, width):
        return jnp.concatenate([ref[rows, base + h * width:base + (h + 1) * width] for h in range(nh)], axis=0)

    def own_block(x):
        return _tree(jnp.add, [jnp.where(rhead == h, x[:, h * dv:(h + 1) * dv], 0.0) for h in range(nh)])

    prep = []
    for ci in range(tb // C):
        rows = slice(ci * C, (ci + 1) * C)
        valid = _valid_rows(t, ci * C, tb, t_valid, nt)
        small = proj_ref[rows, cwid + nh * dv:cwid + nh * dv + LANE]
        g_all = -jnp.exp(alog_ref[...]) * jax.nn.softplus(small + dtb_ref[...])
        beta_all = jax.nn.sigmoid(small)
        if valid is not None:
            g_all = jnp.where(valid, g_all, 0.0)
            beta_all = jnp.where(valid, beta_all, 0.0)
        b_all = _cumsum_rows(g_all)
        b_t = _rows_to_cols(b_all)
        bcol = jnp.concatenate([b_all[:, h:h + 1] for h in range(nh)], axis=0)
        brow = jnp.concatenate([b_t[h:h + 1, :] for h in range(nh)], axis=1)
        beta = jnp.concatenate([beta_all[:, nh + h:nh + h + 1] for h in range(nh)], axis=0)
        bend = jnp.concatenate([jnp.broadcast_to(b_all[C - 1:C, h:h + 1], (C, 1)) for h in range(nh)], axis=0)
        dec_row = jnp.concatenate([jnp.broadcast_to(jnp.exp(b_all[C - 1:C, h:h + 1]), (1, dv)) for h in range(nh)],
                                  axis=1)
        q = stack(ybuf, rows, 0, dk)
        k = stack(ybuf, rows, nh * dk, dk)
        v = stack(ybuf, rows, 2 * nh * dk, dv)
        q = q * lax.rsqrt(jnp.sum(q * q, -1, keepdims=True) + NORM_EPS) * dk ** -0.5
        k = k * lax.rsqrt(jnp.sum(k * k, -1, keepdims=True) + NORM_EPS)
        if valid is not None:
            valid_st = jnp.concatenate([valid] * nh, axis=0)
            q = jnp.where(valid_st, q, 0.0)
            k = jnp.where(valid_st, k, 0.0)
            v = jnp.where(valid_st, v, 0.0)
        decay = jnp.exp(jnp.where(incl, bcol - brow, -jnp.inf))
        a = jnp.where(strict, _dot_nt(k, k, HI) * decay * beta, 0.0)
        a2 = _dot(a, a, HI)
        a4 = _dot(a2, a2, HI)
        a8 = _dot(a4, a4, HI)
        tinv = _dot(_dot(eye - a, eye + a2, HI), _dot(eye + a4, eye + a8, HI), HI)
        eb = jnp.exp(bcol)
        prep.append(dict(u=_dot(tinv, beta * v, HI), wk=_dot(tinv, (beta * eb) * k, HI),
                         att=_dot_nt(q, k) * decay, qe=q * eb, kd=k * jnp.exp(bend - bcol), dec_row=dec_row))

    s_cat = st_scr[...]
    for ci, p in enumerate(prep):
        rows = slice(ci * C, (ci + 1) * C)
        v_new = p['u'] - own_block(_dot(p['wk'], s_cat))
        o = _dot(p['att'], v_new) + own_block(_dot(p['qe'], s_cat))
        v_wide = jnp.concatenate([jnp.where(rhead == h, v_new, 0.0) for h in range(nh)], axis=1)
        s_cat = s_cat * p['dec_row'] + _dot_tn(p['kd'], v_wide)
        o = o * lax.rsqrt(jnp.mean(o * o, -1, keepdims=True) + NORM_EPS) * ng_ref[...]
        for h in range(nh):
            gate = proj_ref[rows, cwid + h * dv:cwid + (h + 1) * dv]
            o_ref[rows, h * dv:(h + 1) * dv] = o[h * C:(h + 1) * C, :] * jax.nn.silu(gate)
    st_scr[...] = s_cat

    @pl.when(t < nt - 1)
    def _():
        cbuf[pre:8, :] = cbuf[pre + tb:8 + tb, :]

    @pl.when(t == nt - 1)
    def _():
        for h in range(nh):
            sout_ref[0, h] = st_scr[:, h * dv:(h + 1) * dv]
        last = t_valid - (nt - 1) * tb
        convout_ref[0] = cbuf[pre + last:8 + last, :]


def _gdn_scan(proj, s0, conv0, cw, ng, alog, dtb, *, nb, nt, tb, t_valid):
    _, nh, dk, dv = s0.shape
    wp = proj.shape[1]
    cwid = 2 * nh * dk + nh * dv
    body = functools.partial(_gdn_body, nh=nh, dk=dk, dv=dv, tb=tb, t_valid=t_valid, nt=nt)
    const = lambda b, t: (0, 0)
    return pl.pallas_call(
        body, grid=(nb, nt),
        in_specs=[pl.BlockSpec((tb, wp), lambda b, t: (b * nt + t, 0)),
                  pl.BlockSpec((1, nh, dk, dv), lambda b, t: (b, 0, 0, 0)),
                  pl.BlockSpec((1, CONV_W - 1, cwid), lambda b, t: (b, 0, 0)),
                  pl.BlockSpec(cw.shape, const), pl.BlockSpec(ng.shape, const),
                  pl.BlockSpec(alog.shape, const), pl.BlockSpec(dtb.shape, const)],
        out_specs=[pl.BlockSpec((tb, nh * dv), lambda b, t: (b * nt + t, 0)),
                   pl.BlockSpec((1, nh, dk, dv), lambda b, t: (b, 0, 0, 0)),
                   pl.BlockSpec((1, CONV_W - 1, cwid), lambda b, t: (b, 0, 0))],
        out_shape=[jax.ShapeDtypeStruct((nb * nt * tb, nh * dv), F32), jax.ShapeDtypeStruct(s0.shape, F32),
                   jax.ShapeDtypeStruct(conv0.shape, F32)],
        scratch_shapes=[pltpu.VMEM((dk, nh * dv), F32), pltpu.VMEM((tb + 8, cwid), F32), pltpu.VMEM((tb, cwid), F32)],
        compiler_params=_cparams(("parallel", "arbitrary")), name="gdn_scan",
    )(proj, s0, conv0, cw, ng, alog, dtb)


def _mlstm_body(proj_ref, c0_ref, n0_ref, m0_ref, ng_ref, bi_ref, bf_ref, o_ref, cout_ref, nout_ref, mout_ref,
                ct_scr, n_scr, m_scr, *, nh, dk, dv, tb, t_valid, nt):
    t = pl.program_id(1)

    @pl.when(t == 0)
    def _():
        for h in range(nh):
            ct_scr[h] = c0_ref[0, h].T
            n_scr[h] = n0_ref[0, h:h + 1, :]
            m_scr[h] = jnp.broadcast_to(m0_ref[0, :, h:h + 1], (1, LANE))

    incl = _incl_mask()
    base = 2 * nh * dk + 2 * nh * dv

    def chunk(ci, carry):
        r0 = pl.multiple_of(ci * C, C)
        rows = pl.ds(r0, C)
        valid = _valid_rows(t, r0, tb, t_valid, nt)
        small = proj_ref[rows, base:base + LANE]
        li_all = small + bi_ref[...]
        lf_all = jax.nn.log_sigmoid(small + bf_ref[...])
        if valid is not None:
            lf_all = jnp.where(valid, lf_all, 0.0)
        b_all = _cumsum_rows(lf_all)
        b_t = _rows_to_cols(b_all)
        li_t = _rows_to_cols(li_all)
        if valid is not None:
            valid_row = (t * tb + r0 + _iota((1, C), 1)) < t_valid
        for h in range(nh):
            q = proj_ref[rows, h * dk:(h + 1) * dk]
            k = proj_ref[rows, nh * dk + h * dk:nh * dk + (h + 1) * dk] * dk ** -0.5
            v = proj_ref[rows, 2 * nh * dk + h * dv:2 * nh * dk + (h + 1) * dv]
            bcol = b_all[:, nh + h:nh + h + 1]
            licol = li_all[:, h:h + 1]
            xrow = b_t[nh + h:nh + h + 1, :] - li_t[h:h + 1, :]
            if valid is not None:
                xrow = jnp.where(valid_row, xrow, jnp.inf)
                licol = jnp.where(valid, licol, -jnp.inf)
            m_prev = m_scr[h][:, 0:1]
            dmat = jnp.where(incl, bcol - xrow, -jnp.inf)
            m_inter = bcol + m_prev
            m_t = jnp.maximum(m_inter, jnp.max(dmat, axis=1, keepdims=True))
            wts = jnp.exp(dmat - m_t)
            a_int = jnp.exp(m_inter - m_t)
            qk = _dot_nt(q, k) * wts
            ct = ct_scr[h]
            n_row = n_scr[h]
            num = _dot(qk, v) + a_int * _dot_nt(q, ct)
            den = jnp.sum(qk, axis=1, keepdims=True) + a_int * jnp.sum(q * n_row, axis=1, keepdims=True)
            hid = num / jnp.maximum(jnp.abs(den), jnp.exp(-m_t))
            b_end = bcol[C - 1:C, :]
            m_end = m_t[C - 1:C, :]
            a_end = a_int[C - 1:C, :]
            wk = jnp.exp(b_end - bcol + licol - m_end) * k
            ct_scr[h] = a_end * ct + _dot_tn(v, wk)
            n_scr[h] = a_end * n_row + jnp.sum(wk, axis=0, keepdims=True)
            m_scr[h] = jnp.broadcast_to(m_end, (1, LANE))
            hc = hid - jnp.mean(hid, -1, keepdims=True)
            hn = hc * lax.rsqrt(jnp.mean(hc * hc, -1, keepdims=True) + NORM_EPS) * ng_ref[:, h * dv:(h + 1) * dv]
            og = jax.nn.sigmoid(proj_ref[rows, 2 * nh * dk + nh * dv + h * dv:2 * nh * dk + nh * dv + (h + 1) * dv])
            o_ref[rows, h * dv:(h + 1) * dv] = og * hn
        return carry

    lax.fori_loop(0, tb // C, chunk, 0)

    @pl.when(t == nt - 1)
    def _():
        lane = _iota((1, nh), 1)
        mrow = jnp.zeros((1, nh), F32)
        for h in range(nh):
            cout_ref[0, h] = ct_scr[h].T
            nout_ref[0, h:h + 1, :] = n_scr[h]
            mrow = jnp.where(lane == h, m_scr[h][:, 0:1], mrow)
        mout_ref[0] = mrow


def _mlstm_scan(proj, c0, n0, m0, ng, bi, bf, *, nb, nt, tb, t_valid):
    _, nh, dk, dv = c0.shape
    wp = proj.shape[1]
    body = functools.partial(_mlstm_body, nh=nh, dk=dk, dv=dv, tb=tb, t_valid=t_valid, nt=nt)
    const = lambda b, t: (0, 0)
    m0 = m0.reshape(nb, 1, nh)
    return pl.pallas_call(
        body, grid=(nb, nt),
        in_specs=[pl.BlockSpec((tb, wp), lambda b, t: (b * nt + t, 0)),
                  pl.BlockSpec((1, nh, dk, dv), lambda b, t: (b, 0, 0, 0)),
                  pl.BlockSpec((1, nh, dk), lambda b, t: (b, 0, 0)),
                  pl.BlockSpec((1, 1, nh), lambda b, t: (b, 0, 0)),
                  pl.BlockSpec(ng.shape, const), pl.BlockSpec(bi.shape, const), pl.BlockSpec(bf.shape, const)],
        out_specs=[pl.BlockSpec((tb, nh * dv), lambda b, t: (b * nt + t, 0)),
                   pl.BlockSpec((1, nh, dk, dv), lambda b, t: (b, 0, 0, 0)),
                   pl.BlockSpec((1, nh, dk), lambda b, t: (b, 0, 0)),
                   pl.BlockSpec((1, 1, nh), lambda b, t: (b, 0, 0))],
        out_shape=[jax.ShapeDtypeStruct((nb * nt * tb, nh * dv), F32), jax.ShapeDtypeStruct(c0.shape, F32),
                   jax.ShapeDtypeStruct(n0.shape, F32), jax.ShapeDtypeStruct((nb, 1, nh), F32)],
        scratch_shapes=[pltpu.VMEM((nh, dv, dk), F32), pltpu.VMEM((nh, 1, dk), F32), pltpu.VMEM((nh, 1, LANE), F32)],
        compiler_params=_cparams(("parallel", "arbitrary")), name="mlstm_scan",
    )(proj, c0, n0, m0, ng, bi, bf)


SUB = 8


def _tree(op, xs):
    xs = list(xs)
    while len(xs) > 1:
        xs = [op(xs[i], xs[i + 1]) if i + 1 < len(xs) else xs[i] for i in range(0, len(xs), 2)]
    return xs[0]


def _all_sublanes(op, m):
    for shift in (4, 2, 1):
        m = op(m, pltpu.roll(m, shift, axis=0))
    return m


def _top16(tiles, want_rank):
    tiles = list(tiles)
    ranks = [jnp.full((SUB, LANE), float(PEER_TOPK), F32) for _ in tiles] if want_rank else None
    vals = []
    for j in range(PEER_TOPK):
        m = _all_sublanes(jnp.maximum, _tree(jnp.maximum, tiles))
        vals.append(m)
        for i in range(len(tiles)):
            hit = tiles[i] == m
            if want_rank:
                ranks[i] = jnp.where(hit, float(j), ranks[i])
            tiles[i] = jnp.where(hit, -jnp.inf, tiles[i])
    return vals, ranks


def _route_body(xt_ref, wq_ref, k1_ref, k2_ref, r2_ref, c1_ref, a1_ref, a2_ref, q_scr, *, tm):
    q_scr[...] = _dot(wq_ref[...], xt_ref[...].astype(BF16))
    nsub = tm // LANE
    ntile = PEER_NKEYS // SUB
    sub = _iota((SUB, LANE), 0)

    def unit(i, carry):
        h = i // nsub
        lanes = pl.ds(pl.multiple_of((i % nsub) * LANE, LANE), LANE)
        q1 = q_scr[pl.ds(pl.multiple_of(h * 2 * PEER_HALF, PEER_HALF), PEER_HALF), lanes]
        q2 = q_scr[pl.ds(pl.multiple_of(h * 2 * PEER_HALF + PEER_HALF, PEER_HALF), PEER_HALF), lanes]
        s1 = _dot(k1_ref[...], q1, HI)
        s2 = _dot(k2_ref[...], q2, HI)
        t1 = [s1[k * SUB:(k + 1) * SUB, :] for k in range(ntile)]
        t2 = [s2[k * SUB:(k + 1) * SUB, :] for k in range(ntile)]
        v1, _ = _top16(t1, False)
        v2, rank2 = _top16(t2, True)
        v2lo, v2hi = v2[SUB - 1], v2[2 * SUB - 1]
        for b in range(SUB - 2, -1, -1):
            v2lo = jnp.where(sub == b, v2[b], v2lo)
            v2hi = jnp.where(sub == b, v2[SUB + b], v2hi)
        cands = []
        for a in range(PEER_TOPK):
            nb_ok = PEER_TOPK // (a + 1)
            cands.append(jnp.where(sub < min(nb_ok, SUB), v1[a] + v2lo, -jnp.inf))
            if nb_ok > SUB:
                cands.append(jnp.where(sub < nb_ok - SUB, v1[a] + v2hi, -jnp.inf))
        work = list(cands)
        tau = None
        for _ in range(PEER_TOPK):
            tau = _all_sublanes(jnp.maximum, _tree(jnp.maximum, work))
            work = [jnp.where(w == tau, -jnp.inf, w) for w in work]
        top = v1[0] + v2[0]
        z = _all_sublanes(jnp.add, _tree(jnp.add, [jnp.where(cd >= tau, jnp.exp(cd - top), 0.0) for cd in cands]))
        inv_z = 1.0 / z
        c1 = []
        for k in range(ntile):
            c1.append(_tree(jnp.add, [jnp.where(t1[k] + v2[b] >= tau, 1.0, 0.0) for b in range(PEER_TOPK)]))
        r2_ref[h, :, lanes] = jnp.concatenate(rank2, axis=0).astype(BF16)
        c1_ref[h, :, lanes] = jnp.concatenate(c1, axis=0)
        a1_ref[h, :, lanes] = jnp.concatenate([jnp.exp(t - v1[0]) for t in t1], axis=0)
        a2_ref[h, :, lanes] = jnp.concatenate([jnp.exp(t - v2[0]) * inv_z for t in t2], axis=0).astype(BF16)
        return carry

    lax.fori_loop(0, PEER_HEADS * nsub, unit, 0)


def _peer_route(xt, wq_t, k1, k2, tm):
    d, n = xt.shape
    ospec = pl.BlockSpec((PEER_HEADS, PEER_NKEYS, tm), lambda i: (0, 0, i))
    shp = lambda dt: jax.ShapeDtypeStruct((PEER_HEADS, PEER_NKEYS, n), dt)
    return pl.pallas_call(
        functools.partial(_route_body, tm=tm), grid=(n // tm,),
        in_specs=[pl.BlockSpec((d, tm), lambda i: (0, i)), pl.BlockSpec(wq_t.shape, lambda i: (0, 0)),
                  pl.BlockSpec(k1.shape, lambda i: (0, 0)), pl.BlockSpec(k2.shape, lambda i: (0, 0))],
        out_specs=[ospec] * 4, out_shape=[shp(BF16), shp(F32), shp(F32), shp(BF16)],
        scratch_shapes=[pltpu.VMEM((wq_t.shape[0], tm), F32)],
        compiler_params=_cparams(("parallel",)), name="peer_route")(xt, wq_t, k1, k2)


def _peer_body(xt_ref, r2_ref, c1_ref, a1_ref, a2_ref, u_ref, vt_ref, g_ref, b_ref, out_ref, acc, xbf):
    c = pl.program_id(1)

    @pl.when(c == 0)
    def _():
        xbf[...] = xt_ref[...].astype(BF16)
        acc[...] = jnp.zeros(acc.shape, F32)

    tm = xbf.shape[1]
    rows16 = 2 * SUB
    sub_w = PEER_SUB_I1 * PEER_NKEYS
    total = None
    for s in range(PEER_I1_PER_STEP // PEER_SUB_I1):
        hid = _dot(u_ref[s * sub_w:(s + 1) * sub_w, :], xbf[...])
        act = (0.5 * hid * (1.0 + lax.erf(hid * 0.5 ** 0.5))).astype(BF16)
        parts = []
        for jj in range(PEER_SUB_I1):
            j = s * PEER_SUB_I1 + jj
            c1b = [jnp.broadcast_to(c1_ref[h, j:j + 1, :], (rows16, tm)).astype(BF16) for h in range(PEER_HEADS)]
            a1b = [jnp.broadcast_to(a1_ref[h, j:j + 1, :], (rows16, tm)).astype(BF16) for h in range(PEER_HEADS)]
            for kb in range(PEER_NKEYS // rows16):
                ks = slice(kb * rows16, (kb + 1) * rows16)
                w = _tree(jnp.add, [jnp.where(r2_ref[h, ks, :] < c1b[h], a2_ref[h, ks, :] * a1b[h], 0.0)
                                    for h in range(PEER_HEADS)])
                es = slice(jj * PEER_NKEYS + kb * rows16, jj * PEER_NKEYS + (kb + 1) * rows16)
                parts.append(w * act[es, :])
        part = _dot(vt_ref[:, s * sub_w:(s + 1) * sub_w], jnp.concatenate(parts, axis=0))
        total = part if total is None else total + part
    acc[...] += total

    @pl.when(c == pl.num_programs(1) - 1)
    def _():
        z = DN_ALPHA * xt_ref[...] + acc[...]
        zc = z - jnp.mean(z, 0, keepdims=True)
        var = jnp.mean(zc * zc, 0, keepdims=True)
        out_ref[...] = (zc * lax.rsqrt(var + LN_EPS) * g_ref[...] + b_ref[...]).T


def _peer_main(xt, r2, c1, a1, a2, u, vt, g, b, tm):
    d, n = xt.shape
    ec = PEER_I1_PER_STEP * PEER_NKEYS
    k2spec = pl.BlockSpec((PEER_HEADS, PEER_NKEYS, tm), lambda i, c: (0, 0, i))
    k1spec = pl.BlockSpec((PEER_HEADS, PEER_I1_PER_STEP, tm), lambda i, c: (0, c, i))
    return pl.pallas_call(
        _peer_body, grid=(n // tm, u.shape[0] // ec),
        in_specs=[pl.BlockSpec((d, tm), lambda i, c: (0, i)), k2spec, k1spec, k1spec, k2spec,
                  pl.BlockSpec((ec, d), lambda i, c: (c, 0)), pl.BlockSpec((d, ec), lambda i, c: (0, c)),
                  pl.BlockSpec((d, 1), lambda i, c: (0, 0)), pl.BlockSpec((d, 1), lambda i, c: (0, 0))],
        out_specs=pl.BlockSpec((tm, d), lambda i, c: (i, 0)),
        out_shape=jax.ShapeDtypeStruct((n, d), F32),
        scratch_shapes=[pltpu.VMEM((d, tm), F32), pltpu.VMEM((d, tm), BF16)],
        compiler_params=_cparams(("parallel", "arbitrary")), name="peer_main",
    )(xt, r2, c1, a1, a2, u, vt, g.reshape(d, 1), b.reshape(d, 1))


def _pad_cols(w, width):
    return jnp.pad(w, ((0, 0), (0, width - w.shape[1])))


def _lane_row(*parts):
    v = jnp.concatenate([p.astype(F32).reshape(-1) for p in parts])
    return jnp.pad(v, (0, LANE - v.shape[0])).reshape(1, LANE)


def _pick_tb(t_pad):
    for tb in (48, 32, 16):
        if t_pad % tb == 0:
            return tb
    raise ValueError(t_pad)


def kernel(x_prompt, x_sample, state_hgrn_S, state_gdn_S, state_gdn_conv, state_mlstm_C, state_mlstm_n, state_mlstm_m, state_gla_S, meta_tokens, hgrn_w_in, hgrn_lb, hgrn_norm_g, hgrn_w_out, gdn_w_in, gdn_conv_w, gdn_a_log, gdn_dt_bias, gdn_norm_g, gdn_w_out, mlstm_w_in, mlstm_b_i, mlstm_b_f, mlstm_norm_g, mlstm_w_out, gla_w_in, gla_w_gate, gla_b_gate, gla_norm_g, gla_w_out, peer_w_q, peer_keys1, peer_keys2, peer_u, peer_v, ln_g, ln_b):
    bp, seq, d = x_prompt.shape
    bs, seq_s, _ = x_sample.shape
    assert d == D_MODEL and len(state_hgrn_S) == 1 and len(state_gdn_S) == 1
    assert len(state_mlstm_C) == 1 and len(state_gla_S) == 1
    tp = N_META + seq
    tp_pad = -(-tp // C) * C
    ts_pad = -(-seq_s // C) * C
    tbp = _pick_tb(tp_pad)
    tbs = _pick_tb(ts_pad)
    np_rows = bp * tp_pad
    n_real = np_rows + bs * seq_s

    meta = jnp.broadcast_to(meta_tokens.astype(F32)[None], (bp, N_META, d))
    hp = jnp.concatenate([meta, x_prompt], axis=1)
    hp = jnp.pad(hp, ((0, 0), (0, tp_pad - tp), (0, 0))).reshape(np_rows, d)
    h = jnp.concatenate([hp, x_sample.reshape(bs * seq_s, d)], axis=0)
    n = -(-n_real // TOK_BLOCK) * TOK_BLOCK
    h = jnp.pad(h, ((0, n - n_real), (0, 0)))
    tm_mm = MM_TOK_BLOCK if n % MM_TOK_BLOCK == 0 else TOK_BLOCK

    def split_cols(w, sizes):
        out, o = [], 0
        for s in sizes:
            out.append(w[:, o:o + s])
            o += s
        return out

    def run_mixer(scan, proj, states_p, states_s, **kw):
        res_p = scan(proj, *states_p, nb=bp, nt=tp_pad // tbp, tb=tbp, t_valid=tp, **kw)
        ps = proj[np_rows:n_real].reshape(bs, seq_s, -1)
        ps = jnp.pad(ps, ((0, 0), (0, ts_pad - seq_s), (0, 0))).reshape(bs * ts_pad, -1)
        res_s = scan(ps, *states_s, nb=bs, nt=ts_pad // tbs, tb=tbs, t_valid=seq_s, **kw)
        o_s = res_s[0].reshape(bs, ts_pad, -1)[:, :seq_s].reshape(bs * seq_s, -1)
        o = jnp.concatenate([res_p[0], o_s], axis=0)
        o = jnp.pad(o, ((0, n - n_real), (0, 0)))
        return o, res_p[1:], res_s[1:]

    lb_all = jnp.cumsum(jax.nn.softmax(hgrn_lb.astype(F32), axis=0), axis=0)
    outs = {}
    for i in range(DEPTH):
        mix = i % 4
        if mix == 0:
            w_in = hgrn_w_in[0].astype(BF16)
            proj = _mm(h, w_in, tm_mm, 1024)
            nh, dk, dv = state_hgrn_S.shape[2:]
            zero = jnp.zeros((bp, nh, dk, dv), F32)
            scan = functools.partial(_gla_scan, ng=hgrn_norm_g[0].reshape(1, dv), p1=lb_all[i].reshape(1, nh * dk),
                                     p2=jnp.zeros((1, LANE), F32), hgrn=True)
            o, (sp,), (ss,) = run_mixer(scan, proj, (zero,), (state_hgrn_S[0],))
            outs['hgrn'] = (sp[None], ss[None])
            w_out = hgrn_w_out[0]
        elif mix == 1:
            nh, dk, dv = state_gdn_S.shape[2:]
            cwid = 2 * nh * dk + nh * dv
            qkv, a, b, g = split_cols(gdn_w_in[0], [cwid, nh, nh, nh * dv])
            w_in = jnp.concatenate([qkv, g, _pad_cols(jnp.concatenate([a, b], axis=1), LANE)], axis=1).astype(BF16)
            proj = _mm(h, w_in, tm_mm, w_in.shape[1] // 3)
            scan = functools.partial(_gdn_scan, cw=gdn_conv_w[0], ng=gdn_norm_g[0].reshape(1, dv),
                                     alog=_lane_row(gdn_a_log[0]), dtb=_lane_row(gdn_dt_bias[0]))
            o, (sp, cp), (ss, cs) = run_mixer(
                scan, proj,
                (jnp.zeros((bp, nh, dk, dv), F32), jnp.zeros((bp, CONV_W - 1, cwid), F32)),
                (state_gdn_S[0], state_gdn_conv[0]))
            outs['gdn'] = (sp[None], cp[None], ss[None], cs[None])
            w_out = gdn_w_out[0]
        elif mix == 2:
            nh, dk, dv = state_mlstm_C.shape[2:]
            q, k, v, ig, fg, og = split_cols(mlstm_w_in[0], [nh * dk, nh * dk, nh * dv, nh, nh, nh * dv])
            w_in = jnp.concatenate([q, k, v, og, _pad_cols(jnp.concatenate([ig, fg], axis=1), LANE)],
                                   axis=1).astype(BF16)
            proj = _mm(h, w_in, tm_mm, w_in.shape[1] // 3)
            zeros_nh = jnp.zeros((nh,), F32)
            scan = functools.partial(_mlstm_scan, ng=mlstm_norm_g[0].reshape(1, nh * dv),
                                     bi=_lane_row(mlstm_b_i[0]), bf=_lane_row(zeros_nh, mlstm_b_f[0]))
            o, (cp, npp, mp), (cs, ns, ms) = run_mixer(
                scan, proj,
                (jnp.zeros((bp, nh, dk, dv), F32), jnp.zeros((bp, nh, dk), F32), jnp.zeros((bp, nh), F32)),
                (state_mlstm_C[0], state_mlstm_n[0], state_mlstm_m[0]))
            outs['mlstm'] = (cp[None], npp[None], mp.reshape(1, bp, nh), cs[None], ns[None], ms.reshape(1, bs, nh))
            w_out = mlstm_w_out[0]
        else:
            nh, dk, dv = state_gla_S.shape[2:]
            rank = gla_w_gate.shape[1]
            q, k, v, g, r = split_cols(gla_w_in[0], [nh * dk, nh * dk, nh * dv, nh * dv, rank])
            w_in = jnp.concatenate([q, k, v, g, _pad_cols(r, LANE)], axis=1).astype(BF16)
            proj = _mm(h, w_in, tm_mm, w_in.shape[1] // 5)
            wg = jnp.pad(gla_w_gate[0].astype(F32), ((0, LANE - rank), (0, 0)))
            scan = functools.partial(_gla_scan, ng=gla_norm_g[0].reshape(1, dv), p1=wg,
                                     p2=gla_b_gate[0].reshape(1, nh * dk), hgrn=False)
            o, (sp,), (ss,) = run_mixer(scan, proj, (jnp.zeros((bp, nh, dk, dv), F32),), (state_gla_S[0],))
            outs['gla'] = (sp[None], ss[None])
            w_out = gla_w_out[0]

        h1t = _mm_res_ln_t(o, w_out.astype(BF16), h, ln_g[i, 0], ln_b[i, 0], TOK_BLOCK)
        r2, c1, a1, a2 = _peer_route(h1t, peer_w_q[i].T.astype(BF16), peer_keys1[i], peer_keys2[i], TOK_BLOCK)
        h = _peer_main(h1t, r2, c1, a1, a2, peer_u[i].astype(BF16), peer_v[i].T.astype(BF16), ln_g[i, 1],
                       ln_b[i, 1], TOK_BLOCK)

    y_prompt = h[:np_rows].reshape(bp, tp_pad, d)[:, N_META:tp]
    y_sample = h[np_rows:n_real].reshape(bs, seq_s, d)
    hg, gd, ml, gl = outs['hgrn'], outs['gdn'], outs['mlstm'], outs['gla']
    return (y_prompt, y_sample, hg[0], gd[0], gd[1], ml[0], ml[1], ml[2], gl[0],
            hg[1], gd[2], gd[3], ml[3], ml[4], ml[5], gl[1])
```

```python
import functools

import jax
import jax.numpy as jnp
from jax import lax
from jax.experimental import pallas as pl
from jax.experimental.pallas import tpu as pltpu

F32 = jnp.float32
BF16 = jnp.bfloat16
HI = lax.Precision.HIGHEST

D_MODEL = 1024
DEPTH = 4
N_META = 16
C = 16
CONV_W = 4
NORM_EPS = 1e-6
LN_EPS = 1e-5
DN_ALPHA = (2 * DEPTH) ** 0.25
GLA_GATE_NORM = 16.0
LANE = 128
PEER_HEADS = 8
PEER_NKEYS = 128
PEER_TOPK = 16
PEER_HALF = 128
PEER_I1_PER_STEP = 8
PEER_SUB_I1 = 2
TOK_BLOCK = 256
MM_TOK_BLOCK = 640
VMEM_LIMIT = 48 * 1024 * 1024


def _cparams(sem):
    return pltpu.CompilerParams(dimension_semantics=sem, vmem_limit_bytes=VMEM_LIMIT)


def _iota(shape, dim):
    return lax.broadcasted_iota(jnp.int32, shape, dim)


def _dot_nt(a, b, precision=None):
    return lax.dot_general(a, b, (((1,), (1,)), ((), ())), precision=precision, preferred_element_type=F32)


def _dot_tn(a, b, precision=None):
    return lax.dot_general(a, b, (((0,), (0,)), ((), ())), precision=precision, preferred_element_type=F32)


def _dot(a, b, precision=None):
    return jnp.dot(a, b, precision=precision, preferred_element_type=F32)


def _cumsum_rows(x):
    tri = (_iota((C, C), 0) >= _iota((C, C), 1)).astype(F32)
    return _dot(tri, x, HI)


def _rows_to_cols(x):
    eye = (_iota((LANE, LANE), 0) == _iota((LANE, LANE), 1)).astype(F32)
    return _dot_nt(eye, x, HI)


def _incl_mask():
    return _iota((C, C), 0) >= _iota((C, C), 1)


def _mm_body(x_ref, w_ref, o_ref):
    o_ref[...] = _dot(x_ref[...].astype(BF16), w_ref[...])


def _mm(x, w, tm, tn):
    m, k = x.shape
    n = w.shape[1]
    return pl.pallas_call(
        _mm_body, grid=(m // tm, n // tn),
        in_specs=[pl.BlockSpec((tm, k), lambda i, j: (i, 0)), pl.BlockSpec((k, tn), lambda i, j: (0, j))],
        out_specs=pl.BlockSpec((tm, tn), lambda i, j: (i, j)),
        out_shape=jax.ShapeDtypeStruct((m, n), F32),
        compiler_params=_cparams(("parallel", "parallel")), name="mm")(x, w)


def _mm_ln_body(o_ref, w_ref, h_ref, g_ref, b_ref, out_ref):
    y = _dot(o_ref[...].astype(BF16), w_ref[...])
    z = DN_ALPHA * h_ref[...] + y
    zc = z - jnp.mean(z, -1, keepdims=True)
    var = jnp.mean(zc * zc, -1, keepdims=True)
    out_ref[...] = (zc * lax.rsqrt(var + LN_EPS) * g_ref[...] + b_ref[...]).T


def _mm_res_ln_t(o, w, h, g, b, tm):
    n, k = o.shape
    d = w.shape[1]
    return pl.pallas_call(
        _mm_ln_body, grid=(n // tm,),
        in_specs=[pl.BlockSpec((tm, k), lambda i: (i, 0)), pl.BlockSpec((k, d), lambda i: (0, 0)),
                  pl.BlockSpec((tm, d), lambda i: (i, 0)), pl.BlockSpec((1, d), lambda i: (0, 0)),
                  pl.BlockSpec((1, d), lambda i: (0, 0))],
        out_specs=pl.BlockSpec((d, tm), lambda i: (0, i)),
        out_shape=jax.ShapeDtypeStruct((d, n), F32),
        compiler_params=_cparams(("parallel",)), name="mm_res_ln")(o, w, h, g.reshape(1, d), b.reshape(1, d))


def _valid_rows(t, r0, tb, t_valid, nt):
    if t_valid == nt * tb:
        return None
    return (t * tb + r0 + _iota((C, 1), 0)) < t_valid


def _gla_chunk(q, k, v, g, st):
    b = _cumsum_rows(g)
    o = _dot_nt(q * jnp.exp(b), st)
    row = _iota((C, 1), 0)
    for s in range(C):
        d = jnp.exp(jnp.where(row >= s, b - b[s:s + 1, :], -jnp.inf))
        col = jnp.sum(q * (k[s:s + 1, :] * d), axis=-1, keepdims=True)
        o = o + col * v[s:s + 1, :]
    b_end = b[C - 1:C, :]
    st_new = st * jnp.exp(b_end) + _dot_tn(v, k * jnp.exp(b_end - b))
    return o, st_new


def _gated_rmsnorm(o, ng, gate):
    return o * lax.rsqrt(jnp.mean(o * o, -1, keepdims=True) + NORM_EPS) * ng * jax.nn.silu(gate)


def _gla_body(proj_ref, s0_ref, ng_ref, p1_ref, p2_ref, o_ref, sout_ref, st_scr, *, hgrn, nh, dk, dv, tb, t_valid,
              nt):
    t = pl.program_id(1)

    @pl.when(t == 0)
    def _():
        for h in range(nh):
            st_scr[h] = s0_ref[0, h].T

    kw = nh * dk
    vw = nh * dv

    def chunk(ci, carry):
        r0 = pl.multiple_of(ci * C, C)
        rows = pl.ds(r0, C)
        valid = _valid_rows(t, r0, tb, t_valid, nt)
        if not hgrn:
            r = proj_ref[rows, 2 * kw + 2 * vw:2 * kw + 2 * vw + LANE]
            logf_all = jax.nn.log_sigmoid(_dot(r, p1_ref[...], HI) + p2_ref[...]) / GLA_GATE_NORM
        for h in range(nh):
            ks = slice(h * dk, (h + 1) * dk)
            if hgrn:
                q = jax.nn.silu(proj_ref[rows, ks])
                lb = p1_ref[:, ks]
                fg = lb + (1.0 - lb) * jax.nn.sigmoid(proj_ref[rows, kw + h * dk:kw + (h + 1) * dk])
                k = 1.0 - fg
                g = jnp.log(fg)
            else:
                q = proj_ref[rows, ks] * dk ** -0.5
                k = proj_ref[rows, kw + h * dk:kw + (h + 1) * dk]
                g = logf_all[:, ks]
            v = proj_ref[rows, 2 * kw + h * dv:2 * kw + (h + 1) * dv]
            gate = proj_ref[rows, 2 * kw + vw + h * dv:2 * kw + vw + (h + 1) * dv]
            if valid is not None:
                k = jnp.where(valid, k, 0.0)
                g = jnp.where(valid, g, 0.0)
            o, st_new = _gla_chunk(q, k, v, g, st_scr[h])
            st_scr[h] = st_new
            o_ref[rows, h * dv:(h + 1) * dv] = _gated_rmsnorm(o, ng_ref[...], gate)
        return carry

    lax.fori_loop(0, tb // C, chunk, 0)

    @pl.when(t == nt - 1)
    def _():
        for h in range(nh):
            sout_ref[0, h] = st_scr[h].T


def _gla_scan(proj, s0, ng, p1, p2, *, hgrn, nb, nt, tb, t_valid):
    _, nh, dk, dv = s0.shape
    wp = proj.shape[1]
    body = functools.partial(_gla_body, hgrn=hgrn, nh=nh, dk=dk, dv=dv, tb=tb, t_valid=t_valid, nt=nt)
    return pl.pallas_call(
        body, grid=(nb, nt),
        in_specs=[pl.BlockSpec((tb, wp), lambda b, t: (b * nt + t, 0)),
                  pl.BlockSpec((1, nh, dk, dv), lambda b, t: (b, 0, 0, 0)),
                  pl.BlockSpec(ng.shape, lambda b, t: (0, 0)),
                  pl.BlockSpec(p1.shape, lambda b, t: (0, 0)),
                  pl.BlockSpec(p2.shape, lambda b, t: (0, 0))],
        out_specs=[pl.BlockSpec((tb, nh * dv), lambda b, t: (b * nt + t, 0)),
                   pl.BlockSpec((1, nh, dk, dv), lambda b, t: (b, 0, 0, 0))],
        out_shape=[jax.ShapeDtypeStruct((nb * nt * tb, nh * dv), F32), jax.ShapeDtypeStruct(s0.shape, F32)],
        scratch_shapes=[pltpu.VMEM((nh, dv, dk), F32)],
        compiler_params=_cparams(("parallel", "arbitrary")), name="hgrn_scan" if hgrn else "gla_scan",
    )(proj, s0, ng, p1, p2)


def _gdn_body(proj_ref, s0_ref, conv0_ref, cw_ref, ng_ref, alog_ref, dtb_ref, o_ref, sout_ref, convout_ref,
              st_scr, cbuf, ybuf, *, nh, dk, dv, tb, t_valid, nt):
    t = pl.program_id(1)
    cwid = 2 * nh * dk + nh * dv
    pre = 8 - (CONV_W - 1)

    @pl.when(t == 0)
    def _():
        for h in range(nh):
            st_scr[:, h * dv:(h + 1) * dv] = s0_ref[0, h]
        cbuf[0:8, :] = jnp.zeros((8, cwid), F32)
        cbuf[pre:8, :] = conv0_ref[0]

    cbuf[8:8 + tb, :] = proj_ref[:, 0:cwid]
    y = cbuf[pre:pre + tb, :] * cw_ref[0:1, :]
    for j in range(1, CONV_W):
        y = y + cbuf[pre + j:pre + j + tb, :] * cw_ref[j:j + 1, :]
    ybuf[...] = jax.nn.silu(y)

    nr = nh * C
    ri = _iota((nr, 1), 0)
    cj = _iota((1, nr), 1)
    rhead = ri // C
    same = rhead == cj // C
    incl = same & (ri % C >= cj % C)
    strict = same & (ri % C > cj % C)
    eye = (ri == cj).astype(F32)

    def stack(ref, rows, base, width):
        return jnp.concatenate([ref[rows, base + h * width:base + (h + 1) * width] for h in range(nh)], axis=0)

    def own_block(x):
        return _tree(jnp.add, [jnp.where(rhead == h, x[:, h * dv:(h + 1) * dv], 0.0) for h in range(nh)])

    prep = []
    for ci in range(tb // C):
        rows = slice(ci * C, (ci + 1) * C)
        valid = _valid_rows(t, ci * C, tb, t_valid, nt)
        small = proj_ref[rows, cwid + nh * dv:cwid + nh * dv + LANE]
        g_all = -jnp.exp(alog_ref[...]) * jax.nn.softplus(small + dtb_ref[...])
        beta_all = jax.nn.sigmoid(small)
        if valid is not None:
            g_all = jnp.where(valid, g_all, 0.0)
            beta_all = jnp.where(valid, beta_all, 0.0)
        b_all = _cumsum_rows(g_all)
        b_t = _rows_to_cols(b_all)
        bcol = jnp.concatenate([b_all[:, h:h + 1] for h in range(nh)], axis=0)
        brow = jnp.concatenate([b_t[h:h + 1, :] for h in range(nh)], axis=1)
        beta = jnp.concatenate([beta_all[:, nh + h:nh + h + 1] for h in range(nh)], axis=0)
        bend = jnp.concatenate([jnp.broadcast_to(b_all[C - 1:C, h:h + 1], (C, 1)) for h in range(nh)], axis=0)
        dec_row = jnp.concatenate([jnp.broadcast_to(jnp.exp(b_all[C - 1:C, h:h + 1]), (1, dv)) for h in range(nh)],
                                  axis=1)
        q = stack(ybuf, rows, 0, dk)
        k = stack(ybuf, rows, nh * dk, dk)
        v = stack(ybuf, rows, 2 * nh * dk, dv)
        q = q * lax.rsqrt(jnp.sum(q * q, -1, keepdims=True) + NORM_EPS) * dk ** -0.5
        k = k * lax.rsqrt(jnp.sum(k * k, -1, keepdims=True) + NORM_EPS)
        if valid is not None:
            valid_st = jnp.concatenate([valid] * nh, axis=0)
            q = jnp.where(valid_st, q, 0.0)
            k = jnp.where(valid_st, k, 0.0)
            v = jnp.where(valid_st, v, 0.0)
        decay = jnp.exp(jnp.where(incl, bcol - brow, -jnp.inf))
        a = jnp.where(strict, _dot_nt(k, k, HI) * decay * beta, 0.0)
        a2 = _dot(a, a, HI)
        a4 = _dot(a2, a2, HI)
        a8 = _dot(a4, a4, HI)
        tinv = _dot(_dot(eye - a, eye + a2, HI), _dot(eye + a4, eye + a8, HI), HI)
        eb = jnp.exp(bcol)
        prep.append(dict(u=_dot(tinv, beta * v, HI), wk=_dot(tinv, (beta * eb) * k, HI),
                         att=_dot_nt(q, k) * decay, qe=q * eb, kd=k * jnp.exp(bend - bcol), dec_row=dec_row))

    s_cat = st_scr[...]
    for ci, p in enumerate(prep):
        rows = slice(ci * C, (ci + 1) * C)
        v_new = p['u'] - own_block(_dot(p['wk'], s_cat))
        o = _dot(p['att'], v_new) + own_block(_dot(p['qe'], s_cat))
        v_wide = jnp.concatenate([jnp.where(rhead == h, v_new, 0.0) for h in range(nh)], axis=1)
        s_cat = s_cat * p['dec_row'] + _dot_tn(p['kd'], v_wide)
        o = o * lax.rsqrt(jnp.mean(o * o, -1, keepdims=True) + NORM_EPS) * ng_ref[...]
        for h in range(nh):
            gate = proj_ref[rows, cwid + h * dv:cwid + (h + 1) * dv]
            o_ref[rows, h * dv:(h + 1) * dv] = o[h * C:(h + 1) * C, :] * jax.nn.silu(gate)
    st_scr[...] = s_cat

    @pl.when(t < nt - 1)
    def _():
        cbuf[pre:8, :] = cbuf[pre + tb:8 + tb, :]

    @pl.when(t == nt - 1)
    def _():
        for h in range(nh):
            sout_ref[0, h] = st_scr[:, h * dv:(h + 1) * dv]
        last = t_valid - (nt - 1) * tb
        convout_ref[0] = cbuf[pre + last:8 + last, :]


def _gdn_scan(proj, s0, conv0, cw, ng, alog, dtb, *, nb, nt, tb, t_valid):
    _, nh, dk, dv = s0.shape
    wp = proj.shape[1]
    cwid = 2 * nh * dk + nh * dv
    body = functools.partial(_gdn_body, nh=nh, dk=dk, dv=dv, tb=tb, t_valid=t_valid, nt=nt)
    const = lambda b, t: (0, 0)
    return pl.pallas_call(
        body, grid=(nb, nt),
        in_specs=[pl.BlockSpec((tb, wp), lambda b, t: (b * nt + t, 0)),
                  pl.BlockSpec((1, nh, dk, dv), lambda b, t: (b, 0, 0, 0)),
                  pl.BlockSpec((1, CONV_W - 1, cwid), lambda b, t: (b, 0, 0)),
                  pl.BlockSpec(cw.shape, const), pl.BlockSpec(ng.shape, const),
                  pl.BlockSpec(alog.shape, const), pl.BlockSpec(dtb.shape, const)],
        out_specs=[pl.BlockSpec((tb, nh * dv), lambda b, t: (b * nt + t, 0)),
                   pl.BlockSpec((1, nh, dk, dv), lambda b, t: (b, 0, 0, 0)),
                   pl.BlockSpec((1, CONV_W - 1, cwid), lambda b, t: (b, 0, 0))],
        out_shape=[jax.ShapeDtypeStruct((nb * nt * tb, nh * dv), F32), jax.ShapeDtypeStruct(s0.shape, F32),
                   jax.ShapeDtypeStruct(conv0.shape, F32)],
        scratch_shapes=[pltpu.VMEM((dk, nh * dv), F32), pltpu.VMEM((tb + 8, cwid), F32), pltpu.VMEM((tb, cwid), F32)],
        compiler_params=_cparams(("parallel", "arbitrary")), name="gdn_scan",
    )(proj, s0, conv0, cw, ng, alog, dtb)


def _mlstm_body(proj_ref, c0_ref, n0_ref, m0_ref, ng_ref, bi_ref, bf_ref, o_ref, cout_ref, nout_ref, mout_ref,
                ct_scr, n_scr, m_scr, *, nh, dk, dv, tb, t_valid, nt):
    t = pl.program_id(1)

    @pl.when(t == 0)
    def _():
        for h in range(nh):
            ct_scr[:, h * dv:(h + 1) * dv] = c0_ref[0, h]
            n_scr[h] = n0_ref[0, h:h + 1, :]
            m_scr[h] = jnp.broadcast_to(m0_ref[0, :, h:h + 1], (1, LANE))

    base = 2 * nh * dk + 2 * nh * dv
    nr = nh * C
    ri = _iota((nr, 1), 0)
    cj = _iota((1, nr), 1)
    rhead = ri // C
    incl = (rhead == cj // C) & (ri % C >= cj % C)

    def stack(rows, col0, width):
        return jnp.concatenate([proj_ref[rows, col0 + h * width:col0 + (h + 1) * width] for h in range(nh)], axis=0)

    def per_head_col(vals):
        return jnp.concatenate([jnp.broadcast_to(x, (C, 1)) for x in vals], axis=0)

    prep = []
    for ci in range(tb // C):
        rows = slice(ci * C, (ci + 1) * C)
        valid = _valid_rows(t, ci * C, tb, t_valid, nt)
        small = proj_ref[rows, base:base + LANE]
        li_all = small + bi_ref[...]
        lf_all = jax.nn.log_sigmoid(small + bf_ref[...])
        if valid is not None:
            lf_all = jnp.where(valid, lf_all, 0.0)
        b_all = _cumsum_rows(lf_all)
        b_t = _rows_to_cols(b_all)
        li_t = _rows_to_cols(li_all)
        bcol = jnp.concatenate([b_all[:, nh + h:nh + h + 1] for h in range(nh)], axis=0)
        licol = jnp.concatenate([li_all[:, h:h + 1] for h in range(nh)], axis=0)
        xrow = jnp.concatenate([b_t[nh + h:nh + h + 1, :] - li_t[h:h + 1, :] for h in range(nh)], axis=1)
        if valid is not None:
            xrow = jnp.where(t * tb + ci * C + cj % C < t_valid, xrow, jnp.inf)
            licol = jnp.where(t * tb + ci * C + ri % C < t_valid, licol, -jnp.inf)
        q = stack(rows, 0, dk)
        k = stack(rows, nh * dk, dk) * dk ** -0.5
        v = stack(rows, 2 * nh * dk, dv)
        dmat = jnp.where(incl, bcol - xrow, -jnp.inf)
        prep.append(dict(q=q, k=k, v=v, bcol=bcol, licol=licol, dmat=dmat, qk=_dot_nt(q, k),
                         rowmax=jnp.max(dmat, axis=1, keepdims=True),
                         b_end=[b_all[C - 1:C, nh + h:nh + h + 1] for h in range(nh)]))

    c_cat = ct_scr[...]
    n_rows = [n_scr[h] for h in range(nh)]
    m_vals = [m_scr[h][:, 0:1] for h in range(nh)]
    for ci, p in enumerate(prep):
        rows = slice(ci * C, (ci + 1) * C)
        q, k, v, bcol = p['q'], p['k'], p['v'], p['bcol']
        m_inter = bcol + per_head_col(m_vals)
        m_t = jnp.maximum(m_inter, p['rowmax'])
        a_int = jnp.exp(m_inter - m_t)
        qk = p['qk'] * jnp.exp(p['dmat'] - m_t)
        qc = _dot(q, c_cat)
        own = _tree(jnp.add, [jnp.where(rhead == h, qc[:, h * dv:(h + 1) * dv], 0.0) for h in range(nh)])
        num = _dot(qk, v) + a_int * own
        n_st = jnp.concatenate([jnp.broadcast_to(n_rows[h], (C, dk)) for h in range(nh)], axis=0)
        den = jnp.sum(qk, axis=1, keepdims=True) + a_int * jnp.sum(q * n_st, axis=1, keepdims=True)
        hid = num / jnp.maximum(jnp.abs(den), jnp.exp(-m_t))
        last = [slice(h * C + C - 1, h * C + C) for h in range(nh)]
        m_vals = [m_t[r, :] for r in last]
        a_end = [a_int[r, :] for r in last]
        wk = jnp.exp(per_head_col(p['b_end']) - bcol + p['licol'] - per_head_col(m_vals)) * k
        v_wide = jnp.concatenate([jnp.where(rhead == h, v, 0.0) for h in range(nh)], axis=1)
        a_row = jnp.concatenate([jnp.broadcast_to(a, (1, dv)) for a in a_end], axis=1)
        c_cat = c_cat * a_row + _dot_tn(wk, v_wide)
        n_rows = [a_end[h] * n_rows[h] + jnp.sum(wk[h * C:(h + 1) * C, :], axis=0, keepdims=True) for h in range(nh)]
        hc = hid - jnp.mean(hid, -1, keepdims=True)
        hn = hc * lax.rsqrt(jnp.mean(hc * hc, -1, keepdims=True) + NORM_EPS)
        for h in range(nh):
            og = jax.nn.sigmoid(proj_ref[rows, 2 * nh * dk + nh * dv + h * dv:2 * nh * dk + nh * dv + (h + 1) * dv])
            o_ref[rows, h * dv:(h + 1) * dv] = og * (hn[h * C:(h + 1) * C, :] * ng_ref[:, h * dv:(h + 1) * dv])
    ct_scr[...] = c_cat
    for h in range(nh):
        n_scr[h] = n_rows[h]
        m_scr[h] = jnp.broadcast_to(m_vals[h], (1, LANE))

    @pl.when(t == nt - 1)
    def _():
        lane = _iota((1, nh), 1)
        mrow = jnp.zeros((1, nh), F32)
        for h in range(nh):
            cout_ref[0, h] = ct_scr[:, h * dv:(h + 1) * dv]
            nout_ref[0, h:h + 1, :] = n_scr[h]
            mrow = jnp.where(lane == h, m_scr[h][:, 0:1], mrow)
        mout_ref[0] = mrow


def _mlstm_scan(proj, c0, n0, m0, ng, bi, bf, *, nb, nt, tb, t_valid):
    _, nh, dk, dv = c0.shape
    wp = proj.shape[1]
    body = functools.partial(_mlstm_body, nh=nh, dk=dk, dv=dv, tb=tb, t_valid=t_valid, nt=nt)
    const = lambda b, t: (0, 0)
    m0 = m0.reshape(nb, 1, nh)
    return pl.pallas_call(
        body, grid=(nb, nt),
        in_specs=[pl.BlockSpec((tb, wp), lambda b, t: (b * nt + t, 0)),
                  pl.BlockSpec((1, nh, dk, dv), lambda b, t: (b, 0, 0, 0)),
                  pl.BlockSpec((1, nh, dk), lambda b, t: (b, 0, 0)),
                  pl.BlockSpec((1, 1, nh), lambda b, t: (b, 0, 0)),
                  pl.BlockSpec(ng.shape, const), pl.BlockSpec(bi.shape, const), pl.BlockSpec(bf.shape, const)],
        out_specs=[pl.BlockSpec((tb, nh * dv), lambda b, t: (b * nt + t, 0)),
                   pl.BlockSpec((1, nh, dk, dv), lambda b, t: (b, 0, 0, 0)),
                   pl.BlockSpec((1, nh, dk), lambda b, t: (b, 0, 0)),
                   pl.BlockSpec((1, 1, nh), lambda b, t: (b, 0, 0))],
        out_shape=[jax.ShapeDtypeStruct((nb * nt * tb, nh * dv), F32), jax.ShapeDtypeStruct(c0.shape, F32),
                   jax.ShapeDtypeStruct(n0.shape, F32), jax.ShapeDtypeStruct((nb, 1, nh), F32)],
        scratch_shapes=[pltpu.VMEM((dk, nh * dv), F32), pltpu.VMEM((nh, 1, dk), F32), pltpu.VMEM((nh, 1, LANE), F32)],
        compiler_params=_cparams(("parallel", "arbitrary")), name="mlstm_scan",
    )(proj, c0, n0, m0, ng, bi, bf)


SUB = 8


def _tree(op, xs):
    xs = list(xs)
    while len(xs) > 1:
        xs = [op(xs[i], xs[i + 1]) if i + 1 < len(xs) else xs[i] for i in range(0, len(xs), 2)]
    return xs[0]


def _all_sublanes(op, m):
    for shift in (4, 2, 1):
        m = op(m, pltpu.roll(m, shift, axis=0))
    return m


def _top16(tiles, want_rank):
    tiles = list(tiles)
    ranks = [jnp.full((SUB, LANE), float(PEER_TOPK), F32) for _ in tiles] if want_rank else None
    vals = []
    for j in range(PEER_TOPK):
        m = _all_sublanes(jnp.maximum, _tree(jnp.maximum, tiles))
        vals.append(m)
        for i in range(len(tiles)):
            hit = tiles[i] == m
            if want_rank:
                ranks[i] = jnp.where(hit, float(j), ranks[i])
            tiles[i] = jnp.where(hit, -jnp.inf, tiles[i])
    return vals, ranks


def _route_body(xt_ref, wq_ref, k1_ref, k2_ref, r2_ref, c1_ref, a1_ref, a2_ref, q_scr, *, tm):
    q_scr[...] = _dot(wq_ref[...], xt_ref[...].astype(BF16))
    nsub = tm // LANE
    ntile = PEER_NKEYS // SUB
    sub = _iota((SUB, LANE), 0)

    def unit(i, carry):
        h = i // nsub
        lanes = pl.ds(pl.multiple_of((i % nsub) * LANE, LANE), LANE)
        q1 = q_scr[pl.ds(pl.multiple_of(h * 2 * PEER_HALF, PEER_HALF), PEER_HALF), lanes]
        q2 = q_scr[pl.ds(pl.multiple_of(h * 2 * PEER_HALF + PEER_HALF, PEER_HALF), PEER_HALF), lanes]
        s1 = _dot(k1_ref[...], q1, HI)
        s2 = _dot(k2_ref[...], q2, HI)
        t1 = [s1[k * SUB:(k + 1) * SUB, :] for k in range(ntile)]
        t2 = [s2[k * SUB:(k + 1) * SUB, :] for k in range(ntile)]
        v1, _ = _top16(t1, False)
        v2, rank2 = _top16(t2, True)
        v2lo, v2hi = v2[SUB - 1], v2[2 * SUB - 1]
        for b in range(SUB - 2, -1, -1):
            v2lo = jnp.where(sub == b, v2[b], v2lo)
            v2hi = jnp.where(sub == b, v2[SUB + b], v2hi)
        cands = []
        for a in range(PEER_TOPK):
            nb_ok = PEER_TOPK // (a + 1)
            cands.append(jnp.where(sub < min(nb_ok, SUB), v1[a] + v2lo, -jnp.inf))
            if nb_ok > SUB:
                cands.append(jnp.where(sub < nb_ok - SUB, v1[a] + v2hi, -jnp.inf))
        work = list(cands)
        tau = None
        for _ in range(PEER_TOPK):
            tau = _all_sublanes(jnp.maximum, _tree(jnp.maximum, work))
            work = [jnp.where(w == tau, -jnp.inf, w) for w in work]
        top = v1[0] + v2[0]
        z = _all_sublanes(jnp.add, _tree(jnp.add, [jnp.where(cd >= tau, jnp.exp(cd - top), 0.0) for cd in cands]))
        inv_z = 1.0 / z
        c1 = []
        for k in range(ntile):
            c1.append(_tree(jnp.add, [jnp.where(t1[k] + v2[b] >= tau, 1.0, 0.0) for b in range(PEER_TOPK)]))
        r2_ref[h, :, lanes] = jnp.concatenate(rank2, axis=0).astype(BF16)
        c1_ref[h, :, lanes] = jnp.concatenate(c1, axis=0)
        a1_ref[h, :, lanes] = jnp.concatenate([jnp.exp(t - v1[0]) for t in t1], axis=0)
        a2_ref[h, :, lanes] = jnp.concatenate([jnp.exp(t - v2[0]) * inv_z for t in t2], axis=0).astype(BF16)
        return carry

    lax.fori_loop(0, PEER_HEADS * nsub, unit, 0)


def _peer_route(xt, wq_t, k1, k2, tm):
    d, n = xt.shape
    ospec = pl.BlockSpec((PEER_HEADS, PEER_NKEYS, tm), lambda i: (0, 0, i))
    shp = lambda dt: jax.ShapeDtypeStruct((PEER_HEADS, PEER_NKEYS, n), dt)
    return pl.pallas_call(
        functools.partial(_route_body, tm=tm), grid=(n // tm,),
        in_specs=[pl.BlockSpec((d, tm), lambda i: (0, i)), pl.BlockSpec(wq_t.shape, lambda i: (0, 0)),
                  pl.BlockSpec(k1.shape, lambda i: (0, 0)), pl.BlockSpec(k2.shape, lambda i: (0, 0))],
        out_specs=[ospec] * 4, out_shape=[shp(BF16), shp(F32), shp(F32), shp(BF16)],
        scratch_shapes=[pltpu.VMEM((wq_t.shape[0], tm), F32)],
        compiler_params=_cparams(("parallel",)), name="peer_route")(xt, wq_t, k1, k2)


def _peer_body(xt_ref, r2_ref, c1_ref, a1_ref, a2_ref, u_ref, vt_ref, g_ref, b_ref, out_ref, acc, xbf, p_scr):
    c = pl.program_id(1)

    @pl.when(c == 0)
    def _():
        xbf[...] = xt_ref[...].astype(BF16)
        acc[...] = jnp.zeros(acc.shape, F32)
        p_scr[...] = jnp.zeros(p_scr.shape, BF16)

    tm = xbf.shape[1]
    rows16 = 2 * SUB
    sub_w = PEER_SUB_I1 * PEER_NKEYS
    total = None
    for s in range(PEER_I1_PER_STEP // PEER_SUB_I1):
        es = slice(s * sub_w, (s + 1) * sub_w)
        part = _dot(vt_ref[:, es], p_scr[es, :])
        total = part if total is None else total + part
        hid = _dot(u_ref[es, :], xbf[...])
        act = (0.5 * hid * (1.0 + lax.erf(hid * 0.5 ** 0.5))).astype(BF16)
        tiles = []
        for j in range(s * PEER_SUB_I1, (s + 1) * PEER_SUB_I1):
            c1b = [jnp.broadcast_to(c1_ref[h, j:j + 1, :], (rows16, tm)).astype(BF16) for h in range(PEER_HEADS)]
            a1b = [jnp.broadcast_to(a1_ref[h, j:j + 1, :], (rows16, tm)).astype(BF16) for h in range(PEER_HEADS)]
            for kb in range(PEER_NKEYS // rows16):
                ks = slice(kb * rows16, (kb + 1) * rows16)
                tiles.append(_tree(jnp.add, [jnp.where(r2_ref[h, ks, :] < c1b[h], a2_ref[h, ks, :] * a1b[h], 0.0)
                                             for h in range(PEER_HEADS)]))
        p_scr[es, :] = jnp.concatenate(tiles, axis=0) * act
    acc[...] += total

    @pl.when(c == pl.num_programs(1) - 1)
    def _():
        z = DN_ALPHA * xt_ref[...] + acc[...]
        zc = z - jnp.mean(z, 0, keepdims=True)
        var = jnp.mean(zc * zc, 0, keepdims=True)
        out_ref[...] = (zc * lax.rsqrt(var + LN_EPS) * g_ref[...] + b_ref[...]).T


def _peer_main(xt, r2, c1, a1, a2, u, vt, g, b, tm):
    d, n = xt.shape
    ec = PEER_I1_PER_STEP * PEER_NKEYS
    nc = u.shape[0] // ec
    build = lambda c: jnp.minimum(c, nc - 1)
    mult = lambda c: jnp.maximum(c - 1, 0)
    k2spec = pl.BlockSpec((PEER_HEADS, PEER_NKEYS, tm), lambda i, c: (0, 0, i))
    k1spec = pl.BlockSpec((PEER_HEADS, PEER_I1_PER_STEP, tm), lambda i, c: (0, build(c), i))
    return pl.pallas_call(
        _peer_body, grid=(n // tm, nc + 1),
        in_specs=[pl.BlockSpec((d, tm), lambda i, c: (0, i)), k2spec, k1spec, k1spec, k2spec,
                  pl.BlockSpec((ec, d), lambda i, c: (build(c), 0)), pl.BlockSpec((d, ec), lambda i, c: (0, mult(c))),
                  pl.BlockSpec((d, 1), lambda i, c: (0, 0)), pl.BlockSpec((d, 1), lambda i, c: (0, 0))],
        out_specs=pl.BlockSpec((tm, d), lambda i, c: (i, 0)),
        out_shape=jax.ShapeDtypeStruct((n, d), F32),
        scratch_shapes=[pltpu.VMEM((d, tm), F32), pltpu.VMEM((d, tm), BF16), pltpu.VMEM((ec, tm), BF16)],
        compiler_params=_cparams(("parallel", "arbitrary")), name="peer_main",
    )(xt, r2, c1, a1, a2, u, vt, g.reshape(d, 1), b.reshape(d, 1))


def _pad_cols(w, width):
    return jnp.pad(w, ((0, 0), (0, width - w.shape[1])))


def _lane_row(*parts):
    v = jnp.concatenate([p.astype(F32).reshape(-1) for p in parts])
    return jnp.pad(v, (0, LANE - v.shape[0])).reshape(1, LANE)


def _pick_tb(t_pad):
    for tb in (48, 32, 16):
        if t_pad % tb == 0:
            return tb
    raise ValueError(t_pad)


def kernel(x_prompt, x_sample, state_hgrn_S, state_gdn_S, state_gdn_conv, state_mlstm_C, state_mlstm_n, state_mlstm_m, state_gla_S, meta_tokens, hgrn_w_in, hgrn_lb, hgrn_norm_g, hgrn_w_out, gdn_w_in, gdn_conv_w, gdn_a_log, gdn_dt_bias, gdn_norm_g, gdn_w_out, mlstm_w_in, mlstm_b_i, mlstm_b_f, mlstm_norm_g, mlstm_w_out, gla_w_in, gla_w_gate, gla_b_gate, gla_norm_g, gla_w_out, peer_w_q, peer_keys1, peer_keys2, peer_u, peer_v, ln_g, ln_b):
    bp, seq, d = x_prompt.shape
    bs, seq_s, _ = x_sample.shape
    assert d == D_MODEL and len(state_hgrn_S) == 1 and len(state_gdn_S) == 1
    assert len(state_mlstm_C) == 1 and len(state_gla_S) == 1
    tp = N_META + seq
    tp_pad = -(-tp // C) * C
    ts_pad = -(-seq_s // C) * C
    tbp = _pick_tb(tp_pad)
    tbs = _pick_tb(ts_pad)
    np_rows = bp * tp_pad
    n_real = np_rows + bs * seq_s

    meta = jnp.broadcast_to(meta_tokens.astype(F32)[None], (bp, N_META, d))
    hp = jnp.concatenate([meta, x_prompt], axis=1)
    hp = jnp.pad(hp, ((0, 0), (0, tp_pad - tp), (0, 0))).reshape(np_rows, d)
    h = jnp.concatenate([hp, x_sample.reshape(bs * seq_s, d)], axis=0)
    n = -(-n_real // TOK_BLOCK) * TOK_BLOCK
    h = jnp.pad(h, ((0, n - n_real), (0, 0)))
    tm_mm = MM_TOK_BLOCK if n % MM_TOK_BLOCK == 0 else TOK_BLOCK

    def split_cols(w, sizes):
        out, o = [], 0
        for s in sizes:
            out.append(w[:, o:o + s])
            o += s
        return out

    def run_mixer(scan, proj, states_p, states_s, **kw):
        res_p = scan(proj, *states_p, nb=bp, nt=tp_pad // tbp, tb=tbp, t_valid=tp, **kw)
        ps = proj[np_rows:n_real].reshape(bs, seq_s, -1)
        ps = jnp.pad(ps, ((0, 0), (0, ts_pad - seq_s), (0, 0))).reshape(bs * ts_pad, -1)
        res_s = scan(ps, *states_s, nb=bs, nt=ts_pad // tbs, tb=tbs, t_valid=seq_s, **kw)
        o_s = res_s[0].reshape(bs, ts_pad, -1)[:, :seq_s].reshape(bs * seq_s, -1)
        o = jnp.concatenate([res_p[0], o_s], axis=0)
        o = jnp.pad(o, ((0, n - n_real), (0, 0)))
        return o, res_p[1:], res_s[1:]

    lb_all = jnp.cumsum(jax.nn.softmax(hgrn_lb.astype(F32), axis=0), axis=0)
    outs = {}
    for i in range(DEPTH):
        mix = i % 4
        if mix == 0:
            w_in = hgrn_w_in[0].astype(BF16)
            proj = _mm(h, w_in, tm_mm, 1024)
            nh, dk, dv = state_hgrn_S.shape[2:]
            zero = jnp.zeros((bp, nh, dk, dv), F32)
            scan = functools.partial(_gla_scan, ng=hgrn_norm_g[0].reshape(1, dv), p1=lb_all[i].reshape(1, nh * dk),
                                     p2=jnp.zeros((1, LANE), F32), hgrn=True)
            o, (sp,), (ss,) = run_mixer(scan, proj, (zero,), (state_hgrn_S[0],))
            outs['hgrn'] = (sp[None], ss[None])
            w_out = hgrn_w_out[0]
        elif mix == 1:
            nh, dk, dv = state_gdn_S.shape[2:]
            cwid = 2 * nh * dk + nh * dv
            qkv, a, b, g = split_cols(gdn_w_in[0], [cwid, nh, nh, nh * dv])
            w_in = jnp.concatenate([qkv, g, _pad_cols(jnp.concatenate([a, b], axis=1), LANE)], axis=1).astype(BF16)
            proj = _mm(h, w_in, tm_mm, w_in.shape[1] // 3)
            scan = functools.partial(_gdn_scan, cw=gdn_conv_w[0], ng=gdn_norm_g[0].reshape(1, dv),
                                     alog=_lane_row(gdn_a_log[0]), dtb=_lane_row(gdn_dt_bias[0]))
            o, (sp, cp), (ss, cs) = run_mixer(
                scan, proj,
                (jnp.zeros((bp, nh, dk, dv), F32), jnp.zeros((bp, CONV_W - 1, cwid), F32)),
                (state_gdn_S[0], state_gdn_conv[0]))
            outs['gdn'] = (sp[None], cp[None], ss[None], cs[None])
            w_out = gdn_w_out[0]
        elif mix == 2:
            nh, dk, dv = state_mlstm_C.shape[2:]
            q, k, v, ig, fg, og = split_cols(mlstm_w_in[0], [nh * dk, nh * dk, nh * dv, nh, nh, nh * dv])
            w_in = jnp.concatenate([q, k, v, og, _pad_cols(jnp.concatenate([ig, fg], axis=1), LANE)],
                                   axis=1).astype(BF16)
            proj = _mm(h, w_in, tm_mm, w_in.shape[1] // 3)
            zeros_nh = jnp.zeros((nh,), F32)
            scan = functools.partial(_mlstm_scan, ng=mlstm_norm_g[0].reshape(1, nh * dv),
                                     bi=_lane_row(mlstm_b_i[0]), bf=_lane_row(zeros_nh, mlstm_b_f[0]))
            o, (cp, npp, mp), (cs, ns, ms) = run_mixer(
                scan, proj,
                (jnp.zeros((bp, nh, dk, dv), F32), jnp.zeros((bp, nh, dk), F32), jnp.zeros((bp, nh), F32)),
                (state_mlstm_C[0], state_mlstm_n[0], state_mlstm_m[0]))
            outs['mlstm'] = (cp[None], npp[None], mp.reshape(1, bp, nh), cs[None], ns[None], ms.reshape(1, bs, nh))
            w_out = mlstm_w_out[0]
        else:
            nh, dk, dv = state_gla_S.shape[2:]
            rank = gla_w_gate.shape[1]
            q, k, v, g, r = split_cols(gla_w_in[0], [nh * dk, nh * dk, nh * dv, nh * dv, rank])
            w_in = jnp.concatenate([q, k, v, g, _pad_cols(r, LANE)], axis=1).astype(BF16)
            proj = _mm(h, w_in, tm_mm, w_in.shape[1] // 5)
            wg = jnp.pad(gla_w_gate[0].astype(F32), ((0, LANE - rank), (0, 0)))
            scan = functools.partial(_gla_scan, ng=gla_norm_g[0].reshape(1, dv), p1=wg,
                                     p2=gla_b_gate[0].reshape(1, nh * dk), hgrn=False)
            o, (sp,), (ss,) = run_mixer(scan, proj, (jnp.zeros((bp, nh, dk, dv), F32),), (state_gla_S[0],))
            outs['gla'] = (sp[None], ss[None])
            w_out = gla_w_out[0]

        h1t = _mm_res_ln_t(o, w_out.astype(BF16), h, ln_g[i, 0], ln_b[i, 0], TOK_BLOCK)
        r2, c1, a1, a2 = _peer_route(h1t, peer_w_q[i].T.astype(BF16), peer_keys1[i], peer_keys2[i], TOK_BLOCK)
        h = _peer_main(h1t, r2, c1, a1, a2, peer_u[i].astype(BF16), peer_v[i].T.astype(BF16), ln_g[i, 1],
                       ln_b[i, 1], TOK_BLOCK)

    y_prompt = h[:np_rows].reshape(bp, tp_pad, d)[:, N_META:tp]
    y_sample = h[np_rows:n_real].reshape(bs, seq_s, d)
    hg, gd, ml, gl = outs['hgrn'], outs['gdn'], outs['mlstm'], outs['gla']
    return (y_prompt, y_sample, hg[0], gd[0], gd[1], ml[0], ml[1], ml[2], gl[0],
            hg[1], gd[2], gd[3], ml[3], ml[4], ml[5], gl[1])
```

```python
import functools

import jax
import jax.numpy as jnp
from jax import lax
from jax.experimental import pallas as pl
from jax.experimental.pallas import tpu as pltpu

F32 = jnp.float32
BF16 = jnp.bfloat16
HI = lax.Precision.HIGHEST

D_MODEL = 1024
DEPTH = 4
N_META = 16
C = 16
CONV_W = 4
NORM_EPS = 1e-6
LN_EPS = 1e-5
DN_ALPHA = (2 * DEPTH) ** 0.25
GLA_GATE_NORM = 16.0
LANE = 128
PEER_HEADS = 8
PEER_NKEYS = 128
PEER_TOPK = 16
PEER_HALF = 128
PEER_I1_PER_STEP = 8
PEER_SUB_I1 = 2
TOK_BLOCK = 256
PEER_TOK_BLOCK = 512
PEER_LANE_BLOCK = 256
MM_TOK_BLOCK = 768
TOK_MULTIPLE = 1536
VMEM_LIMIT = 48 * 1024 * 1024


def _cparams(sem):
    return pltpu.CompilerParams(dimension_semantics=sem, vmem_limit_bytes=VMEM_LIMIT)


def _iota(shape, dim):
    return lax.broadcasted_iota(jnp.int32, shape, dim)


def _dot_nt(a, b, precision=None):
    return lax.dot_general(a, b, (((1,), (1,)), ((), ())), precision=precision, preferred_element_type=F32)


def _dot_tn(a, b, precision=None):
    return lax.dot_general(a, b, (((0,), (0,)), ((), ())), precision=precision, preferred_element_type=F32)


def _dot(a, b, precision=None):
    return jnp.dot(a, b, precision=precision, preferred_element_type=F32)


def _cumsum_rows(x):
    tri = (_iota((C, C), 0) >= _iota((C, C), 1)).astype(F32)
    return _dot(tri, x, HI)


def _rows_to_cols(x):
    eye = (_iota((LANE, LANE), 0) == _iota((LANE, LANE), 1)).astype(F32)
    return _dot_nt(eye, x, HI)


def _incl_mask():
    return _iota((C, C), 0) >= _iota((C, C), 1)


def _mm_body(x_ref, w_ref, o_ref):
    o_ref[...] = _dot(x_ref[...].astype(BF16), w_ref[...])


def _mm(x, w, tm, tn):
    m, k = x.shape
    n = w.shape[1]
    return pl.pallas_call(
        _mm_body, grid=(m // tm, n // tn),
        in_specs=[pl.BlockSpec((tm, k), lambda i, j: (i, 0)), pl.BlockSpec((k, tn), lambda i, j: (0, j))],
        out_specs=pl.BlockSpec((tm, tn), lambda i, j: (i, j)),
        out_shape=jax.ShapeDtypeStruct((m, n), F32),
        compiler_params=_cparams(("parallel", "parallel")), name="mm")(x, w)


def _mm_ln_body(o_ref, w_ref, h_ref, g_ref, b_ref, out_ref):
    y = _dot(o_ref[...].astype(BF16), w_ref[...])
    z = DN_ALPHA * h_ref[...] + y
    zc = z - jnp.mean(z, -1, keepdims=True)
    var = jnp.mean(zc * zc, -1, keepdims=True)
    out_ref[...] = (zc * lax.rsqrt(var + LN_EPS) * g_ref[...] + b_ref[...]).T


def _mm_res_ln_t(o, w, h, g, b, tm):
    n, k = o.shape
    d = w.shape[1]
    return pl.pallas_call(
        _mm_ln_body, grid=(n // tm,),
        in_specs=[pl.BlockSpec((tm, k), lambda i: (i, 0)), pl.BlockSpec((k, d), lambda i: (0, 0)),
                  pl.BlockSpec((tm, d), lambda i: (i, 0)), pl.BlockSpec((1, d), lambda i: (0, 0)),
                  pl.BlockSpec((1, d), lambda i: (0, 0))],
        out_specs=pl.BlockSpec((d, tm), lambda i: (0, i)),
        out_shape=jax.ShapeDtypeStruct((d, n), F32),
        compiler_params=_cparams(("parallel",)), name="mm_res_ln")(o, w, h, g.reshape(1, d), b.reshape(1, d))


def _valid_rows(t, r0, tb, t_valid, nt):
    if t_valid == nt * tb:
        return None
    return (t * tb + r0 + _iota((C, 1), 0)) < t_valid


def _gla_body(proj_ref, s0_ref, ng_ref, p1_ref, p2_ref, o_ref, sout_ref, st_scr, *, hgrn, nh, dk, dv, tb, t_valid,
              nt):
    t = pl.program_id(1)

    @pl.when(t == 0)
    def _():
        for h in range(nh):
            st_scr[h] = s0_ref[0, h].T

    kw = nh * dk
    vw = nh * dv
    nr = nh * C
    ri = _iota((nr, 1), 0)
    cj = _iota((1, nr), 1)
    rhead = ri // C
    tri = ((rhead == cj // C) & (ri % C >= cj % C)).astype(F32)
    row3 = _iota((1, C, 1), 1)

    def stack(rows, col0, width):
        return jnp.concatenate([proj_ref[rows, col0 + h * width:col0 + (h + 1) * width] for h in range(nh)], axis=0)

    prep = []
    for ci in range(tb // C):
        rows = slice(ci * C, (ci + 1) * C)
        if hgrn:
            q = jax.nn.silu(stack(rows, 0, dk))
            lb = jnp.concatenate([jnp.broadcast_to(p1_ref[:, h * dk:(h + 1) * dk], (C, dk)) for h in range(nh)], axis=0)
            fg = lb + (1.0 - lb) * jax.nn.sigmoid(stack(rows, kw, dk))
            k = 1.0 - fg
            g = jnp.log(fg)
        else:
            r = proj_ref[rows, 2 * kw + 2 * vw:2 * kw + 2 * vw + LANE]
            logf_all = jax.nn.log_sigmoid(_dot(r, p1_ref[...], HI) + p2_ref[...]) / GLA_GATE_NORM
            q = stack(rows, 0, dk) * dk ** -0.5
            k = stack(rows, kw, dk)
            g = jnp.concatenate([logf_all[:, h * dk:(h + 1) * dk] for h in range(nh)], axis=0)
        v = stack(rows, 2 * kw, dv)
        if t_valid != nt * tb:
            valid = t * tb + ci * C + ri % C < t_valid
            k = jnp.where(valid, k, 0.0)
            g = jnp.where(valid, g, 0.0)
        b = _dot(tri, g, HI)
        b3, q3, k3, v3 = (x.reshape(nh, C, x.shape[-1]) for x in (b, q, k, v))
        o3 = jnp.zeros((nh, C, dv), F32)
        for s in range(C):
            d = jnp.exp(jnp.where(row3 >= s, b3 - b3[:, s:s + 1, :], -jnp.inf))
            col = jnp.sum(q3 * (k3[:, s:s + 1, :] * d), axis=-1, keepdims=True)
            o3 = o3 + col * v3[:, s:s + 1, :]
        b_end = b3[:, C - 1:C, :]
        prep.append(dict(o=o3.reshape(nr, dv), qe=q * jnp.exp(b), kd=(k3 * jnp.exp(b_end - b3)).reshape(nr, dk),
                         dec=jnp.exp(b_end), v=v))

    st = st_scr[...]
    for ci, p in enumerate(prep):
        rows = slice(ci * C, (ci + 1) * C)
        inter = _dot_nt(p['qe'], st.reshape(nh * dv, dk))
        o = p['o'] + _tree(jnp.add, [jnp.where(rhead == h, inter[:, h * dv:(h + 1) * dv], 0.0) for h in range(nh)])
        v_wide = jnp.concatenate([jnp.where(rhead == h, p['v'], 0.0) for h in range(nh)], axis=1)
        st = st * p['dec'] + _dot_tn(v_wide, p['kd']).reshape(nh, dv, dk)
        o = o * lax.rsqrt(jnp.mean(o * o, -1, keepdims=True) + NORM_EPS) * ng_ref[...]
        for h in range(nh):
            gate = proj_ref[rows, 2 * kw + vw + h * dv:2 * kw + vw + (h + 1) * dv]
            o_ref[rows, h * dv:(h + 1) * dv] = o[h * C:(h + 1) * C, :] * jax.nn.silu(gate)
    st_scr[...] = st

    @pl.when(t == nt - 1)
    def _():
        for h in range(nh):
            sout_ref[0, h] = st_scr[h].T


def _gla_scan(proj, s0, ng, p1, p2, *, hgrn, nb, nt, tb, t_valid):
    _, nh, dk, dv = s0.shape
    wp = proj.shape[1]
    body = functools.partial(_gla_body, hgrn=hgrn, nh=nh, dk=dk, dv=dv, tb=tb, t_valid=t_valid, nt=nt)
    return pl.pallas_call(
        body, grid=(nb, nt),
        in_specs=[pl.BlockSpec((tb, wp), lambda b, t: (b * nt + t, 0)),
                  pl.BlockSpec((1, nh, dk, dv), lambda b, t: (b, 0, 0, 0)),
                  pl.BlockSpec(ng.shape, lambda b, t: (0, 0)),
                  pl.BlockSpec(p1.shape, lambda b, t: (0, 0)),
                  pl.BlockSpec(p2.shape, lambda b, t: (0, 0))],
        out_specs=[pl.BlockSpec((tb, nh * dv), lambda b, t: (b * nt + t, 0)),
                   pl.BlockSpec((1, nh, dk, dv), lambda b, t: (b, 0, 0, 0))],
        out_shape=[jax.ShapeDtypeStruct((nb * nt * tb, nh * dv), F32), jax.ShapeDtypeStruct(s0.shape, F32)],
        scratch_shapes=[pltpu.VMEM((nh, dv, dk), F32)],
        compiler_params=_cparams(("parallel", "arbitrary")), name="hgrn_scan" if hgrn else "gla_scan",
    )(proj, s0, ng, p1, p2)


def _gdn_body(proj_ref, s0_ref, conv0_ref, cw_ref, ng_ref, alog_ref, dtb_ref, o_ref, sout_ref, convout_ref,
              st_scr, cbuf, ybuf, *, nh, dk, dv, tb, t_valid, nt):
    t = pl.program_id(1)
    cwid = 2 * nh * dk + nh * dv
    pre = 8 - (CONV_W - 1)

    @pl.when(t == 0)
    def _():
        for h in range(nh):
            st_scr[:, h * dv:(h + 1) * dv] = s0_ref[0, h]
        cbuf[0:8, :] = jnp.zeros((8, cwid), F32)
        cbuf[pre:8, :] = conv0_ref[0]

    cbuf[8:8 + tb, :] = proj_ref[:, 0:cwid]
    y = cbuf[pre:pre + tb, :] * cw_ref[0:1, :]
    for j in range(1, CONV_W):
        y = y + cbuf[pre + j:pre + j + tb, :] * cw_ref[j:j + 1, :]
    ybuf[...] = jax.nn.silu(y)

    nr = nh * C
    ri = _iota((nr, 1), 0)
    cj = _iota((1, nr), 1)
    rhead = ri // C
    same = rhead == cj // C
    incl = same & (ri % C >= cj % C)
    strict = same & (ri % C > cj % C)
    eye = (ri == cj).astype(F32)

    def stack(ref, rows, base, width):
        return jnp.concatenate([ref[rows, base + h * width:base + (h + 1) * width] for h in range(nh)], axis=0)

    def own_block(x):
        return _tree(jnp.add, [jnp.where(rhead == h, x[:, h * dv:(h + 1) * dv], 0.0) for h in range(nh)])

    prep = []
    for ci in range(tb // C):
        rows = slice(ci * C, (ci + 1) * C)
        valid = _valid_rows(t, ci * C, tb, t_valid, nt)
        small = proj_ref[rows, cwid + nh * dv:cwid + nh * dv + LANE]
        g_all = -jnp.exp(alog_ref[...]) * jax.nn.softplus(small + dtb_ref[...])
        beta_all = jax.nn.sigmoid(small)
        if valid is not None:
            g_all = jnp.where(valid, g_all, 0.0)
            beta_all = jnp.where(valid, beta_all, 0.0)
        b_all = _cumsum_rows(g_all)
        b_t = _rows_to_cols(b_all)
        bcol = jnp.concatenate([b_all[:, h:h + 1] for h in range(nh)], axis=0)
        brow = jnp.concatenate([b_t[h:h + 1, :] for h in range(nh)], axis=1)
        beta = jnp.concatenate([beta_all[:, nh + h:nh + h + 1] for h in range(nh)], axis=0)
        bend = jnp.concatenate([jnp.broadcast_to(b_all[C - 1:C, h:h + 1], (C, 1)) for h in range(nh)], axis=0)
        dec_row = jnp.concatenate([jnp.broadcast_to(jnp.exp(b_all[C - 1:C, h:h + 1]), (1, dv)) for h in range(nh)],
                                  axis=1)
        q = stack(ybuf, rows, 0, dk)
        k = stack(ybuf, rows, nh * dk, dk)
        v = stack(ybuf, rows, 2 * nh * dk, dv)
        q = q * lax.rsqrt(jnp.sum(q * q, -1, keepdims=True) + NORM_EPS) * dk ** -0.5
        k = k * lax.rsqrt(jnp.sum(k * k, -1, keepdims=True) + NORM_EPS)
        if valid is not None:
            valid_st = jnp.concatenate([valid] * nh, axis=0)
            q = jnp.where(valid_st, q, 0.0)
            k = jnp.where(valid_st, k, 0.0)
            v = jnp.where(valid_st, v, 0.0)
        decay = jnp.exp(jnp.where(incl, bcol - brow, -jnp.inf))
        eb = jnp.exp(bcol)
        prep.append(dict(a=jnp.where(strict, _dot_nt(k, k, HI) * decay * beta, 0.0), bv=beta * v, bk=(beta * eb) * k,
                         att=_dot_nt(q, k) * decay, qe=q * eb, kd=k * jnp.exp(bend - bcol), dec_row=dec_row))
    for p in prep:
        p['a2'] = _dot(p['a'], p['a'], HI)
    for p in prep:
        p['a4'] = _dot(p['a2'], p['a2'], HI)
        p['lo'] = _dot(eye - p['a'], eye + p['a2'], HI)
    for p in prep:
        p['a8'] = _dot(p['a4'], p['a4'], HI)
    for p in prep:
        p['hi'] = _dot(eye + p['a4'], eye + p['a8'], HI)
    for p in prep:
        p['tinv'] = _dot(p['lo'], p['hi'], HI)
    for p in prep:
        p['u'] = _dot(p['tinv'], p['bv'], HI)
        p['wk'] = _dot(p['tinv'], p['bk'], HI)

    s_cat = st_scr[...]
    for ci, p in enumerate(prep):
        rows = slice(ci * C, (ci + 1) * C)
        v_new = p['u'] - own_block(_dot(p['wk'], s_cat))
        o = _dot(p['att'], v_new) + own_block(_dot(p['qe'], s_cat))
        v_wide = jnp.concatenate([jnp.where(rhead == h, v_new, 0.0) for h in range(nh)], axis=1)
        s_cat = s_cat * p['dec_row'] + _dot_tn(p['kd'], v_wide)
        o = o * lax.rsqrt(jnp.mean(o * o, -1, keepdims=True) + NORM_EPS) * ng_ref[...]
        for h in range(nh):
            gate = proj_ref[rows, cwid + h * dv:cwid + (h + 1) * dv]
            o_ref[rows, h * dv:(h + 1) * dv] = o[h * C:(h + 1) * C, :] * jax.nn.silu(gate)
    st_scr[...] = s_cat

    @pl.when(t < nt - 1)
    def _():
        cbuf[pre:8, :] = cbuf[pre + tb:8 + tb, :]

    @pl.when(t == nt - 1)
    def _():
        for h in range(nh):
            sout_ref[0, h] = st_scr[:, h * dv:(h + 1) * dv]
        last = t_valid - (nt - 1) * tb
        convout_ref[0] = cbuf[pre + last:8 + last, :]


def _gdn_scan(proj, s0, conv0, cw, ng, alog, dtb, *, nb, nt, tb, t_valid):
    _, nh, dk, dv = s0.shape
    wp = proj.shape[1]
    cwid = 2 * nh * dk + nh * dv
    body = functools.partial(_gdn_body, nh=nh, dk=dk, dv=dv, tb=tb, t_valid=t_valid, nt=nt)
    const = lambda b, t: (0, 0)
    return pl.pallas_call(
        body, grid=(nb, nt),
        in_specs=[pl.BlockSpec((tb, wp), lambda b, t: (b * nt + t, 0)),
                  pl.BlockSpec((1, nh, dk, dv), lambda b, t: (b, 0, 0, 0)),
                  pl.BlockSpec((1, CONV_W - 1, cwid), lambda b, t: (b, 0, 0)),
                  pl.BlockSpec(cw.shape, const), pl.BlockSpec(ng.shape, const),
                  pl.BlockSpec(alog.shape, const), pl.BlockSpec(dtb.shape, const)],
        out_specs=[pl.BlockSpec((tb, nh * dv), lambda b, t: (b * nt + t, 0)),
                   pl.BlockSpec((1, nh, dk, dv), lambda b, t: (b, 0, 0, 0)),
                   pl.BlockSpec((1, CONV_W - 1, cwid), lambda b, t: (b, 0, 0))],
        out_shape=[jax.ShapeDtypeStruct((nb * nt * tb, nh * dv), F32), jax.ShapeDtypeStruct(s0.shape, F32),
                   jax.ShapeDtypeStruct(conv0.shape, F32)],
        scratch_shapes=[pltpu.VMEM((dk, nh * dv), F32), pltpu.VMEM((tb + 8, cwid), F32), pltpu.VMEM((tb, cwid), F32)],
        compiler_params=_cparams(("parallel", "arbitrary")), name="gdn_scan",
    )(proj, s0, conv0, cw, ng, alog, dtb)


def _mlstm_body(proj_ref, c0_ref, n0_ref, m0_ref, ng_ref, bi_ref, bf_ref, o_ref, cout_ref, nout_ref, mout_ref,
                ct_scr, n_scr, m_scr, *, nh, dk, dv, tb, t_valid, nt):
    t = pl.program_id(1)

    @pl.when(t == 0)
    def _():
        for h in range(nh):
            ct_scr[:, h * dv:(h + 1) * dv] = c0_ref[0, h]
            n_scr[h] = n0_ref[0, h:h + 1, :]
            m_scr[h] = jnp.broadcast_to(m0_ref[0, :, h:h + 1], (1, LANE))

    base = 2 * nh * dk + 2 * nh * dv
    nr = nh * C
    ri = _iota((nr, 1), 0)
    cj = _iota((1, nr), 1)
    rhead = ri // C
    incl = (rhead == cj // C) & (ri % C >= cj % C)

    def stack(rows, col0, width):
        return jnp.concatenate([proj_ref[rows, col0 + h * width:col0 + (h + 1) * width] for h in range(nh)], axis=0)

    def per_head_col(vals):
        return jnp.concatenate([jnp.broadcast_to(x, (C, 1)) for x in vals], axis=0)

    prep = []
    for ci in range(tb // C):
        rows = slice(ci * C, (ci + 1) * C)
        valid = _valid_rows(t, ci * C, tb, t_valid, nt)
        small = proj_ref[rows, base:base + LANE]
        li_all = small + bi_ref[...]
        lf_all = jax.nn.log_sigmoid(small + bf_ref[...])
        if valid is not None:
            lf_all = jnp.where(valid, lf_all, 0.0)
        b_all = _cumsum_rows(lf_all)
        b_t = _rows_to_cols(b_all)
        li_t = _rows_to_cols(li_all)
        bcol = jnp.concatenate([b_all[:, nh + h:nh + h + 1] for h in range(nh)], axis=0)
        licol = jnp.concatenate([li_all[:, h:h + 1] for h in range(nh)], axis=0)
        xrow = jnp.concatenate([b_t[nh + h:nh + h + 1, :] - li_t[h:h + 1, :] for h in range(nh)], axis=1)
        if valid is not None:
            xrow = jnp.where(t * tb + ci * C + cj % C < t_valid, xrow, jnp.inf)
            licol = jnp.where(t * tb + ci * C + ri % C < t_valid, licol, -jnp.inf)
        q = stack(rows, 0, dk)
        k = stack(rows, nh * dk, dk) * dk ** -0.5
        v = stack(rows, 2 * nh * dk, dv)
        dmat = jnp.where(incl, bcol - xrow, -jnp.inf)
        prep.append(dict(q=q, k=k, v=v, bcol=bcol, licol=licol, dmat=dmat, qk=_dot_nt(q, k),
                         rowmax=jnp.max(dmat, axis=1, keepdims=True),
                         b_end=[b_all[C - 1:C, nh + h:nh + h + 1] for h in range(nh)]))

    c_cat = ct_scr[...]
    n_rows = [n_scr[h] for h in range(nh)]
    m_vals = [m_scr[h][:, 0:1] for h in range(nh)]
    for ci, p in enumerate(prep):
        rows = slice(ci * C, (ci + 1) * C)
        q, k, v, bcol = p['q'], p['k'], p['v'], p['bcol']
        m_inter = bcol + per_head_col(m_vals)
        m_t = jnp.maximum(m_inter, p['rowmax'])
        a_int = jnp.exp(m_inter - m_t)
        qk = p['qk'] * jnp.exp(p['dmat'] - m_t)
        qc = _dot(q, c_cat)
        own = _tree(jnp.add, [jnp.where(rhead == h, qc[:, h * dv:(h + 1) * dv], 0.0) for h in range(nh)])
        num = _dot(qk, v) + a_int * own
        n_st = jnp.concatenate([jnp.broadcast_to(n_rows[h], (C, dk)) for h in range(nh)], axis=0)
        den = jnp.sum(qk, axis=1, keepdims=True) + a_int * jnp.sum(q * n_st, axis=1, keepdims=True)
        hid = num / jnp.maximum(jnp.abs(den), jnp.exp(-m_t))
        last = [slice(h * C + C - 1, h * C + C) for h in range(nh)]
        m_vals = [m_t[r, :] for r in last]
        a_end = [a_int[r, :] for r in last]
        wk = jnp.exp(per_head_col(p['b_end']) - bcol + p['licol'] - per_head_col(m_vals)) * k
        v_wide = jnp.concatenate([jnp.where(rhead == h, v, 0.0) for h in range(nh)], axis=1)
        a_row = jnp.concatenate([jnp.broadcast_to(a, (1, dv)) for a in a_end], axis=1)
        c_cat = c_cat * a_row + _dot_tn(wk, v_wide)
        n_rows = [a_end[h] * n_rows[h] + jnp.sum(wk[h * C:(h + 1) * C, :], axis=0, keepdims=True) for h in range(nh)]
        hc = hid - jnp.mean(hid, -1, keepdims=True)
        hn = hc * lax.rsqrt(jnp.mean(hc * hc, -1, keepdims=True) + NORM_EPS)
        for h in range(nh):
            og = jax.nn.sigmoid(proj_ref[rows, 2 * nh * dk + nh * dv + h * dv:2 * nh * dk + nh * dv + (h + 1) * dv])
            o_ref[rows, h * dv:(h + 1) * dv] = og * (hn[h * C:(h + 1) * C, :] * ng_ref[:, h * dv:(h + 1) * dv])
    ct_scr[...] = c_cat
    for h in range(nh):
        n_scr[h] = n_rows[h]
        m_scr[h] = jnp.broadcast_to(m_vals[h], (1, LANE))

    @pl.when(t == nt - 1)
    def _():
        lane = _iota((1, nh), 1)
        mrow = jnp.zeros((1, nh), F32)
        for h in range(nh):
            cout_ref[0, h] = ct_scr[:, h * dv:(h + 1) * dv]
            nout_ref[0, h:h + 1, :] = n_scr[h]
            mrow = jnp.where(lane == h, m_scr[h][:, 0:1], mrow)
        mout_ref[0] = mrow


def _mlstm_scan(proj, c0, n0, m0, ng, bi, bf, *, nb, nt, tb, t_valid):
    _, nh, dk, dv = c0.shape
    wp = proj.shape[1]
    body = functools.partial(_mlstm_body, nh=nh, dk=dk, dv=dv, tb=tb, t_valid=t_valid, nt=nt)
    const = lambda b, t: (0, 0)
    m0 = m0.reshape(nb, 1, nh)
    return pl.pallas_call(
        body, grid=(nb, nt),
        in_specs=[pl.BlockSpec((tb, wp), lambda b, t: (b * nt + t, 0)),
                  pl.BlockSpec((1, nh, dk, dv), lambda b, t: (b, 0, 0, 0)),
                  pl.BlockSpec((1, nh, dk), lambda b, t: (b, 0, 0)),
                  pl.BlockSpec((1, 1, nh), lambda b, t: (b, 0, 0)),
                  pl.BlockSpec(ng.shape, const), pl.BlockSpec(bi.shape, const), pl.BlockSpec(bf.shape, const)],
        out_specs=[pl.BlockSpec((tb, nh * dv), lambda b, t: (b * nt + t, 0)),
                   pl.BlockSpec((1, nh, dk, dv), lambda b, t: (b, 0, 0, 0)),
                   pl.BlockSpec((1, nh, dk), lambda b, t: (b, 0, 0)),
                   pl.BlockSpec((1, 1, nh), lambda b, t: (b, 0, 0))],
        out_shape=[jax.ShapeDtypeStruct((nb * nt * tb, nh * dv), F32), jax.ShapeDtypeStruct(c0.shape, F32),
                   jax.ShapeDtypeStruct(n0.shape, F32), jax.ShapeDtypeStruct((nb, 1, nh), F32)],
        scratch_shapes=[pltpu.VMEM((dk, nh * dv), F32), pltpu.VMEM((nh, 1, dk), F32), pltpu.VMEM((nh, 1, LANE), F32)],
        compiler_params=_cparams(("parallel", "arbitrary")), name="mlstm_scan",
    )(proj, c0, n0, m0, ng, bi, bf)


SUB = 8


def _tree(op, xs):
    xs = list(xs)
    while len(xs) > 1:
        xs = [op(xs[i], xs[i + 1]) if i + 1 < len(xs) else xs[i] for i in range(0, len(xs), 2)]
    return xs[0]


def _all_sublanes(op, m):
    for shift in (4, 2, 1):
        m = op(m, pltpu.roll(m, shift, axis=0))
    return m


def _top16(tiles, want_rank):
    tiles = list(tiles)
    ranks = [jnp.full((SUB, LANE), float(PEER_TOPK), F32) for _ in tiles] if want_rank else None
    vals = []
    for j in range(PEER_TOPK):
        m = _all_sublanes(jnp.maximum, _tree(jnp.maximum, tiles))
        vals.append(m)
        for i in range(len(tiles)):
            hit = tiles[i] == m
            if want_rank:
                ranks[i] = jnp.where(hit, float(j), ranks[i])
            tiles[i] = jnp.where(hit, -jnp.inf, tiles[i])
    return vals, ranks


def _route_body(xt_ref, wq_ref, k1_ref, k2_ref, r2_ref, c1_ref, a1_ref, a2_ref, q_scr, *, tm):
    q_scr[...] = _dot(wq_ref[...], xt_ref[...].astype(BF16))
    nsub = tm // LANE
    ntile = PEER_NKEYS // SUB
    sub = _iota((SUB, LANE), 0)

    def unit(i, carry):
        h = i // nsub
        lanes = pl.ds(pl.multiple_of((i % nsub) * LANE, LANE), LANE)
        q1 = q_scr[pl.ds(pl.multiple_of(h * 2 * PEER_HALF, PEER_HALF), PEER_HALF), lanes]
        q2 = q_scr[pl.ds(pl.multiple_of(h * 2 * PEER_HALF + PEER_HALF, PEER_HALF), PEER_HALF), lanes]
        s1 = _dot(k1_ref[...], q1, HI)
        s2 = _dot(k2_ref[...], q2, HI)
        t1 = [s1[k * SUB:(k + 1) * SUB, :] for k in range(ntile)]
        t2 = [s2[k * SUB:(k + 1) * SUB, :] for k in range(ntile)]
        v1, _ = _top16(t1, False)
        v2, rank2 = _top16(t2, True)
        v2lo, v2hi = v2[SUB - 1], v2[2 * SUB - 1]
        for b in range(SUB - 2, -1, -1):
            v2lo = jnp.where(sub == b, v2[b], v2lo)
            v2hi = jnp.where(sub == b, v2[SUB + b], v2hi)
        cands = []
        for a in range(PEER_TOPK):
            nb_ok = PEER_TOPK // (a + 1)
            cands.append(jnp.where(sub < min(nb_ok, SUB), v1[a] + v2lo, -jnp.inf))
            if nb_ok > SUB:
                cands.append(jnp.where(sub < nb_ok - SUB, v1[a] + v2hi, -jnp.inf))
        work = list(cands)
        tau = None
        for _ in range(PEER_TOPK):
            tau = _all_sublanes(jnp.maximum, _tree(jnp.maximum, work))
            work = [jnp.where(w == tau, -jnp.inf, w) for w in work]
        top = v1[0] + v2[0]
        z = _all_sublanes(jnp.add, _tree(jnp.add, [jnp.where(cd >= tau, jnp.exp(cd - top), 0.0) for cd in cands]))
        inv_z = 1.0 / z
        c1 = []
        for k in range(ntile):
            c1.append(_tree(jnp.add, [jnp.where(t1[k] + v2[b] >= tau, 1.0, 0.0) for b in range(PEER_TOPK)]))
        r2_ref[h, :, lanes] = jnp.concatenate(rank2, axis=0).astype(BF16)
        c1_ref[h, :, lanes] = jnp.concatenate(c1, axis=0)
        a1_ref[h, :, lanes] = jnp.concatenate([jnp.exp(t - v1[0]) for t in t1], axis=0)
        a2_ref[h, :, lanes] = jnp.concatenate([jnp.exp(t - v2[0]) * inv_z for t in t2], axis=0).astype(BF16)
        return carry

    lax.fori_loop(0, PEER_HEADS * nsub, unit, 0)


def _peer_route(xt, wq_t, k1, k2, tm):
    d, n = xt.shape
    ospec = pl.BlockSpec((PEER_HEADS, PEER_NKEYS, tm), lambda i: (0, 0, i))
    shp = lambda dt: jax.ShapeDtypeStruct((PEER_HEADS, PEER_NKEYS, n), dt)
    return pl.pallas_call(
        functools.partial(_route_body, tm=tm), grid=(n // tm,),
        in_specs=[pl.BlockSpec((d, tm), lambda i: (0, i)), pl.BlockSpec(wq_t.shape, lambda i: (0, 0)),
                  pl.BlockSpec(k1.shape, lambda i: (0, 0)), pl.BlockSpec(k2.shape, lambda i: (0, 0))],
        out_specs=[ospec] * 4, out_shape=[shp(BF16), shp(F32), shp(F32), shp(BF16)],
        scratch_shapes=[pltpu.VMEM((wq_t.shape[0], tm), F32)],
        compiler_params=_cparams(("parallel",)), name="peer_route")(xt, wq_t, k1, k2)


def _peer_body(xt_ref, r2_ref, c1_ref, a1_ref, a2_ref, u_ref, vt_ref, g_ref, b_ref, out_ref, acc, xbf, p_scr):
    c = pl.program_id(1)

    @pl.when(c == 0)
    def _():
        xbf[...] = xt_ref[...].astype(BF16)
        acc[...] = jnp.zeros(acc.shape, F32)
        p_scr[...] = jnp.zeros(p_scr.shape, BF16)

    tm = xbf.shape[1]
    rows16 = 2 * SUB
    sub_w = PEER_SUB_I1 * PEER_NKEYS
    total = None
    for s in range(PEER_I1_PER_STEP // PEER_SUB_I1):
        es = slice(s * sub_w, (s + 1) * sub_w)
        part = _dot(vt_ref[0, :, es], p_scr[es, :])
        total = part if total is None else total + part
        hid = _dot(u_ref[es, :], xbf[...])
        act = (0.5 * hid * (1.0 + lax.erf(hid * 0.5 ** 0.5))).astype(BF16)
        cols = []
        for l0 in range(0, tm, PEER_LANE_BLOCK):
            ls = slice(l0, l0 + PEER_LANE_BLOCK)
            tiles = []
            for j in range(s * PEER_SUB_I1, (s + 1) * PEER_SUB_I1):
                c1b = [jnp.broadcast_to(c1_ref[h, j:j + 1, ls], (rows16, PEER_LANE_BLOCK)).astype(BF16)
                       for h in range(PEER_HEADS)]
                a1b = [jnp.broadcast_to(a1_ref[h, j:j + 1, ls], (rows16, PEER_LANE_BLOCK)).astype(BF16)
                       for h in range(PEER_HEADS)]
                for kb in range(PEER_NKEYS // rows16):
                    ks = slice(kb * rows16, (kb + 1) * rows16)
                    tiles.append(_tree(jnp.add, [jnp.where(r2_ref[h, ks, ls] < c1b[h], a2_ref[h, ks, ls] * a1b[h], 0.0)
                                                 for h in range(PEER_HEADS)]))
            cols.append(jnp.concatenate(tiles, axis=0))
        p_scr[es, :] = jnp.concatenate(cols, axis=1) * act
    acc[...] += total

    @pl.when(c == pl.num_programs(1) - 1)
    def _():
        z = DN_ALPHA * xt_ref[...] + acc[...]
        zc = z - jnp.mean(z, 0, keepdims=True)
        var = jnp.mean(zc * zc, 0, keepdims=True)
        out_ref[...] = (zc * lax.rsqrt(var + LN_EPS) * g_ref[...] + b_ref[...]).T


def _peer_main(xt, r2, c1, a1, a2, u, vt, g, b, tm):
    d, n = xt.shape
    ec = PEER_I1_PER_STEP * PEER_NKEYS
    nc = u.shape[0] // ec
    build = lambda c: jnp.minimum(c, nc - 1)
    mult = lambda c: jnp.maximum(c - 1, 0)
    k2spec = pl.BlockSpec((PEER_HEADS, PEER_NKEYS, tm), lambda i, c: (0, 0, i))
    k1spec = pl.BlockSpec((PEER_HEADS, PEER_I1_PER_STEP, tm), lambda i, c: (0, build(c), i))
    return pl.pallas_call(
        _peer_body, grid=(n // tm, nc + 1),
        in_specs=[pl.BlockSpec((d, tm), lambda i, c: (0, i)), k2spec, k1spec, k1spec, k2spec,
                  pl.BlockSpec((ec, d), lambda i, c: (build(c), 0)),
                  pl.BlockSpec((1, d, ec), lambda i, c: (mult(c), 0, 0)),
                  pl.BlockSpec((d, 1), lambda i, c: (0, 0)), pl.BlockSpec((d, 1), lambda i, c: (0, 0))],
        out_specs=pl.BlockSpec((tm, d), lambda i, c: (i, 0)),
        out_shape=jax.ShapeDtypeStruct((n, d), F32),
        scratch_shapes=[pltpu.VMEM((d, tm), F32), pltpu.VMEM((d, tm), BF16), pltpu.VMEM((ec, tm), BF16)],
        compiler_params=_cparams(("parallel", "arbitrary")), name="peer_main",
    )(xt, r2, c1, a1, a2, u, vt, g.reshape(d, 1), b.reshape(d, 1))


def _pad_cols(w, width):
    return jnp.pad(w, ((0, 0), (0, width - w.shape[1])))


def _lane_row(*parts):
    v = jnp.concatenate([p.astype(F32).reshape(-1) for p in parts])
    return jnp.pad(v, (0, LANE - v.shape[0])).reshape(1, LANE)


def _pick_tb(t_pad):
    for tb in (48, 32, 16):
        if t_pad % tb == 0:
            return tb
    raise ValueError(t_pad)


def kernel(x_prompt, x_sample, state_hgrn_S, state_gdn_S, state_gdn_conv, state_mlstm_C, state_mlstm_n, state_mlstm_m, state_gla_S, meta_tokens, hgrn_w_in, hgrn_lb, hgrn_norm_g, hgrn_w_out, gdn_w_in, gdn_conv_w, gdn_a_log, gdn_dt_bias, gdn_norm_g, gdn_w_out, mlstm_w_in, mlstm_b_i, mlstm_b_f, mlstm_norm_g, mlstm_w_out, gla_w_in, gla_w_gate, gla_b_gate, gla_norm_g, gla_w_out, peer_w_q, peer_keys1, peer_keys2, peer_u, peer_v, ln_g, ln_b):
    bp, seq, d = x_prompt.shape
    bs, seq_s, _ = x_sample.shape
    assert d == D_MODEL and len(state_hgrn_S) == 1 and len(state_gdn_S) == 1
    assert len(state_mlstm_C) == 1 and len(state_gla_S) == 1
    tp = N_META + seq
    tp_pad = -(-tp // C) * C
    ts_pad = -(-seq_s // C) * C
    tbp = _pick_tb(tp_pad)
    tbs = _pick_tb(ts_pad)
    np_rows = bp * tp_pad
    n_real = np_rows + bs * seq_s

    meta = jnp.broadcast_to(meta_tokens.astype(F32)[None], (bp, N_META, d))
    hp = jnp.concatenate([meta, x_prompt], axis=1)
    hp = jnp.pad(hp, ((0, 0), (0, tp_pad - tp), (0, 0))).reshape(np_rows, d)
    h = jnp.concatenate([hp, x_sample.reshape(bs * seq_s, d)], axis=0)
    n = -(-n_real // TOK_MULTIPLE) * TOK_MULTIPLE
    h = jnp.pad(h, ((0, n - n_real), (0, 0)))
    tm_mm = MM_TOK_BLOCK

    def split_cols(w, sizes):
        out, o = [], 0
        for s in sizes:
            out.append(w[:, o:o + s])
            o += s
        return out

    def run_mixer(scan, proj, states_p, states_s, **kw):
        res_p = scan(proj, *states_p, nb=bp, nt=tp_pad // tbp, tb=tbp, t_valid=tp, **kw)
        ps = proj[np_rows:n_real].reshape(bs, seq_s, -1)
        ps = jnp.pad(ps, ((0, 0), (0, ts_pad - seq_s), (0, 0))).reshape(bs * ts_pad, -1)
        res_s = scan(ps, *states_s, nb=bs, nt=ts_pad // tbs, tb=tbs, t_valid=seq_s, **kw)
        o_s = res_s[0].reshape(bs, ts_pad, -1)[:, :seq_s].reshape(bs * seq_s, -1)
        o = jnp.concatenate([res_p[0], o_s], axis=0)
        o = jnp.pad(o, ((0, n - n_real), (0, 0)))
        return o, res_p[1:], res_s[1:]

    lb_all = jnp.cumsum(jax.nn.softmax(hgrn_lb.astype(F32), axis=0), axis=0)
    outs = {}
    for i in range(DEPTH):
        mix = i % 4
        if mix == 0:
            w_in = hgrn_w_in[0].astype(BF16)
            proj = _mm(h, w_in, tm_mm, 1024)
            nh, dk, dv = state_hgrn_S.shape[2:]
            zero = jnp.zeros((bp, nh, dk, dv), F32)
            scan = functools.partial(_gla_scan, ng=hgrn_norm_g[0].reshape(1, dv), p1=lb_all[i].reshape(1, nh * dk),
                                     p2=jnp.zeros((1, LANE), F32), hgrn=True)
            o, (sp,), (ss,) = run_mixer(scan, proj, (zero,), (state_hgrn_S[0],))
            outs['hgrn'] = (sp[None], ss[None])
            w_out = hgrn_w_out[0]
        elif mix == 1:
            nh, dk, dv = state_gdn_S.shape[2:]
            cwid = 2 * nh * dk + nh * dv
            qkv, a, b, g = split_cols(gdn_w_in[0], [cwid, nh, nh, nh * dv])
            w_in = jnp.concatenate([qkv, g, _pad_cols(jnp.concatenate([a, b], axis=1), LANE)], axis=1).astype(BF16)
            proj = _mm(h, w_in, tm_mm, w_in.shape[1] // 3)
            scan = functools.partial(_gdn_scan, cw=gdn_conv_w[0], ng=gdn_norm_g[0].reshape(1, dv),
                                     alog=_lane_row(gdn_a_log[0]), dtb=_lane_row(gdn_dt_bias[0]))
            o, (sp, cp), (ss, cs) = run_mixer(
                scan, proj,
                (jnp.zeros((bp, nh, dk, dv), F32), jnp.zeros((bp, CONV_W - 1, cwid), F32)),
                (state_gdn_S[0], state_gdn_conv[0]))
            outs['gdn'] = (sp[None], cp[None], ss[None], cs[None])
            w_out = gdn_w_out[0]
        elif mix == 2:
            nh, dk, dv = state_mlstm_C.shape[2:]
            q, k, v, ig, fg, og = split_cols(mlstm_w_in[0], [nh * dk, nh * dk, nh * dv, nh, nh, nh * dv])
            w_in = jnp.concatenate([q, k, v, og, _pad_cols(jnp.concatenate([ig, fg], axis=1), LANE)],
                                   axis=1).astype(BF16)
            proj = _mm(h, w_in, tm_mm, w_in.shape[1] // 3)
            zeros_nh = jnp.zeros((nh,), F32)
            scan = functools.partial(_mlstm_scan, ng=mlstm_norm_g[0].reshape(1, nh * dv),
                                     bi=_lane_row(mlstm_b_i[0]), bf=_lane_row(zeros_nh, mlstm_b_f[0]))
            o, (cp, npp, mp), (cs, ns, ms) = run_mixer(
                scan, proj,
                (jnp.zeros((bp, nh, dk, dv), F32), jnp.zeros((bp, nh, dk), F32), jnp.zeros((bp, nh), F32)),
                (state_mlstm_C[0], state_mlstm_n[0], state_mlstm_m[0]))
            outs['mlstm'] = (cp[None], npp[None], mp.reshape(1, bp, nh), cs[None], ns[None], ms.reshape(1, bs, nh))
            w_out = mlstm_w_out[0]
        else:
            nh, dk, dv = state_gla_S.shape[2:]
            rank = gla_w_gate.shape[1]
            q, k, v, g, r = split_cols(gla_w_in[0], [nh * dk, nh * dk, nh * dv, nh * dv, rank])
            w_in = jnp.concatenate([q, k, v, g, _pad_cols(r, LANE)], axis=1).astype(BF16)
            proj = _mm(h, w_in, tm_mm, w_in.shape[1] // 5)
            wg = jnp.pad(gla_w_gate[0].astype(F32), ((0, LANE - rank), (0, 0)))
            scan = functools.partial(_gla_scan, ng=gla_norm_g[0].reshape(1, dv), p1=wg,
                                     p2=gla_b_gate[0].reshape(1, nh * dk), hgrn=False)
            o, (sp,), (ss,) = run_mixer(scan, proj, (jnp.zeros((bp, nh, dk, dv), F32),), (state_gla_S[0],))
            outs['gla'] = (sp[None], ss[None])
            w_out = gla_w_out[0]

        h1t = _mm_res_ln_t(o, w_out.astype(BF16), h, ln_g[i, 0], ln_b[i, 0], TOK_BLOCK)
        r2, c1, a1, a2 = _peer_route(h1t, peer_w_q[i].T.astype(BF16), peer_keys1[i], peer_keys2[i], TOK_BLOCK)
        ec = PEER_I1_PER_STEP * PEER_NKEYS
        vt = peer_v[i].astype(BF16).reshape(-1, ec, d).transpose(0, 2, 1)
        h = _peer_main(h1t, r2, c1, a1, a2, peer_u[i].astype(BF16), vt, ln_g[i, 1], ln_b[i, 1], PEER_TOK_BLOCK)

    y_prompt = h[:np_rows].reshape(bp, tp_pad, d)[:, N_META:tp]
    y_sample = h[np_rows:n_real].reshape(bs, seq_s, d)
    hg, gd, ml, gl = outs['hgrn'], outs['gdn'], outs['mlstm'], outs['gla']
    return (y_prompt, y_sample, hg[0], gd[0], gd[1], ml[0], ml[1], ml[2], gl[0],
            hg[1], gd[2], gd[3], ml[3], ml[4], ml[5], gl[1])
```

```python
import functools

import jax
import jax.numpy as jnp
from jax import lax
from jax.experimental import pallas as pl
from jax.experimental.pallas import tpu as pltpu

F32 = jnp.float32
BF16 = jnp.bfloat16
HI = lax.Precision.HIGHEST

D_MODEL = 1024
DEPTH = 4
N_META = 16
C = 16
CONV_W = 4
NORM_EPS = 1e-6
LN_EPS = 1e-5
DN_ALPHA = (2 * DEPTH) ** 0.25
GLA_GATE_NORM = 16.0
LANE = 128
PEER_HEADS = 8
PEER_NKEYS = 128
PEER_TOPK = 16
PEER_HALF = 128
PEER_I1_PER_STEP = 8
PEER_SUB_I1 = 2
TOK_BLOCK = 256
PEER_TOK_BLOCK = 512
PEER_LANE_BLOCK = 256
MM_TOK_BLOCK = 768
TOK_MULTIPLE = 1536
VMEM_LIMIT = 48 * 1024 * 1024


def _cparams(sem):
    return pltpu.CompilerParams(dimension_semantics=sem, vmem_limit_bytes=VMEM_LIMIT)


def _iota(shape, dim):
    return lax.broadcasted_iota(jnp.int32, shape, dim)


def _dot_nt(a, b, precision=None):
    return lax.dot_general(a, b, (((1,), (1,)), ((), ())), precision=precision, preferred_element_type=F32)


def _dot_tn(a, b, precision=None):
    return lax.dot_general(a, b, (((0,), (0,)), ((), ())), precision=precision, preferred_element_type=F32)


def _dot(a, b, precision=None):
    return jnp.dot(a, b, precision=precision, preferred_element_type=F32)


def _cumsum_rows(x):
    tri = (_iota((C, C), 0) >= _iota((C, C), 1)).astype(F32)
    return _dot(tri, x, HI)


def _rows_to_cols(x):
    eye = (_iota((LANE, LANE), 0) == _iota((LANE, LANE), 1)).astype(F32)
    return _dot_nt(eye, x, HI)


def _incl_mask():
    return _iota((C, C), 0) >= _iota((C, C), 1)


def _mm_body(x_ref, w_ref, o_ref):
    o_ref[...] = _dot(x_ref[...].astype(BF16), w_ref[...])


def _mm(x, w, tm, tn):
    m, k = x.shape
    n = w.shape[1]
    return pl.pallas_call(
        _mm_body, grid=(m // tm, n // tn),
        in_specs=[pl.BlockSpec((tm, k), lambda i, j: (i, 0)), pl.BlockSpec((k, tn), lambda i, j: (0, j))],
        out_specs=pl.BlockSpec((tm, tn), lambda i, j: (i, j)),
        out_shape=jax.ShapeDtypeStruct((m, n), F32),
        compiler_params=_cparams(("parallel", "parallel")), name="mm")(x, w)


def _mm_ln_body(o_ref, w_ref, h_ref, g_ref, b_ref, out_ref):
    y = _dot(o_ref[...].astype(BF16), w_ref[...])
    z = DN_ALPHA * h_ref[...] + y
    zc = z - jnp.mean(z, -1, keepdims=True)
    var = jnp.mean(zc * zc, -1, keepdims=True)
    out_ref[...] = (zc * lax.rsqrt(var + LN_EPS) * g_ref[...] + b_ref[...]).T


def _mm_res_ln_t(o, w, h, g, b, tm):
    n, k = o.shape
    d = w.shape[1]
    return pl.pallas_call(
        _mm_ln_body, grid=(n // tm,),
        in_specs=[pl.BlockSpec((tm, k), lambda i: (i, 0)), pl.BlockSpec((k, d), lambda i: (0, 0)),
                  pl.BlockSpec((tm, d), lambda i: (i, 0)), pl.BlockSpec((1, d), lambda i: (0, 0)),
                  pl.BlockSpec((1, d), lambda i: (0, 0))],
        out_specs=pl.BlockSpec((d, tm), lambda i: (0, i)),
        out_shape=jax.ShapeDtypeStruct((d, n), F32),
        compiler_params=_cparams(("parallel",)), name="mm_res_ln")(o, w, h, g.reshape(1, d), b.reshape(1, d))


def _valid_rows(t, r0, tb, t_valid, nt):
    if t_valid == nt * tb:
        return None
    return (t * tb + r0 + _iota((C, 1), 0)) < t_valid


def _gla_body(proj_ref, s0_ref, ng_ref, p1_ref, p2_ref, o_ref, sout_ref, st_scr, *, hgrn, nh, dk, dv, tb, t_valid,
              nt):
    t = pl.program_id(1)

    @pl.when(t == 0)
    def _():
        for h in range(nh):
            st_scr[h] = s0_ref[0, h].T

    kw = nh * dk
    vw = nh * dv
    nr = nh * C
    ri = _iota((nr, 1), 0)
    cj = _iota((1, nr), 1)
    rhead = ri // C
    tri = ((rhead == cj // C) & (ri % C >= cj % C)).astype(F32)
    row3 = _iota((1, C, 1), 1)

    def stack(rows, col0, width):
        return jnp.concatenate([proj_ref[rows, col0 + h * width:col0 + (h + 1) * width] for h in range(nh)], axis=0)

    prep = []
    for ci in range(tb // C):
        rows = slice(ci * C, (ci + 1) * C)
        if hgrn:
            q = jax.nn.silu(stack(rows, 0, dk))
            lb = jnp.concatenate([jnp.broadcast_to(p1_ref[:, h * dk:(h + 1) * dk], (C, dk)) for h in range(nh)], axis=0)
            fg = lb + (1.0 - lb) * jax.nn.sigmoid(stack(rows, kw, dk))
            k = 1.0 - fg
            g = jnp.log(fg)
        else:
            r = proj_ref[rows, 2 * kw + 2 * vw:2 * kw + 2 * vw + LANE]
            logf_all = jax.nn.log_sigmoid(_dot(r, p1_ref[...], HI) + p2_ref[...]) / GLA_GATE_NORM
            q = stack(rows, 0, dk) * dk ** -0.5
            k = stack(rows, kw, dk)
            g = jnp.concatenate([logf_all[:, h * dk:(h + 1) * dk] for h in range(nh)], axis=0)
        v = stack(rows, 2 * kw, dv)
        if t_valid != nt * tb:
            valid = t * tb + ci * C + ri % C < t_valid
            k = jnp.where(valid, k, 0.0)
            g = jnp.where(valid, g, 0.0)
        b = _dot(tri, g, HI)
        b3, q3, k3, v3 = (x.reshape(nh, C, x.shape[-1]) for x in (b, q, k, v))
        o3 = jnp.zeros((nh, C, dv), F32)
        for s in range(C):
            d = jnp.exp(jnp.where(row3 >= s, b3 - b3[:, s:s + 1, :], -jnp.inf))
            col = jnp.sum(q3 * (k3[:, s:s + 1, :] * d), axis=-1, keepdims=True)
            o3 = o3 + col * v3[:, s:s + 1, :]
        b_end = b3[:, C - 1:C, :]
        prep.append(dict(o=o3.reshape(nr, dv), qe=q * jnp.exp(b), kd=(k3 * jnp.exp(b_end - b3)).reshape(nr, dk),
                         dec=jnp.exp(b_end), v=v))

    st = st_scr[...]
    for ci, p in enumerate(prep):
        rows = slice(ci * C, (ci + 1) * C)
        inter = _dot_nt(p['qe'], st.reshape(nh * dv, dk))
        o = p['o'] + _tree(jnp.add, [jnp.where(rhead == h, inter[:, h * dv:(h + 1) * dv], 0.0) for h in range(nh)])
        v_wide = jnp.concatenate([jnp.where(rhead == h, p['v'], 0.0) for h in range(nh)], axis=1)
        st = st * p['dec'] + _dot_tn(v_wide, p['kd']).reshape(nh, dv, dk)
        o = o * lax.rsqrt(jnp.mean(o * o, -1, keepdims=True) + NORM_EPS) * ng_ref[...]
        for h in range(nh):
            gate = proj_ref[rows, 2 * kw + vw + h * dv:2 * kw + vw + (h + 1) * dv]
            o_ref[rows, h * dv:(h + 1) * dv] = o[h * C:(h + 1) * C, :] * jax.nn.silu(gate)
    st_scr[...] = st

    @pl.when(t == nt - 1)
    def _():
        for h in range(nh):
            sout_ref[0, h] = st_scr[h].T


def _gla_scan(proj, s0, ng, p1, p2, *, hgrn, nb, nt, tb, t_valid):
    _, nh, dk, dv = s0.shape
    wp = proj.shape[1]
    body = functools.partial(_gla_body, hgrn=hgrn, nh=nh, dk=dk, dv=dv, tb=tb, t_valid=t_valid, nt=nt)
    return pl.pallas_call(
        body, grid=(nb, nt),
        in_specs=[pl.BlockSpec((tb, wp), lambda b, t: (b * nt + t, 0)),
                  pl.BlockSpec((1, nh, dk, dv), lambda b, t: (b, 0, 0, 0)),
                  pl.BlockSpec(ng.shape, lambda b, t: (0, 0)),
                  pl.BlockSpec(p1.shape, lambda b, t: (0, 0)),
                  pl.BlockSpec(p2.shape, lambda b, t: (0, 0))],
        out_specs=[pl.BlockSpec((tb, nh * dv), lambda b, t: (b * nt + t, 0)),
                   pl.BlockSpec((1, nh, dk, dv), lambda b, t: (b, 0, 0, 0))],
        out_shape=[jax.ShapeDtypeStruct((nb * nt * tb, nh * dv), F32), jax.ShapeDtypeStruct(s0.shape, F32)],
        scratch_shapes=[pltpu.VMEM((nh, dv, dk), F32)],
        compiler_params=_cparams(("parallel", "arbitrary")), name="hgrn_scan" if hgrn else "gla_scan",
    )(proj, s0, ng, p1, p2)


def _gdn_body(proj_ref, s0_ref, conv0_ref, cw_ref, ng_ref, alog_ref, dtb_ref, o_ref, sout_ref, convout_ref,
              st_scr, cbuf, ybuf, *, nh, dk, dv, tb, t_valid, nt):
    t = pl.program_id(1)
    cwid = 2 * nh * dk + nh * dv
    pre = 8 - (CONV_W - 1)

    @pl.when(t == 0)
    def _():
        for h in range(nh):
            st_scr[:, h * dv:(h + 1) * dv] = s0_ref[0, h]
        cbuf[0:8, :] = jnp.zeros((8, cwid), F32)
        cbuf[pre:8, :] = conv0_ref[0]

    cbuf[8:8 + tb, :] = proj_ref[:, 0:cwid]
    y = cbuf[pre:pre + tb, :] * cw_ref[0:1, :]
    for j in range(1, CONV_W):
        y = y + cbuf[pre + j:pre + j + tb, :] * cw_ref[j:j + 1, :]
    ybuf[...] = jax.nn.silu(y)

    nr = nh * C
    ri = _iota((nr, 1), 0)
    cj = _iota((1, nr), 1)
    rhead = ri // C
    same = rhead == cj // C
    incl = same & (ri % C >= cj % C)
    strict = same & (ri % C > cj % C)
    eye = (ri == cj).astype(F32)

    def stack(ref, rows, base, width):
        return jnp.concatenate([ref[rows, base + h * width:base + (h + 1) * width] for h in range(nh)], axis=0)

    def own_block(x):
        return _tree(jnp.add, [jnp.where(rhead == h, x[:, h * dv:(h + 1) * dv], 0.0) for h in range(nh)])

    prep = []
    for ci in range(tb // C):
        rows = slice(ci * C, (ci + 1) * C)
        valid = _valid_rows(t, ci * C, tb, t_valid, nt)
        small = proj_ref[rows, cwid + nh * dv:cwid + nh * dv + LANE]
        g_all = -jnp.exp(alog_ref[...]) * jax.nn.softplus(small + dtb_ref[...])
        beta_all = jax.nn.sigmoid(small)
        if valid is not None:
            g_all = jnp.where(valid, g_all, 0.0)
            beta_all = jnp.where(valid, beta_all, 0.0)
        b_all = _cumsum_rows(g_all)
        b_t = _rows_to_cols(b_all)
        bcol = jnp.concatenate([b_all[:, h:h + 1] for h in range(nh)], axis=0)
        brow = jnp.concatenate([b_t[h:h + 1, :] for h in range(nh)], axis=1)
        beta = jnp.concatenate([beta_all[:, nh + h:nh + h + 1] for h in range(nh)], axis=0)
        bend = jnp.concatenate([jnp.broadcast_to(b_all[C - 1:C, h:h + 1], (C, 1)) for h in range(nh)], axis=0)
        dec_row = jnp.concatenate([jnp.broadcast_to(jnp.exp(b_all[C - 1:C, h:h + 1]), (1, dv)) for h in range(nh)],
                                  axis=1)
        q = stack(ybuf, rows, 0, dk)
        k = stack(ybuf, rows, nh * dk, dk)
        v = stack(ybuf, rows, 2 * nh * dk, dv)
        q = q * lax.rsqrt(jnp.sum(q * q, -1, keepdims=True) + NORM_EPS) * dk ** -0.5
        k = k * lax.rsqrt(jnp.sum(k * k, -1, keepdims=True) + NORM_EPS)
        if valid is not None:
            valid_st = jnp.concatenate([valid] * nh, axis=0)
            q = jnp.where(valid_st, q, 0.0)
            k = jnp.where(valid_st, k, 0.0)
            v = jnp.where(valid_st, v, 0.0)
        decay = jnp.exp(jnp.where(incl, bcol - brow, -jnp.inf))
        eb = jnp.exp(bcol)
        prep.append(dict(a=jnp.where(strict, _dot_nt(k, k, HI) * decay * beta, 0.0), bv=beta * v, bk=(beta * eb) * k,
                         att=_dot_nt(q, k) * decay, qe=q * eb, kd=k * jnp.exp(bend - bcol), dec_row=dec_row))
    for p in prep:
        p['a2'] = _dot(p['a'], p['a'], HI)
    for p in prep:
        p['a4'] = _dot(p['a2'], p['a2'], HI)
        p['lo'] = _dot(eye - p['a'], eye + p['a2'], HI)
    for p in prep:
        p['a8'] = _dot(p['a4'], p['a4'], HI)
    for p in prep:
        p['hi'] = _dot(eye + p['a4'], eye + p['a8'], HI)
    for p in prep:
        p['tinv'] = _dot(p['lo'], p['hi'], HI)
    for p in prep:
        p['u'] = _dot(p['tinv'], p['bv'], HI)
        p['wk'] = _dot(p['tinv'], p['bk'], HI)

    s_cat = st_scr[...]
    for ci, p in enumerate(prep):
        rows = slice(ci * C, (ci + 1) * C)
        v_new = p['u'] - own_block(_dot(p['wk'], s_cat))
        o = _dot(p['att'], v_new) + own_block(_dot(p['qe'], s_cat))
        v_wide = jnp.concatenate([jnp.where(rhead == h, v_new, 0.0) for h in range(nh)], axis=1)
        s_cat = s_cat * p['dec_row'] + _dot_tn(p['kd'], v_wide)
        o = o * lax.rsqrt(jnp.mean(o * o, -1, keepdims=True) + NORM_EPS) * ng_ref[...]
        for h in range(nh):
            gate = proj_ref[rows, cwid + h * dv:cwid + (h + 1) * dv]
            o_ref[rows, h * dv:(h + 1) * dv] = o[h * C:(h + 1) * C, :] * jax.nn.silu(gate)
    st_scr[...] = s_cat

    @pl.when(t < nt - 1)
    def _():
        cbuf[pre:8, :] = cbuf[pre + tb:8 + tb, :]

    @pl.when(t == nt - 1)
    def _():
        for h in range(nh):
            sout_ref[0, h] = st_scr[:, h * dv:(h + 1) * dv]
        last = t_valid - (nt - 1) * tb
        convout_ref[0] = cbuf[pre + last:8 + last, :]


def _gdn_scan(proj, s0, conv0, cw, ng, alog, dtb, *, nb, nt, tb, t_valid):
    _, nh, dk, dv = s0.shape
    wp = proj.shape[1]
    cwid = 2 * nh * dk + nh * dv
    body = functools.partial(_gdn_body, nh=nh, dk=dk, dv=dv, tb=tb, t_valid=t_valid, nt=nt)
    const = lambda b, t: (0, 0)
    return pl.pallas_call(
        body, grid=(nb, nt),
        in_specs=[pl.BlockSpec((tb, wp), lambda b, t: (b * nt + t, 0)),
                  pl.BlockSpec((1, nh, dk, dv), lambda b, t: (b, 0, 0, 0)),
                  pl.BlockSpec((1, CONV_W - 1, cwid), lambda b, t: (b, 0, 0)),
                  pl.BlockSpec(cw.shape, const), pl.BlockSpec(ng.shape, const),
                  pl.BlockSpec(alog.shape, const), pl.BlockSpec(dtb.shape, const)],
        out_specs=[pl.BlockSpec((tb, nh * dv), lambda b, t: (b * nt + t, 0)),
                   pl.BlockSpec((1, nh, dk, dv), lambda b, t: (b, 0, 0, 0)),
                   pl.BlockSpec((1, CONV_W - 1, cwid), lambda b, t: (b, 0, 0))],
        out_shape=[jax.ShapeDtypeStruct((nb * nt * tb, nh * dv), F32), jax.ShapeDtypeStruct(s0.shape, F32),
                   jax.ShapeDtypeStruct(conv0.shape, F32)],
        scratch_shapes=[pltpu.VMEM((dk, nh * dv), F32), pltpu.VMEM((tb + 8, cwid), F32), pltpu.VMEM((tb, cwid), F32)],
        compiler_params=_cparams(("parallel", "arbitrary")), name="gdn_scan",
    )(proj, s0, conv0, cw, ng, alog, dtb)


def _mlstm_body(proj_ref, c0_ref, n0_ref, m0_ref, ng_ref, bi_ref, bf_ref, o_ref, cout_ref, nout_ref, mout_ref,
                ct_scr, n_scr, m_scr, *, nh, dk, dv, tb, t_valid, nt):
    t = pl.program_id(1)

    @pl.when(t == 0)
    def _():
        for h in range(nh):
            ct_scr[:, h * dv:(h + 1) * dv] = c0_ref[0, h]
            n_scr[h] = n0_ref[0, h:h + 1, :]
            m_scr[h] = jnp.broadcast_to(m0_ref[0, :, h:h + 1], (1, LANE))

    base = 2 * nh * dk + 2 * nh * dv
    nr = nh * C
    ri = _iota((nr, 1), 0)
    cj = _iota((1, nr), 1)
    rhead = ri // C
    incl = (rhead == cj // C) & (ri % C >= cj % C)

    def stack(rows, col0, width):
        return jnp.concatenate([proj_ref[rows, col0 + h * width:col0 + (h + 1) * width] for h in range(nh)], axis=0)

    def per_head_col(vals):
        return jnp.concatenate([jnp.broadcast_to(x, (C, 1)) for x in vals], axis=0)

    prep = []
    for ci in range(tb // C):
        rows = slice(ci * C, (ci + 1) * C)
        valid = _valid_rows(t, ci * C, tb, t_valid, nt)
        small = proj_ref[rows, base:base + LANE]
        li_all = small + bi_ref[...]
        lf_all = jax.nn.log_sigmoid(small + bf_ref[...])
        if valid is not None:
            lf_all = jnp.where(valid, lf_all, 0.0)
        b_all = _cumsum_rows(lf_all)
        b_t = _rows_to_cols(b_all)
        li_t = _rows_to_cols(li_all)
        bcol = jnp.concatenate([b_all[:, nh + h:nh + h + 1] for h in range(nh)], axis=0)
        licol = jnp.concatenate([li_all[:, h:h + 1] for h in range(nh)], axis=0)
        xrow = jnp.concatenate([b_t[nh + h:nh + h + 1, :] - li_t[h:h + 1, :] for h in range(nh)], axis=1)
        if valid is not None:
            xrow = jnp.where(t * tb + ci * C + cj % C < t_valid, xrow, jnp.inf)
            licol = jnp.where(t * tb + ci * C + ri % C < t_valid, licol, -jnp.inf)
        q = stack(rows, 0, dk)
        k = stack(rows, nh * dk, dk) * dk ** -0.5
        v = stack(rows, 2 * nh * dk, dv)
        dmat = jnp.where(incl, bcol - xrow, -jnp.inf)
        prep.append(dict(q=q, k=k, v=v, bcol=bcol, licol=licol, dmat=dmat, qk=_dot_nt(q, k),
                         rowmax=jnp.max(dmat, axis=1, keepdims=True),
                         b_end=[b_all[C - 1:C, nh + h:nh + h + 1] for h in range(nh)]))

    c_cat = ct_scr[...]
    n_rows = [n_scr[h] for h in range(nh)]
    m_vals = [m_scr[h][:, 0:1] for h in range(nh)]
    for ci, p in enumerate(prep):
        rows = slice(ci * C, (ci + 1) * C)
        q, k, v, bcol = p['q'], p['k'], p['v'], p['bcol']
        m_inter = bcol + per_head_col(m_vals)
        m_t = jnp.maximum(m_inter, p['rowmax'])
        a_int = jnp.exp(m_inter - m_t)
        qk = p['qk'] * jnp.exp(p['dmat'] - m_t)
        qc = _dot(q, c_cat)
        own = _tree(jnp.add, [jnp.where(rhead == h, qc[:, h * dv:(h + 1) * dv], 0.0) for h in range(nh)])
        num = _dot(qk, v) + a_int * own
        n_st = jnp.concatenate([jnp.broadcast_to(n_rows[h], (C, dk)) for h in range(nh)], axis=0)
        den = jnp.sum(qk, axis=1, keepdims=True) + a_int * jnp.sum(q * n_st, axis=1, keepdims=True)
        hid = num / jnp.maximum(jnp.abs(den), jnp.exp(-m_t))
        last = [slice(h * C + C - 1, h * C + C) for h in range(nh)]
        m_vals = [m_t[r, :] for r in last]
        a_end = [a_int[r, :] for r in last]
        wk = jnp.exp(per_head_col(p['b_end']) - bcol + p['licol'] - per_head_col(m_vals)) * k
        v_wide = jnp.concatenate([jnp.where(rhead == h, v, 0.0) for h in range(nh)], axis=1)
        a_row = jnp.concatenate([jnp.broadcast_to(a, (1, dv)) for a in a_end], axis=1)
        c_cat = c_cat * a_row + _dot_tn(wk, v_wide)
        n_rows = [a_end[h] * n_rows[h] + jnp.sum(wk[h * C:(h + 1) * C, :], axis=0, keepdims=True) for h in range(nh)]
        hc = hid - jnp.mean(hid, -1, keepdims=True)
        hn = hc * lax.rsqrt(jnp.mean(hc * hc, -1, keepdims=True) + NORM_EPS)
        for h in range(nh):
            og = jax.nn.sigmoid(proj_ref[rows, 2 * nh * dk + nh * dv + h * dv:2 * nh * dk + nh * dv + (h + 1) * dv])
            o_ref[rows, h * dv:(h + 1) * dv] = og * (hn[h * C:(h + 1) * C, :] * ng_ref[:, h * dv:(h + 1) * dv])
    ct_scr[...] = c_cat
    for h in range(nh):
        n_scr[h] = n_rows[h]
        m_scr[h] = jnp.broadcast_to(m_vals[h], (1, LANE))

    @pl.when(t == nt - 1)
    def _():
        lane = _iota((1, nh), 1)
        mrow = jnp.zeros((1, nh), F32)
        for h in range(nh):
            cout_ref[0, h] = ct_scr[:, h * dv:(h + 1) * dv]
            nout_ref[0, h:h + 1, :] = n_scr[h]
            mrow = jnp.where(lane == h, m_scr[h][:, 0:1], mrow)
        mout_ref[0] = mrow


def _mlstm_scan(proj, c0, n0, m0, ng, bi, bf, *, nb, nt, tb, t_valid):
    _, nh, dk, dv = c0.shape
    wp = proj.shape[1]
    body = functools.partial(_mlstm_body, nh=nh, dk=dk, dv=dv, tb=tb, t_valid=t_valid, nt=nt)
    const = lambda b, t: (0, 0)
    m0 = m0.reshape(nb, 1, nh)
    return pl.pallas_call(
        body, grid=(nb, nt),
        in_specs=[pl.BlockSpec((tb, wp), lambda b, t: (b * nt + t, 0)),
                  pl.BlockSpec((1, nh, dk, dv), lambda b, t: (b, 0, 0, 0)),
                  pl.BlockSpec((1, nh, dk), lambda b, t: (b, 0, 0)),
                  pl.BlockSpec((1, 1, nh), lambda b, t: (b, 0, 0)),
                  pl.BlockSpec(ng.shape, const), pl.BlockSpec(bi.shape, const), pl.BlockSpec(bf.shape, const)],
        out_specs=[pl.BlockSpec((tb, nh * dv), lambda b, t: (b * nt + t, 0)),
                   pl.BlockSpec((1, nh, dk, dv), lambda b, t: (b, 0, 0, 0)),
                   pl.BlockSpec((1, nh, dk), lambda b, t: (b, 0, 0)),
                   pl.BlockSpec((1, 1, nh), lambda b, t: (b, 0, 0))],
        out_shape=[jax.ShapeDtypeStruct((nb * nt * tb, nh * dv), F32), jax.ShapeDtypeStruct(c0.shape, F32),
                   jax.ShapeDtypeStruct(n0.shape, F32), jax.ShapeDtypeStruct((nb, 1, nh), F32)],
        scratch_shapes=[pltpu.VMEM((dk, nh * dv), F32), pltpu.VMEM((nh, 1, dk), F32), pltpu.VMEM((nh, 1, LANE), F32)],
        compiler_params=_cparams(("parallel", "arbitrary")), name="mlstm_scan",
    )(proj, c0, n0, m0, ng, bi, bf)


SUB = 8


def _tree(op, xs):
    xs = list(xs)
    while len(xs) > 1:
        xs = [op(xs[i], xs[i + 1]) if i + 1 < len(xs) else xs[i] for i in range(0, len(xs), 2)]
    return xs[0]


def _all_sublanes(op, m):
    for shift in (4, 2, 1):
        m = op(m, pltpu.roll(m, shift, axis=0))
    return m


def _sorting_network(n):
    def merge(lo, hi, r):
        step = r * 2
        if step < hi - lo:
            yield from merge(lo, hi, step)
            yield from merge(lo + r, hi, step)
            yield from [(i, i + r) for i in range(lo + r, hi - r, step)]
        else:
            yield (lo, lo + r)

    def sort(lo, hi):
        if hi - lo >= 1:
            mid = lo + (hi - lo) // 2
            yield from sort(lo, mid)
            yield from sort(mid + 1, hi)
            yield from merge(lo, hi, 1)

    return list(sort(0, n - 1))


def _sort_tiles_desc(tiles):
    tiles = list(tiles)
    for i, j in _sorting_network(len(tiles)):
        tiles[i], tiles[j] = jnp.maximum(tiles[i], tiles[j]), jnp.minimum(tiles[i], tiles[j])
    return tiles


def _merge_top(lists, singles=None):
    lists = list(lists)
    vals = []
    for j in range(PEER_TOPK):
        head = lists[0] if singles is None else jnp.maximum(lists[0], singles)
        m = _all_sublanes(jnp.maximum, head)
        vals.append(m)
        if j == PEER_TOPK - 1:
            break
        hit = lists[0] == m
        for i in range(min(len(lists) - 1, PEER_TOPK - 1 - j)):
            lists[i] = jnp.where(hit, lists[i + 1], lists[i])
        if singles is not None:
            singles = jnp.where(singles == m, -jnp.inf, singles)
    return vals


def _route_body(xt_ref, wq_ref, k1_ref, k2_ref, r2_ref, c1_ref, a1_ref, a2_ref, q_scr, *, tm):
    q_scr[...] = _dot(wq_ref[...], xt_ref[...].astype(BF16))
    nsub = tm // LANE
    ntile = PEER_NKEYS // SUB
    sub = _iota((SUB, LANE), 0)

    def head(h, carry):
        for sb in range(nsub):
            route_tile(h, slice(sb * LANE, (sb + 1) * LANE))
        return carry

    def route_tile(h, lanes):
        q1 = q_scr[pl.ds(pl.multiple_of(h * 2 * PEER_HALF, PEER_HALF), PEER_HALF), lanes]
        q2 = q_scr[pl.ds(pl.multiple_of(h * 2 * PEER_HALF + PEER_HALF, PEER_HALF), PEER_HALF), lanes]
        s1 = _dot(k1_ref[...], q1, HI)
        s2 = _dot(k2_ref[...], q2, HI)
        t1 = [s1[k * SUB:(k + 1) * SUB, :] for k in range(ntile)]
        t2 = [s2[k * SUB:(k + 1) * SUB, :] for k in range(ntile)]
        v1 = _merge_top(_sort_tiles_desc(t1))
        v2 = _merge_top(_sort_tiles_desc(t2))
        v2lo, v2hi = v2[SUB - 1], v2[2 * SUB - 1]
        for b in range(SUB - 2, -1, -1):
            v2lo = jnp.where(sub == b, v2[b], v2lo)
            v2hi = jnp.where(sub == b, v2[SUB + b], v2hi)
        cand_lo = [jnp.where(sub < min(PEER_TOPK // (a + 1), SUB), v1[a] + v2lo, -jnp.inf) for a in range(PEER_TOPK)]
        cand_hi = v1[0] + v2hi
        tau = _merge_top(cand_lo, cand_hi)[PEER_TOPK - 1]
        top = v1[0] + v2[0]
        z = _all_sublanes(jnp.add, _tree(jnp.add, [jnp.where(cd >= tau, jnp.exp(cd - top), 0.0)
                                                   for cd in cand_lo + [cand_hi]]))
        inv_z = 1.0 / z
        pairs = [_all_sublanes(jnp.add, jnp.where(cd >= tau, 1.0, 0.0)) for cd in cand_lo]
        pairs[0] = pairs[0] + _all_sublanes(jnp.add, jnp.where(cand_hi >= tau, 1.0, 0.0))
        rank2, c1 = [], []
        for k in range(ntile):
            r = jnp.full((SUB, LANE), float(PEER_TOPK), F32)
            cnt = jnp.zeros((SUB, LANE), F32)
            for j in range(PEER_TOPK - 1, -1, -1):
                r = jnp.where(t2[k] >= v2[j], float(j), r)
                cnt = jnp.where(t1[k] == v1[j], pairs[j], cnt)
            rank2.append(r)
            c1.append(cnt)
        r2_ref[h, :, lanes] = jnp.concatenate(rank2, axis=0).astype(BF16)
        c1_ref[h, :, lanes] = jnp.concatenate(c1, axis=0)
        a1_ref[h, :, lanes] = jnp.concatenate([jnp.exp(t - v1[0]) for t in t1], axis=0)
        a2_ref[h, :, lanes] = jnp.concatenate([jnp.exp(t - v2[0]) * inv_z for t in t2], axis=0).astype(BF16)

    lax.fori_loop(0, PEER_HEADS, head, 0)


def _peer_route(xt, wq_t, k1, k2, tm):
    d, n = xt.shape
    ospec = pl.BlockSpec((PEER_HEADS, PEER_NKEYS, tm), lambda i: (0, 0, i))
    shp = lambda dt: jax.ShapeDtypeStruct((PEER_HEADS, PEER_NKEYS, n), dt)
    return pl.pallas_call(
        functools.partial(_route_body, tm=tm), grid=(n // tm,),
        in_specs=[pl.BlockSpec((d, tm), lambda i: (0, i)), pl.BlockSpec(wq_t.shape, lambda i: (0, 0)),
                  pl.BlockSpec(k1.shape, lambda i: (0, 0)), pl.BlockSpec(k2.shape, lambda i: (0, 0))],
        out_specs=[ospec] * 4, out_shape=[shp(BF16), shp(F32), shp(F32), shp(BF16)],
        scratch_shapes=[pltpu.VMEM((wq_t.shape[0], tm), F32)],
        compiler_params=_cparams(("parallel",)), name="peer_route")(xt, wq_t, k1, k2)


def _peer_body(xt_ref, r2_ref, c1_ref, a1_ref, a2_ref, u_ref, vt_ref, g_ref, b_ref, out_ref, acc, xbf, p_scr):
    c = pl.program_id(1)

    @pl.when(c == 0)
    def _():
        xbf[...] = xt_ref[...].astype(BF16)
        acc[...] = jnp.zeros(acc.shape, F32)
        p_scr[...] = jnp.zeros(p_scr.shape, BF16)

    tm = xbf.shape[1]
    rows16 = 2 * SUB
    sub_w = PEER_SUB_I1 * PEER_NKEYS
    total = None
    for s in range(PEER_I1_PER_STEP // PEER_SUB_I1):
        es = slice(s * sub_w, (s + 1) * sub_w)
        part = _dot(vt_ref[0, :, es], p_scr[es, :])
        total = part if total is None else total + part
        hid = _dot(u_ref[es, :], xbf[...])
        act = (0.5 * hid * (1.0 + lax.erf(hid * 0.5 ** 0.5))).astype(BF16)
        cols = []
        for l0 in range(0, tm, PEER_LANE_BLOCK):
            ls = slice(l0, l0 + PEER_LANE_BLOCK)
            tiles = []
            for j in range(s * PEER_SUB_I1, (s + 1) * PEER_SUB_I1):
                c1b = [jnp.broadcast_to(c1_ref[h, j:j + 1, ls], (rows16, PEER_LANE_BLOCK)).astype(BF16)
                       for h in range(PEER_HEADS)]
                a1b = [jnp.broadcast_to(a1_ref[h, j:j + 1, ls], (rows16, PEER_LANE_BLOCK)).astype(BF16)
                       for h in range(PEER_HEADS)]
                for kb in range(PEER_NKEYS // rows16):
                    ks = slice(kb * rows16, (kb + 1) * rows16)
                    tiles.append(_tree(jnp.add, [jnp.where(r2_ref[h, ks, ls] < c1b[h], a2_ref[h, ks, ls] * a1b[h], 0.0)
                                                 for h in range(PEER_HEADS)]))
            cols.append(jnp.concatenate(tiles, axis=0))
        p_scr[es, :] = jnp.concatenate(cols, axis=1) * act
    acc[...] += total

    @pl.when(c == pl.num_programs(1) - 1)
    def _():
        z = DN_ALPHA * xt_ref[...] + acc[...]
        zc = z - jnp.mean(z, 0, keepdims=True)
        var = jnp.mean(zc * zc, 0, keepdims=True)
        out_ref[...] = (zc * lax.rsqrt(var + LN_EPS) * g_ref[...] + b_ref[...]).T


def _peer_main(xt, r2, c1, a1, a2, u, vt, g, b, tm):
    d, n = xt.shape
    ec = PEER_I1_PER_STEP * PEER_NKEYS
    nc = u.shape[0] // ec
    build = lambda c: jnp.minimum(c, nc - 1)
    mult = lambda c: jnp.maximum(c - 1, 0)
    k2spec = pl.BlockSpec((PEER_HEADS, PEER_NKEYS, tm), lambda i, c: (0, 0, i))
    k1spec = pl.BlockSpec((PEER_HEADS, PEER_I1_PER_STEP, tm), lambda i, c: (0, build(c), i))
    return pl.pallas_call(
        _peer_body, grid=(n // tm, nc + 1),
        in_specs=[pl.BlockSpec((d, tm), lambda i, c: (0, i)), k2spec, k1spec, k1spec, k2spec,
                  pl.BlockSpec((ec, d), lambda i, c: (build(c), 0)),
                  pl.BlockSpec((1, d, ec), lambda i, c: (mult(c), 0, 0)),
                  pl.BlockSpec((d, 1), lambda i, c: (0, 0)), pl.BlockSpec((d, 1), lambda i, c: (0, 0))],
        out_specs=pl.BlockSpec((tm, d), lambda i, c: (i, 0)),
        out_shape=jax.ShapeDtypeStruct((n, d), F32),
        scratch_shapes=[pltpu.VMEM((d, tm), F32), pltpu.VMEM((d, tm), BF16), pltpu.VMEM((ec, tm), BF16)],
        compiler_params=_cparams(("parallel", "arbitrary")), name="peer_main",
    )(xt, r2, c1, a1, a2, u, vt, g.reshape(d, 1), b.reshape(d, 1))


def _pad_cols(w, width):
    return jnp.pad(w, ((0, 0), (0, width - w.shape[1])))


def _lane_row(*parts):
    v = jnp.concatenate([p.astype(F32).reshape(-1) for p in parts])
    return jnp.pad(v, (0, LANE - v.shape[0])).reshape(1, LANE)


def _pick_tb(t_pad):
    for tb in (48, 32, 16):
        if t_pad % tb == 0:
            return tb
    raise ValueError(t_pad)


def kernel(x_prompt, x_sample, state_hgrn_S, state_gdn_S, state_gdn_conv, state_mlstm_C, state_mlstm_n, state_mlstm_m, state_gla_S, meta_tokens, hgrn_w_in, hgrn_lb, hgrn_norm_g, hgrn_w_out, gdn_w_in, gdn_conv_w, gdn_a_log, gdn_dt_bias, gdn_norm_g, gdn_w_out, mlstm_w_in, mlstm_b_i, mlstm_b_f, mlstm_norm_g, mlstm_w_out, gla_w_in, gla_w_gate, gla_b_gate, gla_norm_g, gla_w_out, peer_w_q, peer_keys1, peer_keys2, peer_u, peer_v, ln_g, ln_b):
    bp, seq, d = x_prompt.shape
    bs, seq_s, _ = x_sample.shape
    assert d == D_MODEL and len(state_hgrn_S) == 1 and len(state_gdn_S) == 1
    assert len(state_mlstm_C) == 1 and len(state_gla_S) == 1
    tp = N_META + seq
    tp_pad = -(-tp // C) * C
    ts_pad = -(-seq_s // C) * C
    tbp = _pick_tb(tp_pad)
    tbs = _pick_tb(ts_pad)
    np_rows = bp * tp_pad
    n_real = np_rows + bs * seq_s

    meta = jnp.broadcast_to(meta_tokens.astype(F32)[None], (bp, N_META, d))
    hp = jnp.concatenate([meta, x_prompt], axis=1)
    hp = jnp.pad(hp, ((0, 0), (0, tp_pad - tp), (0, 0))).reshape(np_rows, d)
    h = jnp.concatenate([hp, x_sample.reshape(bs * seq_s, d)], axis=0)
    n = -(-n_real // TOK_MULTIPLE) * TOK_MULTIPLE
    h = jnp.pad(h, ((0, n - n_real), (0, 0)))
    tm_mm = MM_TOK_BLOCK

    def split_cols(w, sizes):
        out, o = [], 0
        for s in sizes:
            out.append(w[:, o:o + s])
            o += s
        return out

    def run_mixer(scan, proj, states_p, states_s, **kw):
        res_p = scan(proj, *states_p, nb=bp, nt=tp_pad // tbp, tb=tbp, t_valid=tp, **kw)
        ps = proj[np_rows:n_real].reshape(bs, seq_s, -1)
        ps = jnp.pad(ps, ((0, 0), (0, ts_pad - seq_s), (0, 0))).reshape(bs * ts_pad, -1)
        res_s = scan(ps, *states_s, nb=bs, nt=ts_pad // tbs, tb=tbs, t_valid=seq_s, **kw)
        o_s = res_s[0].reshape(bs, ts_pad, -1)[:, :seq_s].reshape(bs * seq_s, -1)
        o = jnp.concatenate([res_p[0], o_s], axis=0)
        o = jnp.pad(o, ((0, n - n_real), (0, 0)))
        return o, res_p[1:], res_s[1:]

    lb_all = jnp.cumsum(jax.nn.softmax(hgrn_lb.astype(F32), axis=0), axis=0)
    outs = {}
    for i in range(DEPTH):
        mix = i % 4
        if mix == 0:
            w_in = hgrn_w_in[0].astype(BF16)
            proj = _mm(h, w_in, tm_mm, 1024)
            nh, dk, dv = state_hgrn_S.shape[2:]
            zero = jnp.zeros((bp, nh, dk, dv), F32)
            scan = functools.partial(_gla_scan, ng=hgrn_norm_g[0].reshape(1, dv), p1=lb_all[i].reshape(1, nh * dk),
                                     p2=jnp.zeros((1, LANE), F32), hgrn=True)
            o, (sp,), (ss,) = run_mixer(scan, proj, (zero,), (state_hgrn_S[0],))
            outs['hgrn'] = (sp[None], ss[None])
            w_out = hgrn_w_out[0]
        elif mix == 1:
            nh, dk, dv = state_gdn_S.shape[2:]
            cwid = 2 * nh * dk + nh * dv
            qkv, a, b, g = split_cols(gdn_w_in[0], [cwid, nh, nh, nh * dv])
            w_in = jnp.concatenate([qkv, g, _pad_cols(jnp.concatenate([a, b], axis=1), LANE)], axis=1).astype(BF16)
            proj = _mm(h, w_in, tm_mm, w_in.shape[1] // 3)
            scan = functools.partial(_gdn_scan, cw=gdn_conv_w[0], ng=gdn_norm_g[0].reshape(1, dv),
                                     alog=_lane_row(gdn_a_log[0]), dtb=_lane_row(gdn_dt_bias[0]))
            o, (sp, cp), (ss, cs) = run_mixer(
                scan, proj,
                (jnp.zeros((bp, nh, dk, dv), F32), jnp.zeros((bp, CONV_W - 1, cwid), F32)),
                (state_gdn_S[0], state_gdn_conv[0]))
            outs['gdn'] = (sp[None], cp[None], ss[None], cs[None])
            w_out = gdn_w_out[0]
        elif mix == 2:
            nh, dk, dv = state_mlstm_C.shape[2:]
            q, k, v, ig, fg, og = split_cols(mlstm_w_in[0], [nh * dk, nh * dk, nh * dv, nh, nh, nh * dv])
            w_in = jnp.concatenate([q, k, v, og, _pad_cols(jnp.concatenate([ig, fg], axis=1), LANE)],
                                   axis=1).astype(BF16)
            proj = _mm(h, w_in, tm_mm, w_in.shape[1] // 3)
            zeros_nh = jnp.zeros((nh,), F32)
            scan = functools.partial(_mlstm_scan, ng=mlstm_norm_g[0].reshape(1, nh * dv),
                                     bi=_lane_row(mlstm_b_i[0]), bf=_lane_row(zeros_nh, mlstm_b_f[0]))
            o, (cp, npp, mp), (cs, ns, ms) = run_mixer(
                scan, proj,
                (jnp.zeros((bp, nh, dk, dv), F32), jnp.zeros((bp, nh, dk), F32), jnp.zeros((bp, nh), F32)),
                (state_mlstm_C[0], state_mlstm_n[0], state_mlstm_m[0]))
            outs['mlstm'] = (cp[None], npp[None], mp.reshape(1, bp, nh), cs[None], ns[None], ms.reshape(1, bs, nh))
            w_out = mlstm_w_out[0]
        else:
            nh, dk, dv = state_gla_S.shape[2:]
            rank = gla_w_gate.shape[1]
            q, k, v, g, r = split_cols(gla_w_in[0], [nh * dk, nh * dk, nh * dv, nh * dv, rank])
            w_in = jnp.concatenate([q, k, v, g, _pad_cols(r, LANE)], axis=1).astype(BF16)
            proj = _mm(h, w_in, tm_mm, w_in.shape[1] // 5)
            wg = jnp.pad(gla_w_gate[0].astype(F32), ((0, LANE - rank), (0, 0)))
            scan = functools.partial(_gla_scan, ng=gla_norm_g[0].reshape(1, dv), p1=wg,
                                     p2=gla_b_gate[0].reshape(1, nh * dk), hgrn=False)
            o, (sp,), (ss,) = run_mixer(scan, proj, (jnp.zeros((bp, nh, dk, dv), F32),), (state_gla_S[0],))
            outs['gla'] = (sp[None], ss[None])
            w_out = gla_w_out[0]

        h1t = _mm_res_ln_t(o, w_out.astype(BF16), h, ln_g[i, 0], ln_b[i, 0], TOK_BLOCK)
        r2, c1, a1, a2 = _peer_route(h1t, peer_w_q[i].T.astype(BF16), peer_keys1[i], peer_keys2[i], PEER_TOK_BLOCK)
        ec = PEER_I1_PER_STEP * PEER_NKEYS
        vt = peer_v[i].astype(BF16).reshape(-1, ec, d).transpose(0, 2, 1)
        h = _peer_main(h1t, r2, c1, a1, a2, peer_u[i].astype(BF16), vt, ln_g[i, 1], ln_b[i, 1], PEER_TOK_BLOCK)

    y_prompt = h[:np_rows].reshape(bp, tp_pad, d)[:, N_META:tp]
    y_sample = h[np_rows:n_real].reshape(bs, seq_s, d)
    hg, gd, ml, gl = outs['hgrn'], outs['gdn'], outs['mlstm'], outs['gla']
    return (y_prompt, y_sample, hg[0], gd[0], gd[1], ml[0], ml[1], ml[2], gl[0],
            hg[1], gd[2], gd[3], ml[3], ml[4], ml[5], gl[1])
```

```python
import functools

import jax
import jax.numpy as jnp
from jax import lax
from jax.experimental import pallas as pl
from jax.experimental.pallas import tpu as pltpu

F32 = jnp.float32
BF16 = jnp.bfloat16
HI = lax.Precision.HIGHEST

D_MODEL = 1024
DEPTH = 4
N_META = 16
C = 16
CONV_W = 4
NORM_EPS = 1e-6
LN_EPS = 1e-5
DN_ALPHA = (2 * DEPTH) ** 0.25
GLA_GATE_NORM = 16.0
LANE = 128
PEER_HEADS = 8
PEER_NKEYS = 128
PEER_TOPK = 16
PEER_HALF = 128
PEER_I1_PER_STEP = 8
TOK_BLOCK = 256
PEER_TOK_BLOCK = 512
PEER_LANE_BLOCK = 256
MM_TOK_BLOCK = 768
TOK_MULTIPLE = 1536
VMEM_LIMIT = 48 * 1024 * 1024


def _cparams(sem):
    return pltpu.CompilerParams(dimension_semantics=sem, vmem_limit_bytes=VMEM_LIMIT)


def _iota(shape, dim):
    return lax.broadcasted_iota(jnp.int32, shape, dim)


def _dot_nt(a, b, precision=None):
    return lax.dot_general(a, b, (((1,), (1,)), ((), ())), precision=precision, preferred_element_type=F32)


def _dot_tn(a, b, precision=None):
    return lax.dot_general(a, b, (((0,), (0,)), ((), ())), precision=precision, preferred_element_type=F32)


def _dot(a, b, precision=None):
    return jnp.dot(a, b, precision=precision, preferred_element_type=F32)


def _split(x):
    hi = x.astype(BF16)
    return hi, (x - hi.astype(F32)).astype(BF16)


def _dot3(a, b, nt=False):
    (ah, al), (bh, bl) = a, b
    d = _dot_nt if nt else _dot
    return d(ah, bh) + d(ah, bl) + d(al, bh)


def _cumsum_rows(x):
    tri = (_iota((C, C), 0) >= _iota((C, C), 1)).astype(F32)
    return _dot(tri, x, HI)


def _rows_to_cols(x):
    eye = (_iota((LANE, LANE), 0) == _iota((LANE, LANE), 1)).astype(F32)
    return _dot_nt(eye, x, HI)


def _incl_mask():
    return _iota((C, C), 0) >= _iota((C, C), 1)


def _mm_body(x_ref, w_ref, o_ref):
    o_ref[...] = _dot(x_ref[...].astype(BF16), w_ref[...])


def _mm(x, w, tm, tn):
    m, k = x.shape
    n = w.shape[1]
    return pl.pallas_call(
        _mm_body, grid=(m // tm, n // tn),
        in_specs=[pl.BlockSpec((tm, k), lambda i, j: (i, 0)), pl.BlockSpec((k, tn), lambda i, j: (0, j))],
        out_specs=pl.BlockSpec((tm, tn), lambda i, j: (i, j)),
        out_shape=jax.ShapeDtypeStruct((m, n), F32),
        compiler_params=_cparams(("parallel", "parallel")), name="mm")(x, w)


def _mm_ln_body(o_ref, w_ref, h_ref, g_ref, b_ref, out_ref):
    y = _dot(o_ref[...].astype(BF16), w_ref[...])
    z = DN_ALPHA * h_ref[...] + y
    zc = z - jnp.mean(z, -1, keepdims=True)
    var = jnp.mean(zc * zc, -1, keepdims=True)
    out_ref[...] = (zc * lax.rsqrt(var + LN_EPS) * g_ref[...] + b_ref[...]).T


def _mm_res_ln_t(o, w, h, g, b, tm):
    n, k = o.shape
    d = w.shape[1]
    return pl.pallas_call(
        _mm_ln_body, grid=(n // tm,),
        in_specs=[pl.BlockSpec((tm, k), lambda i: (i, 0)), pl.BlockSpec((k, d), lambda i: (0, 0)),
                  pl.BlockSpec((tm, d), lambda i: (i, 0)), pl.BlockSpec((1, d), lambda i: (0, 0)),
                  pl.BlockSpec((1, d), lambda i: (0, 0))],
        out_specs=pl.BlockSpec((d, tm), lambda i: (0, i)),
        out_shape=jax.ShapeDtypeStruct((d, n), F32),
        compiler_params=_cparams(("parallel",)), name="mm_res_ln")(o, w, h, g.reshape(1, d), b.reshape(1, d))


def _valid_rows(t, r0, tb, t_valid, nt):
    if t_valid == nt * tb:
        return None
    return (t * tb + r0 + _iota((C, 1), 0)) < t_valid


def _gla_body(proj_ref, s0_ref, ng_ref, p1_ref, p2_ref, o_ref, sout_ref, st_scr, *, hgrn, nh, dk, dv, tb, t_valid,
              nt):
    t = pl.program_id(1)

    @pl.when(t == 0)
    def _():
        for h in range(nh):
            st_scr[h] = s0_ref[0, h].T

    kw = nh * dk
    vw = nh * dv
    nr = nh * C
    ri = _iota((nr, 1), 0)
    cj = _iota((1, nr), 1)
    rhead = ri // C
    tri = ((rhead == cj // C) & (ri % C >= cj % C)).astype(F32)
    row3 = _iota((1, C, 1), 1)

    def stack(rows, col0, width):
        return jnp.concatenate([proj_ref[rows, col0 + h * width:col0 + (h + 1) * width] for h in range(nh)], axis=0)

    prep = []
    for ci in range(tb // C):
        rows = slice(ci * C, (ci + 1) * C)
        if hgrn:
            q = jax.nn.silu(stack(rows, 0, dk))
            lb = jnp.concatenate([jnp.broadcast_to(p1_ref[:, h * dk:(h + 1) * dk], (C, dk)) for h in range(nh)], axis=0)
            fg = lb + (1.0 - lb) * jax.nn.sigmoid(stack(rows, kw, dk))
            k = 1.0 - fg
            g = jnp.log(fg)
        else:
            r = proj_ref[rows, 2 * kw + 2 * vw:2 * kw + 2 * vw + LANE]
            logf_all = jax.nn.log_sigmoid(_dot(r, p1_ref[...], HI) + p2_ref[...]) / GLA_GATE_NORM
            q = stack(rows, 0, dk) * dk ** -0.5
            k = stack(rows, kw, dk)
            g = jnp.concatenate([logf_all[:, h * dk:(h + 1) * dk] for h in range(nh)], axis=0)
        v = stack(rows, 2 * kw, dv)
        if t_valid != nt * tb:
            valid = t * tb + ci * C + ri % C < t_valid
            k = jnp.where(valid, k, 0.0)
            g = jnp.where(valid, g, 0.0)
        b = _dot(tri, g, HI)
        b3, q3, k3, v3 = (x.reshape(nh, C, x.shape[-1]) for x in (b, q, k, v))
        o3 = jnp.zeros((nh, C, dv), F32)
        for s in range(C):
            d = jnp.exp(jnp.where(row3 >= s, b3 - b3[:, s:s + 1, :], -jnp.inf))
            col = jnp.sum(q3 * (k3[:, s:s + 1, :] * d), axis=-1, keepdims=True)
            o3 = o3 + col * v3[:, s:s + 1, :]
        b_end = b3[:, C - 1:C, :]
        prep.append(dict(o=o3.reshape(nr, dv), qe=q * jnp.exp(b), kd=(k3 * jnp.exp(b_end - b3)).reshape(nr, dk),
                         dec=jnp.exp(b_end), v=v))

    st = st_scr[...]
    for ci, p in enumerate(prep):
        rows = slice(ci * C, (ci + 1) * C)
        inter = _dot_nt(p['qe'], st.reshape(nh * dv, dk))
        o = p['o'] + _tree(jnp.add, [jnp.where(rhead == h, inter[:, h * dv:(h + 1) * dv], 0.0) for h in range(nh)])
        v_wide = jnp.concatenate([jnp.where(rhead == h, p['v'], 0.0) for h in range(nh)], axis=1)
        st = st * p['dec'] + _dot_tn(v_wide, p['kd']).reshape(nh, dv, dk)
        o = o * lax.rsqrt(jnp.mean(o * o, -1, keepdims=True) + NORM_EPS) * ng_ref[...]
        for h in range(nh):
            gate = proj_ref[rows, 2 * kw + vw + h * dv:2 * kw + vw + (h + 1) * dv]
            o_ref[rows, h * dv:(h + 1) * dv] = o[h * C:(h + 1) * C, :] * jax.nn.silu(gate)
    st_scr[...] = st

    @pl.when(t == nt - 1)
    def _():
        for h in range(nh):
            sout_ref[0, h] = st_scr[h].T


def _gla_scan(proj, s0, ng, p1, p2, *, hgrn, nb, nt, tb, t_valid):
    _, nh, dk, dv = s0.shape
    wp = proj.shape[1]
    body = functools.partial(_gla_body, hgrn=hgrn, nh=nh, dk=dk, dv=dv, tb=tb, t_valid=t_valid, nt=nt)
    return pl.pallas_call(
        body, grid=(nb, nt),
        in_specs=[pl.BlockSpec((tb, wp), lambda b, t: (b * nt + t, 0)),
                  pl.BlockSpec((1, nh, dk, dv), lambda b, t: (b, 0, 0, 0)),
                  pl.BlockSpec(ng.shape, lambda b, t: (0, 0)),
                  pl.BlockSpec(p1.shape, lambda b, t: (0, 0)),
                  pl.BlockSpec(p2.shape, lambda b, t: (0, 0))],
        out_specs=[pl.BlockSpec((tb, nh * dv), lambda b, t: (b * nt + t, 0)),
                   pl.BlockSpec((1, nh, dk, dv), lambda b, t: (b, 0, 0, 0))],
        out_shape=[jax.ShapeDtypeStruct((nb * nt * tb, nh * dv), F32), jax.ShapeDtypeStruct(s0.shape, F32)],
        scratch_shapes=[pltpu.VMEM((nh, dv, dk), F32)],
        compiler_params=_cparams(("parallel", "arbitrary")), name="hgrn_scan" if hgrn else "gla_scan",
    )(proj, s0, ng, p1, p2)


def _gdn_body(proj_ref, s0_ref, conv0_ref, cw_ref, ng_ref, alog_ref, dtb_ref, o_ref, sout_ref, convout_ref,
              st_scr, cbuf, ybuf, *, nh, dk, dv, tb, t_valid, nt):
    t = pl.program_id(1)
    cwid = 2 * nh * dk + nh * dv
    pre = 8 - (CONV_W - 1)

    @pl.when(t == 0)
    def _():
        for h in range(nh):
            st_scr[:, h * dv:(h + 1) * dv] = s0_ref[0, h]
        cbuf[0:8, :] = jnp.zeros((8, cwid), F32)
        cbuf[pre:8, :] = conv0_ref[0]

    cbuf[8:8 + tb, :] = proj_ref[:, 0:cwid]
    y = cbuf[pre:pre + tb, :] * cw_ref[0:1, :]
    for j in range(1, CONV_W):
        y = y + cbuf[pre + j:pre + j + tb, :] * cw_ref[j:j + 1, :]
    ybuf[...] = jax.nn.silu(y)

    nr = nh * C
    ri = _iota((nr, 1), 0)
    cj = _iota((1, nr), 1)
    rhead = ri // C
    same = rhead == cj // C
    incl = same & (ri % C >= cj % C)
    strict = same & (ri % C > cj % C)
    eye = (ri == cj).astype(F32)

    def stack(ref, rows, base, width):
        return jnp.concatenate([ref[rows, base + h * width:base + (h + 1) * width] for h in range(nh)], axis=0)

    def own_block(x):
        return _tree(jnp.add, [jnp.where(rhead == h, x[:, h * dv:(h + 1) * dv], 0.0) for h in range(nh)])

    prep = []
    for ci in range(tb // C):
        rows = slice(ci * C, (ci + 1) * C)
        valid = _valid_rows(t, ci * C, tb, t_valid, nt)
        small = proj_ref[rows, cwid + nh * dv:cwid + nh * dv + LANE]
        g_all = -jnp.exp(alog_ref[...]) * jax.nn.softplus(small + dtb_ref[...])
        beta_all = jax.nn.sigmoid(small)
        if valid is not None:
            g_all = jnp.where(valid, g_all, 0.0)
            beta_all = jnp.where(valid, beta_all, 0.0)
        b_all = _cumsum_rows(g_all)
        b_t = _rows_to_cols(b_all)
        bcol = jnp.concatenate([b_all[:, h:h + 1] for h in range(nh)], axis=0)
        brow = jnp.concatenate([b_t[h:h + 1, :] for h in range(nh)], axis=1)
        beta = jnp.concatenate([beta_all[:, nh + h:nh + h + 1] for h in range(nh)], axis=0)
        bend = jnp.concatenate([jnp.broadcast_to(b_all[C - 1:C, h:h + 1], (C, 1)) for h in range(nh)], axis=0)
        dec_row = jnp.concatenate([jnp.broadcast_to(jnp.exp(b_all[C - 1:C, h:h + 1]), (1, dv)) for h in range(nh)],
                                  axis=1)
        q = stack(ybuf, rows, 0, dk)
        k = stack(ybuf, rows, nh * dk, dk)
        v = stack(ybuf, rows, 2 * nh * dk, dv)
        q = q * lax.rsqrt(jnp.sum(q * q, -1, keepdims=True) + NORM_EPS) * dk ** -0.5
        k = k * lax.rsqrt(jnp.sum(k * k, -1, keepdims=True) + NORM_EPS)
        if valid is not None:
            valid_st = jnp.concatenate([valid] * nh, axis=0)
            q = jnp.where(valid_st, q, 0.0)
            k = jnp.where(valid_st, k, 0.0)
            v = jnp.where(valid_st, v, 0.0)
        decay = jnp.exp(jnp.where(incl, bcol - brow, -jnp.inf))
        eb = jnp.exp(bcol)
        ks = _split(k)
        prep.append(dict(a=jnp.where(strict, _dot3(ks, ks, nt=True) * decay * beta, 0.0), bv=beta * v,
                         bk=(beta * eb) * k, att=_dot_nt(q, k) * decay, qe=q * eb, kd=k * jnp.exp(bend - bcol),
                         dec_row=dec_row))
    for p in prep:
        a_s = _split(p['a'])
        p['a2'] = _dot3(a_s, a_s)
    for p in prep:
        a2_s = _split(p['a2'])
        p['a4'] = _dot3(a2_s, a2_s)
        p['lo'] = _dot3(_split(eye - p['a']), _split(eye + p['a2']))
    for p in prep:
        a4_s = _split(p['a4'])
        p['a8'] = _dot3(a4_s, a4_s)
    for p in prep:
        p['hi'] = _dot3(_split(eye + p['a4']), _split(eye + p['a8']))
    for p in prep:
        p['tinv'] = _split(_dot3(_split(p['lo']), _split(p['hi'])))
    for p in prep:
        p['u'] = _dot3(p['tinv'], _split(p['bv']))
        p['wk'] = _dot3(p['tinv'], _split(p['bk']))

    s_cat = st_scr[...]
    for ci, p in enumerate(prep):
        rows = slice(ci * C, (ci + 1) * C)
        v_new = p['u'] - own_block(_dot(p['wk'], s_cat))
        o = _dot(p['att'], v_new) + own_block(_dot(p['qe'], s_cat))
        v_wide = jnp.concatenate([jnp.where(rhead == h, v_new, 0.0) for h in range(nh)], axis=1)
        s_cat = s_cat * p['dec_row'] + _dot_tn(p['kd'], v_wide)
        o = o * lax.rsqrt(jnp.mean(o * o, -1, keepdims=True) + NORM_EPS) * ng_ref[...]
        for h in range(nh):
            gate = proj_ref[rows, cwid + h * dv:cwid + (h + 1) * dv]
            o_ref[rows, h * dv:(h + 1) * dv] = o[h * C:(h + 1) * C, :] * jax.nn.silu(gate)
    st_scr[...] = s_cat

    @pl.when(t < nt - 1)
    def _():
        cbuf[pre:8, :] = cbuf[pre + tb:8 + tb, :]

    @pl.when(t == nt - 1)
    def _():
        for h in range(nh):
            sout_ref[0, h] = st_scr[:, h * dv:(h + 1) * dv]
        last = t_valid - (nt - 1) * tb
        convout_ref[0] = cbuf[pre + last:8 + last, :]


def _gdn_scan(proj, s0, conv0, cw, ng, alog, dtb, *, nb, nt, tb, t_valid):
    _, nh, dk, dv = s0.shape
    wp = proj.shape[1]
    cwid = 2 * nh * dk + nh * dv
    body = functools.partial(_gdn_body, nh=nh, dk=dk, dv=dv, tb=tb, t_valid=t_valid, nt=nt)
    const = lambda b, t: (0, 0)
    return pl.pallas_call(
        body, grid=(nb, nt),
        in_specs=[pl.BlockSpec((tb, wp), lambda b, t: (b * nt + t, 0)),
                  pl.BlockSpec((1, nh, dk, dv), lambda b, t: (b, 0, 0, 0)),
                  pl.BlockSpec((1, CONV_W - 1, cwid), lambda b, t: (b, 0, 0)),
                  pl.BlockSpec(cw.shape, const), pl.BlockSpec(ng.shape, const),
                  pl.BlockSpec(alog.shape, const), pl.BlockSpec(dtb.shape, const)],
        out_specs=[pl.BlockSpec((tb, nh * dv), lambda b, t: (b * nt + t, 0)),
                   pl.BlockSpec((1, nh, dk, dv), lambda b, t: (b, 0, 0, 0)),
                   pl.BlockSpec((1, CONV_W - 1, cwid), lambda b, t: (b, 0, 0))],
        out_shape=[jax.ShapeDtypeStruct((nb * nt * tb, nh * dv), F32), jax.ShapeDtypeStruct(s0.shape, F32),
                   jax.ShapeDtypeStruct(conv0.shape, F32)],
        scratch_shapes=[pltpu.VMEM((dk, nh * dv), F32), pltpu.VMEM((tb + 8, cwid), F32), pltpu.VMEM((tb, cwid), F32)],
        compiler_params=_cparams(("parallel", "arbitrary")), name="gdn_scan",
    )(proj, s0, conv0, cw, ng, alog, dtb)


def _mlstm_body(proj_ref, c0_ref, n0_ref, m0_ref, ng_ref, bi_ref, bf_ref, o_ref, cout_ref, nout_ref, mout_ref,
                ct_scr, n_scr, m_scr, *, nh, dk, dv, tb, t_valid, nt):
    t = pl.program_id(1)

    @pl.when(t == 0)
    def _():
        for h in range(nh):
            ct_scr[:, h * dv:(h + 1) * dv] = c0_ref[0, h]
            n_scr[h] = n0_ref[0, h:h + 1, :]
            m_scr[h] = jnp.broadcast_to(m0_ref[0, :, h:h + 1], (1, LANE))

    base = 2 * nh * dk + 2 * nh * dv
    nr = nh * C
    ri = _iota((nr, 1), 0)
    cj = _iota((1, nr), 1)
    rhead = ri // C
    incl = (rhead == cj // C) & (ri % C >= cj % C)

    def stack(rows, col0, width):
        return jnp.concatenate([proj_ref[rows, col0 + h * width:col0 + (h + 1) * width] for h in range(nh)], axis=0)

    def per_head_col(vals):
        return jnp.concatenate([jnp.broadcast_to(x, (C, 1)) for x in vals], axis=0)

    prep = []
    for ci in range(tb // C):
        rows = slice(ci * C, (ci + 1) * C)
        valid = _valid_rows(t, ci * C, tb, t_valid, nt)
        small = proj_ref[rows, base:base + LANE]
        li_all = small + bi_ref[...]
        lf_all = jax.nn.log_sigmoid(small + bf_ref[...])
        if valid is not None:
            lf_all = jnp.where(valid, lf_all, 0.0)
        b_all = _cumsum_rows(lf_all)
        b_t = _rows_to_cols(b_all)
        li_t = _rows_to_cols(li_all)
        bcol = jnp.concatenate([b_all[:, nh + h:nh + h + 1] for h in range(nh)], axis=0)
        licol = jnp.concatenate([li_all[:, h:h + 1] for h in range(nh)], axis=0)
        xrow = jnp.concatenate([b_t[nh + h:nh + h + 1, :] - li_t[h:h + 1, :] for h in range(nh)], axis=1)
        if valid is not None:
            xrow = jnp.where(t * tb + ci * C + cj % C < t_valid, xrow, jnp.inf)
            licol = jnp.where(t * tb + ci * C + ri % C < t_valid, licol, -jnp.inf)
        q = stack(rows, 0, dk)
        k = stack(rows, nh * dk, dk) * dk ** -0.5
        v = stack(rows, 2 * nh * dk, dv)
        dmat = jnp.where(incl, bcol - xrow, -jnp.inf)
        prep.append(dict(q=q, k=k, v=v, bcol=bcol, licol=licol, dmat=dmat, qk=_dot_nt(q, k),
                         rowmax=jnp.max(dmat, axis=1, keepdims=True),
                         b_end=[b_all[C - 1:C, nh + h:nh + h + 1] for h in range(nh)]))

    c_cat = ct_scr[...]
    n_rows = [n_scr[h] for h in range(nh)]
    m_vals = [m_scr[h][:, 0:1] for h in range(nh)]
    for ci, p in enumerate(prep):
        rows = slice(ci * C, (ci + 1) * C)
        q, k, v, bcol = p['q'], p['k'], p['v'], p['bcol']
        m_inter = bcol + per_head_col(m_vals)
        m_t = jnp.maximum(m_inter, p['rowmax'])
        a_int = jnp.exp(m_inter - m_t)
        qk = p['qk'] * jnp.exp(p['dmat'] - m_t)
        qc = _dot(q, c_cat)
        own = _tree(jnp.add, [jnp.where(rhead == h, qc[:, h * dv:(h + 1) * dv], 0.0) for h in range(nh)])
        num = _dot(qk, v) + a_int * own
        n_st = jnp.concatenate([jnp.broadcast_to(n_rows[h], (C, dk)) for h in range(nh)], axis=0)
        den = jnp.sum(qk, axis=1, keepdims=True) + a_int * jnp.sum(q * n_st, axis=1, keepdims=True)
        hid = num / jnp.maximum(jnp.abs(den), jnp.exp(-m_t))
        last = [slice(h * C + C - 1, h * C + C) for h in range(nh)]
        m_vals = [m_t[r, :] for r in last]
        a_end = [a_int[r, :] for r in last]
        wk = jnp.exp(per_head_col(p['b_end']) - bcol + p['licol'] - per_head_col(m_vals)) * k
        v_wide = jnp.concatenate([jnp.where(rhead == h, v, 0.0) for h in range(nh)], axis=1)
        a_row = jnp.concatenate([jnp.broadcast_to(a, (1, dv)) for a in a_end], axis=1)
        c_cat = c_cat * a_row + _dot_tn(wk, v_wide)
        n_rows = [a_end[h] * n_rows[h] + jnp.sum(wk[h * C:(h + 1) * C, :], axis=0, keepdims=True) for h in range(nh)]
        hc = hid - jnp.mean(hid, -1, keepdims=True)
        hn = hc * lax.rsqrt(jnp.mean(hc * hc, -1, keepdims=True) + NORM_EPS)
        for h in range(nh):
            og = jax.nn.sigmoid(proj_ref[rows, 2 * nh * dk + nh * dv + h * dv:2 * nh * dk + nh * dv + (h + 1) * dv])
            o_ref[rows, h * dv:(h + 1) * dv] = og * (hn[h * C:(h + 1) * C, :] * ng_ref[:, h * dv:(h + 1) * dv])
    ct_scr[...] = c_cat
    for h in range(nh):
        n_scr[h] = n_rows[h]
        m_scr[h] = jnp.broadcast_to(m_vals[h], (1, LANE))

    @pl.when(t == nt - 1)
    def _():
        lane = _iota((1, nh), 1)
        mrow = jnp.zeros((1, nh), F32)
        for h in range(nh):
            cout_ref[0, h] = ct_scr[:, h * dv:(h + 1) * dv]
            nout_ref[0, h:h + 1, :] = n_scr[h]
            mrow = jnp.where(lane == h, m_scr[h][:, 0:1], mrow)
        mout_ref[0] = mrow


def _mlstm_scan(proj, c0, n0, m0, ng, bi, bf, *, nb, nt, tb, t_valid):
    _, nh, dk, dv = c0.shape
    wp = proj.shape[1]
    body = functools.partial(_mlstm_body, nh=nh, dk=dk, dv=dv, tb=tb, t_valid=t_valid, nt=nt)
    const = lambda b, t: (0, 0)
    m0 = m0.reshape(nb, 1, nh)
    return pl.pallas_call(
        body, grid=(nb, nt),
        in_specs=[pl.BlockSpec((tb, wp), lambda b, t: (b * nt + t, 0)),
                  pl.BlockSpec((1, nh, dk, dv), lambda b, t: (b, 0, 0, 0)),
                  pl.BlockSpec((1, nh, dk), lambda b, t: (b, 0, 0)),
                  pl.BlockSpec((1, 1, nh), lambda b, t: (b, 0, 0)),
                  pl.BlockSpec(ng.shape, const), pl.BlockSpec(bi.shape, const), pl.BlockSpec(bf.shape, const)],
        out_specs=[pl.BlockSpec((tb, nh * dv), lambda b, t: (b * nt + t, 0)),
                   pl.BlockSpec((1, nh, dk, dv), lambda b, t: (b, 0, 0, 0)),
                   pl.BlockSpec((1, nh, dk), lambda b, t: (b, 0, 0)),
                   pl.BlockSpec((1, 1, nh), lambda b, t: (b, 0, 0))],
        out_shape=[jax.ShapeDtypeStruct((nb * nt * tb, nh * dv), F32), jax.ShapeDtypeStruct(c0.shape, F32),
                   jax.ShapeDtypeStruct(n0.shape, F32), jax.ShapeDtypeStruct((nb, 1, nh), F32)],
        scratch_shapes=[pltpu.VMEM((dk, nh * dv), F32), pltpu.VMEM((nh, 1, dk), F32), pltpu.VMEM((nh, 1, LANE), F32)],
        compiler_params=_cparams(("parallel", "arbitrary")), name="mlstm_scan",
    )(proj, c0, n0, m0, ng, bi, bf)


SUB = 8


def _tree(op, xs):
    xs = list(xs)
    while len(xs) > 1:
        xs = [op(xs[i], xs[i + 1]) if i + 1 < len(xs) else xs[i] for i in range(0, len(xs), 2)]
    return xs[0]


def _all_sublanes(op, m):
    for shift in (4, 2, 1):
        m = op(m, pltpu.roll(m, shift, axis=0))
    return m


def _sorting_network(n):
    def merge(lo, hi, r):
        step = r * 2
        if step < hi - lo:
            yield from merge(lo, hi, step)
            yield from merge(lo + r, hi, step)
            yield from [(i, i + r) for i in range(lo + r, hi - r, step)]
        else:
            yield (lo, lo + r)

    def sort(lo, hi):
        if hi - lo >= 1:
            mid = lo + (hi - lo) // 2
            yield from sort(lo, mid)
            yield from sort(mid + 1, hi)
            yield from merge(lo, hi, 1)

    return list(sort(0, n - 1))


def _sort_tiles_desc(tiles):
    tiles = list(tiles)
    for i, j in _sorting_network(len(tiles)):
        tiles[i], tiles[j] = jnp.maximum(tiles[i], tiles[j]), jnp.minimum(tiles[i], tiles[j])
    return tiles


def _merge_top(lists, singles=None):
    lists = list(lists)
    vals = []
    for j in range(PEER_TOPK):
        head = lists[0] if singles is None else jnp.maximum(lists[0], singles)
        m = _all_sublanes(jnp.maximum, head)
        vals.append(m)
        if j == PEER_TOPK - 1:
            break
        hit = lists[0] == m
        for i in range(min(len(lists) - 1, PEER_TOPK - 1 - j)):
            lists[i] = jnp.where(hit, lists[i + 1], lists[i])
        if singles is not None:
            singles = jnp.where(singles == m, -jnp.inf, singles)
    return vals


def _route_body(xt_ref, wq_ref, k1_ref, k2_ref, r2_ref, c1_ref, a1_ref, a2_ref, q_scr, *, tm):
    q_scr[...] = _dot(wq_ref[...], xt_ref[...].astype(BF16))
    nsub = tm // LANE
    ntile = PEER_NKEYS // SUB
    sub = _iota((SUB, LANE), 0)

    def head(h, carry):
        for sb in range(nsub):
            route_tile(h, slice(sb * LANE, (sb + 1) * LANE))
        return carry

    def route_tile(h, lanes):
        q1 = q_scr[pl.ds(pl.multiple_of(h * 2 * PEER_HALF, PEER_HALF), PEER_HALF), lanes]
        q2 = q_scr[pl.ds(pl.multiple_of(h * 2 * PEER_HALF + PEER_HALF, PEER_HALF), PEER_HALF), lanes]
        s1 = _dot(k1_ref[...], q1, HI)
        s2 = _dot(k2_ref[...], q2, HI)
        t1 = [s1[k * SUB:(k + 1) * SUB, :] for k in range(ntile)]
        t2 = [s2[k * SUB:(k + 1) * SUB, :] for k in range(ntile)]
        v1 = _merge_top(_sort_tiles_desc(t1))
        v2 = _merge_top(_sort_tiles_desc(t2))
        v2lo, v2hi = v2[SUB - 1], v2[2 * SUB - 1]
        for b in range(SUB - 2, -1, -1):
            v2lo = jnp.where(sub == b, v2[b], v2lo)
            v2hi = jnp.where(sub == b, v2[SUB + b], v2hi)
        cand_lo = [jnp.where(sub < min(PEER_TOPK // (a + 1), SUB), v1[a] + v2lo, -jnp.inf) for a in range(PEER_TOPK)]
        cand_hi = v1[0] + v2hi
        tau = _merge_top(cand_lo, cand_hi)[PEER_TOPK - 1]
        top = v1[0] + v2[0]
        z = _all_sublanes(jnp.add, _tree(jnp.add, [jnp.where(cd >= tau, jnp.exp(cd - top), 0.0)
                                                   for cd in cand_lo + [cand_hi]]))
        inv_z = 1.0 / z
        pairs = [_all_sublanes(jnp.add, jnp.where(cd >= tau, 1.0, 0.0)) for cd in cand_lo]
        pairs[0] = pairs[0] + _all_sublanes(jnp.add, jnp.where(cand_hi >= tau, 1.0, 0.0))
        rank2, c1 = [], []
        for k in range(ntile):
            r = jnp.full((SUB, LANE), float(PEER_TOPK), F32)
            cnt = jnp.zeros((SUB, LANE), F32)
            for j in range(PEER_TOPK - 1, -1, -1):
                r = jnp.where(t2[k] >= v2[j], float(j), r)
                cnt = jnp.where(t1[k] == v1[j], pairs[j], cnt)
            rank2.append(r)
            c1.append(cnt)
        r2_ref[h, :, lanes] = jnp.concatenate(rank2, axis=0).astype(BF16)
        c1_ref[h, :, lanes] = jnp.concatenate(c1, axis=0)
        a1_ref[h, :, lanes] = jnp.concatenate([jnp.exp(t - v1[0]) for t in t1], axis=0)
        a2_ref[h, :, lanes] = jnp.concatenate([jnp.exp(t - v2[0]) * inv_z for t in t2], axis=0).astype(BF16)

    lax.fori_loop(0, PEER_HEADS, head, 0)


def _peer_route(xt, wq_t, k1, k2, tm):
    d, n = xt.shape
    ospec = pl.BlockSpec((PEER_HEADS, PEER_NKEYS, tm), lambda i: (0, 0, i))
    shp = lambda dt: jax.ShapeDtypeStruct((PEER_HEADS, PEER_NKEYS, n), dt)
    return pl.pallas_call(
        functools.partial(_route_body, tm=tm), grid=(n // tm,),
        in_specs=[pl.BlockSpec((d, tm), lambda i: (0, i)), pl.BlockSpec(wq_t.shape, lambda i: (0, 0)),
                  pl.BlockSpec(k1.shape, lambda i: (0, 0)), pl.BlockSpec(k2.shape, lambda i: (0, 0))],
        out_specs=[ospec] * 4, out_shape=[shp(BF16), shp(F32), shp(F32), shp(BF16)],
        scratch_shapes=[pltpu.VMEM((wq_t.shape[0], tm), F32)],
        compiler_params=_cparams(("parallel",)), name="peer_route")(xt, wq_t, k1, k2)


def _peer_body(xt_ref, r2_ref, c1_ref, a1_ref, a2_ref, u_ref, vt_ref, g_ref, b_ref, out_ref, acc, xbf, hid_scr, p_scr):
    s = pl.program_id(1)

    @pl.when(s == 0)
    def _():
        xbf[...] = xt_ref[...].astype(BF16)
        acc[...] = jnp.zeros(acc.shape, F32)
        hid_scr[...] = jnp.zeros(hid_scr.shape, F32)
        p_scr[...] = jnp.zeros(p_scr.shape, BF16)

    tm = xbf.shape[1]
    rows16 = 2 * SUB
    hid_scr[s % 2] = _dot(u_ref[...], xbf[...])
    acc[...] += _dot(vt_ref[0], p_scr[s % 2])
    for j in range(PEER_I1_PER_STEP):
        for l0 in range(0, tm, PEER_LANE_BLOCK):
            ls = slice(l0, l0 + PEER_LANE_BLOCK)
            c1b = [jnp.broadcast_to(c1_ref[h, j:j + 1, ls], (rows16, PEER_LANE_BLOCK)).astype(BF16)
                   for h in range(PEER_HEADS)]
            a1b = [jnp.broadcast_to(a1_ref[h, j:j + 1, ls], (rows16, PEER_LANE_BLOCK)).astype(BF16)
                   for h in range(PEER_HEADS)]
            for kb in range(PEER_NKEYS // rows16):
                ks = slice(kb * rows16, (kb + 1) * rows16)
                es = slice(j * PEER_NKEYS + kb * rows16, j * PEER_NKEYS + (kb + 1) * rows16)
                w = _tree(jnp.add, [jnp.where(r2_ref[h, ks, ls] < c1b[h], a2_ref[h, ks, ls] * a1b[h], 0.0)
                                    for h in range(PEER_HEADS)])
                x = hid_scr[(s + 1) % 2, es, ls]
                act = (0.5 * x * (1.0 + lax.erf(x * 0.5 ** 0.5))).astype(BF16)
                p_scr[(s + 1) % 2, es, ls] = w * act

    @pl.when(s == pl.num_programs(1) - 1)
    def _():
        z = DN_ALPHA * xt_ref[...] + acc[...]
        zc = z - jnp.mean(z, 0, keepdims=True)
        var = jnp.mean(zc * zc, 0, keepdims=True)
        out_ref[...] = (zc * lax.rsqrt(var + LN_EPS) * g_ref[...] + b_ref[...]).T


def _peer_main(xt, r2, c1, a1, a2, u, vt, g, b, tm):
    d, n = xt.shape
    ec = PEER_I1_PER_STEP * PEER_NKEYS
    nc = u.shape[0] // ec
    chunk = lambda s, lag: jnp.clip(s - lag, 0, nc - 1)
    k2spec = pl.BlockSpec((PEER_HEADS, PEER_NKEYS, tm), lambda i, s: (0, 0, i))
    k1spec = pl.BlockSpec((PEER_HEADS, PEER_I1_PER_STEP, tm), lambda i, s: (0, chunk(s, 1), i))
    return pl.pallas_call(
        _peer_body, grid=(n // tm, nc + 2),
        in_specs=[pl.BlockSpec((d, tm), lambda i, s: (0, i)), k2spec, k1spec, k1spec, k2spec,
                  pl.BlockSpec((ec, d), lambda i, s: (chunk(s, 0), 0)),
                  pl.BlockSpec((1, d, ec), lambda i, s: (chunk(s, 2), 0, 0)),
                  pl.BlockSpec((d, 1), lambda i, s: (0, 0)), pl.BlockSpec((d, 1), lambda i, s: (0, 0))],
        out_specs=pl.BlockSpec((tm, d), lambda i, s: (i, 0)),
        out_shape=jax.ShapeDtypeStruct((n, d), F32),
        scratch_shapes=[pltpu.VMEM((d, tm), F32), pltpu.VMEM((d, tm), BF16), pltpu.VMEM((2, ec, tm), F32),
                        pltpu.VMEM((2, ec, tm), BF16)],
        compiler_params=_cparams(("parallel", "arbitrary")), name="peer_main",
    )(xt, r2, c1, a1, a2, u, vt, g.reshape(d, 1), b.reshape(d, 1))


def _pad_cols(w, width):
    return jnp.pad(w, ((0, 0), (0, width - w.shape[1])))


def _lane_row(*parts):
    v = jnp.concatenate([p.astype(F32).reshape(-1) for p in parts])
    return jnp.pad(v, (0, LANE - v.shape[0])).reshape(1, LANE)


def _pick_tb(t_pad):
    for tb in (48, 32, 16):
        if t_pad % tb == 0:
            return tb
    raise ValueError(t_pad)


def kernel(x_prompt, x_sample, state_hgrn_S, state_gdn_S, state_gdn_conv, state_mlstm_C, state_mlstm_n, state_mlstm_m, state_gla_S, meta_tokens, hgrn_w_in, hgrn_lb, hgrn_norm_g, hgrn_w_out, gdn_w_in, gdn_conv_w, gdn_a_log, gdn_dt_bias, gdn_norm_g, gdn_w_out, mlstm_w_in, mlstm_b_i, mlstm_b_f, mlstm_norm_g, mlstm_w_out, gla_w_in, gla_w_gate, gla_b_gate, gla_norm_g, gla_w_out, peer_w_q, peer_keys1, peer_keys2, peer_u, peer_v, ln_g, ln_b):
    bp, seq, d = x_prompt.shape
    bs, seq_s, _ = x_sample.shape
    assert d == D_MODEL and len(state_hgrn_S) == 1 and len(state_gdn_S) == 1
    assert len(state_mlstm_C) == 1 and len(state_gla_S) == 1
    tp = N_META + seq
    tp_pad = -(-tp // C) * C
    ts_pad = -(-seq_s // C) * C
    tbp = _pick_tb(tp_pad)
    tbs = _pick_tb(ts_pad)
    np_rows = bp * tp_pad
    n_real = np_rows + bs * seq_s

    meta = jnp.broadcast_to(meta_tokens.astype(F32)[None], (bp, N_META, d))
    hp = jnp.concatenate([meta, x_prompt], axis=1)
    hp = jnp.pad(hp, ((0, 0), (0, tp_pad - tp), (0, 0))).reshape(np_rows, d)
    h = jnp.concatenate([hp, x_sample.reshape(bs * seq_s, d)], axis=0)
    n = -(-n_real // TOK_MULTIPLE) * TOK_MULTIPLE
    h = jnp.pad(h, ((0, n - n_real), (0, 0)))
    tm_mm = MM_TOK_BLOCK

    def split_cols(w, sizes):
        out, o = [], 0
        for s in sizes:
            out.append(w[:, o:o + s])
            o += s
        return out

    def run_mixer(scan, proj, states_p, states_s, **kw):
        res_p = scan(proj, *states_p, nb=bp, nt=tp_pad // tbp, tb=tbp, t_valid=tp, **kw)
        ps = proj[np_rows:n_real].reshape(bs, seq_s, -1)
        ps = jnp.pad(ps, ((0, 0), (0, ts_pad - seq_s), (0, 0))).reshape(bs * ts_pad, -1)
        res_s = scan(ps, *states_s, nb=bs, nt=ts_pad // tbs, tb=tbs, t_valid=seq_s, **kw)
        o_s = res_s[0].reshape(bs, ts_pad, -1)[:, :seq_s].reshape(bs * seq_s, -1)
        o = jnp.concatenate([res_p[0], o_s], axis=0)
        o = jnp.pad(o, ((0, n - n_real), (0, 0)))
        return o, res_p[1:], res_s[1:]

    lb_all = jnp.cumsum(jax.nn.softmax(hgrn_lb.astype(F32), axis=0), axis=0)
    outs = {}
    for i in range(DEPTH):
        mix = i % 4
        if mix == 0:
            w_in = hgrn_w_in[0].astype(BF16)
            proj = _mm(h, w_in, tm_mm, 1024)
            nh, dk, dv = state_hgrn_S.shape[2:]
            zero = jnp.zeros((bp, nh, dk, dv), F32)
            scan = functools.partial(_gla_scan, ng=hgrn_norm_g[0].reshape(1, dv), p1=lb_all[i].reshape(1, nh * dk),
                                     p2=jnp.zeros((1, LANE), F32), hgrn=True)
            o, (sp,), (ss,) = run_mixer(scan, proj, (zero,), (state_hgrn_S[0],))
            outs['hgrn'] = (sp[None], ss[None])
            w_out = hgrn_w_out[0]
        elif mix == 1:
            nh, dk, dv = state_gdn_S.shape[2:]
            cwid = 2 * nh * dk + nh * dv
            qkv, a, b, g = split_cols(gdn_w_in[0], [cwid, nh, nh, nh * dv])
            w_in = jnp.concatenate([qkv, g, _pad_cols(jnp.concatenate([a, b], axis=1), LANE)], axis=1).astype(BF16)
            proj = _mm(h, w_in, tm_mm, w_in.shape[1] // 3)
            scan = functools.partial(_gdn_scan, cw=gdn_conv_w[0], ng=gdn_norm_g[0].reshape(1, dv),
                                     alog=_lane_row(gdn_a_log[0]), dtb=_lane_row(gdn_dt_bias[0]))
            o, (sp, cp), (ss, cs) = run_mixer(
                scan, proj,
                (jnp.zeros((bp, nh, dk, dv), F32), jnp.zeros((bp, CONV_W - 1, cwid), F32)),
                (state_gdn_S[0], state_gdn_conv[0]))
            outs['gdn'] = (sp[None], cp[None], ss[None], cs[None])
            w_out = gdn_w_out[0]
        elif mix == 2:
            nh, dk, dv = state_mlstm_C.shape[2:]
            q, k, v, ig, fg, og = split_cols(mlstm_w_in[0], [nh * dk, nh * dk, nh * dv, nh, nh, nh * dv])
            w_in = jnp.concatenate([q, k, v, og, _pad_cols(jnp.concatenate([ig, fg], axis=1), LANE)],
                                   axis=1).astype(BF16)
            proj = _mm(h, w_in, tm_mm, w_in.shape[1] // 3)
            zeros_nh = jnp.zeros((nh,), F32)
            scan = functools.partial(_mlstm_scan, ng=mlstm_norm_g[0].reshape(1, nh * dv),
                                     bi=_lane_row(mlstm_b_i[0]), bf=_lane_row(zeros_nh, mlstm_b_f[0]))
            o, (cp, npp, mp), (cs, ns, ms) = run_mixer(
                scan, proj,
                (jnp.zeros((bp, nh, dk, dv), F32), jnp.zeros((bp, nh, dk), F32), jnp.zeros((bp, nh), F32)),
                (state_mlstm_C[0], state_mlstm_n[0], state_mlstm_m[0]))
            outs['mlstm'] = (cp[None], npp[None], mp.reshape(1, bp, nh), cs[None], ns[None], ms.reshape(1, bs, nh))
            w_out = mlstm_w_out[0]
        else:
            nh, dk, dv = state_gla_S.shape[2:]
            rank = gla_w_gate.shape[1]
            q, k, v, g, r = split_cols(gla_w_in[0], [nh * dk, nh * dk, nh * dv, nh * dv, rank])
            w_in = jnp.concatenate([q, k, v, g, _pad_cols(r, LANE)], axis=1).astype(BF16)
            proj = _mm(h, w_in, tm_mm, w_in.shape[1] // 5)
            wg = jnp.pad(gla_w_gate[0].astype(F32), ((0, LANE - rank), (0, 0)))
            scan = functools.partial(_gla_scan, ng=gla_norm_g[0].reshape(1, dv), p1=wg,
                                     p2=gla_b_gate[0].reshape(1, nh * dk), hgrn=False)
            o, (sp,), (ss,) = run_mixer(scan, proj, (jnp.zeros((bp, nh, dk, dv), F32),), (state_gla_S[0],))
            outs['gla'] = (sp[None], ss[None])
            w_out = gla_w_out[0]

        h1t = _mm_res_ln_t(o, w_out.astype(BF16), h, ln_g[i, 0], ln_b[i, 0], TOK_BLOCK)
        r2, c1, a1, a2 = _peer_route(h1t, peer_w_q[i].T.astype(BF16), peer_keys1[i], peer_keys2[i], PEER_TOK_BLOCK)
        ec = PEER_I1_PER_STEP * PEER_NKEYS
        vt = peer_v[i].astype(BF16).reshape(-1, ec, d).transpose(0, 2, 1)
        h = _peer_main(h1t, r2, c1, a1, a2, peer_u[i].astype(BF16), vt, ln_g[i, 1], ln_b[i, 1], PEER_TOK_BLOCK)

    y_prompt = h[:np_rows].reshape(bp, tp_pad, d)[:, N_META:tp]
    y_sample = h[np_rows:n_real].reshape(bs, seq_s, d)
    hg, gd, ml, gl = outs['hgrn'], outs['gdn'], outs['mlstm'], outs['gla']
    return (y_prompt, y_sample, hg[0], gd[0], gd[1], ml[0], ml[1], ml[2], gl[0],
            hg[1], gd[2], gd[3], ml[3], ml[4], ml[5], gl[1])
```

```python
import functools

import jax
import jax.numpy as jnp
from jax import lax
from jax.experimental import pallas as pl
from jax.experimental.pallas import tpu as pltpu

F32 = jnp.float32
BF16 = jnp.bfloat16
HI = lax.Precision.HIGHEST

D_MODEL = 1024
DEPTH = 4
N_META = 16
C = 16
CONV_W = 4
NORM_EPS = 1e-6
LN_EPS = 1e-5
DN_ALPHA = (2 * DEPTH) ** 0.25
GLA_GATE_NORM = 16.0
LANE = 128
PEER_HEADS = 8
PEER_NKEYS = 128
PEER_TOPK = 16
PEER_HALF = 128
PEER_I1_PER_STEP = 8
PEER_SUB_I1 = 2
PEER_VMEM_LIMIT = 56 * 1024 * 1024
TOK_BLOCK = 256
PEER_TOK_BLOCK = 512
PEER_LANE_BLOCK = 256
MM_TOK_BLOCK = 768
TOK_MULTIPLE = 1536
VMEM_LIMIT = 48 * 1024 * 1024


def _cparams(sem):
    return pltpu.CompilerParams(dimension_semantics=sem, vmem_limit_bytes=VMEM_LIMIT)


def _iota(shape, dim):
    return lax.broadcasted_iota(jnp.int32, shape, dim)


def _dot_nt(a, b, precision=None):
    return lax.dot_general(a, b, (((1,), (1,)), ((), ())), precision=precision, preferred_element_type=F32)


def _dot_tn(a, b, precision=None):
    return lax.dot_general(a, b, (((0,), (0,)), ((), ())), precision=precision, preferred_element_type=F32)


def _dot(a, b, precision=None):
    return jnp.dot(a, b, precision=precision, preferred_element_type=F32)


def _split(x):
    hi = x.astype(BF16)
    return hi, (x - hi.astype(F32)).astype(BF16)


def _dot3(a, b, nt=False):
    (ah, al), (bh, bl) = a, b
    d = _dot_nt if nt else _dot
    return d(ah, bh) + d(ah, bl) + d(al, bh)


def _cumsum_rows(x):
    tri = (_iota((C, C), 0) >= _iota((C, C), 1)).astype(F32)
    return _dot(tri, x, HI)


def _rows_to_cols(x):
    eye = (_iota((LANE, LANE), 0) == _iota((LANE, LANE), 1)).astype(F32)
    return _dot_nt(eye, x, HI)


def _incl_mask():
    return _iota((C, C), 0) >= _iota((C, C), 1)


def _mm_body(x_ref, w_ref, o_ref):
    o_ref[...] = _dot(x_ref[...].astype(BF16), w_ref[...])


def _mm(x, w, tm, tn):
    m, k = x.shape
    n = w.shape[1]
    return pl.pallas_call(
        _mm_body, grid=(m // tm, n // tn),
        in_specs=[pl.BlockSpec((tm, k), lambda i, j: (i, 0)), pl.BlockSpec((k, tn), lambda i, j: (0, j))],
        out_specs=pl.BlockSpec((tm, tn), lambda i, j: (i, j)),
        out_shape=jax.ShapeDtypeStruct((m, n), F32),
        compiler_params=_cparams(("parallel", "parallel")), name="mm")(x, w)


def _mm_ln_body(o_ref, w_ref, h_ref, g_ref, b_ref, out_ref):
    y = _dot(o_ref[...].astype(BF16), w_ref[...])
    z = DN_ALPHA * h_ref[...] + y
    zc = z - jnp.mean(z, -1, keepdims=True)
    var = jnp.mean(zc * zc, -1, keepdims=True)
    out_ref[...] = (zc * lax.rsqrt(var + LN_EPS) * g_ref[...] + b_ref[...]).T


def _mm_res_ln_t(o, w, h, g, b, tm):
    n, k = o.shape
    d = w.shape[1]
    return pl.pallas_call(
        _mm_ln_body, grid=(n // tm,),
        in_specs=[pl.BlockSpec((tm, k), lambda i: (i, 0)), pl.BlockSpec((k, d), lambda i: (0, 0)),
                  pl.BlockSpec((tm, d), lambda i: (i, 0)), pl.BlockSpec((1, d), lambda i: (0, 0)),
                  pl.BlockSpec((1, d), lambda i: (0, 0))],
        out_specs=pl.BlockSpec((d, tm), lambda i: (0, i)),
        out_shape=jax.ShapeDtypeStruct((d, n), F32),
        compiler_params=_cparams(("parallel",)), name="mm_res_ln")(o, w, h, g.reshape(1, d), b.reshape(1, d))


def _valid_rows(t, r0, tb, t_valid, nt):
    if t_valid == nt * tb:
        return None
    return (t * tb + r0 + _iota((C, 1), 0)) < t_valid


def _gla_body(proj_ref, s0_ref, ng_ref, p1_ref, p2_ref, o_ref, sout_ref, st_scr, *, hgrn, nh, dk, dv, tb, t_valid,
              nt):
    t = pl.program_id(1)

    @pl.when(t == 0)
    def _():
        for h in range(nh):
            st_scr[h] = s0_ref[0, h].T

    kw = nh * dk
    vw = nh * dv
    nr = nh * C
    ri = _iota((nr, 1), 0)
    cj = _iota((1, nr), 1)
    rhead = ri // C
    tri = ((rhead == cj // C) & (ri % C >= cj % C)).astype(F32)
    row3 = _iota((1, C, 1), 1)

    def stack(rows, col0, width):
        return jnp.concatenate([proj_ref[rows, col0 + h * width:col0 + (h + 1) * width] for h in range(nh)], axis=0)

    prep = []
    for ci in range(tb // C):
        rows = slice(ci * C, (ci + 1) * C)
        if hgrn:
            q = jax.nn.silu(stack(rows, 0, dk))
            lb = jnp.concatenate([jnp.broadcast_to(p1_ref[:, h * dk:(h + 1) * dk], (C, dk)) for h in range(nh)], axis=0)
            fg = lb + (1.0 - lb) * jax.nn.sigmoid(stack(rows, kw, dk))
            k = 1.0 - fg
            g = jnp.log(fg)
        else:
            r = proj_ref[rows, 2 * kw + 2 * vw:2 * kw + 2 * vw + LANE]
            logf_all = jax.nn.log_sigmoid(_dot(r, p1_ref[...], HI) + p2_ref[...]) / GLA_GATE_NORM
            q = stack(rows, 0, dk) * dk ** -0.5
            k = stack(rows, kw, dk)
            g = jnp.concatenate([logf_all[:, h * dk:(h + 1) * dk] for h in range(nh)], axis=0)
        v = stack(rows, 2 * kw, dv)
        if t_valid != nt * tb:
            valid = t * tb + ci * C + ri % C < t_valid
            k = jnp.where(valid, k, 0.0)
            g = jnp.where(valid, g, 0.0)
        b = _dot(tri, g, HI)
        b3, q3, k3, v3 = (x.reshape(nh, C, x.shape[-1]) for x in (b, q, k, v))
        o3 = jnp.zeros((nh, C, dv), F32)
        for s in range(C):
            d = jnp.exp(jnp.where(row3 >= s, b3 - b3[:, s:s + 1, :], -jnp.inf))
            col = jnp.sum(q3 * (k3[:, s:s + 1, :] * d), axis=-1, keepdims=True)
            o3 = o3 + col * v3[:, s:s + 1, :]
        b_end = b3[:, C - 1:C, :]
        prep.append(dict(o=o3.reshape(nr, dv), qe=q * jnp.exp(b), kd=(k3 * jnp.exp(b_end - b3)).reshape(nr, dk),
                         dec=jnp.exp(b_end), v=v))

    st = st_scr[...]
    for ci, p in enumerate(prep):
        rows = slice(ci * C, (ci + 1) * C)
        inter = _dot_nt(p['qe'], st.reshape(nh * dv, dk))
        o = p['o'] + _tree(jnp.add, [jnp.where(rhead == h, inter[:, h * dv:(h + 1) * dv], 0.0) for h in range(nh)])
        v_wide = jnp.concatenate([jnp.where(rhead == h, p['v'], 0.0) for h in range(nh)], axis=1)
        st = st * p['dec'] + _dot_tn(v_wide, p['kd']).reshape(nh, dv, dk)
        o = o * lax.rsqrt(jnp.mean(o * o, -1, keepdims=True) + NORM_EPS) * ng_ref[...]
        for h in range(nh):
            gate = proj_ref[rows, 2 * kw + vw + h * dv:2 * kw + vw + (h + 1) * dv]
            o_ref[rows, h * dv:(h + 1) * dv] = o[h * C:(h + 1) * C, :] * jax.nn.silu(gate)
    st_scr[...] = st

    @pl.when(t == nt - 1)
    def _():
        for h in range(nh):
            sout_ref[0, h] = st_scr[h].T


def _gla_scan(proj, s0, ng, p1, p2, *, hgrn, nb, nt, tb, t_valid):
    _, nh, dk, dv = s0.shape
    wp = proj.shape[1]
    body = functools.partial(_gla_body, hgrn=hgrn, nh=nh, dk=dk, dv=dv, tb=tb, t_valid=t_valid, nt=nt)
    return pl.pallas_call(
        body, grid=(nb, nt),
        in_specs=[pl.BlockSpec((tb, wp), lambda b, t: (b * nt + t, 0)),
                  pl.BlockSpec((1, nh, dk, dv), lambda b, t: (b, 0, 0, 0)),
                  pl.BlockSpec(ng.shape, lambda b, t: (0, 0)),
                  pl.BlockSpec(p1.shape, lambda b, t: (0, 0)),
                  pl.BlockSpec(p2.shape, lambda b, t: (0, 0))],
        out_specs=[pl.BlockSpec((tb, nh * dv), lambda b, t: (b * nt + t, 0)),
                   pl.BlockSpec((1, nh, dk, dv), lambda b, t: (b, 0, 0, 0))],
        out_shape=[jax.ShapeDtypeStruct((nb * nt * tb, nh * dv), F32), jax.ShapeDtypeStruct(s0.shape, F32)],
        scratch_shapes=[pltpu.VMEM((nh, dv, dk), F32)],
        compiler_params=_cparams(("parallel", "arbitrary")), name="hgrn_scan" if hgrn else "gla_scan",
    )(proj, s0, ng, p1, p2)


def _gdn_body(proj_ref, s0_ref, conv0_ref, cw_ref, ng_ref, alog_ref, dtb_ref, o_ref, sout_ref, convout_ref,
              st_scr, cbuf, ybuf, *, nh, dk, dv, tb, t_valid, nt):
    t = pl.program_id(1)
    cwid = 2 * nh * dk + nh * dv
    pre = 8 - (CONV_W - 1)

    @pl.when(t == 0)
    def _():
        for h in range(nh):
            st_scr[:, h * dv:(h + 1) * dv] = s0_ref[0, h]
        cbuf[0:8, :] = jnp.zeros((8, cwid), F32)
        cbuf[pre:8, :] = conv0_ref[0]

    cbuf[8:8 + tb, :] = proj_ref[:, 0:cwid]
    y = cbuf[pre:pre + tb, :] * cw_ref[0:1, :]
    for j in range(1, CONV_W):
        y = y + cbuf[pre + j:pre + j + tb, :] * cw_ref[j:j + 1, :]
    ybuf[...] = jax.nn.silu(y)

    nr = nh * C
    ri = _iota((nr, 1), 0)
    cj = _iota((1, nr), 1)
    rhead = ri // C
    same = rhead == cj // C
    incl = same & (ri % C >= cj % C)
    strict = same & (ri % C > cj % C)
    eye = (ri == cj).astype(F32)

    def stack(ref, rows, base, width):
        return jnp.concatenate([ref[rows, base + h * width:base + (h + 1) * width] for h in range(nh)], axis=0)

    def own_block(x):
        return _tree(jnp.add, [jnp.where(rhead == h, x[:, h * dv:(h + 1) * dv], 0.0) for h in range(nh)])

    prep = []
    for ci in range(tb // C):
        rows = slice(ci * C, (ci + 1) * C)
        valid = _valid_rows(t, ci * C, tb, t_valid, nt)
        small = proj_ref[rows, cwid + nh * dv:cwid + nh * dv + LANE]
        g_all = -jnp.exp(alog_ref[...]) * jax.nn.softplus(small + dtb_ref[...])
        beta_all = jax.nn.sigmoid(small)
        if valid is not None:
            g_all = jnp.where(valid, g_all, 0.0)
            beta_all = jnp.where(valid, beta_all, 0.0)
        b_all = _cumsum_rows(g_all)
        b_t = _rows_to_cols(b_all)
        bcol = jnp.concatenate([b_all[:, h:h + 1] for h in range(nh)], axis=0)
        brow = jnp.concatenate([b_t[h:h + 1, :] for h in range(nh)], axis=1)
        beta = jnp.concatenate([beta_all[:, nh + h:nh + h + 1] for h in range(nh)], axis=0)
        bend = jnp.concatenate([jnp.broadcast_to(b_all[C - 1:C, h:h + 1], (C, 1)) for h in range(nh)], axis=0)
        dec_row = jnp.concatenate([jnp.broadcast_to(jnp.exp(b_all[C - 1:C, h:h + 1]), (1, dv)) for h in range(nh)],
                                  axis=1)
        q = stack(ybuf, rows, 0, dk)
        k = stack(ybuf, rows, nh * dk, dk)
        v = stack(ybuf, rows, 2 * nh * dk, dv)
        q = q * lax.rsqrt(jnp.sum(q * q, -1, keepdims=True) + NORM_EPS) * dk ** -0.5
        k = k * lax.rsqrt(jnp.sum(k * k, -1, keepdims=True) + NORM_EPS)
        if valid is not None:
            valid_st = jnp.concatenate([valid] * nh, axis=0)
            q = jnp.where(valid_st, q, 0.0)
            k = jnp.where(valid_st, k, 0.0)
            v = jnp.where(valid_st, v, 0.0)
        decay = jnp.exp(jnp.where(incl, bcol - brow, -jnp.inf))
        eb = jnp.exp(bcol)
        ks = _split(k)
        prep.append(dict(a=jnp.where(strict, _dot3(ks, ks, nt=True) * decay * beta, 0.0), bv=beta * v,
                         bk=(beta * eb) * k, att=_dot_nt(q, k) * decay, qe=q * eb, kd=k * jnp.exp(bend - bcol),
                         dec_row=dec_row))
    for p in prep:
        a_s = _split(p['a'])
        p['a2'] = _dot3(a_s, a_s)
    for p in prep:
        a2_s = _split(p['a2'])
        p['a4'] = _dot3(a2_s, a2_s)
        p['lo'] = _dot3(_split(eye - p['a']), _split(eye + p['a2']))
    for p in prep:
        a4_s = _split(p['a4'])
        p['a8'] = _dot3(a4_s, a4_s)
    for p in prep:
        p['hi'] = _dot3(_split(eye + p['a4']), _split(eye + p['a8']))
    for p in prep:
        p['tinv'] = _split(_dot3(_split(p['lo']), _split(p['hi'])))
    for p in prep:
        p['u'] = _dot3(p['tinv'], _split(p['bv']))
        p['wk'] = _dot3(p['tinv'], _split(p['bk']))

    s_cat = st_scr[...]
    for ci, p in enumerate(prep):
        rows = slice(ci * C, (ci + 1) * C)
        v_new = p['u'] - own_block(_dot(p['wk'], s_cat))
        o = _dot(p['att'], v_new) + own_block(_dot(p['qe'], s_cat))
        v_wide = jnp.concatenate([jnp.where(rhead == h, v_new, 0.0) for h in range(nh)], axis=1)
        s_cat = s_cat * p['dec_row'] + _dot_tn(p['kd'], v_wide)
        o = o * lax.rsqrt(jnp.mean(o * o, -1, keepdims=True) + NORM_EPS) * ng_ref[...]
        for h in range(nh):
            gate = proj_ref[rows, cwid + h * dv:cwid + (h + 1) * dv]
            o_ref[rows, h * dv:(h + 1) * dv] = o[h * C:(h + 1) * C, :] * jax.nn.silu(gate)
    st_scr[...] = s_cat

    @pl.when(t < nt - 1)
    def _():
        cbuf[pre:8, :] = cbuf[pre + tb:8 + tb, :]

    @pl.when(t == nt - 1)
    def _():
        for h in range(nh):
            sout_ref[0, h] = st_scr[:, h * dv:(h + 1) * dv]
        last = t_valid - (nt - 1) * tb
        convout_ref[0] = cbuf[pre + last:8 + last, :]


def _gdn_scan(proj, s0, conv0, cw, ng, alog, dtb, *, nb, nt, tb, t_valid):
    _, nh, dk, dv = s0.shape
    wp = proj.shape[1]
    cwid = 2 * nh * dk + nh * dv
    body = functools.partial(_gdn_body, nh=nh, dk=dk, dv=dv, tb=tb, t_valid=t_valid, nt=nt)
    const = lambda b, t: (0, 0)
    return pl.pallas_call(
        body, grid=(nb, nt),
        in_specs=[pl.BlockSpec((tb, wp), lambda b, t: (b * nt + t, 0)),
                  pl.BlockSpec((1, nh, dk, dv), lambda b, t: (b, 0, 0, 0)),
                  pl.BlockSpec((1, CONV_W - 1, cwid), lambda b, t: (b, 0, 0)),
                  pl.BlockSpec(cw.shape, const), pl.BlockSpec(ng.shape, const),
                  pl.BlockSpec(alog.shape, const), pl.BlockSpec(dtb.shape, const)],
        out_specs=[pl.BlockSpec((tb, nh * dv), lambda b, t: (b * nt + t, 0)),
                   pl.BlockSpec((1, nh, dk, dv), lambda b, t: (b, 0, 0, 0)),
                   pl.BlockSpec((1, CONV_W - 1, cwid), lambda b, t: (b, 0, 0))],
        out_shape=[jax.ShapeDtypeStruct((nb * nt * tb, nh * dv), F32), jax.ShapeDtypeStruct(s0.shape, F32),
                   jax.ShapeDtypeStruct(conv0.shape, F32)],
        scratch_shapes=[pltpu.VMEM((dk, nh * dv), F32), pltpu.VMEM((tb + 8, cwid), F32), pltpu.VMEM((tb, cwid), F32)],
        compiler_params=_cparams(("parallel", "arbitrary")), name="gdn_scan",
    )(proj, s0, conv0, cw, ng, alog, dtb)


def _mlstm_body(proj_ref, c0_ref, n0_ref, m0_ref, ng_ref, bi_ref, bf_ref, o_ref, cout_ref, nout_ref, mout_ref,
                ct_scr, n_scr, m_scr, *, nh, dk, dv, tb, t_valid, nt):
    t = pl.program_id(1)

    @pl.when(t == 0)
    def _():
        for h in range(nh):
            ct_scr[:, h * dv:(h + 1) * dv] = c0_ref[0, h]
            n_scr[h] = n0_ref[0, h:h + 1, :]
            m_scr[h] = jnp.broadcast_to(m0_ref[0, :, h:h + 1], (1, LANE))

    base = 2 * nh * dk + 2 * nh * dv
    nr = nh * C
    ri = _iota((nr, 1), 0)
    cj = _iota((1, nr), 1)
    rhead = ri // C
    incl = (rhead == cj // C) & (ri % C >= cj % C)

    def stack(rows, col0, width):
        return jnp.concatenate([proj_ref[rows, col0 + h * width:col0 + (h + 1) * width] for h in range(nh)], axis=0)

    def per_head_col(vals):
        return jnp.concatenate([jnp.broadcast_to(x, (C, 1)) for x in vals], axis=0)

    prep = []
    for ci in range(tb // C):
        rows = slice(ci * C, (ci + 1) * C)
        valid = _valid_rows(t, ci * C, tb, t_valid, nt)
        small = proj_ref[rows, base:base + LANE]
        li_all = small + bi_ref[...]
        lf_all = jax.nn.log_sigmoid(small + bf_ref[...])
        if valid is not None:
            lf_all = jnp.where(valid, lf_all, 0.0)
        b_all = _cumsum_rows(lf_all)
        b_t = _rows_to_cols(b_all)
        li_t = _rows_to_cols(li_all)
        bcol = jnp.concatenate([b_all[:, nh + h:nh + h + 1] for h in range(nh)], axis=0)
        licol = jnp.concatenate([li_all[:, h:h + 1] for h in range(nh)], axis=0)
        xrow = jnp.concatenate([b_t[nh + h:nh + h + 1, :] - li_t[h:h + 1, :] for h in range(nh)], axis=1)
        if valid is not None:
            xrow = jnp.where(t * tb + ci * C + cj % C < t_valid, xrow, jnp.inf)
            licol = jnp.where(t * tb + ci * C + ri % C < t_valid, licol, -jnp.inf)
        q = stack(rows, 0, dk)
        k = stack(rows, nh * dk, dk) * dk ** -0.5
        v = stack(rows, 2 * nh * dk, dv)
        dmat = jnp.where(incl, bcol - xrow, -jnp.inf)
        prep.append(dict(q=q, k=k, v=v, bcol=bcol, licol=licol, dmat=dmat, qk=_dot_nt(q, k),
                         rowmax=jnp.max(dmat, axis=1, keepdims=True),
                         b_end=[b_all[C - 1:C, nh + h:nh + h + 1] for h in range(nh)]))

    c_cat = ct_scr[...]
    n_rows = [n_scr[h] for h in range(nh)]
    m_vals = [m_scr[h][:, 0:1] for h in range(nh)]
    for ci, p in enumerate(prep):
        rows = slice(ci * C, (ci + 1) * C)
        q, k, v, bcol = p['q'], p['k'], p['v'], p['bcol']
        m_inter = bcol + per_head_col(m_vals)
        m_t = jnp.maximum(m_inter, p['rowmax'])
        a_int = jnp.exp(m_inter - m_t)
        qk = p['qk'] * jnp.exp(p['dmat'] - m_t)
        qc = _dot(q, c_cat)
        own = _tree(jnp.add, [jnp.where(rhead == h, qc[:, h * dv:(h + 1) * dv], 0.0) for h in range(nh)])
        num = _dot(qk, v) + a_int * own
        n_st = jnp.concatenate([jnp.broadcast_to(n_rows[h], (C, dk)) for h in range(nh)], axis=0)
        den = jnp.sum(qk, axis=1, keepdims=True) + a_int * jnp.sum(q * n_st, axis=1, keepdims=True)
        hid = num / jnp.maximum(jnp.abs(den), jnp.exp(-m_t))
        last = [slice(h * C + C - 1, h * C + C) for h in range(nh)]
        m_vals = [m_t[r, :] for r in last]
        a_end = [a_int[r, :] for r in last]
        wk = jnp.exp(per_head_col(p['b_end']) - bcol + p['licol'] - per_head_col(m_vals)) * k
        v_wide = jnp.concatenate([jnp.where(rhead == h, v, 0.0) for h in range(nh)], axis=1)
        a_row = jnp.concatenate([jnp.broadcast_to(a, (1, dv)) for a in a_end], axis=1)
        c_cat = c_cat * a_row + _dot_tn(wk, v_wide)
        n_rows = [a_end[h] * n_rows[h] + jnp.sum(wk[h * C:(h + 1) * C, :], axis=0, keepdims=True) for h in range(nh)]
        hc = hid - jnp.mean(hid, -1, keepdims=True)
        hn = hc * lax.rsqrt(jnp.mean(hc * hc, -1, keepdims=True) + NORM_EPS)
        for h in range(nh):
            og = jax.nn.sigmoid(proj_ref[rows, 2 * nh * dk + nh * dv + h * dv:2 * nh * dk + nh * dv + (h + 1) * dv])
            o_ref[rows, h * dv:(h + 1) * dv] = og * (hn[h * C:(h + 1) * C, :] * ng_ref[:, h * dv:(h + 1) * dv])
    ct_scr[...] = c_cat
    for h in range(nh):
        n_scr[h] = n_rows[h]
        m_scr[h] = jnp.broadcast_to(m_vals[h], (1, LANE))

    @pl.when(t == nt - 1)
    def _():
        lane = _iota((1, nh), 1)
        mrow = jnp.zeros((1, nh), F32)
        for h in range(nh):
            cout_ref[0, h] = ct_scr[:, h * dv:(h + 1) * dv]
            nout_ref[0, h:h + 1, :] = n_scr[h]
            mrow = jnp.where(lane == h, m_scr[h][:, 0:1], mrow)
        mout_ref[0] = mrow


def _mlstm_scan(proj, c0, n0, m0, ng, bi, bf, *, nb, nt, tb, t_valid):
    _, nh, dk, dv = c0.shape
    wp = proj.shape[1]
    body = functools.partial(_mlstm_body, nh=nh, dk=dk, dv=dv, tb=tb, t_valid=t_valid, nt=nt)
    const = lambda b, t: (0, 0)
    m0 = m0.reshape(nb, 1, nh)
    return pl.pallas_call(
        body, grid=(nb, nt),
        in_specs=[pl.BlockSpec((tb, wp), lambda b, t: (b * nt + t, 0)),
                  pl.BlockSpec((1, nh, dk, dv), lambda b, t: (b, 0, 0, 0)),
                  pl.BlockSpec((1, nh, dk), lambda b, t: (b, 0, 0)),
                  pl.BlockSpec((1, 1, nh), lambda b, t: (b, 0, 0)),
                  pl.BlockSpec(ng.shape, const), pl.BlockSpec(bi.shape, const), pl.BlockSpec(bf.shape, const)],
        out_specs=[pl.BlockSpec((tb, nh * dv), lambda b, t: (b * nt + t, 0)),
                   pl.BlockSpec((1, nh, dk, dv), lambda b, t: (b, 0, 0, 0)),
                   pl.BlockSpec((1, nh, dk), lambda b, t: (b, 0, 0)),
                   pl.BlockSpec((1, 1, nh), lambda b, t: (b, 0, 0))],
        out_shape=[jax.ShapeDtypeStruct((nb * nt * tb, nh * dv), F32), jax.ShapeDtypeStruct(c0.shape, F32),
                   jax.ShapeDtypeStruct(n0.shape, F32), jax.ShapeDtypeStruct((nb, 1, nh), F32)],
        scratch_shapes=[pltpu.VMEM((dk, nh * dv), F32), pltpu.VMEM((nh, 1, dk), F32), pltpu.VMEM((nh, 1, LANE), F32)],
        compiler_params=_cparams(("parallel", "arbitrary")), name="mlstm_scan",
    )(proj, c0, n0, m0, ng, bi, bf)


SUB = 8


def _tree(op, xs):
    xs = list(xs)
    while len(xs) > 1:
        xs = [op(xs[i], xs[i + 1]) if i + 1 < len(xs) else xs[i] for i in range(0, len(xs), 2)]
    return xs[0]


def _all_sublanes(op, m):
    for shift in (4, 2, 1):
        m = op(m, pltpu.roll(m, shift, axis=0))
    return m


def _sorting_network(n):
    def merge(lo, hi, r):
        step = r * 2
        if step < hi - lo:
            yield from merge(lo, hi, step)
            yield from merge(lo + r, hi, step)
            yield from [(i, i + r) for i in range(lo + r, hi - r, step)]
        else:
            yield (lo, lo + r)

    def sort(lo, hi):
        if hi - lo >= 1:
            mid = lo + (hi - lo) // 2
            yield from sort(lo, mid)
            yield from sort(mid + 1, hi)
            yield from merge(lo, hi, 1)

    return list(sort(0, n - 1))


def _sort_tiles_desc(tiles):
    tiles = list(tiles)
    for i, j in _sorting_network(len(tiles)):
        tiles[i], tiles[j] = jnp.maximum(tiles[i], tiles[j]), jnp.minimum(tiles[i], tiles[j])
    return tiles


def _merge_top(lists, singles=None):
    lists = list(lists)
    vals = []
    for j in range(PEER_TOPK):
        head = lists[0] if singles is None else jnp.maximum(lists[0], singles)
        m = _all_sublanes(jnp.maximum, head)
        vals.append(m)
        if j == PEER_TOPK - 1:
            break
        hit = lists[0] == m
        for i in range(min(len(lists) - 1, PEER_TOPK - 1 - j)):
            lists[i] = jnp.where(hit, lists[i + 1], lists[i])
        if singles is not None:
            singles = jnp.where(singles == m, -jnp.inf, singles)
    return vals


def _route_tile(q_scr, k1_ref, k2_ref, r2_ref, c1_ref, a1_ref, a2_ref, h, lanes):
    ntile = PEER_NKEYS // SUB
    sub = _iota((SUB, LANE), 0)
    if True:
        q1 = q_scr[pl.ds(pl.multiple_of(h * 2 * PEER_HALF, PEER_HALF), PEER_HALF), lanes]
        q2 = q_scr[pl.ds(pl.multiple_of(h * 2 * PEER_HALF + PEER_HALF, PEER_HALF), PEER_HALF), lanes]
        s1 = _dot(k1_ref[...], q1, HI)
        s2 = _dot(k2_ref[...], q2, HI)
        t1 = [s1[k * SUB:(k + 1) * SUB, :] for k in range(ntile)]
        t2 = [s2[k * SUB:(k + 1) * SUB, :] for k in range(ntile)]
        v1 = _merge_top(_sort_tiles_desc(t1))
        v2 = _merge_top(_sort_tiles_desc(t2))
        v2lo, v2hi = v2[SUB - 1], v2[2 * SUB - 1]
        for b in range(SUB - 2, -1, -1):
            v2lo = jnp.where(sub == b, v2[b], v2lo)
            v2hi = jnp.where(sub == b, v2[SUB + b], v2hi)
        cand_lo = [jnp.where(sub < min(PEER_TOPK // (a + 1), SUB), v1[a] + v2lo, -jnp.inf) for a in range(PEER_TOPK)]
        cand_hi = v1[0] + v2hi
        tau = _merge_top(cand_lo, cand_hi)[PEER_TOPK - 1]
        top = v1[0] + v2[0]
        z = _all_sublanes(jnp.add, _tree(jnp.add, [jnp.where(cd >= tau, jnp.exp(cd - top), 0.0)
                                                   for cd in cand_lo + [cand_hi]]))
        inv_z = 1.0 / z
        pairs = [_all_sublanes(jnp.add, jnp.where(cd >= tau, 1.0, 0.0)) for cd in cand_lo]
        pairs[0] = pairs[0] + _all_sublanes(jnp.add, jnp.where(cand_hi >= tau, 1.0, 0.0))
        rank2, c1 = [], []
        for k in range(ntile):
            r = jnp.full((SUB, LANE), float(PEER_TOPK), F32)
            cnt = jnp.zeros((SUB, LANE), F32)
            for j in range(PEER_TOPK - 1, -1, -1):
                r = jnp.where(t2[k] >= v2[j], float(j), r)
                cnt = jnp.where(t1[k] == v1[j], pairs[j], cnt)
            rank2.append(r)
            c1.append(cnt)
        r2_ref[h, :, lanes] = jnp.concatenate(rank2, axis=0).astype(BF16)
        c1_ref[h, :, lanes] = jnp.concatenate(c1, axis=0)
        a1_ref[h, :, lanes] = jnp.concatenate([jnp.exp(t - v1[0]) for t in t1], axis=0)
        a2_ref[h, :, lanes] = jnp.concatenate([jnp.exp(t - v2[0]) * inv_z for t in t2], axis=0).astype(BF16)


def _peer_body(xm_ref, xr_ref, wq_ref, k1_ref, k2_ref, u_ref, vt_ref, g_ref, b_ref, out_ref,
               acc, xbf, p_scr, q_scr, r2_scr, c1_scr, a1_scr, a2_scr):
    r = pl.program_id(0)
    c = pl.program_id(1)
    tm = xbf.shape[1]

    @pl.when(c == 0)
    def _():
        xbf[...] = xm_ref[...].astype(BF16)
        acc[...] = jnp.zeros(acc.shape, F32)
        p_scr[...] = jnp.zeros(p_scr.shape, BF16)
        q_scr[...] = _dot(wq_ref[...], xr_ref[...].astype(BF16))

    @pl.when((c == 0) & (r == 0))
    def _():
        for scr in (r2_scr, c1_scr, a1_scr, a2_scr):
            scr[...] = jnp.zeros(scr.shape, scr.dtype)

    ntile = tm // LANE
    units = pl.num_programs(1) - 1
    tiles_per_unit = PEER_HEADS * ntile // units
    unit = (c + units - 1) % units
    h_route = unit // (ntile // tiles_per_unit)
    lane0 = (unit % (ntile // tiles_per_unit)) * tiles_per_unit * LANE
    wr = r % 2
    for sb in range(tiles_per_unit):
        _route_tile(q_scr, k1_ref, k2_ref, r2_scr.at[wr], c1_scr.at[wr], a1_scr.at[wr], a2_scr.at[wr], h_route,
                    pl.ds(pl.multiple_of(lane0 + sb * LANE, LANE), LANE))

    rd = (r + 1) % 2
    i1_base = jnp.minimum(c, pl.num_programs(1) - 2) * PEER_I1_PER_STEP
    rows16 = 2 * SUB
    sub_w = PEER_SUB_I1 * PEER_NKEYS
    total = None
    for s in range(PEER_I1_PER_STEP // PEER_SUB_I1):
        es = slice(s * sub_w, (s + 1) * sub_w)
        part = _dot(vt_ref[0, :, es], p_scr[es, :])
        total = part if total is None else total + part
        hid = _dot(u_ref[es, :], xbf[...])
        act = (0.5 * hid * (1.0 + lax.erf(hid * 0.5 ** 0.5))).astype(BF16)
        cols = []
        for l0 in range(0, tm, PEER_LANE_BLOCK):
            ls = slice(l0, l0 + PEER_LANE_BLOCK)
            tiles = []
            for j in range(s * PEER_SUB_I1, (s + 1) * PEER_SUB_I1):
                row = pl.ds(i1_base + j, 1)
                c1b = [jnp.broadcast_to(c1_scr[rd, h, row, ls], (rows16, PEER_LANE_BLOCK)).astype(BF16)
                       for h in range(PEER_HEADS)]
                a1b = [jnp.broadcast_to(a1_scr[rd, h, row, ls], (rows16, PEER_LANE_BLOCK)).astype(BF16)
                       for h in range(PEER_HEADS)]
                for kb in range(PEER_NKEYS // rows16):
                    ks = slice(kb * rows16, (kb + 1) * rows16)
                    tiles.append(_tree(jnp.add, [jnp.where(r2_scr[rd, h, ks, ls] < c1b[h],
                                                           a2_scr[rd, h, ks, ls] * a1b[h], 0.0)
                                                 for h in range(PEER_HEADS)]))
            cols.append(jnp.concatenate(tiles, axis=0))
        p_scr[es, :] = jnp.concatenate(cols, axis=1) * act
    acc[...] += total

    @pl.when(c == pl.num_programs(1) - 1)
    def _():
        z = DN_ALPHA * xm_ref[...] + acc[...]
        zc = z - jnp.mean(z, 0, keepdims=True)
        var = jnp.mean(zc * zc, 0, keepdims=True)
        out_ref[...] = (zc * lax.rsqrt(var + LN_EPS) * g_ref[...] + b_ref[...]).T


def _peer(xt, wq_t, k1, k2, u, vt, g, b, tm):
    d, n = xt.shape
    nb = n // tm
    ec = PEER_I1_PER_STEP * PEER_NKEYS
    nc = u.shape[0] // ec
    const = lambda r, c: (0, 0)
    rshape = (2, PEER_HEADS, PEER_NKEYS, tm)
    return pl.pallas_call(
        _peer_body, grid=(nb + 1, nc + 1),
        in_specs=[pl.BlockSpec((d, tm), lambda r, c: (0, jnp.maximum(r - 1, 0))),
                  pl.BlockSpec((d, tm), lambda r, c: (0, jnp.minimum(r, nb - 1))),
                  pl.BlockSpec(wq_t.shape, const), pl.BlockSpec(k1.shape, const), pl.BlockSpec(k2.shape, const),
                  pl.BlockSpec((ec, d), lambda r, c: (jnp.minimum(c, nc - 1), 0)),
                  pl.BlockSpec((1, d, ec), lambda r, c: (jnp.maximum(c - 1, 0), 0, 0)),
                  pl.BlockSpec((d, 1), const), pl.BlockSpec((d, 1), const)],
        out_specs=pl.BlockSpec((tm, d), lambda r, c: (jnp.maximum(r - 1, 0), 0)),
        out_shape=jax.ShapeDtypeStruct((n, d), F32),
        scratch_shapes=[pltpu.VMEM((d, tm), F32), pltpu.VMEM((d, tm), BF16), pltpu.VMEM((ec, tm), BF16),
                        pltpu.VMEM((wq_t.shape[0], tm), F32), pltpu.VMEM(rshape, BF16), pltpu.VMEM(rshape, F32),
                        pltpu.VMEM(rshape, F32), pltpu.VMEM(rshape, BF16)],
        compiler_params=pltpu.CompilerParams(dimension_semantics=("arbitrary", "arbitrary"),
                                             vmem_limit_bytes=PEER_VMEM_LIMIT), name="peer",
    )(xt, xt, wq_t, k1, k2, u, vt, g.reshape(d, 1), b.reshape(d, 1))


def _pad_cols(w, width):
    return jnp.pad(w, ((0, 0), (0, width - w.shape[1])))


def _lane_row(*parts):
    v = jnp.concatenate([p.astype(F32).reshape(-1) for p in parts])
    return jnp.pad(v, (0, LANE - v.shape[0])).reshape(1, LANE)


def _pick_tb(t_pad):
    for tb in (48, 32, 16):
        if t_pad % tb == 0:
            return tb
    raise ValueError(t_pad)


def kernel(x_prompt, x_sample, state_hgrn_S, state_gdn_S, state_gdn_conv, state_mlstm_C, state_mlstm_n, state_mlstm_m, state_gla_S, meta_tokens, hgrn_w_in, hgrn_lb, hgrn_norm_g, hgrn_w_out, gdn_w_in, gdn_conv_w, gdn_a_log, gdn_dt_bias, gdn_norm_g, gdn_w_out, mlstm_w_in, mlstm_b_i, mlstm_b_f, mlstm_norm_g, mlstm_w_out, gla_w_in, gla_w_gate, gla_b_gate, gla_norm_g, gla_w_out, peer_w_q, peer_keys1, peer_keys2, peer_u, peer_v, ln_g, ln_b):
    bp, seq, d = x_prompt.shape
    bs, seq_s, _ = x_sample.shape
    assert d == D_MODEL and len(state_hgrn_S) == 1 and len(state_gdn_S) == 1
    assert len(state_mlstm_C) == 1 and len(state_gla_S) == 1
    tp = N_META + seq
    tp_pad = -(-tp // C) * C
    ts_pad = -(-seq_s // C) * C
    tbp = _pick_tb(tp_pad)
    tbs = _pick_tb(ts_pad)
    np_rows = bp * tp_pad
    n_real = np_rows + bs * seq_s

    meta = jnp.broadcast_to(meta_tokens.astype(F32)[None], (bp, N_META, d))
    hp = jnp.concatenate([meta, x_prompt], axis=1)
    hp = jnp.pad(hp, ((0, 0), (0, tp_pad - tp), (0, 0))).reshape(np_rows, d)
    h = jnp.concatenate([hp, x_sample.reshape(bs * seq_s, d)], axis=0)
    n = -(-n_real // TOK_MULTIPLE) * TOK_MULTIPLE
    h = jnp.pad(h, ((0, n - n_real), (0, 0)))
    tm_mm = MM_TOK_BLOCK

    def split_cols(w, sizes):
        out, o = [], 0
        for s in sizes:
            out.append(w[:, o:o + s])
            o += s
        return out

    def run_mixer(scan, proj, states_p, states_s, **kw):
        res_p = scan(proj, *states_p, nb=bp, nt=tp_pad // tbp, tb=tbp, t_valid=tp, **kw)
        ps = proj[np_rows:n_real].reshape(bs, seq_s, -1)
        ps = jnp.pad(ps, ((0, 0), (0, ts_pad - seq_s), (0, 0))).reshape(bs * ts_pad, -1)
        res_s = scan(ps, *states_s, nb=bs, nt=ts_pad // tbs, tb=tbs, t_valid=seq_s, **kw)
        o_s = res_s[0].reshape(bs, ts_pad, -1)[:, :seq_s].reshape(bs * seq_s, -1)
        o = jnp.concatenate([res_p[0], o_s], axis=0)
        o = jnp.pad(o, ((0, n - n_real), (0, 0)))
        return o, res_p[1:], res_s[1:]

    lb_all = jnp.cumsum(jax.nn.softmax(hgrn_lb.astype(F32), axis=0), axis=0)
    outs = {}
    for i in range(DEPTH):
        mix = i % 4
        if mix == 0:
            w_in = hgrn_w_in[0].astype(BF16)
            proj = _mm(h, w_in, tm_mm, 1024)
            nh, dk, dv = state_hgrn_S.shape[2:]
            zero = jnp.zeros((bp, nh, dk, dv), F32)
            scan = functools.partial(_gla_scan, ng=hgrn_norm_g[0].reshape(1, dv), p1=lb_all[i].reshape(1, nh * dk),
                                     p2=jnp.zeros((1, LANE), F32), hgrn=True)
            o, (sp,), (ss,) = run_mixer(scan, proj, (zero,), (state_hgrn_S[0],))
            outs['hgrn'] = (sp[None], ss[None])
            w_out = hgrn_w_out[0]
        elif mix == 1:
            nh, dk, dv = state_gdn_S.shape[2:]
            cwid = 2 * nh * dk + nh * dv
            qkv, a, b, g = split_cols(gdn_w_in[0], [cwid, nh, nh, nh * dv])
            w_in = jnp.concatenate([qkv, g, _pad_cols(jnp.concatenate([a, b], axis=1), LANE)], axis=1).astype(BF16)
            proj = _mm(h, w_in, tm_mm, w_in.shape[1] // 3)
            scan = functools.partial(_gdn_scan, cw=gdn_conv_w[0], ng=gdn_norm_g[0].reshape(1, dv),
                                     alog=_lane_row(gdn_a_log[0]), dtb=_lane_row(gdn_dt_bias[0]))
            o, (sp, cp), (ss, cs) = run_mixer(
                scan, proj,
                (jnp.zeros((bp, nh, dk, dv), F32), jnp.zeros((bp, CONV_W - 1, cwid), F32)),
                (state_gdn_S[0], state_gdn_conv[0]))
            outs['gdn'] = (sp[None], cp[None], ss[None], cs[None])
            w_out = gdn_w_out[0]
        elif mix == 2:
            nh, dk, dv = state_mlstm_C.shape[2:]
            q, k, v, ig, fg, og = split_cols(mlstm_w_in[0], [nh * dk, nh * dk, nh * dv, nh, nh, nh * dv])
            w_in = jnp.concatenate([q, k, v, og, _pad_cols(jnp.concatenate([ig, fg], axis=1), LANE)],
                                   axis=1).astype(BF16)
            proj = _mm(h, w_in, tm_mm, w_in.shape[1] // 3)
            zeros_nh = jnp.zeros((nh,), F32)
            scan = functools.partial(_mlstm_scan, ng=mlstm_norm_g[0].reshape(1, nh * dv),
                                     bi=_lane_row(mlstm_b_i[0]), bf=_lane_row(zeros_nh, mlstm_b_f[0]))
            o, (cp, npp, mp), (cs, ns, ms) = run_mixer(
                scan, proj,
                (jnp.zeros((bp, nh, dk, dv), F32), jnp.zeros((bp, nh, dk), F32), jnp.zeros((bp, nh), F32)),
                (state_mlstm_C[0], state_mlstm_n[0], state_mlstm_m[0]))
            outs['mlstm'] = (cp[None], npp[None], mp.reshape(1, bp, nh), cs[None], ns[None], ms.reshape(1, bs, nh))
            w_out = mlstm_w_out[0]
        else:
            nh, dk, dv = state_gla_S.shape[2:]
            rank = gla_w_gate.shape[1]
            q, k, v, g, r = split_cols(gla_w_in[0], [nh * dk, nh * dk, nh * dv, nh * dv, rank])
            w_in = jnp.concatenate([q, k, v, g, _pad_cols(r, LANE)], axis=1).astype(BF16)
            proj = _mm(h, w_in, tm_mm, w_in.shape[1] // 5)
            wg = jnp.pad(gla_w_gate[0].astype(F32), ((0, LANE - rank), (0, 0)))
            scan = functools.partial(_gla_scan, ng=gla_norm_g[0].reshape(1, dv), p1=wg,
                                     p2=gla_b_gate[0].reshape(1, nh * dk), hgrn=False)
            o, (sp,), (ss,) = run_mixer(scan, proj, (jnp.zeros((bp, nh, dk, dv), F32),), (state_gla_S[0],))
            outs['gla'] = (sp[None], ss[None])
            w_out = gla_w_out[0]

        h1t = _mm_res_ln_t(o, w_out.astype(BF16), h, ln_g[i, 0], ln_b[i, 0], TOK_BLOCK)
        ec = PEER_I1_PER_STEP * PEER_NKEYS
        vt = peer_v[i].astype(BF16).reshape(-1, ec, d).transpose(0, 2, 1)
        h = _peer(h1t, peer_w_q[i].T.astype(BF16), peer_keys1[i], peer_keys2[i], peer_u[i].astype(BF16), vt,
                  ln_g[i, 1], ln_b[i, 1], PEER_TOK_BLOCK)

    y_prompt = h[:np_rows].reshape(bp, tp_pad, d)[:, N_META:tp]
    y_sample = h[np_rows:n_real].reshape(bs, seq_s, d)
    hg, gd, ml, gl = outs['hgrn'], outs['gdn'], outs['mlstm'], outs['gla']
    return (y_prompt, y_sample, hg[0], gd[0], gd[1], ml[0], ml[1], ml[2], gl[0],
            hg[1], gd[2], gd[3], ml[3], ml[4], ml[5], gl[1])
```

```python
import functools

import jax
import jax.numpy as jnp
from jax import lax
from jax.experimental import pallas as pl
from jax.experimental.pallas import tpu as pltpu

F32 = jnp.float32
BF16 = jnp.bfloat16
HI = lax.Precision.HIGHEST

D_MODEL = 1024
DEPTH = 4
N_META = 16
C = 16
CONV_W = 4
NORM_EPS = 1e-6
LN_EPS = 1e-5
DN_ALPHA = (2 * DEPTH) ** 0.25
GLA_GATE_NORM = 16.0
LANE = 128
PEER_HEADS = 8
PEER_NKEYS = 128
PEER_TOPK = 16
PEER_HALF = 128
PEER_I1_PER_STEP = 8
PEER_SUB_I1 = 2
PEER_VMEM_LIMIT = 56 * 1024 * 1024
TOK_BLOCK = 256
PEER_TOK_BLOCK = 512
PEER_LANE_BLOCK = 256
MM_TOK_BLOCK = 768
TOK_MULTIPLE = 1536
VMEM_LIMIT = 48 * 1024 * 1024


def _cparams(sem):
    return pltpu.CompilerParams(dimension_semantics=sem, vmem_limit_bytes=VMEM_LIMIT)


def _iota(shape, dim):
    return lax.broadcasted_iota(jnp.int32, shape, dim)


def _dot_nt(a, b, precision=None):
    return lax.dot_general(a, b, (((1,), (1,)), ((), ())), precision=precision, preferred_element_type=F32)


def _dot_tn(a, b, precision=None):
    return lax.dot_general(a, b, (((0,), (0,)), ((), ())), precision=precision, preferred_element_type=F32)


def _dot(a, b, precision=None):
    return jnp.dot(a, b, precision=precision, preferred_element_type=F32)


def _split(x):
    hi = x.astype(BF16)
    return hi, (x - hi.astype(F32)).astype(BF16)


def _dot3(a, b, nt=False):
    (ah, al), (bh, bl) = a, b
    d = _dot_nt if nt else _dot
    return d(ah, bh) + d(ah, bl) + d(al, bh)


def _cumsum_rows(x):
    tri = (_iota((C, C), 0) >= _iota((C, C), 1)).astype(F32)
    return _dot(tri, x, HI)


def _rows_to_cols(x):
    eye = (_iota((LANE, LANE), 0) == _iota((LANE, LANE), 1)).astype(F32)
    return _dot_nt(eye, x, HI)


def _mm_body(x_ref, w_ref, o_ref):
    o_ref[...] = _dot(x_ref[...].astype(BF16), w_ref[...])


def _mm(x, w, tm, tn):
    m, k = x.shape
    n = w.shape[1]
    return pl.pallas_call(
        _mm_body, grid=(m // tm, n // tn),
        in_specs=[pl.BlockSpec((tm, k), lambda i, j: (i, 0)), pl.BlockSpec((k, tn), lambda i, j: (0, j))],
        out_specs=pl.BlockSpec((tm, tn), lambda i, j: (i, j)),
        out_shape=jax.ShapeDtypeStruct((m, n), F32),
        compiler_params=_cparams(("parallel", "parallel")), name="mm")(x, w)


def _mm_ln_body(o_ref, w_ref, h_ref, g_ref, b_ref, out_ref):
    y = _dot(o_ref[...].astype(BF16), w_ref[...])
    z = DN_ALPHA * h_ref[...] + y
    zc = z - jnp.mean(z, -1, keepdims=True)
    var = jnp.mean(zc * zc, -1, keepdims=True)
    out_ref[...] = (zc * lax.rsqrt(var + LN_EPS) * g_ref[...] + b_ref[...]).T


def _mm_res_ln_t(o, w, h, g, b, tm):
    n, k = o.shape
    d = w.shape[1]
    return pl.pallas_call(
        _mm_ln_body, grid=(n // tm,),
        in_specs=[pl.BlockSpec((tm, k), lambda i: (i, 0)), pl.BlockSpec((k, d), lambda i: (0, 0)),
                  pl.BlockSpec((tm, d), lambda i: (i, 0)), pl.BlockSpec((1, d), lambda i: (0, 0)),
                  pl.BlockSpec((1, d), lambda i: (0, 0))],
        out_specs=pl.BlockSpec((d, tm), lambda i: (0, i)),
        out_shape=jax.ShapeDtypeStruct((d, n), F32),
        compiler_params=_cparams(("parallel",)), name="mm_res_ln")(o, w, h, g.reshape(1, d), b.reshape(1, d))


def _valid_rows(t, r0, tb, t_valid, nt):
    if t_valid == nt * tb:
        return None
    return (t * tb + r0 + _iota((C, 1), 0)) < t_valid


def _gla_body(*refs, nseq, hgrn, nh, dk, dv, tb, t_valid, nt):
    proj_refs = refs[:nseq]
    s0_ref, ng_ref, p1_ref, p2_ref, o_ref, sout_ref, st_scr = refs[nseq:]
    t = pl.program_id(1)

    @pl.when(t == 0)
    def _():
        for sq in range(nseq):
            for h in range(nh):
                st_scr[sq, h] = s0_ref[sq, h].T

    kw = nh * dk
    vw = nh * dv
    nr = nh * C
    ri = _iota((nr, 1), 0)
    cj = _iota((1, nr), 1)
    rhead = ri // C
    tri = ((rhead == cj // C) & (ri % C >= cj % C)).astype(F32)
    row3 = _iota((1, C, 1), 1)

    def stack(ref, rows, col0, width):
        return jnp.concatenate([ref[rows, col0 + h * width:col0 + (h + 1) * width] for h in range(nh)], axis=0)

    prep = {}
    for sq, proj_ref in enumerate(proj_refs):
        for ci in range(tb // C):
            rows = slice(ci * C, (ci + 1) * C)
            if hgrn:
                q = jax.nn.silu(stack(proj_ref, rows, 0, dk))
                lb = jnp.concatenate([jnp.broadcast_to(p1_ref[:, h * dk:(h + 1) * dk], (C, dk)) for h in range(nh)],
                                     axis=0)
                fg = lb + (1.0 - lb) * jax.nn.sigmoid(stack(proj_ref, rows, kw, dk))
                k = 1.0 - fg
                g = jnp.log(fg)
            else:
                r = proj_ref[rows, 2 * kw + 2 * vw:2 * kw + 2 * vw + LANE]
                logf_all = jax.nn.log_sigmoid(_dot(r, p1_ref[...], HI) + p2_ref[...]) / GLA_GATE_NORM
                q = stack(proj_ref, rows, 0, dk) * dk ** -0.5
                k = stack(proj_ref, rows, kw, dk)
                g = jnp.concatenate([logf_all[:, h * dk:(h + 1) * dk] for h in range(nh)], axis=0)
            v = stack(proj_ref, rows, 2 * kw, dv)
            if t_valid != nt * tb:
                valid = t * tb + ci * C + ri % C < t_valid
                k = jnp.where(valid, k, 0.0)
                g = jnp.where(valid, g, 0.0)
            b = _dot(tri, g, HI)
            b3, q3, k3, v3 = (x.reshape(nh, C, x.shape[-1]) for x in (b, q, k, v))
            o3 = jnp.zeros((nh, C, dv), F32)
            for s in range(C):
                d = jnp.exp(jnp.where(row3 >= s, b3 - b3[:, s:s + 1, :], -jnp.inf))
                col = jnp.sum(q3 * (k3[:, s:s + 1, :] * d), axis=-1, keepdims=True)
                o3 = o3 + col * v3[:, s:s + 1, :]
            b_end = b3[:, C - 1:C, :]
            prep[sq, ci] = dict(o=o3.reshape(nr, dv), qe=q * jnp.exp(b), kd=(k3 * jnp.exp(b_end - b3)).reshape(nr, dk),
                                dec=jnp.exp(b_end), v=v)

    sts = [st_scr[sq] for sq in range(nseq)]
    for ci in range(tb // C):
        rows = slice(ci * C, (ci + 1) * C)
        for sq in range(nseq):
            p = prep[sq, ci]
            inter = _dot_nt(p['qe'], sts[sq].reshape(nh * dv, dk))
            o = p['o'] + _tree(jnp.add, [jnp.where(rhead == h, inter[:, h * dv:(h + 1) * dv], 0.0)
                                         for h in range(nh)])
            v_wide = jnp.concatenate([jnp.where(rhead == h, p['v'], 0.0) for h in range(nh)], axis=1)
            sts[sq] = sts[sq] * p['dec'] + _dot_tn(v_wide, p['kd']).reshape(nh, dv, dk)
            o = o * lax.rsqrt(jnp.mean(o * o, -1, keepdims=True) + NORM_EPS) * ng_ref[...]
            for h in range(nh):
                gate = proj_refs[sq][rows, 2 * kw + vw + h * dv:2 * kw + vw + (h + 1) * dv]
                o_ref[sq, rows, h * dv:(h + 1) * dv] = o[h * C:(h + 1) * C, :] * jax.nn.silu(gate)
    for sq in range(nseq):
        st_scr[sq] = sts[sq]

    @pl.when(t == nt - 1)
    def _():
        for sq in range(nseq):
            for h in range(nh):
                sout_ref[sq, h] = st_scr[sq, h].T


def _seqs_per_step(nb, chunks):
    return max(n for n in (8, 4, 2, 1) if nb % n == 0 and n * chunks <= 8)


def _proj_specs(nseq, nt, tb, wp):
    return [pl.BlockSpec((tb, wp), functools.partial(lambda b, t, sq: ((b * nseq + sq) * nt + t, 0), sq=sq))
            for sq in range(nseq)]


def _gla_scan(proj, s0, ng, p1, p2, *, hgrn, nb, nt, tb, t_valid):
    _, nh, dk, dv = s0.shape
    wp = proj.shape[1]
    nseq = _seqs_per_step(nb, tb // C)
    body = functools.partial(_gla_body, nseq=nseq, hgrn=hgrn, nh=nh, dk=dk, dv=dv, tb=tb, t_valid=t_valid, nt=nt)
    const = lambda b, t: (0, 0)
    o, s = pl.pallas_call(
        body, grid=(nb // nseq, nt),
        in_specs=_proj_specs(nseq, nt, tb, wp) + [
            pl.BlockSpec((nseq, nh, dk, dv), lambda b, t: (b, 0, 0, 0)),
            pl.BlockSpec(ng.shape, const), pl.BlockSpec(p1.shape, const), pl.BlockSpec(p2.shape, const)],
        out_specs=[pl.BlockSpec((nseq, tb, nh * dv), lambda b, t: (b, t, 0)),
                   pl.BlockSpec((nseq, nh, dk, dv), lambda b, t: (b, 0, 0, 0))],
        out_shape=[jax.ShapeDtypeStruct((nb, nt * tb, nh * dv), F32), jax.ShapeDtypeStruct(s0.shape, F32)],
        scratch_shapes=[pltpu.VMEM((nseq, nh, dv, dk), F32)],
        compiler_params=_cparams(("parallel", "arbitrary")), name="hgrn_scan" if hgrn else "gla_scan",
    )(*([proj] * nseq), s0, ng, p1, p2)
    return o.reshape(nb * nt * tb, nh * dv), s


def _gdn_body(*refs, nseq, nh, dk, dv, tb, t_valid, nt):
    proj_refs = refs[:nseq]
    (s0_ref, conv0_ref, cw_ref, ng_ref, alog_ref, dtb_ref, o_ref, sout_ref, convout_ref,
     st_scr, cbuf, ybuf) = refs[nseq:]
    t = pl.program_id(1)
    cwid = 2 * nh * dk + nh * dv
    pre = 8 - (CONV_W - 1)

    @pl.when(t == 0)
    def _():
        for sq in range(nseq):
            for h in range(nh):
                st_scr[sq, :, h * dv:(h + 1) * dv] = s0_ref[sq, h]
            cbuf[sq, 0:8, :] = jnp.zeros((8, cwid), F32)
            cbuf[sq, pre:8, :] = conv0_ref[sq]

    for sq in range(nseq):
        cbuf[sq, 8:8 + tb, :] = proj_refs[sq][:, 0:cwid]
        y = cbuf[sq, pre:pre + tb, :] * cw_ref[0:1, :]
        for j in range(1, CONV_W):
            y = y + cbuf[sq, pre + j:pre + j + tb, :] * cw_ref[j:j + 1, :]
        ybuf[sq] = jax.nn.silu(y)

    nr = nh * C
    ri = _iota((nr, 1), 0)
    cj = _iota((1, nr), 1)
    rhead = ri // C
    same = rhead == cj // C
    incl = same & (ri % C >= cj % C)
    strict = same & (ri % C > cj % C)
    eye = (ri == cj).astype(F32)

    def stack(ref, rows, base, width):
        return jnp.concatenate([ref[rows, base + h * width:base + (h + 1) * width] for h in range(nh)], axis=0)

    def own_block(x):
        return _tree(jnp.add, [jnp.where(rhead == h, x[:, h * dv:(h + 1) * dv], 0.0) for h in range(nh)])

    prep = []
    for sq, ci in [(sq, ci) for sq in range(nseq) for ci in range(tb // C)]:
        rows = slice(ci * C, (ci + 1) * C)
        valid = _valid_rows(t, ci * C, tb, t_valid, nt)
        small = proj_refs[sq][rows, cwid + nh * dv:cwid + nh * dv + LANE]
        g_all = -jnp.exp(alog_ref[...]) * jax.nn.softplus(small + dtb_ref[...])
        beta_all = jax.nn.sigmoid(small)
        if valid is not None:
            g_all = jnp.where(valid, g_all, 0.0)
            beta_all = jnp.where(valid, beta_all, 0.0)
        b_all = _cumsum_rows(g_all)
        b_t = _rows_to_cols(b_all)
        bcol = jnp.concatenate([b_all[:, h:h + 1] for h in range(nh)], axis=0)
        brow = jnp.concatenate([b_t[h:h + 1, :] for h in range(nh)], axis=1)
        beta = jnp.concatenate([beta_all[:, nh + h:nh + h + 1] for h in range(nh)], axis=0)
        bend = jnp.concatenate([jnp.broadcast_to(b_all[C - 1:C, h:h + 1], (C, 1)) for h in range(nh)], axis=0)
        dec_row = jnp.concatenate([jnp.broadcast_to(jnp.exp(b_all[C - 1:C, h:h + 1]), (1, dv)) for h in range(nh)],
                                  axis=1)
        q = stack(ybuf.at[sq], rows, 0, dk)
        k = stack(ybuf.at[sq], rows, nh * dk, dk)
        v = stack(ybuf.at[sq], rows, 2 * nh * dk, dv)
        q = q * lax.rsqrt(jnp.sum(q * q, -1, keepdims=True) + NORM_EPS) * dk ** -0.5
        k = k * lax.rsqrt(jnp.sum(k * k, -1, keepdims=True) + NORM_EPS)
        if valid is not None:
            valid_st = jnp.concatenate([valid] * nh, axis=0)
            q = jnp.where(valid_st, q, 0.0)
            k = jnp.where(valid_st, k, 0.0)
            v = jnp.where(valid_st, v, 0.0)
        decay = jnp.exp(jnp.where(incl, bcol - brow, -jnp.inf))
        eb = jnp.exp(bcol)
        ks = _split(k)
        prep.append(dict(a=jnp.where(strict, _dot3(ks, ks, nt=True) * decay * beta, 0.0), bv=beta * v,
                         bk=(beta * eb) * k, att=_dot_nt(q, k) * decay, qe=q * eb, kd=k * jnp.exp(bend - bcol),
                         dec_row=dec_row, sq=sq, ci=ci))
    for p in prep:
        a_s = _split(p['a'])
        p['a2'] = _dot3(a_s, a_s)
    for p in prep:
        a2_s = _split(p['a2'])
        p['a4'] = _dot3(a2_s, a2_s)
        p['lo'] = _dot3(_split(eye - p['a']), _split(eye + p['a2']))
    for p in prep:
        a4_s = _split(p['a4'])
        p['a8'] = _dot3(a4_s, a4_s)
    for p in prep:
        p['hi'] = _dot3(_split(eye + p['a4']), _split(eye + p['a8']))
    for p in prep:
        p['tinv'] = _split(_dot3(_split(p['lo']), _split(p['hi'])))
    for p in prep:
        p['u'] = _dot3(p['tinv'], _split(p['bv']))
        p['wk'] = _dot3(p['tinv'], _split(p['bk']))

    s_cat = [st_scr[sq] for sq in range(nseq)]
    for p in sorted(prep, key=lambda p: (p['ci'], p['sq'])):
        sq = p['sq']
        rows = slice(p['ci'] * C, (p['ci'] + 1) * C)
        v_new = p['u'] - own_block(_dot(p['wk'], s_cat[sq]))
        o = _dot(p['att'], v_new) + own_block(_dot(p['qe'], s_cat[sq]))
        v_wide = jnp.concatenate([jnp.where(rhead == h, v_new, 0.0) for h in range(nh)], axis=1)
        s_cat[sq] = s_cat[sq] * p['dec_row'] + _dot_tn(p['kd'], v_wide)
        o = o * lax.rsqrt(jnp.mean(o * o, -1, keepdims=True) + NORM_EPS) * ng_ref[...]
        for h in range(nh):
            gate = proj_refs[sq][rows, cwid + h * dv:cwid + (h + 1) * dv]
            o_ref[sq, rows, h * dv:(h + 1) * dv] = o[h * C:(h + 1) * C, :] * jax.nn.silu(gate)
    for sq in range(nseq):
        st_scr[sq] = s_cat[sq]

    @pl.when(t < nt - 1)
    def _():
        for sq in range(nseq):
            cbuf[sq, pre:8, :] = cbuf[sq, pre + tb:8 + tb, :]

    @pl.when(t == nt - 1)
    def _():
        last = t_valid - (nt - 1) * tb
        for sq in range(nseq):
            for h in range(nh):
                sout_ref[sq, h] = st_scr[sq, :, h * dv:(h + 1) * dv]
            convout_ref[sq] = cbuf[sq, pre + last:8 + last, :]


def _gdn_scan(proj, s0, conv0, cw, ng, alog, dtb, *, nb, nt, tb, t_valid):
    _, nh, dk, dv = s0.shape
    wp = proj.shape[1]
    cwid = 2 * nh * dk + nh * dv
    nseq = _seqs_per_step(nb, tb // C)
    body = functools.partial(_gdn_body, nseq=nseq, nh=nh, dk=dk, dv=dv, tb=tb, t_valid=t_valid, nt=nt)
    const = lambda b, t: (0, 0)
    o, s, conv = pl.pallas_call(
        body, grid=(nb // nseq, nt),
        in_specs=_proj_specs(nseq, nt, tb, wp) + [
            pl.BlockSpec((nseq, nh, dk, dv), lambda b, t: (b, 0, 0, 0)),
            pl.BlockSpec((nseq, CONV_W - 1, cwid), lambda b, t: (b, 0, 0)),
            pl.BlockSpec(cw.shape, const), pl.BlockSpec(ng.shape, const),
            pl.BlockSpec(alog.shape, const), pl.BlockSpec(dtb.shape, const)],
        out_specs=[pl.BlockSpec((nseq, tb, nh * dv), lambda b, t: (b, t, 0)),
                   pl.BlockSpec((nseq, nh, dk, dv), lambda b, t: (b, 0, 0, 0)),
                   pl.BlockSpec((nseq, CONV_W - 1, cwid), lambda b, t: (b, 0, 0))],
        out_shape=[jax.ShapeDtypeStruct((nb, nt * tb, nh * dv), F32), jax.ShapeDtypeStruct(s0.shape, F32),
                   jax.ShapeDtypeStruct(conv0.shape, F32)],
        scratch_shapes=[pltpu.VMEM((nseq, dk, nh * dv), F32), pltpu.VMEM((nseq, tb + 8, cwid), F32),
                        pltpu.VMEM((nseq, tb, cwid), F32)],
        compiler_params=_cparams(("parallel", "arbitrary")), name="gdn_scan",
    )(*([proj] * nseq), s0, conv0, cw, ng, alog, dtb)
    return o.reshape(nb * nt * tb, nh * dv), s, conv


def _mlstm_body(*refs, nseq, nh, dk, dv, tb, t_valid, nt):
    proj_refs = refs[:nseq]
    (c0_ref, n0_ref, m0_ref, ng_ref, bi_ref, bf_ref, o_ref, cout_ref, nout_ref, mout_ref,
     ct_scr, n_scr, m_scr) = refs[nseq:]
    t = pl.program_id(1)

    @pl.when(t == 0)
    def _():
        for sq in range(nseq):
            for h in range(nh):
                ct_scr[sq, :, h * dv:(h + 1) * dv] = c0_ref[sq, h]
                n_scr[sq, h] = n0_ref[sq, h:h + 1, :]
                m_scr[sq, h] = jnp.broadcast_to(m0_ref[sq, :, h:h + 1], (1, LANE))

    base = 2 * nh * dk + 2 * nh * dv
    nr = nh * C
    ri = _iota((nr, 1), 0)
    cj = _iota((1, nr), 1)
    rhead = ri // C
    incl = (rhead == cj // C) & (ri % C >= cj % C)

    def stack(ref, rows, col0, width):
        return jnp.concatenate([ref[rows, col0 + h * width:col0 + (h + 1) * width] for h in range(nh)], axis=0)

    def per_head_col(vals):
        return jnp.concatenate([jnp.broadcast_to(x, (C, 1)) for x in vals], axis=0)

    prep = {}
    for sq, ci in [(sq, ci) for sq in range(nseq) for ci in range(tb // C)]:
        proj_ref = proj_refs[sq]
        rows = slice(ci * C, (ci + 1) * C)
        valid = _valid_rows(t, ci * C, tb, t_valid, nt)
        small = proj_ref[rows, base:base + LANE]
        li_all = small + bi_ref[...]
        lf_all = jax.nn.log_sigmoid(small + bf_ref[...])
        if valid is not None:
            lf_all = jnp.where(valid, lf_all, 0.0)
        b_all = _cumsum_rows(lf_all)
        b_t = _rows_to_cols(b_all)
        li_t = _rows_to_cols(li_all)
        bcol = jnp.concatenate([b_all[:, nh + h:nh + h + 1] for h in range(nh)], axis=0)
        licol = jnp.concatenate([li_all[:, h:h + 1] for h in range(nh)], axis=0)
        xrow = jnp.concatenate([b_t[nh + h:nh + h + 1, :] - li_t[h:h + 1, :] for h in range(nh)], axis=1)
        if valid is not None:
            xrow = jnp.where(t * tb + ci * C + cj % C < t_valid, xrow, jnp.inf)
            licol = jnp.where(t * tb + ci * C + ri % C < t_valid, licol, -jnp.inf)
        q = stack(proj_ref, rows, 0, dk)
        k = stack(proj_ref, rows, nh * dk, dk) * dk ** -0.5
        v = stack(proj_ref, rows, 2 * nh * dk, dv)
        dmat = jnp.where(incl, bcol - xrow, -jnp.inf)
        prep[sq, ci] = dict(q=q, k=k, v=v, bcol=bcol, licol=licol, dmat=dmat, qk=_dot_nt(q, k),
                            rowmax=jnp.max(dmat, axis=1, keepdims=True),
                            b_end=[b_all[C - 1:C, nh + h:nh + h + 1] for h in range(nh)])

    c_cat = [ct_scr[sq] for sq in range(nseq)]
    n_rows = [[n_scr[sq, h] for h in range(nh)] for sq in range(nseq)]
    m_vals = [[m_scr[sq, h][:, 0:1] for h in range(nh)] for sq in range(nseq)]
    for ci, sq in [(ci, sq) for ci in range(tb // C) for sq in range(nseq)]:
        p = prep[sq, ci]
        rows = slice(ci * C, (ci + 1) * C)
        q, k, v, bcol = p['q'], p['k'], p['v'], p['bcol']
        m_inter = bcol + per_head_col(m_vals[sq])
        m_t = jnp.maximum(m_inter, p['rowmax'])
        a_int = jnp.exp(m_inter - m_t)
        qk = p['qk'] * jnp.exp(p['dmat'] - m_t)
        qc = _dot(q, c_cat[sq])
        own = _tree(jnp.add, [jnp.where(rhead == h, qc[:, h * dv:(h + 1) * dv], 0.0) for h in range(nh)])
        num = _dot(qk, v) + a_int * own
        n_st = jnp.concatenate([jnp.broadcast_to(n_rows[sq][h], (C, dk)) for h in range(nh)], axis=0)
        den = jnp.sum(qk, axis=1, keepdims=True) + a_int * jnp.sum(q * n_st, axis=1, keepdims=True)
        hid = num / jnp.maximum(jnp.abs(den), jnp.exp(-m_t))
        last = [slice(h * C + C - 1, h * C + C) for h in range(nh)]
        m_vals[sq] = [m_t[r, :] for r in last]
        a_end = [a_int[r, :] for r in last]
        wk = jnp.exp(per_head_col(p['b_end']) - bcol + p['licol'] - per_head_col(m_vals[sq])) * k
        v_wide = jnp.concatenate([jnp.where(rhead == h, v, 0.0) for h in range(nh)], axis=1)
        a_row = jnp.concatenate([jnp.broadcast_to(a, (1, dv)) for a in a_end], axis=1)
        c_cat[sq] = c_cat[sq] * a_row + _dot_tn(wk, v_wide)
        n_rows[sq] = [a_end[h] * n_rows[sq][h] + jnp.sum(wk[h * C:(h + 1) * C, :], axis=0, keepdims=True)
                      for h in range(nh)]
        hc = hid - jnp.mean(hid, -1, keepdims=True)
        hn = hc * lax.rsqrt(jnp.mean(hc * hc, -1, keepdims=True) + NORM_EPS)
        for h in range(nh):
            og = jax.nn.sigmoid(
                proj_refs[sq][rows, 2 * nh * dk + nh * dv + h * dv:2 * nh * dk + nh * dv + (h + 1) * dv])
            o_ref[sq, rows, h * dv:(h + 1) * dv] = og * (hn[h * C:(h + 1) * C, :] * ng_ref[:, h * dv:(h + 1) * dv])
    for sq in range(nseq):
        ct_scr[sq] = c_cat[sq]
        for h in range(nh):
            n_scr[sq, h] = n_rows[sq][h]
            m_scr[sq, h] = jnp.broadcast_to(m_vals[sq][h], (1, LANE))

    @pl.when(t == nt - 1)
    def _():
        lane = _iota((1, nh), 1)
        for sq in range(nseq):
            mrow = jnp.zeros((1, nh), F32)
            for h in range(nh):
                cout_ref[sq, h] = ct_scr[sq, :, h * dv:(h + 1) * dv]
                nout_ref[sq, h:h + 1, :] = n_scr[sq, h]
                mrow = jnp.where(lane == h, m_scr[sq, h][:, 0:1], mrow)
            mout_ref[sq] = mrow


def _mlstm_scan(proj, c0, n0, m0, ng, bi, bf, *, nb, nt, tb, t_valid):
    _, nh, dk, dv = c0.shape
    wp = proj.shape[1]
    nseq = _seqs_per_step(nb, tb // C)
    body = functools.partial(_mlstm_body, nseq=nseq, nh=nh, dk=dk, dv=dv, tb=tb, t_valid=t_valid, nt=nt)
    const = lambda b, t: (0, 0)
    m0 = m0.reshape(nb, 1, nh)
    o, c, n, m = pl.pallas_call(
        body, grid=(nb // nseq, nt),
        in_specs=_proj_specs(nseq, nt, tb, wp) + [
            pl.BlockSpec((nseq, nh, dk, dv), lambda b, t: (b, 0, 0, 0)),
            pl.BlockSpec((nseq, nh, dk), lambda b, t: (b, 0, 0)),
            pl.BlockSpec((nseq, 1, nh), lambda b, t: (b, 0, 0)),
            pl.BlockSpec(ng.shape, const), pl.BlockSpec(bi.shape, const), pl.BlockSpec(bf.shape, const)],
        out_specs=[pl.BlockSpec((nseq, tb, nh * dv), lambda b, t: (b, t, 0)),
                   pl.BlockSpec((nseq, nh, dk, dv), lambda b, t: (b, 0, 0, 0)),
                   pl.BlockSpec((nseq, nh, dk), lambda b, t: (b, 0, 0)),
                   pl.BlockSpec((nseq, 1, nh), lambda b, t: (b, 0, 0))],
        out_shape=[jax.ShapeDtypeStruct((nb, nt * tb, nh * dv), F32), jax.ShapeDtypeStruct(c0.shape, F32),
                   jax.ShapeDtypeStruct(n0.shape, F32), jax.ShapeDtypeStruct((nb, 1, nh), F32)],
        scratch_shapes=[pltpu.VMEM((nseq, dk, nh * dv), F32), pltpu.VMEM((nseq, nh, 1, dk), F32),
                        pltpu.VMEM((nseq, nh, 1, LANE), F32)],
        compiler_params=_cparams(("parallel", "arbitrary")), name="mlstm_scan",
    )(*([proj] * nseq), c0, n0, m0, ng, bi, bf)
    return o.reshape(nb * nt * tb, nh * dv), c, n, m


SUB = 8


def _tree(op, xs):
    xs = list(xs)
    while len(xs) > 1:
        xs = [op(xs[i], xs[i + 1]) if i + 1 < len(xs) else xs[i] for i in range(0, len(xs), 2)]
    return xs[0]


def _all_sublanes(op, m):
    for shift in (4, 2, 1):
        m = op(m, pltpu.roll(m, shift, axis=0))
    return m


def _sorting_network(n):
    def merge(lo, hi, r):
        step = r * 2
        if step < hi - lo:
            yield from merge(lo, hi, step)
            yield from merge(lo + r, hi, step)
            yield from [(i, i + r) for i in range(lo + r, hi - r, step)]
        else:
            yield (lo, lo + r)

    def sort(lo, hi):
        if hi - lo >= 1:
            mid = lo + (hi - lo) // 2
            yield from sort(lo, mid)
            yield from sort(mid + 1, hi)
            yield from merge(lo, hi, 1)

    return list(sort(0, n - 1))


def _sort_tiles_desc(tiles):
    tiles = list(tiles)
    for i, j in _sorting_network(len(tiles)):
        tiles[i], tiles[j] = jnp.maximum(tiles[i], tiles[j]), jnp.minimum(tiles[i], tiles[j])
    return tiles


def _merge_top(lists, singles=None):
    lists = list(lists)
    vals = []
    for j in range(PEER_TOPK):
        head = lists[0] if singles is None else jnp.maximum(lists[0], singles)
        m = _all_sublanes(jnp.maximum, head)
        vals.append(m)
        if j == PEER_TOPK - 1:
            break
        hit = lists[0] == m
        for i in range(min(len(lists) - 1, PEER_TOPK - 1 - j)):
            lists[i] = jnp.where(hit, lists[i + 1], lists[i])
        if singles is not None:
            singles = jnp.where(singles == m, -jnp.inf, singles)
    return vals


def _route_tile(q_scr, k1_ref, k2_ref, r2_ref, c1_ref, a1_ref, a2_ref, h, lanes):
    ntile = PEER_NKEYS // SUB
    sub = _iota((SUB, LANE), 0)
    q1 = q_scr[pl.ds(pl.multiple_of(h * 2 * PEER_HALF, PEER_HALF), PEER_HALF), lanes]
    q2 = q_scr[pl.ds(pl.multiple_of(h * 2 * PEER_HALF + PEER_HALF, PEER_HALF), PEER_HALF), lanes]
    s1 = _dot(k1_ref[...], q1, HI)
    s2 = _dot(k2_ref[...], q2, HI)
    t1 = [s1[k * SUB:(k + 1) * SUB, :] for k in range(ntile)]
    t2 = [s2[k * SUB:(k + 1) * SUB, :] for k in range(ntile)]
    v1 = _merge_top(_sort_tiles_desc(t1))
    v2 = _merge_top(_sort_tiles_desc(t2))
    v2lo, v2hi = v2[SUB - 1], v2[2 * SUB - 1]
    for b in range(SUB - 2, -1, -1):
        v2lo = jnp.where(sub == b, v2[b], v2lo)
        v2hi = jnp.where(sub == b, v2[SUB + b], v2hi)
    cand_lo = [jnp.where(sub < min(PEER_TOPK // (a + 1), SUB), v1[a] + v2lo, -jnp.inf) for a in range(PEER_TOPK)]
    cand_hi = v1[0] + v2hi
    tau = _merge_top(cand_lo, cand_hi)[PEER_TOPK - 1]
    top = v1[0] + v2[0]
    z = _all_sublanes(jnp.add, _tree(jnp.add, [jnp.where(cd >= tau, jnp.exp(cd - top), 0.0)
                                               for cd in cand_lo + [cand_hi]]))
    inv_z = 1.0 / z
    pairs = [_all_sublanes(jnp.add, jnp.where(cd >= tau, 1.0, 0.0)) for cd in cand_lo]
    pairs[0] = pairs[0] + _all_sublanes(jnp.add, jnp.where(cand_hi >= tau, 1.0, 0.0))
    rank2, c1 = [], []
    for k in range(ntile):
        r = jnp.full((SUB, LANE), float(PEER_TOPK), F32)
        cnt = jnp.zeros((SUB, LANE), F32)
        for j in range(PEER_TOPK - 1, -1, -1):
            r = jnp.where(t2[k] >= v2[j], float(j), r)
            cnt = jnp.where(t1[k] == v1[j], pairs[j], cnt)
        rank2.append(r)
        c1.append(cnt)
    r2_ref[h, :, lanes] = jnp.concatenate(rank2, axis=0).astype(BF16)
    c1_ref[h, :, lanes] = jnp.concatenate(c1, axis=0)
    a1_ref[h, :, lanes] = jnp.concatenate([jnp.exp(t - v1[0]) for t in t1], axis=0)
    a2_ref[h, :, lanes] = jnp.concatenate([jnp.exp(t - v2[0]) * inv_z for t in t2], axis=0).astype(BF16)


def _peer_body(xm_ref, xr_ref, wq_ref, k1_ref, k2_ref, u_ref, vt_ref, g_ref, b_ref, out_ref,
               acc, xbf, p_scr, q_scr, r2_scr, c1_scr, a1_scr, a2_scr):
    r = pl.program_id(0)
    c = pl.program_id(1)
    tm = xbf.shape[1]

    @pl.when(c == 0)
    def _():
        xbf[...] = xm_ref[...].astype(BF16)
        acc[...] = jnp.zeros(acc.shape, F32)
        p_scr[...] = jnp.zeros(p_scr.shape, BF16)
        q_scr[...] = _dot(wq_ref[...], xr_ref[...].astype(BF16))

    @pl.when((c == 0) & (r == 0))
    def _():
        for scr in (r2_scr, c1_scr, a1_scr, a2_scr):
            scr[...] = jnp.zeros(scr.shape, scr.dtype)

    ntile = tm // LANE
    units = pl.num_programs(1) - 1
    tiles_per_unit = PEER_HEADS * ntile // units
    unit = (c + units - 1) % units
    h_route = unit // (ntile // tiles_per_unit)
    lane0 = (unit % (ntile // tiles_per_unit)) * tiles_per_unit * LANE
    wr = r % 2
    for sb in range(tiles_per_unit):
        _route_tile(q_scr, k1_ref, k2_ref, r2_scr.at[wr], c1_scr.at[wr], a1_scr.at[wr], a2_scr.at[wr], h_route,
                    pl.ds(pl.multiple_of(lane0 + sb * LANE, LANE), LANE))

    rd = (r + 1) % 2
    i1_base = jnp.minimum(c, pl.num_programs(1) - 2) * PEER_I1_PER_STEP
    rows16 = 2 * SUB
    sub_w = PEER_SUB_I1 * PEER_NKEYS
    total = None
    for s in range(PEER_I1_PER_STEP // PEER_SUB_I1):
        es = slice(s * sub_w, (s + 1) * sub_w)
        part = _dot(vt_ref[0, :, es], p_scr[es, :])
        total = part if total is None else total + part
        hid = _dot(u_ref[es, :], xbf[...])
        act = (0.5 * hid * (1.0 + lax.erf(hid * 0.5 ** 0.5))).astype(BF16)
        cols = []
        for l0 in range(0, tm, PEER_LANE_BLOCK):
            ls = slice(l0, l0 + PEER_LANE_BLOCK)
            tiles = []
            for j in range(s * PEER_SUB_I1, (s + 1) * PEER_SUB_I1):
                row = pl.ds(i1_base + j, 1)
                c1b = [jnp.broadcast_to(c1_scr[rd, h, row, ls], (rows16, PEER_LANE_BLOCK)).astype(BF16)
                       for h in range(PEER_HEADS)]
                a1b = [jnp.broadcast_to(a1_scr[rd, h, row, ls], (rows16, PEER_LANE_BLOCK)).astype(BF16)
                       for h in range(PEER_HEADS)]
                for kb in range(PEER_NKEYS // rows16):
                    ks = slice(kb * rows16, (kb + 1) * rows16)
                    tiles.append(_tree(jnp.add, [jnp.where(r2_scr[rd, h, ks, ls] < c1b[h],
                                                           a2_scr[rd, h, ks, ls] * a1b[h], 0.0)
                                                 for h in range(PEER_HEADS)]))
            cols.append(jnp.concatenate(tiles, axis=0))
        p_scr[es, :] = jnp.concatenate(cols, axis=1) * act
    acc[...] += total

    @pl.when(c == pl.num_programs(1) - 1)
    def _():
        z = DN_ALPHA * xm_ref[...] + acc[...]
        zc = z - jnp.mean(z, 0, keepdims=True)
        var = jnp.mean(zc * zc, 0, keepdims=True)
        out_ref[...] = (zc * lax.rsqrt(var + LN_EPS) * g_ref[...] + b_ref[...]).T


def _peer(xt, wq_t, k1, k2, u, vt, g, b, tm):
    d, n = xt.shape
    nb = n // tm
    ec = PEER_I1_PER_STEP * PEER_NKEYS
    nc = u.shape[0] // ec
    const = lambda r, c: (0, 0)
    rshape = (2, PEER_HEADS, PEER_NKEYS, tm)
    return pl.pallas_call(
        _peer_body, grid=(nb + 1, nc + 1),
        in_specs=[pl.BlockSpec((d, tm), lambda r, c: (0, jnp.maximum(r - 1, 0))),
                  pl.BlockSpec((d, tm), lambda r, c: (0, jnp.minimum(r, nb - 1))),
                  pl.BlockSpec(wq_t.shape, const), pl.BlockSpec(k1.shape, const), pl.BlockSpec(k2.shape, const),
                  pl.BlockSpec((ec, d), lambda r, c: (jnp.minimum(c, nc - 1), 0)),
                  pl.BlockSpec((1, d, ec), lambda r, c: (jnp.maximum(c - 1, 0), 0, 0)),
                  pl.BlockSpec((d, 1), const), pl.BlockSpec((d, 1), const)],
        out_specs=pl.BlockSpec((tm, d), lambda r, c: (jnp.maximum(r - 1, 0), 0)),
        out_shape=jax.ShapeDtypeStruct((n, d), F32),
        scratch_shapes=[pltpu.VMEM((d, tm), F32), pltpu.VMEM((d, tm), BF16), pltpu.VMEM((ec, tm), BF16),
                        pltpu.VMEM((wq_t.shape[0], tm), F32), pltpu.VMEM(rshape, BF16), pltpu.VMEM(rshape, F32),
                        pltpu.VMEM(rshape, F32), pltpu.VMEM(rshape, BF16)],
        compiler_params=pltpu.CompilerParams(dimension_semantics=("arbitrary", "arbitrary"),
                                             vmem_limit_bytes=PEER_VMEM_LIMIT), name="peer",
    )(xt, xt, wq_t, k1, k2, u, vt, g.reshape(d, 1), b.reshape(d, 1))


def _pad_cols(w, width):
    return jnp.pad(w, ((0, 0), (0, width - w.shape[1])))


def _lane_row(*parts):
    v = jnp.concatenate([p.astype(F32).reshape(-1) for p in parts])
    return jnp.pad(v, (0, LANE - v.shape[0])).reshape(1, LANE)


def _pick_tb(t_pad):
    for tb in (48, 32, 16):
        if t_pad % tb == 0:
            return tb
    raise ValueError(t_pad)


def kernel(x_prompt, x_sample, state_hgrn_S, state_gdn_S, state_gdn_conv, state_mlstm_C, state_mlstm_n, state_mlstm_m, state_gla_S, meta_tokens, hgrn_w_in, hgrn_lb, hgrn_norm_g, hgrn_w_out, gdn_w_in, gdn_conv_w, gdn_a_log, gdn_dt_bias, gdn_norm_g, gdn_w_out, mlstm_w_in, mlstm_b_i, mlstm_b_f, mlstm_norm_g, mlstm_w_out, gla_w_in, gla_w_gate, gla_b_gate, gla_norm_g, gla_w_out, peer_w_q, peer_keys1, peer_keys2, peer_u, peer_v, ln_g, ln_b):
    bp, seq, d = x_prompt.shape
    bs, seq_s, _ = x_sample.shape
    assert d == D_MODEL and len(state_hgrn_S) == 1 and len(state_gdn_S) == 1
    assert len(state_mlstm_C) == 1 and len(state_gla_S) == 1
    tp = N_META + seq
    tp_pad = -(-tp // C) * C
    ts_pad = -(-seq_s // C) * C
    tbp = _pick_tb(tp_pad)
    tbs = _pick_tb(ts_pad)
    np_rows = bp * tp_pad
    n_real = np_rows + bs * seq_s

    meta = jnp.broadcast_to(meta_tokens.astype(F32)[None], (bp, N_META, d))
    hp = jnp.concatenate([meta, x_prompt], axis=1)
    hp = jnp.pad(hp, ((0, 0), (0, tp_pad - tp), (0, 0))).reshape(np_rows, d)
    h = jnp.concatenate([hp, x_sample.reshape(bs * seq_s, d)], axis=0)
    n = -(-n_real // TOK_MULTIPLE) * TOK_MULTIPLE
    h = jnp.pad(h, ((0, n - n_real), (0, 0)))
    tm_mm = MM_TOK_BLOCK

    def split_cols(w, sizes):
        out, o = [], 0
        for s in sizes:
            out.append(w[:, o:o + s])
            o += s
        return out

    def run_mixer(scan, proj, states_p, states_s, **kw):
        res_p = scan(proj, *states_p, nb=bp, nt=tp_pad // tbp, tb=tbp, t_valid=tp, **kw)
        ps = proj[np_rows:n_real].reshape(bs, seq_s, -1)
        ps = jnp.pad(ps, ((0, 0), (0, ts_pad - seq_s), (0, 0))).reshape(bs * ts_pad, -1)
        res_s = scan(ps, *states_s, nb=bs, nt=ts_pad // tbs, tb=tbs, t_valid=seq_s, **kw)
        o_s = res_s[0].reshape(bs, ts_pad, -1)[:, :seq_s].reshape(bs * seq_s, -1)
        o = jnp.concatenate([res_p[0], o_s], axis=0)
        o = jnp.pad(o, ((0, n - n_real), (0, 0)))
        return o, res_p[1:], res_s[1:]

    lb_all = jnp.cumsum(jax.nn.softmax(hgrn_lb.astype(F32), axis=0), axis=0)
    outs = {}
    for i in range(DEPTH):
        mix = i % 4
        if mix == 0:
            w_in = hgrn_w_in[0].astype(BF16)
            proj = _mm(h, w_in, tm_mm, 1024)
            nh, dk, dv = state_hgrn_S.shape[2:]
            zero = jnp.zeros((bp, nh, dk, dv), F32)
            scan = functools.partial(_gla_scan, ng=hgrn_norm_g[0].reshape(1, dv), p1=lb_all[i].reshape(1, nh * dk),
                                     p2=jnp.zeros((1, LANE), F32), hgrn=True)
            o, (sp,), (ss,) = run_mixer(scan, proj, (zero,), (state_hgrn_S[0],))
            outs['hgrn'] = (sp[None], ss[None])
            w_out = hgrn_w_out[0]
        elif mix == 1:
            nh, dk, dv = state_gdn_S.shape[2:]
            cwid = 2 * nh * dk + nh * dv
            qkv, a, b, g = split_cols(gdn_w_in[0], [cwid, nh, nh, nh * dv])
            w_in = jnp.concatenate([qkv, g, _pad_cols(jnp.concatenate([a, b], axis=1), LANE)], axis=1).astype(BF16)
            proj = _mm(h, w_in, tm_mm, w_in.shape[1] // 3)
            scan = functools.partial(_gdn_scan, cw=gdn_conv_w[0], ng=gdn_norm_g[0].reshape(1, dv),
                                     alog=_lane_row(gdn_a_log[0]), dtb=_lane_row(gdn_dt_bias[0]))
            o, (sp, cp), (ss, cs) = run_mixer(
                scan, proj,
                (jnp.zeros((bp, nh, dk, dv), F32), jnp.zeros((bp, CONV_W - 1, cwid), F32)),
                (state_gdn_S[0], state_gdn_conv[0]))
            outs['gdn'] = (sp[None], cp[None], ss[None], cs[None])
            w_out = gdn_w_out[0]
        elif mix == 2:
            nh, dk, dv = state_mlstm_C.shape[2:]
            q, k, v, ig, fg, og = split_cols(mlstm_w_in[0], [nh * dk, nh * dk, nh * dv, nh, nh, nh * dv])
            w_in = jnp.concatenate([q, k, v, og, _pad_cols(jnp.concatenate([ig, fg], axis=1), LANE)],
                                   axis=1).astype(BF16)
            proj = _mm(h, w_in, tm_mm, w_in.shape[1] // 3)
            zeros_nh = jnp.zeros((nh,), F32)
            scan = functools.partial(_mlstm_scan, ng=mlstm_norm_g[0].reshape(1, nh * dv),
                                     bi=_lane_row(mlstm_b_i[0]), bf=_lane_row(zeros_nh, mlstm_b_f[0]))
            o, (cp, npp, mp), (cs, ns, ms) = run_mixer(
                scan, proj,
                (jnp.zeros((bp, nh, dk, dv), F32), jnp.zeros((bp, nh, dk), F32), jnp.zeros((bp, nh), F32)),
                (state_mlstm_C[0], state_mlstm_n[0], state_mlstm_m[0]))
            outs['mlstm'] = (cp[None], npp[None], mp.reshape(1, bp, nh), cs[None], ns[None], ms.reshape(1, bs, nh))
            w_out = mlstm_w_out[0]
        else:
            nh, dk, dv = state_gla_S.shape[2:]
            rank = gla_w_gate.shape[1]
            q, k, v, g, r = split_cols(gla_w_in[0], [nh * dk, nh * dk, nh * dv, nh * dv, rank])
            w_in = jnp.concatenate([q, k, v, g, _pad_cols(r, LANE)], axis=1).astype(BF16)
            proj = _mm(h, w_in, tm_mm, w_in.shape[1] // 5)
            wg = jnp.pad(gla_w_gate[0].astype(F32), ((0, LANE - rank), (0, 0)))
            scan = functools.partial(_gla_scan, ng=gla_norm_g[0].reshape(1, dv), p1=wg,
                                     p2=gla_b_gate[0].reshape(1, nh * dk), hgrn=False)
            o, (sp,), (ss,) = run_mixer(scan, proj, (jnp.zeros((bp, nh, dk, dv), F32),), (state_gla_S[0],))
            outs['gla'] = (sp[None], ss[None])
            w_out = gla_w_out[0]

        h1t = _mm_res_ln_t(o, w_out.astype(BF16), h, ln_g[i, 0], ln_b[i, 0], TOK_BLOCK)
        ec = PEER_I1_PER_STEP * PEER_NKEYS
        vt = peer_v[i].astype(BF16).reshape(-1, ec, d).transpose(0, 2, 1)
        h = _peer(h1t, peer_w_q[i].T.astype(BF16), peer_keys1[i], peer_keys2[i], peer_u[i].astype(BF16), vt,
                  ln_g[i, 1], ln_b[i, 1], PEER_TOK_BLOCK)

    y_prompt = h[:np_rows].reshape(bp, tp_pad, d)[:, N_META:tp]
    y_sample = h[np_rows:n_real].reshape(bs, seq_s, d)
    hg, gd, ml, gl = outs['hgrn'], outs['gdn'], outs['mlstm'], outs['gla']
    return (y_prompt, y_sample, hg[0], gd[0], gd[1], ml[0], ml[1], ml[2], gl[0],
            hg[1], gd[2], gd[3], ml[3], ml[4], ml[5], gl[1])
```

```python
import functools

import jax
import jax.numpy as jnp
from jax import lax
from jax.experimental import pallas as pl
from jax.experimental.pallas import tpu as pltpu

F32 = jnp.float32
BF16 = jnp.bfloat16
HI = lax.Precision.HIGHEST

D_MODEL = 1024
DEPTH = 4
N_META = 16
C = 16
CONV_W = 4
NORM_EPS = 1e-6
LN_EPS = 1e-5
DN_ALPHA = (2 * DEPTH) ** 0.25
GLA_GATE_NORM = 16.0
LANE = 128
PEER_HEADS = 8
PEER_NKEYS = 128
PEER_TOPK = 16
PEER_HALF = 128
PEER_I1_PER_STEP = 8
PEER_SUB_I1 = 8
PEER_VMEM_LIMIT = 56 * 1024 * 1024
TOK_BLOCK = 256
PEER_TOK_BLOCK = 512
PEER_LANE_BLOCK = 256
MM_TOK_BLOCK = 768
TOK_MULTIPLE = 1536
VMEM_LIMIT = 48 * 1024 * 1024


def _cparams(sem):
    return pltpu.CompilerParams(dimension_semantics=sem, vmem_limit_bytes=VMEM_LIMIT)


def _iota(shape, dim):
    return lax.broadcasted_iota(jnp.int32, shape, dim)


def _dot_nt(a, b, precision=None):
    return lax.dot_general(a, b, (((1,), (1,)), ((), ())), precision=precision, preferred_element_type=F32)


def _dot_tn(a, b, precision=None):
    return lax.dot_general(a, b, (((0,), (0,)), ((), ())), precision=precision, preferred_element_type=F32)


def _dot(a, b, precision=None):
    return jnp.dot(a, b, precision=precision, preferred_element_type=F32)


def _split(x):
    hi = x.astype(BF16)
    return hi, (x - hi.astype(F32)).astype(BF16)


def _dot3(a, b, nt=False):
    (ah, al), (bh, bl) = a, b
    d = _dot_nt if nt else _dot
    return d(ah, bh) + d(ah, bl) + d(al, bh)


def _cumsum_rows(x):
    tri = (_iota((C, C), 0) >= _iota((C, C), 1)).astype(F32)
    return _dot(tri, x, HI)


def _rows_to_cols(x):
    eye = (_iota((LANE, LANE), 0) == _iota((LANE, LANE), 1)).astype(F32)
    return _dot_nt(eye, x, HI)


def _mm_body(x_ref, w_ref, o_ref):
    o_ref[...] = _dot(x_ref[...].astype(BF16), w_ref[...])


def _mm(x, w, tm, tn):
    m, k = x.shape
    n = w.shape[1]
    return pl.pallas_call(
        _mm_body, grid=(m // tm, n // tn),
        in_specs=[pl.BlockSpec((tm, k), lambda i, j: (i, 0)), pl.BlockSpec((k, tn), lambda i, j: (0, j))],
        out_specs=pl.BlockSpec((tm, tn), lambda i, j: (i, j)),
        out_shape=jax.ShapeDtypeStruct((m, n), F32),
        compiler_params=_cparams(("parallel", "parallel")), name="mm")(x, w)


def _mm_ln_body(o_ref, w_ref, h_ref, g_ref, b_ref, out_ref):
    y = _dot(o_ref[...].astype(BF16), w_ref[...])
    z = DN_ALPHA * h_ref[...] + y
    zc = z - jnp.mean(z, -1, keepdims=True)
    var = jnp.mean(zc * zc, -1, keepdims=True)
    out_ref[...] = (zc * lax.rsqrt(var + LN_EPS) * g_ref[...] + b_ref[...]).T


def _mm_res_ln_t(o, w, h, g, b, tm):
    n, k = o.shape
    d = w.shape[1]
    return pl.pallas_call(
        _mm_ln_body, grid=(n // tm,),
        in_specs=[pl.BlockSpec((tm, k), lambda i: (i, 0)), pl.BlockSpec((k, d), lambda i: (0, 0)),
                  pl.BlockSpec((tm, d), lambda i: (i, 0)), pl.BlockSpec((1, d), lambda i: (0, 0)),
                  pl.BlockSpec((1, d), lambda i: (0, 0))],
        out_specs=pl.BlockSpec((d, tm), lambda i: (0, i)),
        out_shape=jax.ShapeDtypeStruct((d, n), F32),
        compiler_params=_cparams(("parallel",)), name="mm_res_ln")(o, w, h, g.reshape(1, d), b.reshape(1, d))


def _valid_rows(t, r0, tb, t_valid, nt):
    if t_valid == nt * tb:
        return None
    return (t * tb + r0 + _iota((C, 1), 0)) < t_valid


def _gla_body(*refs, nseq, hgrn, nh, dk, dv, tb, t_valid, nt):
    proj_refs = refs[:nseq]
    s0_ref, ng_ref, p1_ref, p2_ref, o_ref, sout_ref, st_scr = refs[nseq:]
    t = pl.program_id(1)

    @pl.when(t == 0)
    def _():
        for sq in range(nseq):
            for h in range(nh):
                st_scr[sq, h] = s0_ref[sq, h].T

    kw = nh * dk
    vw = nh * dv
    nr = nh * C
    ri = _iota((nr, 1), 0)
    cj = _iota((1, nr), 1)
    rhead = ri // C
    tri = ((rhead == cj // C) & (ri % C >= cj % C)).astype(F32)
    row3 = _iota((1, C, 1), 1)

    def stack(ref, rows, col0, width):
        return jnp.concatenate([ref[rows, col0 + h * width:col0 + (h + 1) * width] for h in range(nh)], axis=0)

    prep = {}
    for sq, proj_ref in enumerate(proj_refs):
        for ci in range(tb // C):
            rows = slice(ci * C, (ci + 1) * C)
            if hgrn:
                q = jax.nn.silu(stack(proj_ref, rows, 0, dk))
                lb = jnp.concatenate([jnp.broadcast_to(p1_ref[:, h * dk:(h + 1) * dk], (C, dk)) for h in range(nh)],
                                     axis=0)
                fg = lb + (1.0 - lb) * jax.nn.sigmoid(stack(proj_ref, rows, kw, dk))
                k = 1.0 - fg
                g = jnp.log(fg)
            else:
                r = proj_ref[rows, 2 * kw + 2 * vw:2 * kw + 2 * vw + LANE]
                logf_all = jax.nn.log_sigmoid(_dot(r, p1_ref[...], HI) + p2_ref[...]) / GLA_GATE_NORM
                q = stack(proj_ref, rows, 0, dk) * dk ** -0.5
                k = stack(proj_ref, rows, kw, dk)
                g = jnp.concatenate([logf_all[:, h * dk:(h + 1) * dk] for h in range(nh)], axis=0)
            v = stack(proj_ref, rows, 2 * kw, dv)
            if t_valid != nt * tb:
                valid = t * tb + ci * C + ri % C < t_valid
                k = jnp.where(valid, k, 0.0)
                g = jnp.where(valid, g, 0.0)
            b = _dot(tri, g, HI)
            b3, q3, k3, v3 = (x.reshape(nh, C, x.shape[-1]) for x in (b, q, k, v))
            o3 = jnp.zeros((nh, C, dv), F32)
            for s in range(C):
                d = jnp.exp(jnp.where(row3 >= s, b3 - b3[:, s:s + 1, :], -jnp.inf))
                col = jnp.sum(q3 * (k3[:, s:s + 1, :] * d), axis=-1, keepdims=True)
                o3 = o3 + col * v3[:, s:s + 1, :]
            b_end = b3[:, C - 1:C, :]
            prep[sq, ci] = dict(o=o3.reshape(nr, dv), qe=q * jnp.exp(b), kd=(k3 * jnp.exp(b_end - b3)).reshape(nr, dk),
                                dec=jnp.exp(b_end), v=v)

    sts = [st_scr[sq] for sq in range(nseq)]
    for ci in range(tb // C):
        rows = slice(ci * C, (ci + 1) * C)
        for sq in range(nseq):
            p = prep[sq, ci]
            inter = _dot_nt(p['qe'], sts[sq].reshape(nh * dv, dk))
            o = p['o'] + _tree(jnp.add, [jnp.where(rhead == h, inter[:, h * dv:(h + 1) * dv], 0.0)
                                         for h in range(nh)])
            v_wide = jnp.concatenate([jnp.where(rhead == h, p['v'], 0.0) for h in range(nh)], axis=1)
            sts[sq] = sts[sq] * p['dec'] + _dot_tn(v_wide, p['kd']).reshape(nh, dv, dk)
            o = o * lax.rsqrt(jnp.mean(o * o, -1, keepdims=True) + NORM_EPS) * ng_ref[...]
            for h in range(nh):
                gate = proj_refs[sq][rows, 2 * kw + vw + h * dv:2 * kw + vw + (h + 1) * dv]
                o_ref[sq, rows, h * dv:(h + 1) * dv] = o[h * C:(h + 1) * C, :] * jax.nn.silu(gate)
    for sq in range(nseq):
        st_scr[sq] = sts[sq]

    @pl.when(t == nt - 1)
    def _():
        for sq in range(nseq):
            for h in range(nh):
                sout_ref[sq, h] = st_scr[sq, h].T


def _seqs_per_step(nb, chunks):
    return max(n for n in (8, 4, 2, 1) if nb % n == 0 and n * chunks <= 8)


def _proj_specs(nseq, nt, tb, wp):
    return [pl.BlockSpec((tb, wp), functools.partial(lambda b, t, sq: ((b * nseq + sq) * nt + t, 0), sq=sq))
            for sq in range(nseq)]


def _gla_scan(proj, s0, ng, p1, p2, *, hgrn, nb, nt, tb, t_valid):
    _, nh, dk, dv = s0.shape
    wp = proj.shape[1]
    nseq = _seqs_per_step(nb, tb // C)
    body = functools.partial(_gla_body, nseq=nseq, hgrn=hgrn, nh=nh, dk=dk, dv=dv, tb=tb, t_valid=t_valid, nt=nt)
    const = lambda b, t: (0, 0)
    o, s = pl.pallas_call(
        body, grid=(nb // nseq, nt),
        in_specs=_proj_specs(nseq, nt, tb, wp) + [
            pl.BlockSpec((nseq, nh, dk, dv), lambda b, t: (b, 0, 0, 0)),
            pl.BlockSpec(ng.shape, const), pl.BlockSpec(p1.shape, const), pl.BlockSpec(p2.shape, const)],
        out_specs=[pl.BlockSpec((nseq, tb, nh * dv), lambda b, t: (b, t, 0)),
                   pl.BlockSpec((nseq, nh, dk, dv), lambda b, t: (b, 0, 0, 0))],
        out_shape=[jax.ShapeDtypeStruct((nb, nt * tb, nh * dv), F32), jax.ShapeDtypeStruct(s0.shape, F32)],
        scratch_shapes=[pltpu.VMEM((nseq, nh, dv, dk), F32)],
        compiler_params=_cparams(("parallel", "arbitrary")), name="hgrn_scan" if hgrn else "gla_scan",
    )(*([proj] * nseq), s0, ng, p1, p2)
    return o.reshape(nb * nt * tb, nh * dv), s


def _gdn_body(*refs, nseq, nh, dk, dv, tb, t_valid, nt):
    proj_refs = refs[:nseq]
    (s0_ref, conv0_ref, cw_ref, ng_ref, alog_ref, dtb_ref, o_ref, sout_ref, convout_ref,
     st_scr, cbuf, ybuf) = refs[nseq:]
    t = pl.program_id(1)
    cwid = 2 * nh * dk + nh * dv
    pre = 8 - (CONV_W - 1)

    @pl.when(t == 0)
    def _():
        for sq in range(nseq):
            for h in range(nh):
                st_scr[sq, :, h * dv:(h + 1) * dv] = s0_ref[sq, h]
            cbuf[sq, 0:8, :] = jnp.zeros((8, cwid), F32)
            cbuf[sq, pre:8, :] = conv0_ref[sq]

    for sq in range(nseq):
        cbuf[sq, 8:8 + tb, :] = proj_refs[sq][:, 0:cwid]
        y = cbuf[sq, pre:pre + tb, :] * cw_ref[0:1, :]
        for j in range(1, CONV_W):
            y = y + cbuf[sq, pre + j:pre + j + tb, :] * cw_ref[j:j + 1, :]
        ybuf[sq] = jax.nn.silu(y)

    nr = nh * C
    ri = _iota((nr, 1), 0)
    cj = _iota((1, nr), 1)
    rhead = ri // C
    same = rhead == cj // C
    incl = same & (ri % C >= cj % C)
    strict = same & (ri % C > cj % C)
    eye = (ri == cj).astype(F32)

    def stack(ref, rows, base, width):
        return jnp.concatenate([ref[rows, base + h * width:base + (h + 1) * width] for h in range(nh)], axis=0)

    def own_block(x):
        return _tree(jnp.add, [jnp.where(rhead == h, x[:, h * dv:(h + 1) * dv], 0.0) for h in range(nh)])

    prep = []
    for sq, ci in [(sq, ci) for sq in range(nseq) for ci in range(tb // C)]:
        rows = slice(ci * C, (ci + 1) * C)
        valid = _valid_rows(t, ci * C, tb, t_valid, nt)
        small = proj_refs[sq][rows, cwid + nh * dv:cwid + nh * dv + LANE]
        g_all = -jnp.exp(alog_ref[...]) * jax.nn.softplus(small + dtb_ref[...])
        beta_all = jax.nn.sigmoid(small)
        if valid is not None:
            g_all = jnp.where(valid, g_all, 0.0)
            beta_all = jnp.where(valid, beta_all, 0.0)
        b_all = _cumsum_rows(g_all)
        b_t = _rows_to_cols(b_all)
        bcol = jnp.concatenate([b_all[:, h:h + 1] for h in range(nh)], axis=0)
        brow = jnp.concatenate([b_t[h:h + 1, :] for h in range(nh)], axis=1)
        beta = jnp.concatenate([beta_all[:, nh + h:nh + h + 1] for h in range(nh)], axis=0)
        bend = jnp.concatenate([jnp.broadcast_to(b_all[C - 1:C, h:h + 1], (C, 1)) for h in range(nh)], axis=0)
        dec_row = jnp.concatenate([jnp.broadcast_to(jnp.exp(b_all[C - 1:C, h:h + 1]), (1, dv)) for h in range(nh)],
                                  axis=1)
        q = stack(ybuf.at[sq], rows, 0, dk)
        k = stack(ybuf.at[sq], rows, nh * dk, dk)
        v = stack(ybuf.at[sq], rows, 2 * nh * dk, dv)
        q = q * lax.rsqrt(jnp.sum(q * q, -1, keepdims=True) + NORM_EPS) * dk ** -0.5
        k = k * lax.rsqrt(jnp.sum(k * k, -1, keepdims=True) + NORM_EPS)
        if valid is not None:
            valid_st = jnp.concatenate([valid] * nh, axis=0)
            q = jnp.where(valid_st, q, 0.0)
            k = jnp.where(valid_st, k, 0.0)
            v = jnp.where(valid_st, v, 0.0)
        decay = jnp.exp(jnp.where(incl, bcol - brow, -jnp.inf))
        eb = jnp.exp(bcol)
        ks = _split(k)
        prep.append(dict(a=jnp.where(strict, _dot3(ks, ks, nt=True) * decay * beta, 0.0), bv=beta * v,
                         bk=(beta * eb) * k, att=_dot_nt(q, k) * decay, qe=q * eb, kd=k * jnp.exp(bend - bcol),
                         dec_row=dec_row, sq=sq, ci=ci))
    for p in prep:
        a_s = _split(p['a'])
        p['a2'] = _dot3(a_s, a_s)
    for p in prep:
        a2_s = _split(p['a2'])
        p['a4'] = _dot3(a2_s, a2_s)
        p['lo'] = _dot3(_split(eye - p['a']), _split(eye + p['a2']))
    for p in prep:
        a4_s = _split(p['a4'])
        p['a8'] = _dot3(a4_s, a4_s)
    for p in prep:
        p['hi'] = _dot3(_split(eye + p['a4']), _split(eye + p['a8']))
    for p in prep:
        p['tinv'] = _split(_dot3(_split(p['lo']), _split(p['hi'])))
    for p in prep:
        p['u'] = _dot3(p['tinv'], _split(p['bv']))
        p['wk'] = _dot3(p['tinv'], _split(p['bk']))

    s_cat = [st_scr[sq] for sq in range(nseq)]
    for p in sorted(prep, key=lambda p: (p['ci'], p['sq'])):
        sq = p['sq']
        rows = slice(p['ci'] * C, (p['ci'] + 1) * C)
        v_new = p['u'] - own_block(_dot(p['wk'], s_cat[sq]))
        o = _dot(p['att'], v_new) + own_block(_dot(p['qe'], s_cat[sq]))
        v_wide = jnp.concatenate([jnp.where(rhead == h, v_new, 0.0) for h in range(nh)], axis=1)
        s_cat[sq] = s_cat[sq] * p['dec_row'] + _dot_tn(p['kd'], v_wide)
        o = o * lax.rsqrt(jnp.mean(o * o, -1, keepdims=True) + NORM_EPS) * ng_ref[...]
        for h in range(nh):
            gate = proj_refs[sq][rows, cwid + h * dv:cwid + (h + 1) * dv]
            o_ref[sq, rows, h * dv:(h + 1) * dv] = o[h * C:(h + 1) * C, :] * jax.nn.silu(gate)
    for sq in range(nseq):
        st_scr[sq] = s_cat[sq]

    @pl.when(t < nt - 1)
    def _():
        for sq in range(nseq):
            cbuf[sq, pre:8, :] = cbuf[sq, pre + tb:8 + tb, :]

    @pl.when(t == nt - 1)
    def _():
        last = t_valid - (nt - 1) * tb
        for sq in range(nseq):
            for h in range(nh):
                sout_ref[sq, h] = st_scr[sq, :, h * dv:(h + 1) * dv]
            convout_ref[sq] = cbuf[sq, pre + last:8 + last, :]


def _gdn_scan(proj, s0, conv0, cw, ng, alog, dtb, *, nb, nt, tb, t_valid):
    _, nh, dk, dv = s0.shape
    wp = proj.shape[1]
    cwid = 2 * nh * dk + nh * dv
    nseq = _seqs_per_step(nb, tb // C)
    body = functools.partial(_gdn_body, nseq=nseq, nh=nh, dk=dk, dv=dv, tb=tb, t_valid=t_valid, nt=nt)
    const = lambda b, t: (0, 0)
    o, s, conv = pl.pallas_call(
        body, grid=(nb // nseq, nt),
        in_specs=_proj_specs(nseq, nt, tb, wp) + [
            pl.BlockSpec((nseq, nh, dk, dv), lambda b, t: (b, 0, 0, 0)),
            pl.BlockSpec((nseq, CONV_W - 1, cwid), lambda b, t: (b, 0, 0)),
            pl.BlockSpec(cw.shape, const), pl.BlockSpec(ng.shape, const),
            pl.BlockSpec(alog.shape, const), pl.BlockSpec(dtb.shape, const)],
        out_specs=[pl.BlockSpec((nseq, tb, nh * dv), lambda b, t: (b, t, 0)),
                   pl.BlockSpec((nseq, nh, dk, dv), lambda b, t: (b, 0, 0, 0)),
                   pl.BlockSpec((nseq, CONV_W - 1, cwid), lambda b, t: (b, 0, 0))],
        out_shape=[jax.ShapeDtypeStruct((nb, nt * tb, nh * dv), F32), jax.ShapeDtypeStruct(s0.shape, F32),
                   jax.ShapeDtypeStruct(conv0.shape, F32)],
        scratch_shapes=[pltpu.VMEM((nseq, dk, nh * dv), F32), pltpu.VMEM((nseq, tb + 8, cwid), F32),
                        pltpu.VMEM((nseq, tb, cwid), F32)],
        compiler_params=_cparams(("parallel", "arbitrary")), name="gdn_scan",
    )(*([proj] * nseq), s0, conv0, cw, ng, alog, dtb)
    return o.reshape(nb * nt * tb, nh * dv), s, conv


def _mlstm_body(*refs, nseq, nh, dk, dv, tb, t_valid, nt):
    proj_refs = refs[:nseq]
    (c0_ref, n0_ref, m0_ref, ng_ref, bi_ref, bf_ref, o_ref, cout_ref, nout_ref, mout_ref,
     ct_scr, n_scr, m_scr) = refs[nseq:]
    t = pl.program_id(1)

    @pl.when(t == 0)
    def _():
        for sq in range(nseq):
            for h in range(nh):
                ct_scr[sq, :, h * dv:(h + 1) * dv] = c0_ref[sq, h]
                n_scr[sq, h] = n0_ref[sq, h:h + 1, :]
                m_scr[sq, h] = jnp.broadcast_to(m0_ref[sq, :, h:h + 1], (1, LANE))

    base = 2 * nh * dk + 2 * nh * dv
    nr = nh * C
    ri = _iota((nr, 1), 0)
    cj = _iota((1, nr), 1)
    rhead = ri // C
    incl = (rhead == cj // C) & (ri % C >= cj % C)

    def stack(ref, rows, col0, width):
        return jnp.concatenate([ref[rows, col0 + h * width:col0 + (h + 1) * width] for h in range(nh)], axis=0)

    def per_head_col(vals):
        return jnp.concatenate([jnp.broadcast_to(x, (C, 1)) for x in vals], axis=0)

    prep = {}
    for sq, ci in [(sq, ci) for sq in range(nseq) for ci in range(tb // C)]:
        proj_ref = proj_refs[sq]
        rows = slice(ci * C, (ci + 1) * C)
        valid = _valid_rows(t, ci * C, tb, t_valid, nt)
        small = proj_ref[rows, base:base + LANE]
        li_all = small + bi_ref[...]
        lf_all = jax.nn.log_sigmoid(small + bf_ref[...])
        if valid is not None:
            lf_all = jnp.where(valid, lf_all, 0.0)
        b_all = _cumsum_rows(lf_all)
        b_t = _rows_to_cols(b_all)
        li_t = _rows_to_cols(li_all)
        bcol = jnp.concatenate([b_all[:, nh + h:nh + h + 1] for h in range(nh)], axis=0)
        licol = jnp.concatenate([li_all[:, h:h + 1] for h in range(nh)], axis=0)
        xrow = jnp.concatenate([b_t[nh + h:nh + h + 1, :] - li_t[h:h + 1, :] for h in range(nh)], axis=1)
        if valid is not None:
            xrow = jnp.where(t * tb + ci * C + cj % C < t_valid, xrow, jnp.inf)
            licol = jnp.where(t * tb + ci * C + ri % C < t_valid, licol, -jnp.inf)
        q = stack(proj_ref, rows, 0, dk)
        k = stack(proj_ref, rows, nh * dk, dk) * dk ** -0.5
        v = stack(proj_ref, rows, 2 * nh * dk, dv)
        dmat = jnp.where(incl, bcol - xrow, -jnp.inf)
        prep[sq, ci] = dict(q=q, k=k, v=v, bcol=bcol, licol=licol, dmat=dmat, qk=_dot_nt(q, k),
                            rowmax=jnp.max(dmat, axis=1, keepdims=True),
                            b_end=[b_all[C - 1:C, nh + h:nh + h + 1] for h in range(nh)])

    c_cat = [ct_scr[sq] for sq in range(nseq)]
    n_rows = [[n_scr[sq, h] for h in range(nh)] for sq in range(nseq)]
    m_vals = [[m_scr[sq, h][:, 0:1] for h in range(nh)] for sq in range(nseq)]
    for ci, sq in [(ci, sq) for ci in range(tb // C) for sq in range(nseq)]:
        p = prep[sq, ci]
        rows = slice(ci * C, (ci + 1) * C)
        q, k, v, bcol = p['q'], p['k'], p['v'], p['bcol']
        m_inter = bcol + per_head_col(m_vals[sq])
        m_t = jnp.maximum(m_inter, p['rowmax'])
        a_int = jnp.exp(m_inter - m_t)
        qk = p['qk'] * jnp.exp(p['dmat'] - m_t)
        qc = _dot(q, c_cat[sq])
        own = _tree(jnp.add, [jnp.where(rhead == h, qc[:, h * dv:(h + 1) * dv], 0.0) for h in range(nh)])
        num = _dot(qk, v) + a_int * own
        n_st = jnp.concatenate([jnp.broadcast_to(n_rows[sq][h], (C, dk)) for h in range(nh)], axis=0)
        den = jnp.sum(qk, axis=1, keepdims=True) + a_int * jnp.sum(q * n_st, axis=1, keepdims=True)
        hid = num / jnp.maximum(jnp.abs(den), jnp.exp(-m_t))
        last = [slice(h * C + C - 1, h * C + C) for h in range(nh)]
        m_vals[sq] = [m_t[r, :] for r in last]
        a_end = [a_int[r, :] for r in last]
        wk = jnp.exp(per_head_col(p['b_end']) - bcol + p['licol'] - per_head_col(m_vals[sq])) * k
        v_wide = jnp.concatenate([jnp.where(rhead == h, v, 0.0) for h in range(nh)], axis=1)
        a_row = jnp.concatenate([jnp.broadcast_to(a, (1, dv)) for a in a_end], axis=1)
        c_cat[sq] = c_cat[sq] * a_row + _dot_tn(wk, v_wide)
        n_rows[sq] = [a_end[h] * n_rows[sq][h] + jnp.sum(wk[h * C:(h + 1) * C, :], axis=0, keepdims=True)
                      for h in range(nh)]
        hc = hid - jnp.mean(hid, -1, keepdims=True)
        hn = hc * lax.rsqrt(jnp.mean(hc * hc, -1, keepdims=True) + NORM_EPS)
        for h in range(nh):
            og = jax.nn.sigmoid(
                proj_refs[sq][rows, 2 * nh * dk + nh * dv + h * dv:2 * nh * dk + nh * dv + (h + 1) * dv])
            o_ref[sq, rows, h * dv:(h + 1) * dv] = og * (hn[h * C:(h + 1) * C, :] * ng_ref[:, h * dv:(h + 1) * dv])
    for sq in range(nseq):
        ct_scr[sq] = c_cat[sq]
        for h in range(nh):
            n_scr[sq, h] = n_rows[sq][h]
            m_scr[sq, h] = jnp.broadcast_to(m_vals[sq][h], (1, LANE))

    @pl.when(t == nt - 1)
    def _():
        lane = _iota((1, nh), 1)
        for sq in range(nseq):
            mrow = jnp.zeros((1, nh), F32)
            for h in range(nh):
                cout_ref[sq, h] = ct_scr[sq, :, h * dv:(h + 1) * dv]
                nout_ref[sq, h:h + 1, :] = n_scr[sq, h]
                mrow = jnp.where(lane == h, m_scr[sq, h][:, 0:1], mrow)
            mout_ref[sq] = mrow


def _mlstm_scan(proj, c0, n0, m0, ng, bi, bf, *, nb, nt, tb, t_valid):
    _, nh, dk, dv = c0.shape
    wp = proj.shape[1]
    nseq = _seqs_per_step(nb, tb // C)
    body = functools.partial(_mlstm_body, nseq=nseq, nh=nh, dk=dk, dv=dv, tb=tb, t_valid=t_valid, nt=nt)
    const = lambda b, t: (0, 0)
    m0 = m0.reshape(nb, 1, nh)
    o, c, n, m = pl.pallas_call(
        body, grid=(nb // nseq, nt),
        in_specs=_proj_specs(nseq, nt, tb, wp) + [
            pl.BlockSpec((nseq, nh, dk, dv), lambda b, t: (b, 0, 0, 0)),
            pl.BlockSpec((nseq, nh, dk), lambda b, t: (b, 0, 0)),
            pl.BlockSpec((nseq, 1, nh), lambda b, t: (b, 0, 0)),
            pl.BlockSpec(ng.shape, const), pl.BlockSpec(bi.shape, const), pl.BlockSpec(bf.shape, const)],
        out_specs=[pl.BlockSpec((nseq, tb, nh * dv), lambda b, t: (b, t, 0)),
                   pl.BlockSpec((nseq, nh, dk, dv), lambda b, t: (b, 0, 0, 0)),
                   pl.BlockSpec((nseq, nh, dk), lambda b, t: (b, 0, 0)),
                   pl.BlockSpec((nseq, 1, nh), lambda b, t: (b, 0, 0))],
        out_shape=[jax.ShapeDtypeStruct((nb, nt * tb, nh * dv), F32), jax.ShapeDtypeStruct(c0.shape, F32),
                   jax.ShapeDtypeStruct(n0.shape, F32), jax.ShapeDtypeStruct((nb, 1, nh), F32)],
        scratch_shapes=[pltpu.VMEM((nseq, dk, nh * dv), F32), pltpu.VMEM((nseq, nh, 1, dk), F32),
                        pltpu.VMEM((nseq, nh, 1, LANE), F32)],
        compiler_params=_cparams(("parallel", "arbitrary")), name="mlstm_scan",
    )(*([proj] * nseq), c0, n0, m0, ng, bi, bf)
    return o.reshape(nb * nt * tb, nh * dv), c, n, m


SUB = 8


def _tree(op, xs):
    xs = list(xs)
    while len(xs) > 1:
        xs = [op(xs[i], xs[i + 1]) if i + 1 < len(xs) else xs[i] for i in range(0, len(xs), 2)]
    return xs[0]


def _all_sublanes(op, m):
    for shift in (4, 2, 1):
        m = op(m, pltpu.roll(m, shift, axis=0))
    return m


def _sorting_network(n):
    def merge(lo, hi, r):
        step = r * 2
        if step < hi - lo:
            yield from merge(lo, hi, step)
            yield from merge(lo + r, hi, step)
            yield from [(i, i + r) for i in range(lo + r, hi - r, step)]
        else:
            yield (lo, lo + r)

    def sort(lo, hi):
        if hi - lo >= 1:
            mid = lo + (hi - lo) // 2
            yield from sort(lo, mid)
            yield from sort(mid + 1, hi)
            yield from merge(lo, hi, 1)

    return list(sort(0, n - 1))


def _sort_tiles_desc(tiles):
    tiles = list(tiles)
    for i, j in _sorting_network(len(tiles)):
        tiles[i], tiles[j] = jnp.maximum(tiles[i], tiles[j]), jnp.minimum(tiles[i], tiles[j])
    return tiles


def _merge_top(lists, singles=None):
    lists = list(lists)
    vals = []
    for j in range(PEER_TOPK):
        head = lists[0] if singles is None else jnp.maximum(lists[0], singles)
        m = _all_sublanes(jnp.maximum, head)
        vals.append(m)
        if j == PEER_TOPK - 1:
            break
        hit = lists[0] == m
        for i in range(min(len(lists) - 1, PEER_TOPK - 1 - j)):
            lists[i] = jnp.where(hit, lists[i + 1], lists[i])
        if singles is not None:
            singles = jnp.where(singles == m, -jnp.inf, singles)
    return vals


def _route_tile(q_scr, k1_ref, k2_ref, r2_ref, c1_ref, a1_ref, a2_ref, h, lanes):
    ntile = PEER_NKEYS // SUB
    sub = _iota((SUB, LANE), 0)
    q1 = q_scr[pl.ds(pl.multiple_of(h * 2 * PEER_HALF, PEER_HALF), PEER_HALF), lanes]
    q2 = q_scr[pl.ds(pl.multiple_of(h * 2 * PEER_HALF + PEER_HALF, PEER_HALF), PEER_HALF), lanes]
    s1 = _dot(k1_ref[...], q1, HI)
    s2 = _dot(k2_ref[...], q2, HI)
    t1 = [s1[k * SUB:(k + 1) * SUB, :] for k in range(ntile)]
    t2 = [s2[k * SUB:(k + 1) * SUB, :] for k in range(ntile)]
    v1 = _merge_top(_sort_tiles_desc(t1))
    v2 = _merge_top(_sort_tiles_desc(t2))
    v2lo, v2hi = v2[SUB - 1], v2[2 * SUB - 1]
    for b in range(SUB - 2, -1, -1):
        v2lo = jnp.where(sub == b, v2[b], v2lo)
        v2hi = jnp.where(sub == b, v2[SUB + b], v2hi)
    cand_lo = [jnp.where(sub < min(PEER_TOPK // (a + 1), SUB), v1[a] + v2lo, -jnp.inf) for a in range(PEER_TOPK)]
    cand_hi = v1[0] + v2hi
    tau = _merge_top(cand_lo, cand_hi)[PEER_TOPK - 1]
    top = v1[0] + v2[0]
    z = _all_sublanes(jnp.add, _tree(jnp.add, [jnp.where(cd >= tau, jnp.exp(cd - top), 0.0)
                                               for cd in cand_lo + [cand_hi]]))
    inv_z = 1.0 / z
    pairs = [_all_sublanes(jnp.add, jnp.where(cd >= tau, 1.0, 0.0)) for cd in cand_lo]
    pairs[0] = pairs[0] + _all_sublanes(jnp.add, jnp.where(cand_hi >= tau, 1.0, 0.0))
    rank2, c1 = [], []
    for k in range(ntile):
        r = jnp.full((SUB, LANE), float(PEER_TOPK), F32)
        cnt = jnp.zeros((SUB, LANE), F32)
        for j in range(PEER_TOPK - 1, -1, -1):
            r = jnp.where(t2[k] >= v2[j], float(j), r)
            cnt = jnp.where(t1[k] == v1[j], pairs[j], cnt)
        rank2.append(r)
        c1.append(cnt)
    r2_ref[h, :, lanes] = jnp.concatenate(rank2, axis=0).astype(BF16)
    c1_ref[h, :, lanes] = jnp.concatenate(c1, axis=0)
    a1_ref[h, :, lanes] = jnp.concatenate([jnp.exp(t - v1[0]) for t in t1], axis=0)
    a2_ref[h, :, lanes] = jnp.concatenate([jnp.exp(t - v2[0]) * inv_z for t in t2], axis=0).astype(BF16)


def _peer_body(xm_ref, xr_ref, wq_ref, k1_ref, k2_ref, u_ref, vt_ref, g_ref, b_ref, out_ref,
               acc, xbf, p_scr, q_scr, r2_scr, c1_scr, a1_scr, a2_scr):
    r = pl.program_id(0)
    c = pl.program_id(1)
    tm = xbf.shape[1]

    @pl.when(c == 0)
    def _():
        xbf[...] = xm_ref[...].astype(BF16)
        acc[...] = jnp.zeros(acc.shape, F32)
        p_scr[...] = jnp.zeros(p_scr.shape, BF16)
        q_scr[...] = _dot(wq_ref[...], xr_ref[...].astype(BF16))

    @pl.when((c == 0) & (r == 0))
    def _():
        for scr in (r2_scr, c1_scr, a1_scr, a2_scr):
            scr[...] = jnp.zeros(scr.shape, scr.dtype)

    ntile = tm // LANE
    units = pl.num_programs(1) - 1
    tiles_per_unit = PEER_HEADS * ntile // units
    unit = (c + units - 1) % units
    h_route = unit // (ntile // tiles_per_unit)
    lane0 = (unit % (ntile // tiles_per_unit)) * tiles_per_unit * LANE
    wr = r % 2
    for sb in range(tiles_per_unit):
        _route_tile(q_scr, k1_ref, k2_ref, r2_scr.at[wr], c1_scr.at[wr], a1_scr.at[wr], a2_scr.at[wr], h_route,
                    pl.ds(pl.multiple_of(lane0 + sb * LANE, LANE), LANE))

    rd = (r + 1) % 2
    i1_base = jnp.minimum(c, pl.num_programs(1) - 2) * PEER_I1_PER_STEP
    rows16 = 2 * SUB
    sub_w = PEER_SUB_I1 * PEER_NKEYS
    acc[...] += _dot(vt_ref[0], p_scr[...])
    for s in range(PEER_I1_PER_STEP // PEER_SUB_I1):
        es = slice(s * sub_w, (s + 1) * sub_w)
        hid = _dot(u_ref[es, :], xbf[...])
        act = (0.5 * hid * (1.0 + lax.erf(hid * 0.5 ** 0.5))).astype(BF16)
        cols = []
        for l0 in range(0, tm, PEER_LANE_BLOCK):
            ls = slice(l0, l0 + PEER_LANE_BLOCK)
            tiles = []
            for j in range(s * PEER_SUB_I1, (s + 1) * PEER_SUB_I1):
                row = pl.ds(i1_base + j, 1)
                c1b = [jnp.broadcast_to(c1_scr[rd, h, row, ls], (rows16, PEER_LANE_BLOCK)).astype(BF16)
                       for h in range(PEER_HEADS)]
                a1b = [jnp.broadcast_to(a1_scr[rd, h, row, ls], (rows16, PEER_LANE_BLOCK)).astype(BF16)
                       for h in range(PEER_HEADS)]
                for kb in range(PEER_NKEYS // rows16):
                    ks = slice(kb * rows16, (kb + 1) * rows16)
                    tiles.append(_tree(jnp.add, [jnp.where(r2_scr[rd, h, ks, ls] < c1b[h],
                                                           a2_scr[rd, h, ks, ls] * a1b[h], 0.0)
                                                 for h in range(PEER_HEADS)]))
            cols.append(jnp.concatenate(tiles, axis=0))
        p_scr[es, :] = jnp.concatenate(cols, axis=1) * act

    @pl.when(c == pl.num_programs(1) - 1)
    def _():
        z = DN_ALPHA * xm_ref[...] + acc[...]
        zc = z - jnp.mean(z, 0, keepdims=True)
        var = jnp.mean(zc * zc, 0, keepdims=True)
        out_ref[...] = (zc * lax.rsqrt(var + LN_EPS) * g_ref[...] + b_ref[...]).T


def _peer(xt, wq_t, k1, k2, u, vt, g, b, tm):
    d, n = xt.shape
    nb = n // tm
    ec = PEER_I1_PER_STEP * PEER_NKEYS
    nc = u.shape[0] // ec
    const = lambda r, c: (0, 0)
    rshape = (2, PEER_HEADS, PEER_NKEYS, tm)
    return pl.pallas_call(
        _peer_body, grid=(nb + 1, nc + 1),
        in_specs=[pl.BlockSpec((d, tm), lambda r, c: (0, jnp.maximum(r - 1, 0))),
                  pl.BlockSpec((d, tm), lambda r, c: (0, jnp.minimum(r, nb - 1))),
                  pl.BlockSpec(wq_t.shape, const), pl.BlockSpec(k1.shape, const), pl.BlockSpec(k2.shape, const),
                  pl.BlockSpec((ec, d), lambda r, c: (jnp.minimum(c, nc - 1), 0)),
                  pl.BlockSpec((1, d, ec), lambda r, c: (jnp.maximum(c - 1, 0), 0, 0)),
                  pl.BlockSpec((d, 1), const), pl.BlockSpec((d, 1), const)],
        out_specs=pl.BlockSpec((tm, d), lambda r, c: (jnp.maximum(r - 1, 0), 0)),
        out_shape=jax.ShapeDtypeStruct((n, d), F32),
        scratch_shapes=[pltpu.VMEM((d, tm), F32), pltpu.VMEM((d, tm), BF16), pltpu.VMEM((ec, tm), BF16),
                        pltpu.VMEM((wq_t.shape[0], tm), F32), pltpu.VMEM(rshape, BF16), pltpu.VMEM(rshape, F32),
                        pltpu.VMEM(rshape, F32), pltpu.VMEM(rshape, BF16)],
        compiler_params=pltpu.CompilerParams(dimension_semantics=("arbitrary", "arbitrary"),
                                             vmem_limit_bytes=PEER_VMEM_LIMIT), name="peer",
    )(xt, xt, wq_t, k1, k2, u, vt, g.reshape(d, 1), b.reshape(d, 1))


def _pad_cols(w, width):
    return jnp.pad(w, ((0, 0), (0, width - w.shape[1])))


def _lane_row(*parts):
    v = jnp.concatenate([p.astype(F32).reshape(-1) for p in parts])
    return jnp.pad(v, (0, LANE - v.shape[0])).reshape(1, LANE)


def _pick_tb(t_pad):
    for tb in (48, 32, 16):
        if t_pad % tb == 0:
            return tb
    raise ValueError(t_pad)


def kernel(x_prompt, x_sample, state_hgrn_S, state_gdn_S, state_gdn_conv, state_mlstm_C, state_mlstm_n, state_mlstm_m, state_gla_S, meta_tokens, hgrn_w_in, hgrn_lb, hgrn_norm_g, hgrn_w_out, gdn_w_in, gdn_conv_w, gdn_a_log, gdn_dt_bias, gdn_norm_g, gdn_w_out, mlstm_w_in, mlstm_b_i, mlstm_b_f, mlstm_norm_g, mlstm_w_out, gla_w_in, gla_w_gate, gla_b_gate, gla_norm_g, gla_w_out, peer_w_q, peer_keys1, peer_keys2, peer_u, peer_v, ln_g, ln_b):
    bp, seq, d = x_prompt.shape
    bs, seq_s, _ = x_sample.shape
    assert d == D_MODEL and len(state_hgrn_S) == 1 and len(state_gdn_S) == 1
    assert len(state_mlstm_C) == 1 and len(state_gla_S) == 1
    tp = N_META + seq
    tp_pad = -(-tp // C) * C
    ts_pad = -(-seq_s // C) * C
    tbp = _pick_tb(tp_pad)
    tbs = _pick_tb(ts_pad)
    np_rows = bp * tp_pad
    n_real = np_rows + bs * seq_s

    meta = jnp.broadcast_to(meta_tokens.astype(F32)[None], (bp, N_META, d))
    hp = jnp.concatenate([meta, x_prompt], axis=1)
    hp = jnp.pad(hp, ((0, 0), (0, tp_pad - tp), (0, 0))).reshape(np_rows, d)
    h = jnp.concatenate([hp, x_sample.reshape(bs * seq_s, d)], axis=0)
    n = -(-n_real // TOK_MULTIPLE) * TOK_MULTIPLE
    h = jnp.pad(h, ((0, n - n_real), (0, 0)))
    tm_mm = MM_TOK_BLOCK

    def split_cols(w, sizes):
        out, o = [], 0
        for s in sizes:
            out.append(w[:, o:o + s])
            o += s
        return out

    def run_mixer(scan, proj, states_p, states_s, **kw):
        res_p = scan(proj, *states_p, nb=bp, nt=tp_pad // tbp, tb=tbp, t_valid=tp, **kw)
        ps = proj[np_rows:n_real].reshape(bs, seq_s, -1)
        ps = jnp.pad(ps, ((0, 0), (0, ts_pad - seq_s), (0, 0))).reshape(bs * ts_pad, -1)
        res_s = scan(ps, *states_s, nb=bs, nt=ts_pad // tbs, tb=tbs, t_valid=seq_s, **kw)
        o_s = res_s[0].reshape(bs, ts_pad, -1)[:, :seq_s].reshape(bs * seq_s, -1)
        o = jnp.concatenate([res_p[0], o_s], axis=0)
        o = jnp.pad(o, ((0, n - n_real), (0, 0)))
        return o, res_p[1:], res_s[1:]

    lb_all = jnp.cumsum(jax.nn.softmax(hgrn_lb.astype(F32), axis=0), axis=0)
    outs = {}
    for i in range(DEPTH):
        mix = i % 4
        if mix == 0:
            w_in = hgrn_w_in[0].astype(BF16)
            proj = _mm(h, w_in, tm_mm, 1024)
            nh, dk, dv = state_hgrn_S.shape[2:]
            zero = jnp.zeros((bp, nh, dk, dv), F32)
            scan = functools.partial(_gla_scan, ng=hgrn_norm_g[0].reshape(1, dv), p1=lb_all[i].reshape(1, nh * dk),
                                     p2=jnp.zeros((1, LANE), F32), hgrn=True)
            o, (sp,), (ss,) = run_mixer(scan, proj, (zero,), (state_hgrn_S[0],))
            outs['hgrn'] = (sp[None], ss[None])
            w_out = hgrn_w_out[0]
        elif mix == 1:
            nh, dk, dv = state_gdn_S.shape[2:]
            cwid = 2 * nh * dk + nh * dv
            qkv, a, b, g = split_cols(gdn_w_in[0], [cwid, nh, nh, nh * dv])
            w_in = jnp.concatenate([qkv, g, _pad_cols(jnp.concatenate([a, b], axis=1), LANE)], axis=1).astype(BF16)
            proj = _mm(h, w_in, tm_mm, w_in.shape[1] // 3)
            scan = functools.partial(_gdn_scan, cw=gdn_conv_w[0], ng=gdn_norm_g[0].reshape(1, dv),
                                     alog=_lane_row(gdn_a_log[0]), dtb=_lane_row(gdn_dt_bias[0]))
            o, (sp, cp), (ss, cs) = run_mixer(
                scan, proj,
                (jnp.zeros((bp, nh, dk, dv), F32), jnp.zeros((bp, CONV_W - 1, cwid), F32)),
                (state_gdn_S[0], state_gdn_conv[0]))
            outs['gdn'] = (sp[None], cp[None], ss[None], cs[None])
            w_out = gdn_w_out[0]
        elif mix == 2:
            nh, dk, dv = state_mlstm_C.shape[2:]
            q, k, v, ig, fg, og = split_cols(mlstm_w_in[0], [nh * dk, nh * dk, nh * dv, nh, nh, nh * dv])
            w_in = jnp.concatenate([q, k, v, og, _pad_cols(jnp.concatenate([ig, fg], axis=1), LANE)],
                                   axis=1).astype(BF16)
            proj = _mm(h, w_in, tm_mm, w_in.shape[1] // 3)
            zeros_nh = jnp.zeros((nh,), F32)
            scan = functools.partial(_mlstm_scan, ng=mlstm_norm_g[0].reshape(1, nh * dv),
                                     bi=_lane_row(mlstm_b_i[0]), bf=_lane_row(zeros_nh, mlstm_b_f[0]))
            o, (cp, npp, mp), (cs, ns, ms) = run_mixer(
                scan, proj,
                (jnp.zeros((bp, nh, dk, dv), F32), jnp.zeros((bp, nh, dk), F32), jnp.zeros((bp, nh), F32)),
                (state_mlstm_C[0], state_mlstm_n[0], state_mlstm_m[0]))
            outs['mlstm'] = (cp[None], npp[None], mp.reshape(1, bp, nh), cs[None], ns[None], ms.reshape(1, bs, nh))
            w_out = mlstm_w_out[0]
        else:
            nh, dk, dv = state_gla_S.shape[2:]
            rank = gla_w_gate.shape[1]
            q, k, v, g, r = split_cols(gla_w_in[0], [nh * dk, nh * dk, nh * dv, nh * dv, rank])
            w_in = jnp.concatenate([q, k, v, g, _pad_cols(r, LANE)], axis=1).astype(BF16)
            proj = _mm(h, w_in, tm_mm, w_in.shape[1] // 5)
            wg = jnp.pad(gla_w_gate[0].astype(F32), ((0, LANE - rank), (0, 0)))
            scan = functools.partial(_gla_scan, ng=gla_norm_g[0].reshape(1, dv), p1=wg,
                                     p2=gla_b_gate[0].reshape(1, nh * dk), hgrn=False)
            o, (sp,), (ss,) = run_mixer(scan, proj, (jnp.zeros((bp, nh, dk, dv), F32),), (state_gla_S[0],))
            outs['gla'] = (sp[None], ss[None])
            w_out = gla_w_out[0]

        h1t = _mm_res_ln_t(o, w_out.astype(BF16), h, ln_g[i, 0], ln_b[i, 0], TOK_BLOCK)
        ec = PEER_I1_PER_STEP * PEER_NKEYS
        vt = peer_v[i].astype(BF16).reshape(-1, ec, d).transpose(0, 2, 1)
        h = _peer(h1t, peer_w_q[i].T.astype(BF16), peer_keys1[i], peer_keys2[i], peer_u[i].astype(BF16), vt,
                  ln_g[i, 1], ln_b[i, 1], PEER_TOK_BLOCK)

    y_prompt = h[:np_rows].reshape(bp, tp_pad, d)[:, N_META:tp]
    y_sample = h[np_rows:n_real].reshape(bs, seq_s, d)
    hg, gd, ml, gl = outs['hgrn'], outs['gdn'], outs['mlstm'], outs['gla']
    return (y_prompt, y_sample, hg[0], gd[0], gd[1], ml[0], ml[1], ml[2], gl[0],
            hg[1], gd[2], gd[3], ml[3], ml[4], ml[5], gl[1])
```

```python
import functools

import jax
import jax.numpy as jnp
from jax import lax
from jax.experimental import pallas as pl
from jax.experimental.pallas import tpu as pltpu

F32 = jnp.float32
BF16 = jnp.bfloat16
HI = lax.Precision.HIGHEST

D_MODEL = 1024
DEPTH = 4
N_META = 16
C = 16
CONV_W = 4
NORM_EPS = 1e-6
LN_EPS = 1e-5
DN_ALPHA = (2 * DEPTH) ** 0.25
GLA_GATE_NORM = 16.0
LANE = 128
PEER_HEADS = 8
PEER_NKEYS = 128
PEER_TOPK = 16
PEER_HALF = 128
PEER_I1_PER_STEP = 8
PEER_VMEM_LIMIT = 56 * 1024 * 1024
TOK_BLOCK = 256
PEER_TOK_BLOCK = 512
PEER_LANE_BLOCK = 256
MM_TOK_BLOCK = 768
TOK_MULTIPLE = 1536
VMEM_LIMIT = 48 * 1024 * 1024


def _cparams(sem):
    return pltpu.CompilerParams(dimension_semantics=sem, vmem_limit_bytes=VMEM_LIMIT)


def _iota(shape, dim):
    return lax.broadcasted_iota(jnp.int32, shape, dim)


def _dot_nt(a, b, precision=None):
    return lax.dot_general(a, b, (((1,), (1,)), ((), ())), precision=precision, preferred_element_type=F32)


def _dot_tn(a, b, precision=None):
    return lax.dot_general(a, b, (((0,), (0,)), ((), ())), precision=precision, preferred_element_type=F32)


def _dot(a, b, precision=None):
    return jnp.dot(a, b, precision=precision, preferred_element_type=F32)


def _split(x):
    hi = x.astype(BF16)
    return hi, (x - hi.astype(F32)).astype(BF16)


def _dot3(a, b, nt=False):
    (ah, al), (bh, bl) = a, b
    d = _dot_nt if nt else _dot
    return d(ah, bh) + d(ah, bl) + d(al, bh)


def _cumsum_rows(x):
    tri = (_iota((C, C), 0) >= _iota((C, C), 1)).astype(F32)
    return _dot(tri, x, HI)


def _rows_to_cols(x):
    eye = (_iota((LANE, LANE), 0) == _iota((LANE, LANE), 1)).astype(F32)
    return _dot_nt(eye, x, HI)


def _mm_body(x_ref, w_ref, o_ref):
    o_ref[...] = _dot(x_ref[...].astype(BF16), w_ref[...])


def _mm(x, w, tm, tn):
    m, k = x.shape
    n = w.shape[1]
    return pl.pallas_call(
        _mm_body, grid=(m // tm, n // tn),
        in_specs=[pl.BlockSpec((tm, k), lambda i, j: (i, 0)), pl.BlockSpec((k, tn), lambda i, j: (0, j))],
        out_specs=pl.BlockSpec((tm, tn), lambda i, j: (i, j)),
        out_shape=jax.ShapeDtypeStruct((m, n), F32),
        compiler_params=_cparams(("parallel", "parallel")), name="mm")(x, w)


def _mm_ln_body(o_ref, w_ref, h_ref, g_ref, b_ref, out_ref):
    y = _dot(o_ref[...].astype(BF16), w_ref[...])
    z = DN_ALPHA * h_ref[...] + y
    zc = z - jnp.mean(z, -1, keepdims=True)
    var = jnp.mean(zc * zc, -1, keepdims=True)
    out_ref[...] = (zc * lax.rsqrt(var + LN_EPS) * g_ref[...] + b_ref[...]).T


def _mm_res_ln_t(o, w, h, g, b, tm):
    n, k = o.shape
    d = w.shape[1]
    return pl.pallas_call(
        _mm_ln_body, grid=(n // tm,),
        in_specs=[pl.BlockSpec((tm, k), lambda i: (i, 0)), pl.BlockSpec((k, d), lambda i: (0, 0)),
                  pl.BlockSpec((tm, d), lambda i: (i, 0)), pl.BlockSpec((1, d), lambda i: (0, 0)),
                  pl.BlockSpec((1, d), lambda i: (0, 0))],
        out_specs=pl.BlockSpec((d, tm), lambda i: (0, i)),
        out_shape=jax.ShapeDtypeStruct((d, n), F32),
        compiler_params=_cparams(("parallel",)), name="mm_res_ln")(o, w, h, g.reshape(1, d), b.reshape(1, d))


def _valid_rows(t, r0, tb, t_valid, nt):
    if t_valid == nt * tb:
        return None
    return (t * tb + r0 + _iota((C, 1), 0)) < t_valid


def _gla_body(*refs, nseq, hgrn, nh, dk, dv, tb, t_valid, nt):
    proj_refs = refs[:nseq]
    s0_ref, ng_ref, p1_ref, p2_ref, o_ref, sout_ref, st_scr = refs[nseq:]
    t = pl.program_id(1)

    @pl.when(t == 0)
    def _():
        for sq in range(nseq):
            for h in range(nh):
                st_scr[sq, h] = s0_ref[sq, h].T

    kw = nh * dk
    vw = nh * dv
    nr = nh * C
    ri = _iota((nr, 1), 0)
    cj = _iota((1, nr), 1)
    rhead = ri // C
    tri = ((rhead == cj // C) & (ri % C >= cj % C)).astype(F32)
    row3 = _iota((1, C, 1), 1)

    def stack(ref, rows, col0, width):
        return jnp.concatenate([ref[rows, col0 + h * width:col0 + (h + 1) * width] for h in range(nh)], axis=0)

    prep = {}
    for sq, proj_ref in enumerate(proj_refs):
        for ci in range(tb // C):
            rows = slice(ci * C, (ci + 1) * C)
            if hgrn:
                q = jax.nn.silu(stack(proj_ref, rows, 0, dk))
                lb = jnp.concatenate([jnp.broadcast_to(p1_ref[:, h * dk:(h + 1) * dk], (C, dk)) for h in range(nh)],
                                     axis=0)
                fg = lb + (1.0 - lb) * jax.nn.sigmoid(stack(proj_ref, rows, kw, dk))
                k = 1.0 - fg
                g = jnp.log(fg)
            else:
                r = proj_ref[rows, 2 * kw + 2 * vw:2 * kw + 2 * vw + LANE]
                logf_all = jax.nn.log_sigmoid(_dot(r, p1_ref[...], HI) + p2_ref[...]) / GLA_GATE_NORM
                q = stack(proj_ref, rows, 0, dk) * dk ** -0.5
                k = stack(proj_ref, rows, kw, dk)
                g = jnp.concatenate([logf_all[:, h * dk:(h + 1) * dk] for h in range(nh)], axis=0)
            v = stack(proj_ref, rows, 2 * kw, dv)
            if t_valid != nt * tb:
                valid = t * tb + ci * C + ri % C < t_valid
                k = jnp.where(valid, k, 0.0)
                g = jnp.where(valid, g, 0.0)
            b = _dot(tri, g, HI)
            b3, q3, k3, v3 = (x.reshape(nh, C, x.shape[-1]) for x in (b, q, k, v))
            o3 = jnp.zeros((nh, C, dv), F32)
            for s in range(C):
                d = jnp.exp(jnp.where(row3 >= s, b3 - b3[:, s:s + 1, :], -jnp.inf))
                col = jnp.sum(q3 * (k3[:, s:s + 1, :] * d), axis=-1, keepdims=True)
                o3 = o3 + col * v3[:, s:s + 1, :]
            b_end = b3[:, C - 1:C, :]
            prep[sq, ci] = dict(o=o3.reshape(nr, dv), qe=q * jnp.exp(b), kd=(k3 * jnp.exp(b_end - b3)).reshape(nr, dk),
                                dec=jnp.exp(b_end), v=v)

    sts = [st_scr[sq] for sq in range(nseq)]
    for ci in range(tb // C):
        rows = slice(ci * C, (ci + 1) * C)
        for sq in range(nseq):
            p = prep[sq, ci]
            inter = _dot_nt(p['qe'], sts[sq].reshape(nh * dv, dk))
            o = p['o'] + _tree(jnp.add, [jnp.where(rhead == h, inter[:, h * dv:(h + 1) * dv], 0.0)
                                         for h in range(nh)])
            v_wide = jnp.concatenate([jnp.where(rhead == h, p['v'], 0.0) for h in range(nh)], axis=1)
            sts[sq] = sts[sq] * p['dec'] + _dot_tn(v_wide, p['kd']).reshape(nh, dv, dk)
            o = o * lax.rsqrt(jnp.mean(o * o, -1, keepdims=True) + NORM_EPS) * ng_ref[...]
            for h in range(nh):
                gate = proj_refs[sq][rows, 2 * kw + vw + h * dv:2 * kw + vw + (h + 1) * dv]
                o_ref[sq, rows, h * dv:(h + 1) * dv] = o[h * C:(h + 1) * C, :] * jax.nn.silu(gate)
    for sq in range(nseq):
        st_scr[sq] = sts[sq]

    @pl.when(t == nt - 1)
    def _():
        for sq in range(nseq):
            for h in range(nh):
                sout_ref[sq, h] = st_scr[sq, h].T


def _seqs_per_step(nb, chunks):
    return max(n for n in (8, 4, 2, 1) if nb % n == 0 and n * chunks <= 8)


def _proj_specs(nseq, nt, tb, wp):
    return [pl.BlockSpec((tb, wp), functools.partial(lambda b, t, sq: ((b * nseq + sq) * nt + t, 0), sq=sq))
            for sq in range(nseq)]


def _gla_scan(proj, s0, ng, p1, p2, *, hgrn, nb, nt, tb, t_valid):
    _, nh, dk, dv = s0.shape
    wp = proj.shape[1]
    nseq = _seqs_per_step(nb, tb // C)
    body = functools.partial(_gla_body, nseq=nseq, hgrn=hgrn, nh=nh, dk=dk, dv=dv, tb=tb, t_valid=t_valid, nt=nt)
    const = lambda b, t: (0, 0)
    o, s = pl.pallas_call(
        body, grid=(nb // nseq, nt),
        in_specs=_proj_specs(nseq, nt, tb, wp) + [
            pl.BlockSpec((nseq, nh, dk, dv), lambda b, t: (b, 0, 0, 0)),
            pl.BlockSpec(ng.shape, const), pl.BlockSpec(p1.shape, const), pl.BlockSpec(p2.shape, const)],
        out_specs=[pl.BlockSpec((nseq, tb, nh * dv), lambda b, t: (b, t, 0)),
                   pl.BlockSpec((nseq, nh, dk, dv), lambda b, t: (b, 0, 0, 0))],
        out_shape=[jax.ShapeDtypeStruct((nb, nt * tb, nh * dv), F32), jax.ShapeDtypeStruct(s0.shape, F32)],
        scratch_shapes=[pltpu.VMEM((nseq, nh, dv, dk), F32)],
        compiler_params=_cparams(("parallel", "arbitrary")), name="hgrn_scan" if hgrn else "gla_scan",
    )(*([proj] * nseq), s0, ng, p1, p2)
    return o.reshape(nb * nt * tb, nh * dv), s


def _gdn_body(*refs, nseq, nh, dk, dv, tb, t_valid, nt):
    proj_refs = refs[:nseq]
    (s0_ref, conv0_ref, cw_ref, ng_ref, alog_ref, dtb_ref, o_ref, sout_ref, convout_ref,
     st_scr, cbuf, ybuf) = refs[nseq:]
    t = pl.program_id(1)
    cwid = 2 * nh * dk + nh * dv
    pre = 8 - (CONV_W - 1)

    @pl.when(t == 0)
    def _():
        for sq in range(nseq):
            for h in range(nh):
                st_scr[sq, :, h * dv:(h + 1) * dv] = s0_ref[sq, h]
            cbuf[sq, 0:8, :] = jnp.zeros((8, cwid), F32)
            cbuf[sq, pre:8, :] = conv0_ref[sq]

    for sq in range(nseq):
        cbuf[sq, 8:8 + tb, :] = proj_refs[sq][:, 0:cwid]
        y = cbuf[sq, pre:pre + tb, :] * cw_ref[0:1, :]
        for j in range(1, CONV_W):
            y = y + cbuf[sq, pre + j:pre + j + tb, :] * cw_ref[j:j + 1, :]
        ybuf[sq] = jax.nn.silu(y)

    nr = nh * C
    ri = _iota((nr, 1), 0)
    cj = _iota((1, nr), 1)
    rhead = ri // C
    same = rhead == cj // C
    incl = same & (ri % C >= cj % C)
    strict = same & (ri % C > cj % C)
    eye = (ri == cj).astype(F32)

    def stack(ref, rows, base, width):
        return jnp.concatenate([ref[rows, base + h * width:base + (h + 1) * width] for h in range(nh)], axis=0)

    def own_block(x):
        return _tree(jnp.add, [jnp.where(rhead == h, x[:, h * dv:(h + 1) * dv], 0.0) for h in range(nh)])

    prep = []
    for sq, ci in [(sq, ci) for sq in range(nseq) for ci in range(tb // C)]:
        rows = slice(ci * C, (ci + 1) * C)
        valid = _valid_rows(t, ci * C, tb, t_valid, nt)
        small = proj_refs[sq][rows, cwid + nh * dv:cwid + nh * dv + LANE]
        g_all = -jnp.exp(alog_ref[...]) * jax.nn.softplus(small + dtb_ref[...])
        beta_all = jax.nn.sigmoid(small)
        if valid is not None:
            g_all = jnp.where(valid, g_all, 0.0)
            beta_all = jnp.where(valid, beta_all, 0.0)
        b_all = _cumsum_rows(g_all)
        b_t = _rows_to_cols(b_all)
        bcol = jnp.concatenate([b_all[:, h:h + 1] for h in range(nh)], axis=0)
        brow = jnp.concatenate([b_t[h:h + 1, :] for h in range(nh)], axis=1)
        beta = jnp.concatenate([beta_all[:, nh + h:nh + h + 1] for h in range(nh)], axis=0)
        bend = jnp.concatenate([jnp.broadcast_to(b_all[C - 1:C, h:h + 1], (C, 1)) for h in range(nh)], axis=0)
        dec_row = jnp.concatenate([jnp.broadcast_to(jnp.exp(b_all[C - 1:C, h:h + 1]), (1, dv)) for h in range(nh)],
                                  axis=1)
        q = stack(ybuf.at[sq], rows, 0, dk)
        k = stack(ybuf.at[sq], rows, nh * dk, dk)
        v = stack(ybuf.at[sq], rows, 2 * nh * dk, dv)
        q = q * lax.rsqrt(jnp.sum(q * q, -1, keepdims=True) + NORM_EPS) * dk ** -0.5
        k = k * lax.rsqrt(jnp.sum(k * k, -1, keepdims=True) + NORM_EPS)
        if valid is not None:
            valid_st = jnp.concatenate([valid] * nh, axis=0)
            q = jnp.where(valid_st, q, 0.0)
            k = jnp.where(valid_st, k, 0.0)
            v = jnp.where(valid_st, v, 0.0)
        decay = jnp.exp(jnp.where(incl, bcol - brow, -jnp.inf))
        eb = jnp.exp(bcol)
        ks = _split(k)
        prep.append(dict(a=jnp.where(strict, _dot3(ks, ks, nt=True) * decay * beta, 0.0), bv=beta * v,
                         bk=(beta * eb) * k, att=_dot_nt(q, k) * decay, qe=q * eb, kd=k * jnp.exp(bend - bcol),
                         dec_row=dec_row, sq=sq, ci=ci))
    for p in prep:
        a_s = _split(p['a'])
        p['a2'] = _dot3(a_s, a_s)
    for p in prep:
        a2_s = _split(p['a2'])
        p['a4'] = _dot3(a2_s, a2_s)
        p['lo'] = _dot3(_split(eye - p['a']), _split(eye + p['a2']))
    for p in prep:
        a4_s = _split(p['a4'])
        p['a8'] = _dot3(a4_s, a4_s)
    for p in prep:
        p['hi'] = _dot3(_split(eye + p['a4']), _split(eye + p['a8']))
    for p in prep:
        p['tinv'] = _split(_dot3(_split(p['lo']), _split(p['hi'])))
    for p in prep:
        p['u'] = _dot3(p['tinv'], _split(p['bv']))
        p['wk'] = _dot3(p['tinv'], _split(p['bk']))

    s_cat = [st_scr[sq] for sq in range(nseq)]
    for p in sorted(prep, key=lambda p: (p['ci'], p['sq'])):
        sq = p['sq']
        rows = slice(p['ci'] * C, (p['ci'] + 1) * C)
        v_new = p['u'] - own_block(_dot(p['wk'], s_cat[sq]))
        o = _dot(p['att'], v_new) + own_block(_dot(p['qe'], s_cat[sq]))
        v_wide = jnp.concatenate([jnp.where(rhead == h, v_new, 0.0) for h in range(nh)], axis=1)
        s_cat[sq] = s_cat[sq] * p['dec_row'] + _dot_tn(p['kd'], v_wide)
        o = o * lax.rsqrt(jnp.mean(o * o, -1, keepdims=True) + NORM_EPS) * ng_ref[...]
        for h in range(nh):
            gate = proj_refs[sq][rows, cwid + h * dv:cwid + (h + 1) * dv]
            o_ref[sq, rows, h * dv:(h + 1) * dv] = o[h * C:(h + 1) * C, :] * jax.nn.silu(gate)
    for sq in range(nseq):
        st_scr[sq] = s_cat[sq]

    @pl.when(t < nt - 1)
    def _():
        for sq in range(nseq):
            cbuf[sq, pre:8, :] = cbuf[sq, pre + tb:8 + tb, :]

    @pl.when(t == nt - 1)
    def _():
        last = t_valid - (nt - 1) * tb
        for sq in range(nseq):
            for h in range(nh):
                sout_ref[sq, h] = st_scr[sq, :, h * dv:(h + 1) * dv]
            convout_ref[sq] = cbuf[sq, pre + last:8 + last, :]


def _gdn_scan(proj, s0, conv0, cw, ng, alog, dtb, *, nb, nt, tb, t_valid):
    _, nh, dk, dv = s0.shape
    wp = proj.shape[1]
    cwid = 2 * nh * dk + nh * dv
    nseq = _seqs_per_step(nb, tb // C)
    body = functools.partial(_gdn_body, nseq=nseq, nh=nh, dk=dk, dv=dv, tb=tb, t_valid=t_valid, nt=nt)
    const = lambda b, t: (0, 0)
    o, s, conv = pl.pallas_call(
        body, grid=(nb // nseq, nt),
        in_specs=_proj_specs(nseq, nt, tb, wp) + [
            pl.BlockSpec((nseq, nh, dk, dv), lambda b, t: (b, 0, 0, 0)),
            pl.BlockSpec((nseq, CONV_W - 1, cwid), lambda b, t: (b, 0, 0)),
            pl.BlockSpec(cw.shape, const), pl.BlockSpec(ng.shape, const),
            pl.BlockSpec(alog.shape, const), pl.BlockSpec(dtb.shape, const)],
        out_specs=[pl.BlockSpec((nseq, tb, nh * dv), lambda b, t: (b, t, 0)),
                   pl.BlockSpec((nseq, nh, dk, dv), lambda b, t: (b, 0, 0, 0)),
                   pl.BlockSpec((nseq, CONV_W - 1, cwid), lambda b, t: (b, 0, 0))],
        out_shape=[jax.ShapeDtypeStruct((nb, nt * tb, nh * dv), F32), jax.ShapeDtypeStruct(s0.shape, F32),
                   jax.ShapeDtypeStruct(conv0.shape, F32)],
        scratch_shapes=[pltpu.VMEM((nseq, dk, nh * dv), F32), pltpu.VMEM((nseq, tb + 8, cwid), F32),
                        pltpu.VMEM((nseq, tb, cwid), F32)],
        compiler_params=_cparams(("parallel", "arbitrary")), name="gdn_scan",
    )(*([proj] * nseq), s0, conv0, cw, ng, alog, dtb)
    return o.reshape(nb * nt * tb, nh * dv), s, conv


def _mlstm_body(*refs, nseq, nh, dk, dv, tb, t_valid, nt):
    proj_refs = refs[:nseq]
    (c0_ref, n0_ref, m0_ref, ng_ref, bi_ref, bf_ref, o_ref, cout_ref, nout_ref, mout_ref,
     ct_scr, n_scr, m_scr) = refs[nseq:]
    t = pl.program_id(1)

    @pl.when(t == 0)
    def _():
        for sq in range(nseq):
            for h in range(nh):
                ct_scr[sq, :, h * dv:(h + 1) * dv] = c0_ref[sq, h]
                n_scr[sq, h] = n0_ref[sq, h:h + 1, :]
                m_scr[sq, h] = jnp.broadcast_to(m0_ref[sq, :, h:h + 1], (1, LANE))

    base = 2 * nh * dk + 2 * nh * dv
    nr = nh * C
    ri = _iota((nr, 1), 0)
    cj = _iota((1, nr), 1)
    rhead = ri // C
    incl = (rhead == cj // C) & (ri % C >= cj % C)

    def stack(ref, rows, col0, width):
        return jnp.concatenate([ref[rows, col0 + h * width:col0 + (h + 1) * width] for h in range(nh)], axis=0)

    def per_head_col(vals):
        return jnp.concatenate([jnp.broadcast_to(x, (C, 1)) for x in vals], axis=0)

    prep = {}
    for sq, ci in [(sq, ci) for sq in range(nseq) for ci in range(tb // C)]:
        proj_ref = proj_refs[sq]
        rows = slice(ci * C, (ci + 1) * C)
        valid = _valid_rows(t, ci * C, tb, t_valid, nt)
        small = proj_ref[rows, base:base + LANE]
        li_all = small + bi_ref[...]
        lf_all = jax.nn.log_sigmoid(small + bf_ref[...])
        if valid is not None:
            lf_all = jnp.where(valid, lf_all, 0.0)
        b_all = _cumsum_rows(lf_all)
        b_t = _rows_to_cols(b_all)
        li_t = _rows_to_cols(li_all)
        bcol = jnp.concatenate([b_all[:, nh + h:nh + h + 1] for h in range(nh)], axis=0)
        licol = jnp.concatenate([li_all[:, h:h + 1] for h in range(nh)], axis=0)
        xrow = jnp.concatenate([b_t[nh + h:nh + h + 1, :] - li_t[h:h + 1, :] for h in range(nh)], axis=1)
        if valid is not None:
            xrow = jnp.where(t * tb + ci * C + cj % C < t_valid, xrow, jnp.inf)
            licol = jnp.where(t * tb + ci * C + ri % C < t_valid, licol, -jnp.inf)
        q = stack(proj_ref, rows, 0, dk)
        k = stack(proj_ref, rows, nh * dk, dk) * dk ** -0.5
        v = stack(proj_ref, rows, 2 * nh * dk, dv)
        dmat = jnp.where(incl, bcol - xrow, -jnp.inf)
        prep[sq, ci] = dict(q=q, k=k, v=v, bcol=bcol, licol=licol, dmat=dmat, qk=_dot_nt(q, k),
                            rowmax=jnp.max(dmat, axis=1, keepdims=True),
                            b_end=[b_all[C - 1:C, nh + h:nh + h + 1] for h in range(nh)])

    c_cat = [ct_scr[sq] for sq in range(nseq)]
    n_rows = [[n_scr[sq, h] for h in range(nh)] for sq in range(nseq)]
    m_vals = [[m_scr[sq, h][:, 0:1] for h in range(nh)] for sq in range(nseq)]
    for ci, sq in [(ci, sq) for ci in range(tb // C) for sq in range(nseq)]:
        p = prep[sq, ci]
        rows = slice(ci * C, (ci + 1) * C)
        q, k, v, bcol = p['q'], p['k'], p['v'], p['bcol']
        m_inter = bcol + per_head_col(m_vals[sq])
        m_t = jnp.maximum(m_inter, p['rowmax'])
        a_int = jnp.exp(m_inter - m_t)
        qk = p['qk'] * jnp.exp(p['dmat'] - m_t)
        qc = _dot(q, c_cat[sq])
        own = _tree(jnp.add, [jnp.where(rhead == h, qc[:, h * dv:(h + 1) * dv], 0.0) for h in range(nh)])
        num = _dot(qk, v) + a_int * own
        n_st = jnp.concatenate([jnp.broadcast_to(n_rows[sq][h], (C, dk)) for h in range(nh)], axis=0)
        den = jnp.sum(qk, axis=1, keepdims=True) + a_int * jnp.sum(q * n_st, axis=1, keepdims=True)
        hid = num / jnp.maximum(jnp.abs(den), jnp.exp(-m_t))
        last = [slice(h * C + C - 1, h * C + C) for h in range(nh)]
        m_vals[sq] = [m_t[r, :] for r in last]
        a_end = [a_int[r, :] for r in last]
        wk = jnp.exp(per_head_col(p['b_end']) - bcol + p['licol'] - per_head_col(m_vals[sq])) * k
        v_wide = jnp.concatenate([jnp.where(rhead == h, v, 0.0) for h in range(nh)], axis=1)
        a_row = jnp.concatenate([jnp.broadcast_to(a, (1, dv)) for a in a_end], axis=1)
        c_cat[sq] = c_cat[sq] * a_row + _dot_tn(wk, v_wide)
        n_rows[sq] = [a_end[h] * n_rows[sq][h] + jnp.sum(wk[h * C:(h + 1) * C, :], axis=0, keepdims=True)
                      for h in range(nh)]
        hc = hid - jnp.mean(hid, -1, keepdims=True)
        hn = hc * lax.rsqrt(jnp.mean(hc * hc, -1, keepdims=True) + NORM_EPS)
        for h in range(nh):
            og = jax.nn.sigmoid(
                proj_refs[sq][rows, 2 * nh * dk + nh * dv + h * dv:2 * nh * dk + nh * dv + (h + 1) * dv])
            o_ref[sq, rows, h * dv:(h + 1) * dv] = og * (hn[h * C:(h + 1) * C, :] * ng_ref[:, h * dv:(h + 1) * dv])
    for sq in range(nseq):
        ct_scr[sq] = c_cat[sq]
        for h in range(nh):
            n_scr[sq, h] = n_rows[sq][h]
            m_scr[sq, h] = jnp.broadcast_to(m_vals[sq][h], (1, LANE))

    @pl.when(t == nt - 1)
    def _():
        lane = _iota((1, nh), 1)
        for sq in range(nseq):
            mrow = jnp.zeros((1, nh), F32)
            for h in range(nh):
                cout_ref[sq, h] = ct_scr[sq, :, h * dv:(h + 1) * dv]
                nout_ref[sq, h:h + 1, :] = n_scr[sq, h]
                mrow = jnp.where(lane == h, m_scr[sq, h][:, 0:1], mrow)
            mout_ref[sq] = mrow


def _mlstm_scan(proj, c0, n0, m0, ng, bi, bf, *, nb, nt, tb, t_valid):
    _, nh, dk, dv = c0.shape
    wp = proj.shape[1]
    nseq = _seqs_per_step(nb, tb // C)
    body = functools.partial(_mlstm_body, nseq=nseq, nh=nh, dk=dk, dv=dv, tb=tb, t_valid=t_valid, nt=nt)
    const = lambda b, t: (0, 0)
    m0 = m0.reshape(nb, 1, nh)
    o, c, n, m = pl.pallas_call(
        body, grid=(nb // nseq, nt),
        in_specs=_proj_specs(nseq, nt, tb, wp) + [
            pl.BlockSpec((nseq, nh, dk, dv), lambda b, t: (b, 0, 0, 0)),
            pl.BlockSpec((nseq, nh, dk), lambda b, t: (b, 0, 0)),
            pl.BlockSpec((nseq, 1, nh), lambda b, t: (b, 0, 0)),
            pl.BlockSpec(ng.shape, const), pl.BlockSpec(bi.shape, const), pl.BlockSpec(bf.shape, const)],
        out_specs=[pl.BlockSpec((nseq, tb, nh * dv), lambda b, t: (b, t, 0)),
                   pl.BlockSpec((nseq, nh, dk, dv), lambda b, t: (b, 0, 0, 0)),
                   pl.BlockSpec((nseq, nh, dk), lambda b, t: (b, 0, 0)),
                   pl.BlockSpec((nseq, 1, nh), lambda b, t: (b, 0, 0))],
        out_shape=[jax.ShapeDtypeStruct((nb, nt * tb, nh * dv), F32), jax.ShapeDtypeStruct(c0.shape, F32),
                   jax.ShapeDtypeStruct(n0.shape, F32), jax.ShapeDtypeStruct((nb, 1, nh), F32)],
        scratch_shapes=[pltpu.VMEM((nseq, dk, nh * dv), F32), pltpu.VMEM((nseq, nh, 1, dk), F32),
                        pltpu.VMEM((nseq, nh, 1, LANE), F32)],
        compiler_params=_cparams(("parallel", "arbitrary")), name="mlstm_scan",
    )(*([proj] * nseq), c0, n0, m0, ng, bi, bf)
    return o.reshape(nb * nt * tb, nh * dv), c, n, m


SUB = 8


def _tree(op, xs):
    xs = list(xs)
    while len(xs) > 1:
        xs = [op(xs[i], xs[i + 1]) if i + 1 < len(xs) else xs[i] for i in range(0, len(xs), 2)]
    return xs[0]


def _all_sublanes(op, m):
    for shift in (4, 2, 1):
        m = op(m, pltpu.roll(m, shift, axis=0))
    return m


def _sorting_network(n):
    def merge(lo, hi, r):
        step = r * 2
        if step < hi - lo:
            yield from merge(lo, hi, step)
            yield from merge(lo + r, hi, step)
            yield from [(i, i + r) for i in range(lo + r, hi - r, step)]
        else:
            yield (lo, lo + r)

    def sort(lo, hi):
        if hi - lo >= 1:
            mid = lo + (hi - lo) // 2
            yield from sort(lo, mid)
            yield from sort(mid + 1, hi)
            yield from merge(lo, hi, 1)

    return list(sort(0, n - 1))


def _sort_tiles_desc(tiles):
    tiles = list(tiles)
    for i, j in _sorting_network(len(tiles)):
        tiles[i], tiles[j] = jnp.maximum(tiles[i], tiles[j]), jnp.minimum(tiles[i], tiles[j])
    return tiles


def _merge_top(lists, singles=None):
    lists = list(lists)
    vals = []
    for j in range(PEER_TOPK):
        head = lists[0] if singles is None else jnp.maximum(lists[0], singles)
        m = _all_sublanes(jnp.maximum, head)
        vals.append(m)
        if j == PEER_TOPK - 1:
            break
        hit = lists[0] == m
        for i in range(min(len(lists) - 1, PEER_TOPK - 1 - j)):
            lists[i] = jnp.where(hit, lists[i + 1], lists[i])
        if singles is not None:
            singles = jnp.where(singles == m, -jnp.inf, singles)
    return vals


def _route_tile(q_scr, k1_ref, k2_ref, r2_ref, c1_ref, a1_ref, a2_ref, h, lanes):
    ntile = PEER_NKEYS // SUB
    sub = _iota((SUB, LANE), 0)
    q1 = q_scr[pl.ds(pl.multiple_of(h * 2 * PEER_HALF, PEER_HALF), PEER_HALF), lanes]
    q2 = q_scr[pl.ds(pl.multiple_of(h * 2 * PEER_HALF + PEER_HALF, PEER_HALF), PEER_HALF), lanes]
    s1 = _dot(k1_ref[...], q1, HI)
    s2 = _dot(k2_ref[...], q2, HI)
    t1 = [s1[k * SUB:(k + 1) * SUB, :] for k in range(ntile)]
    t2 = [s2[k * SUB:(k + 1) * SUB, :] for k in range(ntile)]
    v1 = _merge_top(_sort_tiles_desc(t1))
    v2 = _merge_top(_sort_tiles_desc(t2))
    v2lo, v2hi = v2[SUB - 1], v2[2 * SUB - 1]
    for b in range(SUB - 2, -1, -1):
        v2lo = jnp.where(sub == b, v2[b], v2lo)
        v2hi = jnp.where(sub == b, v2[SUB + b], v2hi)
    cand_lo = [jnp.where(sub < min(PEER_TOPK // (a + 1), SUB), v1[a] + v2lo, -jnp.inf) for a in range(PEER_TOPK)]
    cand_hi = v1[0] + v2hi
    tau = _merge_top(cand_lo, cand_hi)[PEER_TOPK - 1]
    top = v1[0] + v2[0]
    z = _all_sublanes(jnp.add, _tree(jnp.add, [jnp.where(cd >= tau, jnp.exp(cd - top), 0.0)
                                               for cd in cand_lo + [cand_hi]]))
    inv_z = 1.0 / z
    pairs = [_all_sublanes(jnp.add, jnp.where(cd >= tau, 1.0, 0.0)) for cd in cand_lo]
    pairs[0] = pairs[0] + _all_sublanes(jnp.add, jnp.where(cand_hi >= tau, 1.0, 0.0))
    rank2, c1 = [], []
    for k in range(ntile):
        r = jnp.full((SUB, LANE), float(PEER_TOPK), F32)
        cnt = jnp.zeros((SUB, LANE), F32)
        for j in range(PEER_TOPK - 1, -1, -1):
            r = jnp.where(t2[k] >= v2[j], float(j), r)
            cnt = jnp.where(t1[k] == v1[j], pairs[j], cnt)
        rank2.append(r)
        c1.append(cnt)
    r2_ref[h, :, lanes] = jnp.concatenate(rank2, axis=0).astype(BF16)
    c1_ref[h, :, lanes] = jnp.concatenate(c1, axis=0)
    a1_ref[h, :, lanes] = jnp.concatenate([jnp.exp(t - v1[0]) for t in t1], axis=0)
    a2_ref[h, :, lanes] = jnp.concatenate([jnp.exp(t - v2[0]) * inv_z for t in t2], axis=0).astype(BF16)


def _peer_body(xm_ref, xr_ref, wq_ref, k1_ref, k2_ref, u_ref, vt_ref, g_ref, b_ref, out_ref,
               acc, xbf, p_scr, q_scr, r2_scr, c1_scr, a1_scr, a2_scr):
    r = pl.program_id(0)
    c = pl.program_id(1)
    tm = xbf.shape[1]

    @pl.when(c == 0)
    def _():
        xbf[...] = xm_ref[...].astype(BF16)
        acc[...] = jnp.zeros(acc.shape, F32)
        p_scr[...] = jnp.zeros(p_scr.shape, BF16)
        q_scr[...] = _dot(wq_ref[...], xr_ref[...].astype(BF16))

    @pl.when((c == 0) & (r == 0))
    def _():
        for scr in (r2_scr, c1_scr, a1_scr, a2_scr):
            scr[...] = jnp.zeros(scr.shape, scr.dtype)

    ntile = tm // LANE
    units = pl.num_programs(1) - 1
    tiles_per_unit = PEER_HEADS * ntile // units
    unit = (c + units - 1) % units
    h_route = unit // (ntile // tiles_per_unit)
    lane0 = (unit % (ntile // tiles_per_unit)) * tiles_per_unit * LANE
    wr = r % 2
    for sb in range(tiles_per_unit):
        _route_tile(q_scr, k1_ref, k2_ref, r2_scr.at[wr], c1_scr.at[wr], a1_scr.at[wr], a2_scr.at[wr], h_route,
                    pl.ds(pl.multiple_of(lane0 + sb * LANE, LANE), LANE))

    rd = (r + 1) % 2
    i1_base = jnp.minimum(c, pl.num_programs(1) - 2) * PEER_I1_PER_STEP
    rows16 = 2 * SUB
    acc[...] += _dot(vt_ref[0], p_scr[...])
    hid = _dot(u_ref[...], xbf[...])
    act = (0.5 * hid * (1.0 + lax.erf(hid * 0.5 ** 0.5))).astype(BF16)
    for l0 in range(0, tm, PEER_LANE_BLOCK):
        ls = slice(l0, l0 + PEER_LANE_BLOCK)
        for j in range(PEER_I1_PER_STEP):
            row = pl.ds(i1_base + j, 1)
            nkb = PEER_NKEYS // rows16
            w = [None] * nkb
            for h in range(PEER_HEADS):
                c1b = jnp.broadcast_to(c1_scr[rd, h, row, ls], (rows16, PEER_LANE_BLOCK)).astype(BF16)
                a1b = jnp.broadcast_to(a1_scr[rd, h, row, ls], (rows16, PEER_LANE_BLOCK)).astype(BF16)
                for kb in range(nkb):
                    ks = slice(kb * rows16, (kb + 1) * rows16)
                    term = jnp.where(r2_scr[rd, h, ks, ls] < c1b, a2_scr[rd, h, ks, ls] * a1b, 0.0)
                    w[kb] = term if w[kb] is None else w[kb] + term
            for kb in range(nkb):
                es = slice(j * PEER_NKEYS + kb * rows16, j * PEER_NKEYS + (kb + 1) * rows16)
                p_scr[es, ls] = w[kb] * act[es, ls]

    @pl.when(c == pl.num_programs(1) - 1)
    def _():
        z = DN_ALPHA * xm_ref[...] + acc[...]
        zc = z - jnp.mean(z, 0, keepdims=True)
        var = jnp.mean(zc * zc, 0, keepdims=True)
        out_ref[...] = (zc * lax.rsqrt(var + LN_EPS) * g_ref[...] + b_ref[...]).T


def _peer(xt, wq_t, k1, k2, u, vt, g, b, tm):
    d, n = xt.shape
    nb = n // tm
    ec = PEER_I1_PER_STEP * PEER_NKEYS
    nc = u.shape[0] // ec
    const = lambda r, c: (0, 0)
    rshape = (2, PEER_HEADS, PEER_NKEYS, tm)
    return pl.pallas_call(
        _peer_body, grid=(nb + 1, nc + 1),
        in_specs=[pl.BlockSpec((d, tm), lambda r, c: (0, jnp.maximum(r - 1, 0))),
                  pl.BlockSpec((d, tm), lambda r, c: (0, jnp.minimum(r, nb - 1))),
                  pl.BlockSpec(wq_t.shape, const), pl.BlockSpec(k1.shape, const), pl.BlockSpec(k2.shape, const),
                  pl.BlockSpec((ec, d), lambda r, c: (jnp.minimum(c, nc - 1), 0)),
                  pl.BlockSpec((1, d, ec), lambda r, c: (jnp.maximum(c - 1, 0), 0, 0)),
                  pl.BlockSpec((d, 1), const), pl.BlockSpec((d, 1), const)],
        out_specs=pl.BlockSpec((tm, d), lambda r, c: (jnp.maximum(r - 1, 0), 0)),
        out_shape=jax.ShapeDtypeStruct((n, d), F32),
        scratch_shapes=[pltpu.VMEM((d, tm), F32), pltpu.VMEM((d, tm), BF16), pltpu.VMEM((ec, tm), BF16),
                        pltpu.VMEM((wq_t.shape[0], tm), F32), pltpu.VMEM(rshape, BF16), pltpu.VMEM(rshape, F32),
                        pltpu.VMEM(rshape, F32), pltpu.VMEM(rshape, BF16)],
        compiler_params=pltpu.CompilerParams(dimension_semantics=("arbitrary", "arbitrary"),
                                             vmem_limit_bytes=PEER_VMEM_LIMIT), name="peer",
    )(xt, xt, wq_t, k1, k2, u, vt, g.reshape(d, 1), b.reshape(d, 1))


def _pad_cols(w, width):
    return jnp.pad(w, ((0, 0), (0, width - w.shape[1])))


def _lane_row(*parts):
    v = jnp.concatenate([p.astype(F32).reshape(-1) for p in parts])
    return jnp.pad(v, (0, LANE - v.shape[0])).reshape(1, LANE)


def _pick_tb(t_pad):
    for tb in (48, 32, 16):
        if t_pad % tb == 0:
            return tb
    raise ValueError(t_pad)


def kernel(x_prompt, x_sample, state_hgrn_S, state_gdn_S, state_gdn_conv, state_mlstm_C, state_mlstm_n, state_mlstm_m, state_gla_S, meta_tokens, hgrn_w_in, hgrn_lb, hgrn_norm_g, hgrn_w_out, gdn_w_in, gdn_conv_w, gdn_a_log, gdn_dt_bias, gdn_norm_g, gdn_w_out, mlstm_w_in, mlstm_b_i, mlstm_b_f, mlstm_norm_g, mlstm_w_out, gla_w_in, gla_w_gate, gla_b_gate, gla_norm_g, gla_w_out, peer_w_q, peer_keys1, peer_keys2, peer_u, peer_v, ln_g, ln_b):
    bp, seq, d = x_prompt.shape
    bs, seq_s, _ = x_sample.shape
    assert d == D_MODEL and len(state_hgrn_S) == 1 and len(state_gdn_S) == 1
    assert len(state_mlstm_C) == 1 and len(state_gla_S) == 1
    tp = N_META + seq
    tp_pad = -(-tp // C) * C
    ts_pad = -(-seq_s // C) * C
    tbp = _pick_tb(tp_pad)
    tbs = _pick_tb(ts_pad)
    np_rows = bp * tp_pad
    n_real = np_rows + bs * seq_s

    meta = jnp.broadcast_to(meta_tokens.astype(F32)[None], (bp, N_META, d))
    hp = jnp.concatenate([meta, x_prompt], axis=1)
    hp = jnp.pad(hp, ((0, 0), (0, tp_pad - tp), (0, 0))).reshape(np_rows, d)
    h = jnp.concatenate([hp, x_sample.reshape(bs * seq_s, d)], axis=0)
    n = -(-n_real // TOK_MULTIPLE) * TOK_MULTIPLE
    h = jnp.pad(h, ((0, n - n_real), (0, 0)))
    tm_mm = MM_TOK_BLOCK

    def split_cols(w, sizes):
        out, o = [], 0
        for s in sizes:
            out.append(w[:, o:o + s])
            o += s
        return out

    def run_mixer(scan, proj, states_p, states_s, **kw):
        res_p = scan(proj, *states_p, nb=bp, nt=tp_pad // tbp, tb=tbp, t_valid=tp, **kw)
        ps = proj[np_rows:n_real].reshape(bs, seq_s, -1)
        ps = jnp.pad(ps, ((0, 0), (0, ts_pad - seq_s), (0, 0))).reshape(bs * ts_pad, -1)
        res_s = scan(ps, *states_s, nb=bs, nt=ts_pad // tbs, tb=tbs, t_valid=seq_s, **kw)
        o_s = res_s[0].reshape(bs, ts_pad, -1)[:, :seq_s].reshape(bs * seq_s, -1)
        o = jnp.concatenate([res_p[0], o_s], axis=0)
        o = jnp.pad(o, ((0, n - n_real), (0, 0)))
        return o, res_p[1:], res_s[1:]

    lb_all = jnp.cumsum(jax.nn.softmax(hgrn_lb.astype(F32), axis=0), axis=0)
    outs = {}
    for i in range(DEPTH):
        mix = i % 4
        if mix == 0:
            w_in = hgrn_w_in[0].astype(BF16)
            proj = _mm(h, w_in, tm_mm, 1024)
            nh, dk, dv = state_hgrn_S.shape[2:]
            zero = jnp.zeros((bp, nh, dk, dv), F32)
            scan = functools.partial(_gla_scan, ng=hgrn_norm_g[0].reshape(1, dv), p1=lb_all[i].reshape(1, nh * dk),
                                     p2=jnp.zeros((1, LANE), F32), hgrn=True)
            o, (sp,), (ss,) = run_mixer(scan, proj, (zero,), (state_hgrn_S[0],))
            outs['hgrn'] = (sp[None], ss[None])
            w_out = hgrn_w_out[0]
        elif mix == 1:
            nh, dk, dv = state_gdn_S.shape[2:]
            cwid = 2 * nh * dk + nh * dv
            qkv, a, b, g = split_cols(gdn_w_in[0], [cwid, nh, nh, nh * dv])
            w_in = jnp.concatenate([qkv, g, _pad_cols(jnp.concatenate([a, b], axis=1), LANE)], axis=1).astype(BF16)
            proj = _mm(h, w_in, tm_mm, w_in.shape[1] // 3)
            scan = functools.partial(_gdn_scan, cw=gdn_conv_w[0], ng=gdn_norm_g[0].reshape(1, dv),
                                     alog=_lane_row(gdn_a_log[0]), dtb=_lane_row(gdn_dt_bias[0]))
            o, (sp, cp), (ss, cs) = run_mixer(
                scan, proj,
                (jnp.zeros((bp, nh, dk, dv), F32), jnp.zeros((bp, CONV_W - 1, cwid), F32)),
                (state_gdn_S[0], state_gdn_conv[0]))
            outs['gdn'] = (sp[None], cp[None], ss[None], cs[None])
            w_out = gdn_w_out[0]
        elif mix == 2:
            nh, dk, dv = state_mlstm_C.shape[2:]
            q, k, v, ig, fg, og = split_cols(mlstm_w_in[0], [nh * dk, nh * dk, nh * dv, nh, nh, nh * dv])
            w_in = jnp.concatenate([q, k, v, og, _pad_cols(jnp.concatenate([ig, fg], axis=1), LANE)],
                                   axis=1).astype(BF16)
            proj = _mm(h, w_in, tm_mm, w_in.shape[1] // 3)
            zeros_nh = jnp.zeros((nh,), F32)
            scan = functools.partial(_mlstm_scan, ng=mlstm_norm_g[0].reshape(1, nh * dv),
                                     bi=_lane_row(mlstm_b_i[0]), bf=_lane_row(zeros_nh, mlstm_b_f[0]))
            o, (cp, npp, mp), (cs, ns, ms) = run_mixer(
                scan, proj,
                (jnp.zeros((bp, nh, dk, dv), F32), jnp.zeros((bp, nh, dk), F32), jnp.zeros((bp, nh), F32)),
                (state_mlstm_C[0], state_mlstm_n[0], state_mlstm_m[0]))
            outs['mlstm'] = (cp[None], npp[None], mp.reshape(1, bp, nh), cs[None], ns[None], ms.reshape(1, bs, nh))
            w_out = mlstm_w_out[0]
        else:
            nh, dk, dv = state_gla_S.shape[2:]
            rank = gla_w_gate.shape[1]
            q, k, v, g, r = split_cols(gla_w_in[0], [nh * dk, nh * dk, nh * dv, nh * dv, rank])
            w_in = jnp.concatenate([q, k, v, g, _pad_cols(r, LANE)], axis=1).astype(BF16)
            proj = _mm(h, w_in, tm_mm, w_in.shape[1] // 5)
            wg = jnp.pad(gla_w_gate[0].astype(F32), ((0, LANE - rank), (0, 0)))
            scan = functools.partial(_gla_scan, ng=gla_norm_g[0].reshape(1, dv), p1=wg,
                                     p2=gla_b_gate[0].reshape(1, nh * dk), hgrn=False)
            o, (sp,), (ss,) = run_mixer(scan, proj, (jnp.zeros((bp, nh, dk, dv), F32),), (state_gla_S[0],))
            outs['gla'] = (sp[None], ss[None])
            w_out = gla_w_out[0]

        h1t = _mm_res_ln_t(o, w_out.astype(BF16), h, ln_g[i, 0], ln_b[i, 0], TOK_BLOCK)
        ec = PEER_I1_PER_STEP * PEER_NKEYS
        vt = peer_v[i].astype(BF16).reshape(-1, ec, d).transpose(0, 2, 1)
        h = _peer(h1t, peer_w_q[i].T.astype(BF16), peer_keys1[i], peer_keys2[i], peer_u[i].astype(BF16), vt,
                  ln_g[i, 1], ln_b[i, 1], PEER_TOK_BLOCK)

    y_prompt = h[:np_rows].reshape(bp, tp_pad, d)[:, N_META:tp]
    y_sample = h[np_rows:n_real].reshape(bs, seq_s, d)
    hg, gd, ml, gl = outs['hgrn'], outs['gdn'], outs['mlstm'], outs['gla']
    return (y_prompt, y_sample, hg[0], gd[0], gd[1], ml[0], ml[1], ml[2], gl[0],
            hg[1], gd[2], gd[3], ml[3], ml[4], ml[5], gl[1])
```

```python
import functools

import jax
import jax.numpy as jnp
from jax import lax
from jax.experimental import pallas as pl
from jax.experimental.pallas import tpu as pltpu

F32 = jnp.float32
BF16 = jnp.bfloat16
HI = lax.Precision.HIGHEST

D_MODEL = 1024
DEPTH = 4
N_META = 16
C = 16
CONV_W = 4
NORM_EPS = 1e-6
LN_EPS = 1e-5
DN_ALPHA = (2 * DEPTH) ** 0.25
GLA_GATE_NORM = 16.0
LANE = 128
PEER_HEADS = 8
PEER_NKEYS = 128
PEER_TOPK = 16
PEER_HALF = 128
PEER_I1_PER_STEP = 8
TOK_BLOCK = 512
PEER_TOK_BLOCK = 512
PEER_LANE_BLOCK = 256
MM_TOK_BLOCK = 768
TOK_MULTIPLE = 1536
VMEM_LIMIT = 48 * 1024 * 1024
PEER_VMEM_LIMIT = 56 * 1024 * 1024


def _cparams(sem):
    return pltpu.CompilerParams(dimension_semantics=sem, vmem_limit_bytes=VMEM_LIMIT)


def _iota(shape, dim):
    return lax.broadcasted_iota(jnp.int32, shape, dim)


def _dot_nt(a, b, precision=None):
    return lax.dot_general(a, b, (((1,), (1,)), ((), ())), precision=precision, preferred_element_type=F32)


def _dot_tn(a, b, precision=None):
    return lax.dot_general(a, b, (((0,), (0,)), ((), ())), precision=precision, preferred_element_type=F32)


def _dot(a, b, precision=None):
    return jnp.dot(a, b, precision=precision, preferred_element_type=F32)


def _split(x):
    hi = x.astype(BF16)
    return hi, (x - hi.astype(F32)).astype(BF16)


def _dot3(a, b, nt=False):
    (ah, al), (bh, bl) = a, b
    d = _dot_nt if nt else _dot
    return d(ah, bh) + d(ah, bl) + d(al, bh)


def _cumsum_rows(x):
    tri = (_iota((C, C), 0) >= _iota((C, C), 1)).astype(F32)
    return _dot(tri, x, HI)


def _rows_to_cols(x):
    eye = (_iota((LANE, LANE), 0) == _iota((LANE, LANE), 1)).astype(F32)
    return _dot_nt(eye, x, HI)


def _mm_body(x_ref, w_ref, o_ref):
    o_ref[...] = _dot(x_ref[...].astype(BF16), w_ref[...])


def _mm(x, w, tm, tn):
    m, k = x.shape
    n = w.shape[1]
    return pl.pallas_call(
        _mm_body, grid=(m // tm, n // tn),
        in_specs=[pl.BlockSpec((tm, k), lambda i, j: (i, 0)), pl.BlockSpec((k, tn), lambda i, j: (0, j))],
        out_specs=pl.BlockSpec((tm, tn), lambda i, j: (i, j)),
        out_shape=jax.ShapeDtypeStruct((m, n), F32),
        compiler_params=_cparams(("parallel", "parallel")), name="mm")(x, w)


def _mm_ln_body(o_ref, w_ref, h_ref, g_ref, b_ref, out_ref):
    y = _dot(o_ref[...].astype(BF16), w_ref[...])
    z = DN_ALPHA * h_ref[...] + y
    zc = z - jnp.mean(z, -1, keepdims=True)
    var = jnp.mean(zc * zc, -1, keepdims=True)
    out_ref[...] = (zc * lax.rsqrt(var + LN_EPS) * g_ref[...] + b_ref[...]).T


def _mm_res_ln_t(o, w, h, g, b, tm):
    n, k = o.shape
    d = w.shape[1]
    return pl.pallas_call(
        _mm_ln_body, grid=(n // tm,),
        in_specs=[pl.BlockSpec((tm, k), lambda i: (i, 0)), pl.BlockSpec((k, d), lambda i: (0, 0)),
                  pl.BlockSpec((tm, d), lambda i: (i, 0)), pl.BlockSpec((1, d), lambda i: (0, 0)),
                  pl.BlockSpec((1, d), lambda i: (0, 0))],
        out_specs=pl.BlockSpec((d, tm), lambda i: (0, i)),
        out_shape=jax.ShapeDtypeStruct((d, n), F32),
        compiler_params=_cparams(("parallel",)), name="mm_res_ln")(o, w, h, g.reshape(1, d), b.reshape(1, d))


def _valid_rows(t, r0, tb, t_valid, nt):
    if t_valid == nt * tb:
        return None
    return (t * tb + r0 + _iota((C, 1), 0)) < t_valid


def _gla_body(*refs, nseq, hgrn, nh, dk, dv, tb, t_valid, nt):
    proj_refs = refs[:nseq]
    s0_ref, ng_ref, p1_ref, p2_ref, o_ref, sout_ref, st_scr = refs[nseq:]
    t = pl.program_id(1)

    @pl.when(t == 0)
    def _():
        for sq in range(nseq):
            for h in range(nh):
                st_scr[sq, h] = s0_ref[sq, h].T

    kw = nh * dk
    vw = nh * dv
    nr = nh * C
    ri = _iota((nr, 1), 0)
    cj = _iota((1, nr), 1)
    rhead = ri // C
    tri = ((rhead == cj // C) & (ri % C >= cj % C)).astype(F32)
    row3 = _iota((1, C, 1), 1)

    def stack(ref, rows, col0, width):
        return jnp.concatenate([ref[rows, col0 + h * width:col0 + (h + 1) * width] for h in range(nh)], axis=0)

    prep = {}
    for sq, proj_ref in enumerate(proj_refs):
        for ci in range(tb // C):
            rows = slice(ci * C, (ci + 1) * C)
            if hgrn:
                q = jax.nn.silu(stack(proj_ref, rows, 0, dk))
                lb = jnp.concatenate([jnp.broadcast_to(p1_ref[:, h * dk:(h + 1) * dk], (C, dk)) for h in range(nh)],
                                     axis=0)
                fg = lb + (1.0 - lb) * jax.nn.sigmoid(stack(proj_ref, rows, kw, dk))
                k = 1.0 - fg
                g = jnp.log(fg)
            else:
                r = proj_ref[rows, 2 * kw + 2 * vw:2 * kw + 2 * vw + LANE]
                logf_all = jax.nn.log_sigmoid(_dot(r, p1_ref[...], HI) + p2_ref[...]) / GLA_GATE_NORM
                q = stack(proj_ref, rows, 0, dk) * dk ** -0.5
                k = stack(proj_ref, rows, kw, dk)
                g = jnp.concatenate([logf_all[:, h * dk:(h + 1) * dk] for h in range(nh)], axis=0)
            v = stack(proj_ref, rows, 2 * kw, dv)
            if t_valid != nt * tb:
                valid = t * tb + ci * C + ri % C < t_valid
                k = jnp.where(valid, k, 0.0)
                g = jnp.where(valid, g, 0.0)
            b = _dot(tri, g, HI)
            b3, q3, k3, v3 = (x.reshape(nh, C, x.shape[-1]) for x in (b, q, k, v))
            o3 = jnp.zeros((nh, C, dv), F32)
            for s in range(C):
                d = jnp.exp(jnp.where(row3 >= s, b3 - b3[:, s:s + 1, :], -jnp.inf))
                col = jnp.sum(q3 * (k3[:, s:s + 1, :] * d), axis=-1, keepdims=True)
                o3 = o3 + col * v3[:, s:s + 1, :]
            b_end = b3[:, C - 1:C, :]
            prep[sq, ci] = dict(o=o3.reshape(nr, dv), qe=q * jnp.exp(b), kd=(k3 * jnp.exp(b_end - b3)).reshape(nr, dk),
                                dec=jnp.exp(b_end), v=v)

    sts = [st_scr[sq] for sq in range(nseq)]
    for ci in range(tb // C):
        rows = slice(ci * C, (ci + 1) * C)
        for sq in range(nseq):
            p = prep[sq, ci]
            inter = _dot_nt(p['qe'], sts[sq].reshape(nh * dv, dk))
            o = p['o'] + _tree(jnp.add, [jnp.where(rhead == h, inter[:, h * dv:(h + 1) * dv], 0.0)
                                         for h in range(nh)])
            v_wide = jnp.concatenate([jnp.where(rhead == h, p['v'], 0.0) for h in range(nh)], axis=1)
            sts[sq] = sts[sq] * p['dec'] + _dot_tn(v_wide, p['kd']).reshape(nh, dv, dk)
            o = o * lax.rsqrt(jnp.mean(o * o, -1, keepdims=True) + NORM_EPS) * ng_ref[...]
            for h in range(nh):
                gate = proj_refs[sq][rows, 2 * kw + vw + h * dv:2 * kw + vw + (h + 1) * dv]
                o_ref[sq, rows, h * dv:(h + 1) * dv] = o[h * C:(h + 1) * C, :] * jax.nn.silu(gate)
    for sq in range(nseq):
        st_scr[sq] = sts[sq]

    @pl.when(t == nt - 1)
    def _():
        for sq in range(nseq):
            for h in range(nh):
                sout_ref[sq, h] = st_scr[sq, h].T


def _seqs_per_step(nb, chunks):
    return max(n for n in (8, 4, 2, 1) if nb % n == 0 and n * chunks <= 8)


def _proj_specs(nseq, nt, tb, wp):
    return [pl.BlockSpec((tb, wp), functools.partial(lambda b, t, sq: ((b * nseq + sq) * nt + t, 0), sq=sq))
            for sq in range(nseq)]


def _gla_scan(proj, s0, ng, p1, p2, *, hgrn, nb, nt, tb, t_valid):
    _, nh, dk, dv = s0.shape
    wp = proj.shape[1]
    nseq = _seqs_per_step(nb, tb // C)
    body = functools.partial(_gla_body, nseq=nseq, hgrn=hgrn, nh=nh, dk=dk, dv=dv, tb=tb, t_valid=t_valid, nt=nt)
    const = lambda b, t: (0, 0)
    o, s = pl.pallas_call(
        body, grid=(nb // nseq, nt),
        in_specs=_proj_specs(nseq, nt, tb, wp) + [
            pl.BlockSpec((nseq, nh, dk, dv), lambda b, t: (b, 0, 0, 0)),
            pl.BlockSpec(ng.shape, const), pl.BlockSpec(p1.shape, const), pl.BlockSpec(p2.shape, const)],
        out_specs=[pl.BlockSpec((nseq, tb, nh * dv), lambda b, t: (b, t, 0)),
                   pl.BlockSpec((nseq, nh, dk, dv), lambda b, t: (b, 0, 0, 0))],
        out_shape=[jax.ShapeDtypeStruct((nb, nt * tb, nh * dv), F32), jax.ShapeDtypeStruct(s0.shape, F32)],
        scratch_shapes=[pltpu.VMEM((nseq, nh, dv, dk), F32)],
        compiler_params=_cparams(("parallel", "arbitrary")), name="hgrn_scan" if hgrn else "gla_scan",
    )(*([proj] * nseq), s0, ng, p1, p2)
    return o.reshape(nb * nt * tb, nh * dv), s


def _gdn_body(*refs, nseq, nh, dk, dv, tb, t_valid, nt):
    proj_refs = refs[:nseq]
    (s0_ref, conv0_ref, cw_ref, ng_ref, alog_ref, dtb_ref, o_ref, sout_ref, convout_ref,
     st_scr, cbuf, ybuf) = refs[nseq:]
    t = pl.program_id(1)
    cwid = 2 * nh * dk + nh * dv
    pre = 8 - (CONV_W - 1)

    @pl.when(t == 0)
    def _():
        for sq in range(nseq):
            for h in range(nh):
                st_scr[sq, :, h * dv:(h + 1) * dv] = s0_ref[sq, h]
            cbuf[sq, 0:8, :] = jnp.zeros((8, cwid), F32)
            cbuf[sq, pre:8, :] = conv0_ref[sq]

    for sq in range(nseq):
        cbuf[sq, 8:8 + tb, :] = proj_refs[sq][:, 0:cwid]
        y = cbuf[sq, pre:pre + tb, :] * cw_ref[0:1, :]
        for j in range(1, CONV_W):
            y = y + cbuf[sq, pre + j:pre + j + tb, :] * cw_ref[j:j + 1, :]
        ybuf[sq] = jax.nn.silu(y)

    nr = nh * C
    ri = _iota((nr, 1), 0)
    cj = _iota((1, nr), 1)
    rhead = ri // C
    same = rhead == cj // C
    incl = same & (ri % C >= cj % C)
    strict = same & (ri % C > cj % C)
    eye = (ri == cj).astype(F32)

    def stack(ref, rows, base, width):
        return jnp.concatenate([ref[rows, base + h * width:base + (h + 1) * width] for h in range(nh)], axis=0)

    def own_block(x):
        return _tree(jnp.add, [jnp.where(rhead == h, x[:, h * dv:(h + 1) * dv], 0.0) for h in range(nh)])

    prep = []
    for sq, ci in [(sq, ci) for sq in range(nseq) for ci in range(tb // C)]:
        rows = slice(ci * C, (ci + 1) * C)
        valid = _valid_rows(t, ci * C, tb, t_valid, nt)
        small = proj_refs[sq][rows, cwid + nh * dv:cwid + nh * dv + LANE]
        g_all = -jnp.exp(alog_ref[...]) * jax.nn.softplus(small + dtb_ref[...])
        beta_all = jax.nn.sigmoid(small)
        if valid is not None:
            g_all = jnp.where(valid, g_all, 0.0)
            beta_all = jnp.where(valid, beta_all, 0.0)
        b_all = _cumsum_rows(g_all)
        b_t = _rows_to_cols(b_all)
        bcol = jnp.concatenate([b_all[:, h:h + 1] for h in range(nh)], axis=0)
        brow = jnp.concatenate([b_t[h:h + 1, :] for h in range(nh)], axis=1)
        beta = jnp.concatenate([beta_all[:, nh + h:nh + h + 1] for h in range(nh)], axis=0)
        bend = jnp.concatenate([jnp.broadcast_to(b_all[C - 1:C, h:h + 1], (C, 1)) for h in range(nh)], axis=0)
        dec_row = jnp.concatenate([jnp.broadcast_to(jnp.exp(b_all[C - 1:C, h:h + 1]), (1, dv)) for h in range(nh)],
                                  axis=1)
        q = stack(ybuf.at[sq], rows, 0, dk)
        k = stack(ybuf.at[sq], rows, nh * dk, dk)
        v = stack(ybuf.at[sq], rows, 2 * nh * dk, dv)
        q = q * lax.rsqrt(jnp.sum(q * q, -1, keepdims=True) + NORM_EPS) * dk ** -0.5
        k = k * lax.rsqrt(jnp.sum(k * k, -1, keepdims=True) + NORM_EPS)
        if valid is not None:
            valid_st = jnp.concatenate([valid] * nh, axis=0)
            q = jnp.where(valid_st, q, 0.0)
            k = jnp.where(valid_st, k, 0.0)
            v = jnp.where(valid_st, v, 0.0)
        decay = jnp.exp(jnp.where(incl, bcol - brow, -jnp.inf))
        eb = jnp.exp(bcol)
        ks = _split(k)
        prep.append(dict(a=jnp.where(strict, _dot3(ks, ks, nt=True) * decay * beta, 0.0), bv=beta * v,
                         bk=(beta * eb) * k, att=_dot_nt(q, k) * decay, qe=q * eb, kd=k * jnp.exp(bend - bcol),
                         dec_row=dec_row, sq=sq, ci=ci))
    for p in prep:
        a_s = _split(p['a'])
        p['a2'] = _dot3(a_s, a_s)
    for p in prep:
        a2_s = _split(p['a2'])
        p['a4'] = _dot3(a2_s, a2_s)
        p['lo'] = _dot3(_split(eye - p['a']), _split(eye + p['a2']))
    for p in prep:
        a4_s = _split(p['a4'])
        p['a8'] = _dot3(a4_s, a4_s)
    for p in prep:
        p['hi'] = _dot3(_split(eye + p['a4']), _split(eye + p['a8']))
    for p in prep:
        p['tinv'] = _split(_dot3(_split(p['lo']), _split(p['hi'])))
    for p in prep:
        p['u'] = _dot3(p['tinv'], _split(p['bv']))
        p['wk'] = _dot3(p['tinv'], _split(p['bk']))

    s_cat = [st_scr[sq] for sq in range(nseq)]
    for p in sorted(prep, key=lambda p: (p['ci'], p['sq'])):
        sq = p['sq']
        rows = slice(p['ci'] * C, (p['ci'] + 1) * C)
        v_new = p['u'] - own_block(_dot(p['wk'], s_cat[sq]))
        o = _dot(p['att'], v_new) + own_block(_dot(p['qe'], s_cat[sq]))
        v_wide = jnp.concatenate([jnp.where(rhead == h, v_new, 0.0) for h in range(nh)], axis=1)
        s_cat[sq] = s_cat[sq] * p['dec_row'] + _dot_tn(p['kd'], v_wide)
        o = o * lax.rsqrt(jnp.mean(o * o, -1, keepdims=True) + NORM_EPS) * ng_ref[...]
        for h in range(nh):
            gate = proj_refs[sq][rows, cwid + h * dv:cwid + (h + 1) * dv]
            o_ref[sq, rows, h * dv:(h + 1) * dv] = o[h * C:(h + 1) * C, :] * jax.nn.silu(gate)
    for sq in range(nseq):
        st_scr[sq] = s_cat[sq]

    @pl.when(t < nt - 1)
    def _():
        for sq in range(nseq):
            cbuf[sq, pre:8, :] = cbuf[sq, pre + tb:8 + tb, :]

    @pl.when(t == nt - 1)
    def _():
        last = t_valid - (nt - 1) * tb
        for sq in range(nseq):
            for h in range(nh):
                sout_ref[sq, h] = st_scr[sq, :, h * dv:(h + 1) * dv]
            convout_ref[sq] = cbuf[sq, pre + last:8 + last, :]


def _gdn_scan(proj, s0, conv0, cw, ng, alog, dtb, *, nb, nt, tb, t_valid):
    _, nh, dk, dv = s0.shape
    wp = proj.shape[1]
    cwid = 2 * nh * dk + nh * dv
    nseq = _seqs_per_step(nb, tb // C)
    body = functools.partial(_gdn_body, nseq=nseq, nh=nh, dk=dk, dv=dv, tb=tb, t_valid=t_valid, nt=nt)
    const = lambda b, t: (0, 0)
    o, s, conv = pl.pallas_call(
        body, grid=(nb // nseq, nt),
        in_specs=_proj_specs(nseq, nt, tb, wp) + [
            pl.BlockSpec((nseq, nh, dk, dv), lambda b, t: (b, 0, 0, 0)),
            pl.BlockSpec((nseq, CONV_W - 1, cwid), lambda b, t: (b, 0, 0)),
            pl.BlockSpec(cw.shape, const), pl.BlockSpec(ng.shape, const),
            pl.BlockSpec(alog.shape, const), pl.BlockSpec(dtb.shape, const)],
        out_specs=[pl.BlockSpec((nseq, tb, nh * dv), lambda b, t: (b, t, 0)),
                   pl.BlockSpec((nseq, nh, dk, dv), lambda b, t: (b, 0, 0, 0)),
                   pl.BlockSpec((nseq, CONV_W - 1, cwid), lambda b, t: (b, 0, 0))],
        out_shape=[jax.ShapeDtypeStruct((nb, nt * tb, nh * dv), F32), jax.ShapeDtypeStruct(s0.shape, F32),
                   jax.ShapeDtypeStruct(conv0.shape, F32)],
        scratch_shapes=[pltpu.VMEM((nseq, dk, nh * dv), F32), pltpu.VMEM((nseq, tb + 8, cwid), F32),
                        pltpu.VMEM((nseq, tb, cwid), F32)],
        compiler_params=_cparams(("parallel", "arbitrary")), name="gdn_scan",
    )(*([proj] * nseq), s0, conv0, cw, ng, alog, dtb)
    return o.reshape(nb * nt * tb, nh * dv), s, conv


def _mlstm_body(*refs, nseq, nh, dk, dv, tb, t_valid, nt):
    proj_refs = refs[:nseq]
    (c0_ref, n0_ref, m0_ref, ng_ref, bi_ref, bf_ref, o_ref, cout_ref, nout_ref, mout_ref,
     ct_scr, n_scr, m_scr) = refs[nseq:]
    t = pl.program_id(1)

    @pl.when(t == 0)
    def _():
        for sq in range(nseq):
            for h in range(nh):
                ct_scr[sq, :, h * dv:(h + 1) * dv] = c0_ref[sq, h]
                n_scr[sq, h] = n0_ref[sq, h:h + 1, :]
                m_scr[sq, h] = jnp.broadcast_to(m0_ref[sq, :, h:h + 1], (1, LANE))

    base = 2 * nh * dk + 2 * nh * dv
    nr = nh * C
    ri = _iota((nr, 1), 0)
    cj = _iota((1, nr), 1)
    rhead = ri // C
    incl = (rhead == cj // C) & (ri % C >= cj % C)

    def stack(ref, rows, col0, width):
        return jnp.concatenate([ref[rows, col0 + h * width:col0 + (h + 1) * width] for h in range(nh)], axis=0)

    def per_head_col(vals):
        return jnp.concatenate([jnp.broadcast_to(x, (C, 1)) for x in vals], axis=0)

    prep = {}
    for sq, ci in [(sq, ci) for sq in range(nseq) for ci in range(tb // C)]:
        proj_ref = proj_refs[sq]
        rows = slice(ci * C, (ci + 1) * C)
        valid = _valid_rows(t, ci * C, tb, t_valid, nt)
        small = proj_ref[rows, base:base + LANE]
        li_all = small + bi_ref[...]
        lf_all = jax.nn.log_sigmoid(small + bf_ref[...])
        if valid is not None:
            lf_all = jnp.where(valid, lf_all, 0.0)
        b_all = _cumsum_rows(lf_all)
        b_t = _rows_to_cols(b_all)
        li_t = _rows_to_cols(li_all)
        bcol = jnp.concatenate([b_all[:, nh + h:nh + h + 1] for h in range(nh)], axis=0)
        licol = jnp.concatenate([li_all[:, h:h + 1] for h in range(nh)], axis=0)
        xrow = jnp.concatenate([b_t[nh + h:nh + h + 1, :] - li_t[h:h + 1, :] for h in range(nh)], axis=1)
        if valid is not None:
            xrow = jnp.where(t * tb + ci * C + cj % C < t_valid, xrow, jnp.inf)
            licol = jnp.where(t * tb + ci * C + ri % C < t_valid, licol, -jnp.inf)
        q = stack(proj_ref, rows, 0, dk)
        k = stack(proj_ref, rows, nh * dk, dk) * dk ** -0.5
        v = stack(proj_ref, rows, 2 * nh * dk, dv)
        dmat = jnp.where(incl, bcol - xrow, -jnp.inf)
        prep[sq, ci] = dict(q=q, k=k, v=v, bcol=bcol, licol=licol, dmat=dmat, qk=_dot_nt(q, k),
                            rowmax=jnp.max(dmat, axis=1, keepdims=True),
                            b_end=[b_all[C - 1:C, nh + h:nh + h + 1] for h in range(nh)])

    c_cat = [ct_scr[sq] for sq in range(nseq)]
    n_rows = [[n_scr[sq, h] for h in range(nh)] for sq in range(nseq)]
    m_vals = [[m_scr[sq, h][:, 0:1] for h in range(nh)] for sq in range(nseq)]
    for ci, sq in [(ci, sq) for ci in range(tb // C) for sq in range(nseq)]:
        p = prep[sq, ci]
        rows = slice(ci * C, (ci + 1) * C)
        q, k, v, bcol = p['q'], p['k'], p['v'], p['bcol']
        m_inter = bcol + per_head_col(m_vals[sq])
        m_t = jnp.maximum(m_inter, p['rowmax'])
        a_int = jnp.exp(m_inter - m_t)
        qk = p['qk'] * jnp.exp(p['dmat'] - m_t)
        qc = _dot(q, c_cat[sq])
        own = _tree(jnp.add, [jnp.where(rhead == h, qc[:, h * dv:(h + 1) * dv], 0.0) for h in range(nh)])
        num = _dot(qk, v) + a_int * own
        n_st = jnp.concatenate([jnp.broadcast_to(n_rows[sq][h], (C, dk)) for h in range(nh)], axis=0)
        den = jnp.sum(qk, axis=1, keepdims=True) + a_int * jnp.sum(q * n_st, axis=1, keepdims=True)
        hid = num / jnp.maximum(jnp.abs(den), jnp.exp(-m_t))
        last = [slice(h * C + C - 1, h * C + C) for h in range(nh)]
        m_vals[sq] = [m_t[r, :] for r in last]
        a_end = [a_int[r, :] for r in last]
        wk = jnp.exp(per_head_col(p['b_end']) - bcol + p['licol'] - per_head_col(m_vals[sq])) * k
        v_wide = jnp.concatenate([jnp.where(rhead == h, v, 0.0) for h in range(nh)], axis=1)
        a_row = jnp.concatenate([jnp.broadcast_to(a, (1, dv)) for a in a_end], axis=1)
        c_cat[sq] = c_cat[sq] * a_row + _dot_tn(wk, v_wide)
        n_rows[sq] = [a_end[h] * n_rows[sq][h] + jnp.sum(wk[h * C:(h + 1) * C, :], axis=0, keepdims=True)
                      for h in range(nh)]
        hc = hid - jnp.mean(hid, -1, keepdims=True)
        hn = hc * lax.rsqrt(jnp.mean(hc * hc, -1, keepdims=True) + NORM_EPS)
        for h in range(nh):
            og = jax.nn.sigmoid(
                proj_refs[sq][rows, 2 * nh * dk + nh * dv + h * dv:2 * nh * dk + nh * dv + (h + 1) * dv])
            o_ref[sq, rows, h * dv:(h + 1) * dv] = og * (hn[h * C:(h + 1) * C, :] * ng_ref[:, h * dv:(h + 1) * dv])
    for sq in range(nseq):
        ct_scr[sq] = c_cat[sq]
        for h in range(nh):
            n_scr[sq, h] = n_rows[sq][h]
            m_scr[sq, h] = jnp.broadcast_to(m_vals[sq][h], (1, LANE))

    @pl.when(t == nt - 1)
    def _():
        lane = _iota((1, nh), 1)
        for sq in range(nseq):
            mrow = jnp.zeros((1, nh), F32)
            for h in range(nh):
                cout_ref[sq, h] = ct_scr[sq, :, h * dv:(h + 1) * dv]
                nout_ref[sq, h:h + 1, :] = n_scr[sq, h]
                mrow = jnp.where(lane == h, m_scr[sq, h][:, 0:1], mrow)
            mout_ref[sq] = mrow


def _mlstm_scan(proj, c0, n0, m0, ng, bi, bf, *, nb, nt, tb, t_valid):
    _, nh, dk, dv = c0.shape
    wp = proj.shape[1]
    nseq = _seqs_per_step(nb, tb // C)
    body = functools.partial(_mlstm_body, nseq=nseq, nh=nh, dk=dk, dv=dv, tb=tb, t_valid=t_valid, nt=nt)
    const = lambda b, t: (0, 0)
    m0 = m0.reshape(nb, 1, nh)
    o, c, n, m = pl.pallas_call(
        body, grid=(nb // nseq, nt),
        in_specs=_proj_specs(nseq, nt, tb, wp) + [
            pl.BlockSpec((nseq, nh, dk, dv), lambda b, t: (b, 0, 0, 0)),
            pl.BlockSpec((nseq, nh, dk), lambda b, t: (b, 0, 0)),
            pl.BlockSpec((nseq, 1, nh), lambda b, t: (b, 0, 0)),
            pl.BlockSpec(ng.shape, const), pl.BlockSpec(bi.shape, const), pl.BlockSpec(bf.shape, const)],
        out_specs=[pl.BlockSpec((nseq, tb, nh * dv), lambda b, t: (b, t, 0)),
                   pl.BlockSpec((nseq, nh, dk, dv), lambda b, t: (b, 0, 0, 0)),
                   pl.BlockSpec((nseq, nh, dk), lambda b, t: (b, 0, 0)),
                   pl.BlockSpec((nseq, 1, nh), lambda b, t: (b, 0, 0))],
        out_shape=[jax.ShapeDtypeStruct((nb, nt * tb, nh * dv), F32), jax.ShapeDtypeStruct(c0.shape, F32),
                   jax.ShapeDtypeStruct(n0.shape, F32), jax.ShapeDtypeStruct((nb, 1, nh), F32)],
        scratch_shapes=[pltpu.VMEM((nseq, dk, nh * dv), F32), pltpu.VMEM((nseq, nh, 1, dk), F32),
                        pltpu.VMEM((nseq, nh, 1, LANE), F32)],
        compiler_params=_cparams(("parallel", "arbitrary")), name="mlstm_scan",
    )(*([proj] * nseq), c0, n0, m0, ng, bi, bf)
    return o.reshape(nb * nt * tb, nh * dv), c, n, m


SUB = 8


def _tree(op, xs):
    xs = list(xs)
    while len(xs) > 1:
        xs = [op(xs[i], xs[i + 1]) if i + 1 < len(xs) else xs[i] for i in range(0, len(xs), 2)]
    return xs[0]


def _all_sublanes(op, m):
    for shift in (4, 2, 1):
        m = op(m, pltpu.roll(m, shift, axis=0))
    return m


def _sorting_network(n):
    def merge(lo, hi, r):
        step = r * 2
        if step < hi - lo:
            yield from merge(lo, hi, step)
            yield from merge(lo + r, hi, step)
            yield from [(i, i + r) for i in range(lo + r, hi - r, step)]
        else:
            yield (lo, lo + r)

    def sort(lo, hi):
        if hi - lo >= 1:
            mid = lo + (hi - lo) // 2
            yield from sort(lo, mid)
            yield from sort(mid + 1, hi)
            yield from merge(lo, hi, 1)

    return list(sort(0, n - 1))


def _sort_tiles_desc(tiles):
    tiles = list(tiles)
    for i, j in _sorting_network(len(tiles)):
        tiles[i], tiles[j] = jnp.maximum(tiles[i], tiles[j]), jnp.minimum(tiles[i], tiles[j])
    return tiles


def _merge_top(lists, singles=None):
    lists = list(lists)
    vals = []
    for j in range(PEER_TOPK):
        head = lists[0] if singles is None else jnp.maximum(lists[0], singles)
        m = _all_sublanes(jnp.maximum, head)
        vals.append(m)
        if j == PEER_TOPK - 1:
            break
        hit = lists[0] == m
        for i in range(min(len(lists) - 1, PEER_TOPK - 1 - j)):
            lists[i] = jnp.where(hit, lists[i + 1], lists[i])
        if singles is not None:
            singles = jnp.where(singles == m, -jnp.inf, singles)
    return vals


def _route_tile(q_scr, k1_ref, k2_ref, r2_ref, c1_ref, a1_ref, a2_ref, h, lanes):
    ntile = PEER_NKEYS // SUB
    sub = _iota((SUB, LANE), 0)
    q1 = q_scr[pl.ds(pl.multiple_of(h * 2 * PEER_HALF, PEER_HALF), PEER_HALF), lanes]
    q2 = q_scr[pl.ds(pl.multiple_of(h * 2 * PEER_HALF + PEER_HALF, PEER_HALF), PEER_HALF), lanes]
    s1 = _dot(k1_ref[...], q1, HI)
    s2 = _dot(k2_ref[...], q2, HI)
    t1 = [s1[k * SUB:(k + 1) * SUB, :] for k in range(ntile)]
    t2 = [s2[k * SUB:(k + 1) * SUB, :] for k in range(ntile)]
    v1 = _merge_top(_sort_tiles_desc(t1))
    v2 = _merge_top(_sort_tiles_desc(t2))
    v2lo, v2hi = v2[SUB - 1], v2[2 * SUB - 1]
    for b in range(SUB - 2, -1, -1):
        v2lo = jnp.where(sub == b, v2[b], v2lo)
        v2hi = jnp.where(sub == b, v2[SUB + b], v2hi)
    cand_lo = [jnp.where(sub < min(PEER_TOPK // (a + 1), SUB), v1[a] + v2lo, -jnp.inf) for a in range(PEER_TOPK)]
    cand_hi = v1[0] + v2hi
    tau = _merge_top(cand_lo, cand_hi)[PEER_TOPK - 1]
    top = v1[0] + v2[0]
    z = _all_sublanes(jnp.add, _tree(jnp.add, [jnp.where(cd >= tau, jnp.exp(cd - top), 0.0)
                                               for cd in cand_lo + [cand_hi]]))
    inv_z = 1.0 / z
    pairs = [_all_sublanes(jnp.add, jnp.where(cd >= tau, 1.0, 0.0)) for cd in cand_lo]
    pairs[0] = pairs[0] + _all_sublanes(jnp.add, jnp.where(cand_hi >= tau, 1.0, 0.0))
    rank2, c1 = [], []
    for k in range(ntile):
        r = jnp.full((SUB, LANE), float(PEER_TOPK), F32)
        cnt = jnp.zeros((SUB, LANE), F32)
        for j in range(PEER_TOPK - 1, -1, -1):
            r = jnp.where(t2[k] >= v2[j], float(j), r)
            cnt = jnp.where(t1[k] == v1[j], pairs[j], cnt)
        rank2.append(r)
        c1.append(cnt)
    r2_ref[h, :, lanes] = jnp.concatenate(rank2, axis=0).astype(BF16)
    c1_ref[h, :, lanes] = jnp.concatenate(c1, axis=0)
    a1_ref[h, :, lanes] = jnp.concatenate([jnp.exp(t - v1[0]) for t in t1], axis=0)
    a2_ref[h, :, lanes] = jnp.concatenate([jnp.exp(t - v2[0]) * inv_z for t in t2], axis=0).astype(BF16)


def _peer_body(xm_ref, xr_ref, wq_ref, k1_ref, k2_ref, u_ref, vt_ref, g_ref, b_ref, out_ref,
               acc, xbf, p_scr, q_scr, r2_scr, c1_scr, a1_scr, a2_scr):
    r = pl.program_id(0)
    c = pl.program_id(1)
    tm = xbf.shape[1]

    @pl.when(c == 0)
    def _():
        xbf[...] = xm_ref[...].astype(BF16)
        acc[...] = jnp.zeros(acc.shape, F32)
        p_scr[...] = jnp.zeros(p_scr.shape, BF16)
        q_scr[...] = _dot(wq_ref[...], xr_ref[...].astype(BF16))

    @pl.when((c == 0) & (r == 0))
    def _():
        for scr in (r2_scr, c1_scr, a1_scr, a2_scr):
            scr[...] = jnp.zeros(scr.shape, scr.dtype)

    ntile = tm // LANE
    units = pl.num_programs(1) - 1
    tiles_per_unit = PEER_HEADS * ntile // units
    unit = (c + units - 1) % units
    h_route = unit // (ntile // tiles_per_unit)
    lane0 = (unit % (ntile // tiles_per_unit)) * tiles_per_unit * LANE
    wr = r % 2
    for sb in range(tiles_per_unit):
        _route_tile(q_scr, k1_ref, k2_ref, r2_scr.at[wr], c1_scr.at[wr], a1_scr.at[wr], a2_scr.at[wr], h_route,
                    pl.ds(pl.multiple_of(lane0 + sb * LANE, LANE), LANE))

    rd = (r + 1) % 2
    i1_base = jnp.minimum(c, pl.num_programs(1) - 2) * PEER_I1_PER_STEP
    rows16 = 2 * SUB
    acc[...] += _dot(vt_ref[0], p_scr[...])
    hid = _dot(u_ref[...], xbf[...])
    act = (0.5 * hid * (1.0 + lax.erf(hid * 0.5 ** 0.5))).astype(BF16)
    for l0 in range(0, tm, PEER_LANE_BLOCK):
        ls = slice(l0, l0 + PEER_LANE_BLOCK)
        for j in range(PEER_I1_PER_STEP):
            row = pl.ds(i1_base + j, 1)
            nkb = PEER_NKEYS // rows16
            w = [None] * nkb
            for h in range(PEER_HEADS):
                c1b = jnp.broadcast_to(c1_scr[rd, h, row, ls], (rows16, PEER_LANE_BLOCK)).astype(BF16)
                a1b = jnp.broadcast_to(a1_scr[rd, h, row, ls], (rows16, PEER_LANE_BLOCK)).astype(BF16)
                for kb in range(nkb):
                    ks = slice(kb * rows16, (kb + 1) * rows16)
                    term = jnp.where(r2_scr[rd, h, ks, ls] < c1b, a2_scr[rd, h, ks, ls] * a1b, 0.0)
                    w[kb] = term if w[kb] is None else w[kb] + term
            for kb in range(nkb):
                es = slice(j * PEER_NKEYS + kb * rows16, j * PEER_NKEYS + (kb + 1) * rows16)
                p_scr[es, ls] = w[kb] * act[es, ls]

    @pl.when(c == pl.num_programs(1) - 1)
    def _():
        z = DN_ALPHA * xm_ref[...] + acc[...]
        zc = z - jnp.mean(z, 0, keepdims=True)
        var = jnp.mean(zc * zc, 0, keepdims=True)
        out_ref[...] = (zc * lax.rsqrt(var + LN_EPS) * g_ref[...] + b_ref[...]).T


def _peer(xt, wq_t, k1, k2, u, vt, g, b, tm):
    d, n = xt.shape
    nb = n // tm
    ec = PEER_I1_PER_STEP * PEER_NKEYS
    nc = u.shape[0] // ec
    ntile = tm // LANE
    assert (PEER_HEADS * ntile) % nc == 0 and ntile % (PEER_HEADS * ntile // nc) == 0, "routing units per row"
    const = lambda r, c: (0, 0)
    rshape = (2, PEER_HEADS, PEER_NKEYS, tm)
    return pl.pallas_call(
        _peer_body, grid=(nb + 1, nc + 1),
        in_specs=[pl.BlockSpec((d, tm), lambda r, c: (0, jnp.maximum(r - 1, 0))),
                  pl.BlockSpec((d, tm), lambda r, c: (0, jnp.minimum(r, nb - 1))),
                  pl.BlockSpec(wq_t.shape, const), pl.BlockSpec(k1.shape, const), pl.BlockSpec(k2.shape, const),
                  pl.BlockSpec((ec, d), lambda r, c: (jnp.minimum(c, nc - 1), 0)),
                  pl.BlockSpec((1, d, ec), lambda r, c: (jnp.maximum(c - 1, 0), 0, 0)),
                  pl.BlockSpec((d, 1), const), pl.BlockSpec((d, 1), const)],
        out_specs=pl.BlockSpec((tm, d), lambda r, c: (jnp.maximum(r - 1, 0), 0)),
        out_shape=jax.ShapeDtypeStruct((n, d), F32),
        scratch_shapes=[pltpu.VMEM((d, tm), F32), pltpu.VMEM((d, tm), BF16), pltpu.VMEM((ec, tm), BF16),
                        pltpu.VMEM((wq_t.shape[0], tm), F32), pltpu.VMEM(rshape, BF16), pltpu.VMEM(rshape, F32),
                        pltpu.VMEM(rshape, F32), pltpu.VMEM(rshape, BF16)],
        compiler_params=pltpu.CompilerParams(dimension_semantics=("arbitrary", "arbitrary"),
                                             vmem_limit_bytes=PEER_VMEM_LIMIT), name="peer",
    )(xt, xt, wq_t, k1, k2, u, vt, g.reshape(d, 1), b.reshape(d, 1))


def _pad_cols(w, width):
    return jnp.pad(w, ((0, 0), (0, width - w.shape[1])))


def _lane_row(*parts):
    v = jnp.concatenate([p.astype(F32).reshape(-1) for p in parts])
    return jnp.pad(v, (0, LANE - v.shape[0])).reshape(1, LANE)


def _pick_tb(t_pad):
    for tb in (48, 32, 16):
        if t_pad % tb == 0:
            return tb
    raise ValueError(t_pad)


def kernel(x_prompt, x_sample, state_hgrn_S, state_gdn_S, state_gdn_conv, state_mlstm_C, state_mlstm_n, state_mlstm_m, state_gla_S, meta_tokens, hgrn_w_in, hgrn_lb, hgrn_norm_g, hgrn_w_out, gdn_w_in, gdn_conv_w, gdn_a_log, gdn_dt_bias, gdn_norm_g, gdn_w_out, mlstm_w_in, mlstm_b_i, mlstm_b_f, mlstm_norm_g, mlstm_w_out, gla_w_in, gla_w_gate, gla_b_gate, gla_norm_g, gla_w_out, peer_w_q, peer_keys1, peer_keys2, peer_u, peer_v, ln_g, ln_b):
    bp, seq, d = x_prompt.shape
    bs, seq_s, _ = x_sample.shape
    assert d == D_MODEL and len(state_hgrn_S) == 1 and len(state_gdn_S) == 1
    assert len(state_mlstm_C) == 1 and len(state_gla_S) == 1
    tp = N_META + seq
    tp_pad = -(-tp // C) * C
    ts_pad = -(-seq_s // C) * C
    tbp = _pick_tb(tp_pad)
    tbs = _pick_tb(ts_pad)
    np_rows = bp * tp_pad
    n_real = np_rows + bs * seq_s

    meta = jnp.broadcast_to(meta_tokens.astype(F32)[None], (bp, N_META, d))
    hp = jnp.concatenate([meta, x_prompt], axis=1)
    hp = jnp.pad(hp, ((0, 0), (0, tp_pad - tp), (0, 0))).reshape(np_rows, d)
    h = jnp.concatenate([hp, x_sample.reshape(bs * seq_s, d)], axis=0)
    n = -(-n_real // TOK_MULTIPLE) * TOK_MULTIPLE
    h = jnp.pad(h, ((0, n - n_real), (0, 0)))
    tm_mm = MM_TOK_BLOCK

    def split_cols(w, sizes):
        out, o = [], 0
        for s in sizes:
            out.append(w[:, o:o + s])
            o += s
        return out

    def run_mixer(scan, proj, states_p, states_s, **kw):
        res_p = scan(proj, *states_p, nb=bp, nt=tp_pad // tbp, tb=tbp, t_valid=tp, **kw)
        ps = proj[np_rows:n_real].reshape(bs, seq_s, -1)
        ps = jnp.pad(ps, ((0, 0), (0, ts_pad - seq_s), (0, 0))).reshape(bs * ts_pad, -1)
        res_s = scan(ps, *states_s, nb=bs, nt=ts_pad // tbs, tb=tbs, t_valid=seq_s, **kw)
        o_s = res_s[0].reshape(bs, ts_pad, -1)[:, :seq_s].reshape(bs * seq_s, -1)
        o = jnp.concatenate([res_p[0], o_s], axis=0)
        o = jnp.pad(o, ((0, n - n_real), (0, 0)))
        return o, res_p[1:], res_s[1:]

    lb_all = jnp.cumsum(jax.nn.softmax(hgrn_lb.astype(F32), axis=0), axis=0)
    outs = {}
    for i in range(DEPTH):
        mix = i % 4
        if mix == 0:
            w_in = hgrn_w_in[0].astype(BF16)
            proj = _mm(h, w_in, tm_mm, 1024)
            nh, dk, dv = state_hgrn_S.shape[2:]
            zero = jnp.zeros((bp, nh, dk, dv), F32)
            scan = functools.partial(_gla_scan, ng=hgrn_norm_g[0].reshape(1, dv), p1=lb_all[i].reshape(1, nh * dk),
                                     p2=jnp.zeros((1, LANE), F32), hgrn=True)
            o, (sp,), (ss,) = run_mixer(scan, proj, (zero,), (state_hgrn_S[0],))
            outs['hgrn'] = (sp[None], ss[None])
            w_out = hgrn_w_out[0]
        elif mix == 1:
            nh, dk, dv = state_gdn_S.shape[2:]
            cwid = 2 * nh * dk + nh * dv
            qkv, a, b, g = split_cols(gdn_w_in[0], [cwid, nh, nh, nh * dv])
            w_in = jnp.concatenate([qkv, g, _pad_cols(jnp.concatenate([a, b], axis=1), LANE)], axis=1).astype(BF16)
            proj = _mm(h, w_in, tm_mm, w_in.shape[1] // 3)
            scan = functools.partial(_gdn_scan, cw=gdn_conv_w[0], ng=gdn_norm_g[0].reshape(1, dv),
                                     alog=_lane_row(gdn_a_log[0]), dtb=_lane_row(gdn_dt_bias[0]))
            o, (sp, cp), (ss, cs) = run_mixer(
                scan, proj,
                (jnp.zeros((bp, nh, dk, dv), F32), jnp.zeros((bp, CONV_W - 1, cwid), F32)),
                (state_gdn_S[0], state_gdn_conv[0]))
            outs['gdn'] = (sp[None], cp[None], ss[None], cs[None])
            w_out = gdn_w_out[0]
        elif mix == 2:
            nh, dk, dv = state_mlstm_C.shape[2:]
            q, k, v, ig, fg, og = split_cols(mlstm_w_in[0], [nh * dk, nh * dk, nh * dv, nh, nh, nh * dv])
            w_in = jnp.concatenate([q, k, v, og, _pad_cols(jnp.concatenate([ig, fg], axis=1), LANE)],
                                   axis=1).astype(BF16)
            proj = _mm(h, w_in, tm_mm, w_in.shape[1] // 3)
            zeros_nh = jnp.zeros((nh,), F32)
            scan = functools.partial(_mlstm_scan, ng=mlstm_norm_g[0].reshape(1, nh * dv),
                                     bi=_lane_row(mlstm_b_i[0]), bf=_lane_row(zeros_nh, mlstm_b_f[0]))
            o, (cp, npp, mp), (cs, ns, ms) = run_mixer(
                scan, proj,
                (jnp.zeros((bp, nh, dk, dv), F32), jnp.zeros((bp, nh, dk), F32), jnp.zeros((bp, nh), F32)),
                (state_mlstm_C[0], state_mlstm_n[0], state_mlstm_m[0]))
            outs['mlstm'] = (cp[None], npp[None], mp.reshape(1, bp, nh), cs[None], ns[None], ms.reshape(1, bs, nh))
            w_out = mlstm_w_out[0]
        else:
            nh, dk, dv = state_gla_S.shape[2:]
            rank = gla_w_gate.shape[1]
            q, k, v, g, r = split_cols(gla_w_in[0], [nh * dk, nh * dk, nh * dv, nh * dv, rank])
            w_in = jnp.concatenate([q, k, v, g, _pad_cols(r, LANE)], axis=1).astype(BF16)
            proj = _mm(h, w_in, tm_mm, w_in.shape[1] // 5)
            wg = jnp.pad(gla_w_gate[0].astype(F32), ((0, LANE - rank), (0, 0)))
            scan = functools.partial(_gla_scan, ng=gla_norm_g[0].reshape(1, dv), p1=wg,
                                     p2=gla_b_gate[0].reshape(1, nh * dk), hgrn=False)
            o, (sp,), (ss,) = run_mixer(scan, proj, (jnp.zeros((bp, nh, dk, dv), F32),), (state_gla_S[0],))
            outs['gla'] = (sp[None], ss[None])
            w_out = gla_w_out[0]

        h1t = _mm_res_ln_t(o, w_out.astype(BF16), h, ln_g[i, 0], ln_b[i, 0], TOK_BLOCK)
        ec = PEER_I1_PER_STEP * PEER_NKEYS
        vt = peer_v[i].astype(BF16).reshape(-1, ec, d).transpose(0, 2, 1)
        h = _peer(h1t, peer_w_q[i].T.astype(BF16), peer_keys1[i], peer_keys2[i], peer_u[i].astype(BF16), vt,
                  ln_g[i, 1], ln_b[i, 1], PEER_TOK_BLOCK)

    y_prompt = h[:np_rows].reshape(bp, tp_pad, d)[:, N_META:tp]
    y_sample = h[np_rows:n_real].reshape(bs, seq_s, d)
    hg, gd, ml, gl = outs['hgrn'], outs['gdn'], outs['mlstm'], outs['gla']
    return (y_prompt, y_sample, hg[0], gd[0], gd[1], ml[0], ml[1], ml[2], gl[0],
            hg[1], gd[2], gd[3], ml[3], ml[4], ml[5], gl[1])
```

```python
import functools

import jax
import jax.numpy as jnp
from jax import lax
from jax.experimental import pallas as pl
from jax.experimental.pallas import tpu as pltpu

F32 = jnp.float32
BF16 = jnp.bfloat16
HI = lax.Precision.HIGHEST

D_MODEL = 1024
DEPTH = 4
N_META = 16
C = 16
CONV_W = 4
NORM_EPS = 1e-6
LN_EPS = 1e-5
DN_ALPHA = (2 * DEPTH) ** 0.25
GLA_GATE_NORM = 16.0
LANE = 128
PEER_HEADS = 8
PEER_NKEYS = 128
PEER_TOPK = 16
PEER_HALF = 128
PEER_I1_PER_STEP = 8
TOK_BLOCK = 512
PEER_TOK_BLOCK = 512
PEER_LANE_BLOCK = 256
MM_TOK_BLOCK = 1536
TOK_MULTIPLE = 1536
VMEM_LIMIT = 48 * 1024 * 1024
PEER_VMEM_LIMIT = 56 * 1024 * 1024


def _cparams(sem):
    return pltpu.CompilerParams(dimension_semantics=sem, vmem_limit_bytes=VMEM_LIMIT)


def _iota(shape, dim):
    return lax.broadcasted_iota(jnp.int32, shape, dim)


def _dot_nt(a, b, precision=None):
    return lax.dot_general(a, b, (((1,), (1,)), ((), ())), precision=precision, preferred_element_type=F32)


def _dot_tn(a, b, precision=None):
    return lax.dot_general(a, b, (((0,), (0,)), ((), ())), precision=precision, preferred_element_type=F32)


def _dot(a, b, precision=None):
    return jnp.dot(a, b, precision=precision, preferred_element_type=F32)


def _split(x):
    hi = x.astype(BF16)
    return hi, (x - hi.astype(F32)).astype(BF16)


def _dot3(a, b, nt=False):
    (ah, al), (bh, bl) = a, b
    d = _dot_nt if nt else _dot
    return d(ah, bh) + d(ah, bl) + d(al, bh)


def _cumsum_rows(x):
    tri = (_iota((C, C), 0) >= _iota((C, C), 1)).astype(F32)
    return _dot(tri, x, HI)


def _rows_to_cols(x):
    eye = (_iota((LANE, LANE), 0) == _iota((LANE, LANE), 1)).astype(F32)
    return _dot_nt(eye, x, HI)


def _mm_body(x_ref, w_ref, o_ref):
    o_ref[...] = _dot(x_ref[...].astype(BF16), w_ref[...])


def _mm(x, w, tm, tn):
    m, k = x.shape
    n = w.shape[1]
    return pl.pallas_call(
        _mm_body, grid=(m // tm, n // tn),
        in_specs=[pl.BlockSpec((tm, k), lambda i, j: (i, 0)), pl.BlockSpec((k, tn), lambda i, j: (0, j))],
        out_specs=pl.BlockSpec((tm, tn), lambda i, j: (i, j)),
        out_shape=jax.ShapeDtypeStruct((m, n), F32),
        compiler_params=_cparams(("parallel", "parallel")), name="mm")(x, w)


def _mm_ln_body(o_ref, w_ref, h_ref, g_ref, b_ref, out_ref):
    y = _dot(o_ref[...].astype(BF16), w_ref[...])
    z = DN_ALPHA * h_ref[...] + y
    zc = z - jnp.mean(z, -1, keepdims=True)
    var = jnp.mean(zc * zc, -1, keepdims=True)
    out_ref[...] = (zc * lax.rsqrt(var + LN_EPS) * g_ref[...] + b_ref[...]).T


def _mm_res_ln_t(o, w, h, g, b, tm):
    n, k = o.shape
    d = w.shape[1]
    return pl.pallas_call(
        _mm_ln_body, grid=(n // tm,),
        in_specs=[pl.BlockSpec((tm, k), lambda i: (i, 0)), pl.BlockSpec((k, d), lambda i: (0, 0)),
                  pl.BlockSpec((tm, d), lambda i: (i, 0)), pl.BlockSpec((1, d), lambda i: (0, 0)),
                  pl.BlockSpec((1, d), lambda i: (0, 0))],
        out_specs=pl.BlockSpec((d, tm), lambda i: (0, i)),
        out_shape=jax.ShapeDtypeStruct((d, n), F32),
        compiler_params=_cparams(("parallel",)), name="mm_res_ln")(o, w, h, g.reshape(1, d), b.reshape(1, d))


def _valid_rows(t, r0, tb, t_valid, nt):
    if t_valid == nt * tb:
        return None
    return (t * tb + r0 + _iota((C, 1), 0)) < t_valid


def _gla_body(*refs, nseq, hgrn, nh, dk, dv, tb, t_valid, nt):
    proj_refs = refs[:nseq]
    s0_ref, ng_ref, p1_ref, p2_ref, o_ref, sout_ref, st_scr = refs[nseq:]
    t = pl.program_id(1)

    @pl.when(t == 0)
    def _():
        for sq in range(nseq):
            for h in range(nh):
                st_scr[sq, h] = s0_ref[sq, h].T

    kw = nh * dk
    vw = nh * dv
    nr = nh * C
    ri = _iota((nr, 1), 0)
    cj = _iota((1, nr), 1)
    rhead = ri // C
    tri = ((rhead == cj // C) & (ri % C >= cj % C)).astype(F32)
    row3 = _iota((1, C, 1), 1)

    def stack(ref, rows, col0, width):
        return jnp.concatenate([ref[rows, col0 + h * width:col0 + (h + 1) * width] for h in range(nh)], axis=0)

    prep = {}
    for sq, proj_ref in enumerate(proj_refs):
        for ci in range(tb // C):
            rows = slice(ci * C, (ci + 1) * C)
            if hgrn:
                q = jax.nn.silu(stack(proj_ref, rows, 0, dk))
                lb = jnp.concatenate([jnp.broadcast_to(p1_ref[:, h * dk:(h + 1) * dk], (C, dk)) for h in range(nh)],
                                     axis=0)
                fg = lb + (1.0 - lb) * jax.nn.sigmoid(stack(proj_ref, rows, kw, dk))
                k = 1.0 - fg
                g = jnp.log(fg)
            else:
                r = proj_ref[rows, 2 * kw + 2 * vw:2 * kw + 2 * vw + LANE]
                logf_all = jax.nn.log_sigmoid(_dot(r, p1_ref[...], HI) + p2_ref[...]) / GLA_GATE_NORM
                q = stack(proj_ref, rows, 0, dk) * dk ** -0.5
                k = stack(proj_ref, rows, kw, dk)
                g = jnp.concatenate([logf_all[:, h * dk:(h + 1) * dk] for h in range(nh)], axis=0)
            v = stack(proj_ref, rows, 2 * kw, dv)
            if t_valid != nt * tb:
                valid = t * tb + ci * C + ri % C < t_valid
                k = jnp.where(valid, k, 0.0)
                g = jnp.where(valid, g, 0.0)
            b = _dot(tri, g, HI)
            b3, q3, k3, v3 = (x.reshape(nh, C, x.shape[-1]) for x in (b, q, k, v))
            o3 = jnp.zeros((nh, C, dv), F32)
            for s in range(C):
                d = jnp.exp(jnp.where(row3 >= s, b3 - b3[:, s:s + 1, :], -jnp.inf))
                col = jnp.sum(q3 * (k3[:, s:s + 1, :] * d), axis=-1, keepdims=True)
                o3 = o3 + col * v3[:, s:s + 1, :]
            b_end = b3[:, C - 1:C, :]
            prep[sq, ci] = dict(o=o3.reshape(nr, dv), qe=q * jnp.exp(b), kd=(k3 * jnp.exp(b_end - b3)).reshape(nr, dk),
                                dec=jnp.exp(b_end), v=v)

    sts = [st_scr[sq] for sq in range(nseq)]
    for ci in range(tb // C):
        rows = slice(ci * C, (ci + 1) * C)
        for sq in range(nseq):
            p = prep[sq, ci]
            inter = _dot_nt(p['qe'], sts[sq].reshape(nh * dv, dk))
            o = p['o'] + _tree(jnp.add, [jnp.where(rhead == h, inter[:, h * dv:(h + 1) * dv], 0.0)
                                         for h in range(nh)])
            v_wide = jnp.concatenate([jnp.where(rhead == h, p['v'], 0.0) for h in range(nh)], axis=1)
            sts[sq] = sts[sq] * p['dec'] + _dot_tn(v_wide, p['kd']).reshape(nh, dv, dk)
            o = o * lax.rsqrt(jnp.mean(o * o, -1, keepdims=True) + NORM_EPS) * ng_ref[...]
            for h in range(nh):
                gate = proj_refs[sq][rows, 2 * kw + vw + h * dv:2 * kw + vw + (h + 1) * dv]
                o_ref[sq, rows, h * dv:(h + 1) * dv] = o[h * C:(h + 1) * C, :] * jax.nn.silu(gate)
    for sq in range(nseq):
        st_scr[sq] = sts[sq]

    @pl.when(t == nt - 1)
    def _():
        for sq in range(nseq):
            for h in range(nh):
                sout_ref[sq, h] = st_scr[sq, h].T


def _seqs_per_step(nb, chunks):
    return max(n for n in (8, 4, 2, 1) if nb % n == 0 and n * chunks <= 8)


def _proj_specs(nseq, nt, tb, wp):
    return [pl.BlockSpec((tb, wp), functools.partial(lambda b, t, sq: ((b * nseq + sq) * nt + t, 0), sq=sq))
            for sq in range(nseq)]


def _gla_scan(proj, s0, ng, p1, p2, *, hgrn, nb, nt, tb, t_valid):
    _, nh, dk, dv = s0.shape
    wp = proj.shape[1]
    nseq = _seqs_per_step(nb, tb // C)
    body = functools.partial(_gla_body, nseq=nseq, hgrn=hgrn, nh=nh, dk=dk, dv=dv, tb=tb, t_valid=t_valid, nt=nt)
    const = lambda b, t: (0, 0)
    o, s = pl.pallas_call(
        body, grid=(nb // nseq, nt),
        in_specs=_proj_specs(nseq, nt, tb, wp) + [
            pl.BlockSpec((nseq, nh, dk, dv), lambda b, t: (b, 0, 0, 0)),
            pl.BlockSpec(ng.shape, const), pl.BlockSpec(p1.shape, const), pl.BlockSpec(p2.shape, const)],
        out_specs=[pl.BlockSpec((nseq, tb, nh * dv), lambda b, t: (b, t, 0)),
                   pl.BlockSpec((nseq, nh, dk, dv), lambda b, t: (b, 0, 0, 0))],
        out_shape=[jax.ShapeDtypeStruct((nb, nt * tb, nh * dv), F32), jax.ShapeDtypeStruct(s0.shape, F32)],
        scratch_shapes=[pltpu.VMEM((nseq, nh, dv, dk), F32)],
        compiler_params=_cparams(("parallel", "arbitrary")), name="hgrn_scan" if hgrn else "gla_scan",
    )(*([proj] * nseq), s0, ng, p1, p2)
    return o.reshape(nb * nt * tb, nh * dv), s


def _gdn_body(*refs, nseq, nh, dk, dv, tb, t_valid, nt):
    proj_refs = refs[:nseq]
    (s0_ref, conv0_ref, cw_ref, ng_ref, alog_ref, dtb_ref, o_ref, sout_ref, convout_ref,
     st_scr, cbuf, ybuf) = refs[nseq:]
    t = pl.program_id(1)
    cwid = 2 * nh * dk + nh * dv
    pre = 8 - (CONV_W - 1)

    @pl.when(t == 0)
    def _():
        for sq in range(nseq):
            for h in range(nh):
                st_scr[sq, :, h * dv:(h + 1) * dv] = s0_ref[sq, h]
            cbuf[sq, 0:8, :] = jnp.zeros((8, cwid), F32)
            cbuf[sq, pre:8, :] = conv0_ref[sq]

    for sq in range(nseq):
        cbuf[sq, 8:8 + tb, :] = proj_refs[sq][:, 0:cwid]
        y = cbuf[sq, pre:pre + tb, :] * cw_ref[0:1, :]
        for j in range(1, CONV_W):
            y = y + cbuf[sq, pre + j:pre + j + tb, :] * cw_ref[j:j + 1, :]
        ybuf[sq] = jax.nn.silu(y)

    nr = nh * C
    ri = _iota((nr, 1), 0)
    cj = _iota((1, nr), 1)
    rhead = ri // C
    same = rhead == cj // C
    incl = same & (ri % C >= cj % C)
    strict = same & (ri % C > cj % C)
    eye = (ri == cj).astype(F32)

    def stack(ref, rows, base, width):
        return jnp.concatenate([ref[rows, base + h * width:base + (h + 1) * width] for h in range(nh)], axis=0)

    def own_block(x):
        return _tree(jnp.add, [jnp.where(rhead == h, x[:, h * dv:(h + 1) * dv], 0.0) for h in range(nh)])

    prep = []
    for sq, ci in [(sq, ci) for sq in range(nseq) for ci in range(tb // C)]:
        rows = slice(ci * C, (ci + 1) * C)
        valid = _valid_rows(t, ci * C, tb, t_valid, nt)
        small = proj_refs[sq][rows, cwid + nh * dv:cwid + nh * dv + LANE]
        g_all = -jnp.exp(alog_ref[...]) * jax.nn.softplus(small + dtb_ref[...])
        beta_all = jax.nn.sigmoid(small)
        if valid is not None:
            g_all = jnp.where(valid, g_all, 0.0)
            beta_all = jnp.where(valid, beta_all, 0.0)
        b_all = _cumsum_rows(g_all)
        b_t = _rows_to_cols(b_all)
        bcol = jnp.concatenate([b_all[:, h:h + 1] for h in range(nh)], axis=0)
        brow = jnp.concatenate([b_t[h:h + 1, :] for h in range(nh)], axis=1)
        beta = jnp.concatenate([beta_all[:, nh + h:nh + h + 1] for h in range(nh)], axis=0)
        bend = jnp.concatenate([jnp.broadcast_to(b_all[C - 1:C, h:h + 1], (C, 1)) for h in range(nh)], axis=0)
        dec_row = jnp.concatenate([jnp.broadcast_to(jnp.exp(b_all[C - 1:C, h:h + 1]), (1, dv)) for h in range(nh)],
                                  axis=1)
        q = stack(ybuf.at[sq], rows, 0, dk)
        k = stack(ybuf.at[sq], rows, nh * dk, dk)
        v = stack(ybuf.at[sq], rows, 2 * nh * dk, dv)
        q = q * lax.rsqrt(jnp.sum(q * q, -1, keepdims=True) + NORM_EPS) * dk ** -0.5
        k = k * lax.rsqrt(jnp.sum(k * k, -1, keepdims=True) + NORM_EPS)
        if valid is not None:
            valid_st = jnp.concatenate([valid] * nh, axis=0)
            q = jnp.where(valid_st, q, 0.0)
            k = jnp.where(valid_st, k, 0.0)
            v = jnp.where(valid_st, v, 0.0)
        decay = jnp.exp(jnp.where(incl, bcol - brow, -jnp.inf))
        eb = jnp.exp(bcol)
        ks = _split(k)
        prep.append(dict(a=jnp.where(strict, _dot3(ks, ks, nt=True) * decay * beta, 0.0), bv=beta * v,
                         bk=(beta * eb) * k, att=_dot_nt(q, k) * decay, qe=q * eb, kd=k * jnp.exp(bend - bcol),
                         dec_row=dec_row, sq=sq, ci=ci))
    for p in prep:
        a_s = _split(p['a'])
        p['a2'] = _dot3(a_s, a_s)
    for p in prep:
        a2_s = _split(p['a2'])
        p['a4'] = _dot3(a2_s, a2_s)
        p['lo'] = _dot3(_split(eye - p['a']), _split(eye + p['a2']))
    for p in prep:
        a4_s = _split(p['a4'])
        p['a8'] = _dot3(a4_s, a4_s)
    for p in prep:
        p['hi'] = _dot3(_split(eye + p['a4']), _split(eye + p['a8']))
    for p in prep:
        p['tinv'] = _split(_dot3(_split(p['lo']), _split(p['hi'])))
    for p in prep:
        p['u'] = _dot3(p['tinv'], _split(p['bv']))
        p['wk'] = _dot3(p['tinv'], _split(p['bk']))

    s_cat = [st_scr[sq] for sq in range(nseq)]
    for p in sorted(prep, key=lambda p: (p['ci'], p['sq'])):
        sq = p['sq']
        rows = slice(p['ci'] * C, (p['ci'] + 1) * C)
        v_new = p['u'] - own_block(_dot(p['wk'], s_cat[sq]))
        o = _dot(p['att'], v_new) + own_block(_dot(p['qe'], s_cat[sq]))
        v_wide = jnp.concatenate([jnp.where(rhead == h, v_new, 0.0) for h in range(nh)], axis=1)
        s_cat[sq] = s_cat[sq] * p['dec_row'] + _dot_tn(p['kd'], v_wide)
        o = o * lax.rsqrt(jnp.mean(o * o, -1, keepdims=True) + NORM_EPS) * ng_ref[...]
        for h in range(nh):
            gate = proj_refs[sq][rows, cwid + h * dv:cwid + (h + 1) * dv]
            o_ref[sq, rows, h * dv:(h + 1) * dv] = o[h * C:(h + 1) * C, :] * jax.nn.silu(gate)
    for sq in range(nseq):
        st_scr[sq] = s_cat[sq]

    @pl.when(t < nt - 1)
    def _():
        for sq in range(nseq):
            cbuf[sq, pre:8, :] = cbuf[sq, pre + tb:8 + tb, :]

    @pl.when(t == nt - 1)
    def _():
        last = t_valid - (nt - 1) * tb
        for sq in range(nseq):
            for h in range(nh):
                sout_ref[sq, h] = st_scr[sq, :, h * dv:(h + 1) * dv]
            convout_ref[sq] = cbuf[sq, pre + last:8 + last, :]


def _gdn_scan(proj, s0, conv0, cw, ng, alog, dtb, *, nb, nt, tb, t_valid):
    _, nh, dk, dv = s0.shape
    wp = proj.shape[1]
    cwid = 2 * nh * dk + nh * dv
    nseq = _seqs_per_step(nb, tb // C)
    body = functools.partial(_gdn_body, nseq=nseq, nh=nh, dk=dk, dv=dv, tb=tb, t_valid=t_valid, nt=nt)
    const = lambda b, t: (0, 0)
    o, s, conv = pl.pallas_call(
        body, grid=(nb // nseq, nt),
        in_specs=_proj_specs(nseq, nt, tb, wp) + [
            pl.BlockSpec((nseq, nh, dk, dv), lambda b, t: (b, 0, 0, 0)),
            pl.BlockSpec((nseq, CONV_W - 1, cwid), lambda b, t: (b, 0, 0)),
            pl.BlockSpec(cw.shape, const), pl.BlockSpec(ng.shape, const),
            pl.BlockSpec(alog.shape, const), pl.BlockSpec(dtb.shape, const)],
        out_specs=[pl.BlockSpec((nseq, tb, nh * dv), lambda b, t: (b, t, 0)),
                   pl.BlockSpec((nseq, nh, dk, dv), lambda b, t: (b, 0, 0, 0)),
                   pl.BlockSpec((nseq, CONV_W - 1, cwid), lambda b, t: (b, 0, 0))],
        out_shape=[jax.ShapeDtypeStruct((nb, nt * tb, nh * dv), F32), jax.ShapeDtypeStruct(s0.shape, F32),
                   jax.ShapeDtypeStruct(conv0.shape, F32)],
        scratch_shapes=[pltpu.VMEM((nseq, dk, nh * dv), F32), pltpu.VMEM((nseq, tb + 8, cwid), F32),
                        pltpu.VMEM((nseq, tb, cwid), F32)],
        compiler_params=_cparams(("parallel", "arbitrary")), name="gdn_scan",
    )(*([proj] * nseq), s0, conv0, cw, ng, alog, dtb)
    return o.reshape(nb * nt * tb, nh * dv), s, conv


def _mlstm_body(*refs, nseq, nh, dk, dv, tb, t_valid, nt):
    proj_refs = refs[:nseq]
    (c0_ref, n0_ref, m0_ref, ng_ref, bi_ref, bf_ref, o_ref, cout_ref, nout_ref, mout_ref,
     ct_scr, n_scr, m_scr) = refs[nseq:]
    t = pl.program_id(1)

    @pl.when(t == 0)
    def _():
        for sq in range(nseq):
            for h in range(nh):
                ct_scr[sq, :, h * dv:(h + 1) * dv] = c0_ref[sq, h]
                n_scr[sq, h] = n0_ref[sq, h:h + 1, :]
                m_scr[sq, h] = jnp.broadcast_to(m0_ref[sq, :, h:h + 1], (1, LANE))

    base = 2 * nh * dk + 2 * nh * dv
    nr = nh * C
    ri = _iota((nr, 1), 0)
    cj = _iota((1, nr), 1)
    rhead = ri // C
    incl = (rhead == cj // C) & (ri % C >= cj % C)

    def stack(ref, rows, col0, width):
        return jnp.concatenate([ref[rows, col0 + h * width:col0 + (h + 1) * width] for h in range(nh)], axis=0)

    def per_head_col(vals):
        return jnp.concatenate([jnp.broadcast_to(x, (C, 1)) for x in vals], axis=0)

    prep = {}
    for sq, ci in [(sq, ci) for sq in range(nseq) for ci in range(tb // C)]:
        proj_ref = proj_refs[sq]
        rows = slice(ci * C, (ci + 1) * C)
        valid = _valid_rows(t, ci * C, tb, t_valid, nt)
        small = proj_ref[rows, base:base + LANE]
        li_all = small + bi_ref[...]
        lf_all = jax.nn.log_sigmoid(small + bf_ref[...])
        if valid is not None:
            lf_all = jnp.where(valid, lf_all, 0.0)
        b_all = _cumsum_rows(lf_all)
        b_t = _rows_to_cols(b_all)
        li_t = _rows_to_cols(li_all)
        bcol = jnp.concatenate([b_all[:, nh + h:nh + h + 1] for h in range(nh)], axis=0)
        licol = jnp.concatenate([li_all[:, h:h + 1] for h in range(nh)], axis=0)
        xrow = jnp.concatenate([b_t[nh + h:nh + h + 1, :] - li_t[h:h + 1, :] for h in range(nh)], axis=1)
        if valid is not None:
            xrow = jnp.where(t * tb + ci * C + cj % C < t_valid, xrow, jnp.inf)
            licol = jnp.where(t * tb + ci * C + ri % C < t_valid, licol, -jnp.inf)
        q = stack(proj_ref, rows, 0, dk)
        k = stack(proj_ref, rows, nh * dk, dk) * dk ** -0.5
        v = stack(proj_ref, rows, 2 * nh * dk, dv)
        dmat = jnp.where(incl, bcol - xrow, -jnp.inf)
        prep[sq, ci] = dict(q=q, k=k, v=v, bcol=bcol, licol=licol, dmat=dmat, qk=_dot_nt(q, k),
                            rowmax=jnp.max(dmat, axis=1, keepdims=True),
                            b_end=[b_all[C - 1:C, nh + h:nh + h + 1] for h in range(nh)])

    c_cat = [ct_scr[sq] for sq in range(nseq)]
    n_rows = [[n_scr[sq, h] for h in range(nh)] for sq in range(nseq)]
    m_vals = [[m_scr[sq, h][:, 0:1] for h in range(nh)] for sq in range(nseq)]
    for ci, sq in [(ci, sq) for ci in range(tb // C) for sq in range(nseq)]:
        p = prep[sq, ci]
        rows = slice(ci * C, (ci + 1) * C)
        q, k, v, bcol = p['q'], p['k'], p['v'], p['bcol']
        m_inter = bcol + per_head_col(m_vals[sq])
        m_t = jnp.maximum(m_inter, p['rowmax'])
        a_int = jnp.exp(m_inter - m_t)
        qk = p['qk'] * jnp.exp(p['dmat'] - m_t)
        qc = _dot(q, c_cat[sq])
        own = _tree(jnp.add, [jnp.where(rhead == h, qc[:, h * dv:(h + 1) * dv], 0.0) for h in range(nh)])
        num = _dot(qk, v) + a_int * own
        n_st = jnp.concatenate([jnp.broadcast_to(n_rows[sq][h], (C, dk)) for h in range(nh)], axis=0)
        den = jnp.sum(qk, axis=1, keepdims=True) + a_int * jnp.sum(q * n_st, axis=1, keepdims=True)
        hid = num / jnp.maximum(jnp.abs(den), jnp.exp(-m_t))
        last = [slice(h * C + C - 1, h * C + C) for h in range(nh)]
        m_vals[sq] = [m_t[r, :] for r in last]
        a_end = [a_int[r, :] for r in last]
        wk = jnp.exp(per_head_col(p['b_end']) - bcol + p['licol'] - per_head_col(m_vals[sq])) * k
        v_wide = jnp.concatenate([jnp.where(rhead == h, v, 0.0) for h in range(nh)], axis=1)
        a_row = jnp.concatenate([jnp.broadcast_to(a, (1, dv)) for a in a_end], axis=1)
        c_cat[sq] = c_cat[sq] * a_row + _dot_tn(wk, v_wide)
        n_rows[sq] = [a_end[h] * n_rows[sq][h] + jnp.sum(wk[h * C:(h + 1) * C, :], axis=0, keepdims=True)
                      for h in range(nh)]
        hc = hid - jnp.mean(hid, -1, keepdims=True)
        hn = hc * lax.rsqrt(jnp.mean(hc * hc, -1, keepdims=True) + NORM_EPS)
        for h in range(nh):
            og = jax.nn.sigmoid(
                proj_refs[sq][rows, 2 * nh * dk + nh * dv + h * dv:2 * nh * dk + nh * dv + (h + 1) * dv])
            o_ref[sq, rows, h * dv:(h + 1) * dv] = og * (hn[h * C:(h + 1) * C, :] * ng_ref[:, h * dv:(h + 1) * dv])
    for sq in range(nseq):
        ct_scr[sq] = c_cat[sq]
        for h in range(nh):
            n_scr[sq, h] = n_rows[sq][h]
            m_scr[sq, h] = jnp.broadcast_to(m_vals[sq][h], (1, LANE))

    @pl.when(t == nt - 1)
    def _():
        lane = _iota((1, nh), 1)
        for sq in range(nseq):
            mrow = jnp.zeros((1, nh), F32)
            for h in range(nh):
                cout_ref[sq, h] = ct_scr[sq, :, h * dv:(h + 1) * dv]
                nout_ref[sq, h:h + 1, :] = n_scr[sq, h]
                mrow = jnp.where(lane == h, m_scr[sq, h][:, 0:1], mrow)
            mout_ref[sq] = mrow


def _mlstm_scan(proj, c0, n0, m0, ng, bi, bf, *, nb, nt, tb, t_valid):
    _, nh, dk, dv = c0.shape
    wp = proj.shape[1]
    nseq = _seqs_per_step(nb, tb // C)
    body = functools.partial(_mlstm_body, nseq=nseq, nh=nh, dk=dk, dv=dv, tb=tb, t_valid=t_valid, nt=nt)
    const = lambda b, t: (0, 0)
    m0 = m0.reshape(nb, 1, nh)
    o, c, n, m = pl.pallas_call(
        body, grid=(nb // nseq, nt),
        in_specs=_proj_specs(nseq, nt, tb, wp) + [
            pl.BlockSpec((nseq, nh, dk, dv), lambda b, t: (b, 0, 0, 0)),
            pl.BlockSpec((nseq, nh, dk), lambda b, t: (b, 0, 0)),
            pl.BlockSpec((nseq, 1, nh), lambda b, t: (b, 0, 0)),
            pl.BlockSpec(ng.shape, const), pl.BlockSpec(bi.shape, const), pl.BlockSpec(bf.shape, const)],
        out_specs=[pl.BlockSpec((nseq, tb, nh * dv), lambda b, t: (b, t, 0)),
                   pl.BlockSpec((nseq, nh, dk, dv), lambda b, t: (b, 0, 0, 0)),
                   pl.BlockSpec((nseq, nh, dk), lambda b, t: (b, 0, 0)),
                   pl.BlockSpec((nseq, 1, nh), lambda b, t: (b, 0, 0))],
        out_shape=[jax.ShapeDtypeStruct((nb, nt * tb, nh * dv), F32), jax.ShapeDtypeStruct(c0.shape, F32),
                   jax.ShapeDtypeStruct(n0.shape, F32), jax.ShapeDtypeStruct((nb, 1, nh), F32)],
        scratch_shapes=[pltpu.VMEM((nseq, dk, nh * dv), F32), pltpu.VMEM((nseq, nh, 1, dk), F32),
                        pltpu.VMEM((nseq, nh, 1, LANE), F32)],
        compiler_params=_cparams(("parallel", "arbitrary")), name="mlstm_scan",
    )(*([proj] * nseq), c0, n0, m0, ng, bi, bf)
    return o.reshape(nb * nt * tb, nh * dv), c, n, m


SUB = 8


def _tree(op, xs):
    xs = list(xs)
    while len(xs) > 1:
        xs = [op(xs[i], xs[i + 1]) if i + 1 < len(xs) else xs[i] for i in range(0, len(xs), 2)]
    return xs[0]


def _all_sublanes(op, m):
    for shift in (4, 2, 1):
        m = op(m, pltpu.roll(m, shift, axis=0))
    return m


def _sorting_network(n):
    def merge(lo, hi, r):
        step = r * 2
        if step < hi - lo:
            yield from merge(lo, hi, step)
            yield from merge(lo + r, hi, step)
            yield from [(i, i + r) for i in range(lo + r, hi - r, step)]
        else:
            yield (lo, lo + r)

    def sort(lo, hi):
        if hi - lo >= 1:
            mid = lo + (hi - lo) // 2
            yield from sort(lo, mid)
            yield from sort(mid + 1, hi)
            yield from merge(lo, hi, 1)

    return list(sort(0, n - 1))


def _sort_tiles_desc(tiles):
    tiles = list(tiles)
    for i, j in _sorting_network(len(tiles)):
        tiles[i], tiles[j] = jnp.maximum(tiles[i], tiles[j]), jnp.minimum(tiles[i], tiles[j])
    return tiles


def _merge_top(lists, singles=None):
    lists = list(lists)
    vals = []
    for j in range(PEER_TOPK):
        head = lists[0] if singles is None else jnp.maximum(lists[0], singles)
        m = _all_sublanes(jnp.maximum, head)
        vals.append(m)
        if j == PEER_TOPK - 1:
            break
        hit = lists[0] == m
        for i in range(min(len(lists) - 1, PEER_TOPK - 1 - j)):
            lists[i] = jnp.where(hit, lists[i + 1], lists[i])
        if singles is not None:
            singles = jnp.where(singles == m, -jnp.inf, singles)
    return vals


def _route_tile(q_scr, k1_ref, k2_ref, r2_ref, c1_ref, a1_ref, a2_ref, h, lanes):
    ntile = PEER_NKEYS // SUB
    sub = _iota((SUB, LANE), 0)
    q1 = q_scr[pl.ds(pl.multiple_of(h * 2 * PEER_HALF, PEER_HALF), PEER_HALF), lanes]
    q2 = q_scr[pl.ds(pl.multiple_of(h * 2 * PEER_HALF + PEER_HALF, PEER_HALF), PEER_HALF), lanes]
    s1 = _dot(k1_ref[...], q1, HI)
    s2 = _dot(k2_ref[...], q2, HI)
    t1 = [s1[k * SUB:(k + 1) * SUB, :] for k in range(ntile)]
    t2 = [s2[k * SUB:(k + 1) * SUB, :] for k in range(ntile)]
    v1 = _merge_top(_sort_tiles_desc(t1))
    v2 = _merge_top(_sort_tiles_desc(t2))
    v2lo, v2hi = v2[SUB - 1], v2[2 * SUB - 1]
    for b in range(SUB - 2, -1, -1):
        v2lo = jnp.where(sub == b, v2[b], v2lo)
        v2hi = jnp.where(sub == b, v2[SUB + b], v2hi)
    cand_lo = [jnp.where(sub < min(PEER_TOPK // (a + 1), SUB), v1[a] + v2lo, -jnp.inf) for a in range(PEER_TOPK)]
    cand_hi = v1[0] + v2hi
    tau = _merge_top(cand_lo, cand_hi)[PEER_TOPK - 1]
    top = v1[0] + v2[0]
    z = _all_sublanes(jnp.add, _tree(jnp.add, [jnp.where(cd >= tau, jnp.exp(cd - top), 0.0)
                                               for cd in cand_lo + [cand_hi]]))
    inv_z = 1.0 / z
    pairs = [_all_sublanes(jnp.add, jnp.where(cd >= tau, 1.0, 0.0)) for cd in cand_lo]
    pairs[0] = pairs[0] + _all_sublanes(jnp.add, jnp.where(cand_hi >= tau, 1.0, 0.0))
    rank2, c1 = [], []
    for k in range(ntile):
        r = jnp.full((SUB, LANE), float(PEER_TOPK), F32)
        cnt = jnp.zeros((SUB, LANE), F32)
        for j in range(PEER_TOPK - 1, -1, -1):
            r = jnp.where(t2[k] >= v2[j], float(j), r)
            cnt = jnp.where(t1[k] == v1[j], pairs[j], cnt)
        rank2.append(r)
        c1.append(cnt)
    r2_ref[h, :, lanes] = jnp.concatenate(rank2, axis=0).astype(BF16)
    c1_ref[h, :, lanes] = jnp.concatenate(c1, axis=0)
    a1_ref[h, :, lanes] = jnp.concatenate([jnp.exp(t - v1[0]) for t in t1], axis=0)
    a2_ref[h, :, lanes] = jnp.concatenate([jnp.exp(t - v2[0]) * inv_z for t in t2], axis=0).astype(BF16)


def _peer_body(xm_ref, xr_ref, wq_ref, k1_ref, k2_ref, u_ref, vt_ref, g_ref, b_ref, out_ref,
               acc, xbf, p_scr, q_scr, r2_scr, c1_scr, a1_scr, a2_scr):
    r = pl.program_id(0)
    c = pl.program_id(1)
    tm = xbf.shape[1]

    @pl.when(c == 0)
    def _():
        xbf[...] = xm_ref[...].astype(BF16)
        acc[...] = jnp.zeros(acc.shape, F32)
        p_scr[...] = jnp.zeros(p_scr.shape, BF16)
        q_scr[...] = _dot(wq_ref[...], xr_ref[...].astype(BF16))

    @pl.when((c == 0) & (r == 0))
    def _():
        for scr in (r2_scr, c1_scr, a1_scr, a2_scr):
            scr[...] = jnp.zeros(scr.shape, scr.dtype)

    ntile = tm // LANE
    units = pl.num_programs(1) - 1
    tiles_per_unit = PEER_HEADS * ntile // units
    unit = (c + units - 1) % units
    h_route = unit // (ntile // tiles_per_unit)
    lane0 = (unit % (ntile // tiles_per_unit)) * tiles_per_unit * LANE
    wr = r % 2
    for sb in range(tiles_per_unit):
        _route_tile(q_scr, k1_ref, k2_ref, r2_scr.at[wr], c1_scr.at[wr], a1_scr.at[wr], a2_scr.at[wr], h_route,
                    pl.ds(pl.multiple_of(lane0 + sb * LANE, LANE), LANE))

    rd = (r + 1) % 2
    i1_base = jnp.minimum(c, pl.num_programs(1) - 2) * PEER_I1_PER_STEP
    rows16 = 2 * SUB
    acc[...] += _dot(vt_ref[0], p_scr[...])
    hid = _dot(u_ref[...], xbf[...])
    act = (0.5 * hid * (1.0 + lax.erf(hid * 0.5 ** 0.5))).astype(BF16)
    for l0 in range(0, tm, PEER_LANE_BLOCK):
        ls = slice(l0, l0 + PEER_LANE_BLOCK)
        for j in range(PEER_I1_PER_STEP):
            row = pl.ds(i1_base + j, 1)
            nkb = PEER_NKEYS // rows16
            w = [None] * nkb
            for h in range(PEER_HEADS):
                c1b = jnp.broadcast_to(c1_scr[rd, h, row, ls], (rows16, PEER_LANE_BLOCK)).astype(BF16)
                a1b = jnp.broadcast_to(a1_scr[rd, h, row, ls], (rows16, PEER_LANE_BLOCK)).astype(BF16)
                for kb in range(nkb):
                    ks = slice(kb * rows16, (kb + 1) * rows16)
                    term = jnp.where(r2_scr[rd, h, ks, ls] < c1b, a2_scr[rd, h, ks, ls] * a1b, 0.0)
                    w[kb] = term if w[kb] is None else w[kb] + term
            for kb in range(nkb):
                es = slice(j * PEER_NKEYS + kb * rows16, j * PEER_NKEYS + (kb + 1) * rows16)
                p_scr[es, ls] = w[kb] * act[es, ls]

    @pl.when(c == pl.num_programs(1) - 1)
    def _():
        z = DN_ALPHA * xm_ref[...] + acc[...]
        zc = z - jnp.mean(z, 0, keepdims=True)
        var = jnp.mean(zc * zc, 0, keepdims=True)
        out_ref[...] = (zc * lax.rsqrt(var + LN_EPS) * g_ref[...] + b_ref[...]).T


def _peer(xt, wq_t, k1, k2, u, vt, g, b, tm):
    d, n = xt.shape
    nb = n // tm
    ec = PEER_I1_PER_STEP * PEER_NKEYS
    nc = u.shape[0] // ec
    ntile = tm // LANE
    assert (PEER_HEADS * ntile) % nc == 0 and ntile % (PEER_HEADS * ntile // nc) == 0, "routing units per row"
    const = lambda r, c: (0, 0)
    rshape = (2, PEER_HEADS, PEER_NKEYS, tm)
    return pl.pallas_call(
        _peer_body, grid=(nb + 1, nc + 1),
        in_specs=[pl.BlockSpec((d, tm), lambda r, c: (0, jnp.maximum(r - 1, 0))),
                  pl.BlockSpec((d, tm), lambda r, c: (0, jnp.minimum(r, nb - 1))),
                  pl.BlockSpec(wq_t.shape, const), pl.BlockSpec(k1.shape, const), pl.BlockSpec(k2.shape, const),
                  pl.BlockSpec((ec, d), lambda r, c: (jnp.minimum(c, nc - 1), 0)),
                  pl.BlockSpec((1, d, ec), lambda r, c: (jnp.maximum(c - 1, 0), 0, 0)),
                  pl.BlockSpec((d, 1), const), pl.BlockSpec((d, 1), const)],
        out_specs=pl.BlockSpec((tm, d), lambda r, c: (jnp.maximum(r - 1, 0), 0)),
        out_shape=jax.ShapeDtypeStruct((n, d), F32),
        scratch_shapes=[pltpu.VMEM((d, tm), F32), pltpu.VMEM((d, tm), BF16), pltpu.VMEM((ec, tm), BF16),
                        pltpu.VMEM((wq_t.shape[0], tm), F32), pltpu.VMEM(rshape, BF16), pltpu.VMEM(rshape, F32),
                        pltpu.VMEM(rshape, F32), pltpu.VMEM(rshape, BF16)],
        compiler_params=pltpu.CompilerParams(dimension_semantics=("arbitrary", "arbitrary"),
                                             vmem_limit_bytes=PEER_VMEM_LIMIT), name="peer",
    )(xt, xt, wq_t, k1, k2, u, vt, g.reshape(d, 1), b.reshape(d, 1))


def _pad_cols(w, width):
    return jnp.pad(w, ((0, 0), (0, width - w.shape[1])))


def _lane_row(*parts):
    v = jnp.concatenate([p.astype(F32).reshape(-1) for p in parts])
    return jnp.pad(v, (0, LANE - v.shape[0])).reshape(1, LANE)


def _pick_tb(t_pad):
    for tb in (48, 32, 16):
        if t_pad % tb == 0:
            return tb
    raise ValueError(t_pad)


def kernel(x_prompt, x_sample, state_hgrn_S, state_gdn_S, state_gdn_conv, state_mlstm_C, state_mlstm_n, state_mlstm_m, state_gla_S, meta_tokens, hgrn_w_in, hgrn_lb, hgrn_norm_g, hgrn_w_out, gdn_w_in, gdn_conv_w, gdn_a_log, gdn_dt_bias, gdn_norm_g, gdn_w_out, mlstm_w_in, mlstm_b_i, mlstm_b_f, mlstm_norm_g, mlstm_w_out, gla_w_in, gla_w_gate, gla_b_gate, gla_norm_g, gla_w_out, peer_w_q, peer_keys1, peer_keys2, peer_u, peer_v, ln_g, ln_b):
    bp, seq, d = x_prompt.shape
    bs, seq_s, _ = x_sample.shape
    assert d == D_MODEL and len(state_hgrn_S) == 1 and len(state_gdn_S) == 1
    assert len(state_mlstm_C) == 1 and len(state_gla_S) == 1
    tp = N_META + seq
    tp_pad = -(-tp // C) * C
    ts_pad = -(-seq_s // C) * C
    tbp = _pick_tb(tp_pad)
    tbs = _pick_tb(ts_pad)
    np_rows = bp * tp_pad
    n_real = np_rows + bs * seq_s

    meta = jnp.broadcast_to(meta_tokens.astype(F32)[None], (bp, N_META, d))
    hp = jnp.concatenate([meta, x_prompt], axis=1)
    hp = jnp.pad(hp, ((0, 0), (0, tp_pad - tp), (0, 0))).reshape(np_rows, d)
    h = jnp.concatenate([hp, x_sample.reshape(bs * seq_s, d)], axis=0)
    n = -(-n_real // TOK_MULTIPLE) * TOK_MULTIPLE
    h = jnp.pad(h, ((0, n - n_real), (0, 0)))
    tm_mm = MM_TOK_BLOCK

    def split_cols(w, sizes):
        out, o = [], 0
        for s in sizes:
            out.append(w[:, o:o + s])
            o += s
        return out

    def run_mixer(scan, proj, states_p, states_s, **kw):
        res_p = scan(proj, *states_p, nb=bp, nt=tp_pad // tbp, tb=tbp, t_valid=tp, **kw)
        ps = proj[np_rows:n_real].reshape(bs, seq_s, -1)
        ps = jnp.pad(ps, ((0, 0), (0, ts_pad - seq_s), (0, 0))).reshape(bs * ts_pad, -1)
        res_s = scan(ps, *states_s, nb=bs, nt=ts_pad // tbs, tb=tbs, t_valid=seq_s, **kw)
        o_s = res_s[0].reshape(bs, ts_pad, -1)[:, :seq_s].reshape(bs * seq_s, -1)
        o = jnp.concatenate([res_p[0], o_s], axis=0)
        o = jnp.pad(o, ((0, n - n_real), (0, 0)))
        return o, res_p[1:], res_s[1:]

    lb_all = jnp.cumsum(jax.nn.softmax(hgrn_lb.astype(F32), axis=0), axis=0)
    outs = {}
    for i in range(DEPTH):
        mix = i % 4
        if mix == 0:
            w_in = hgrn_w_in[0].astype(BF16)
            proj = _mm(h, w_in, tm_mm, 1024)
            nh, dk, dv = state_hgrn_S.shape[2:]
            zero = jnp.zeros((bp, nh, dk, dv), F32)
            scan = functools.partial(_gla_scan, ng=hgrn_norm_g[0].reshape(1, dv), p1=lb_all[i].reshape(1, nh * dk),
                                     p2=jnp.zeros((1, LANE), F32), hgrn=True)
            o, (sp,), (ss,) = run_mixer(scan, proj, (zero,), (state_hgrn_S[0],))
            outs['hgrn'] = (sp[None], ss[None])
            w_out = hgrn_w_out[0]
        elif mix == 1:
            nh, dk, dv = state_gdn_S.shape[2:]
            cwid = 2 * nh * dk + nh * dv
            qkv, a, b, g = split_cols(gdn_w_in[0], [cwid, nh, nh, nh * dv])
            w_in = jnp.concatenate([qkv, g, _pad_cols(jnp.concatenate([a, b], axis=1), LANE)], axis=1).astype(BF16)
            proj = _mm(h, w_in, tm_mm, w_in.shape[1] // 3)
            scan = functools.partial(_gdn_scan, cw=gdn_conv_w[0], ng=gdn_norm_g[0].reshape(1, dv),
                                     alog=_lane_row(gdn_a_log[0]), dtb=_lane_row(gdn_dt_bias[0]))
            o, (sp, cp), (ss, cs) = run_mixer(
                scan, proj,
                (jnp.zeros((bp, nh, dk, dv), F32), jnp.zeros((bp, CONV_W - 1, cwid), F32)),
                (state_gdn_S[0], state_gdn_conv[0]))
            outs['gdn'] = (sp[None], cp[None], ss[None], cs[None])
            w_out = gdn_w_out[0]
        elif mix == 2:
            nh, dk, dv = state_mlstm_C.shape[2:]
            q, k, v, ig, fg, og = split_cols(mlstm_w_in[0], [nh * dk, nh * dk, nh * dv, nh, nh, nh * dv])
            w_in = jnp.concatenate([q, k, v, og, _pad_cols(jnp.concatenate([ig, fg], axis=1), LANE)],
                                   axis=1).astype(BF16)
            proj = _mm(h, w_in, tm_mm, w_in.shape[1] // 3)
            zeros_nh = jnp.zeros((nh,), F32)
            scan = functools.partial(_mlstm_scan, ng=mlstm_norm_g[0].reshape(1, nh * dv),
                                     bi=_lane_row(mlstm_b_i[0]), bf=_lane_row(zeros_nh, mlstm_b_f[0]))
            o, (cp, npp, mp), (cs, ns, ms) = run_mixer(
                scan, proj,
                (jnp.zeros((bp, nh, dk, dv), F32), jnp.zeros((bp, nh, dk), F32), jnp.zeros((bp, nh), F32)),
                (state_mlstm_C[0], state_mlstm_n[0], state_mlstm_m[0]))
            outs['mlstm'] = (cp[None], npp[None], mp.reshape(1, bp, nh), cs[None], ns[None], ms.reshape(1, bs, nh))
            w_out = mlstm_w_out[0]
        else:
            nh, dk, dv = state_gla_S.shape[2:]
            rank = gla_w_gate.shape[1]
            q, k, v, g, r = split_cols(gla_w_in[0], [nh * dk, nh * dk, nh * dv, nh * dv, rank])
            w_in = jnp.concatenate([q, k, v, g, _pad_cols(r, LANE)], axis=1).astype(BF16)
            proj = _mm(h, w_in, tm_mm, w_in.shape[1] // 5)
            wg = jnp.pad(gla_w_gate[0].astype(F32), ((0, LANE - rank), (0, 0)))
            scan = functools.partial(_gla_scan, ng=gla_norm_g[0].reshape(1, dv), p1=wg,
                                     p2=gla_b_gate[0].reshape(1, nh * dk), hgrn=False)
            o, (sp,), (ss,) = run_mixer(scan, proj, (jnp.zeros((bp, nh, dk, dv), F32),), (state_gla_S[0],))
            outs['gla'] = (sp[None], ss[None])
            w_out = gla_w_out[0]

        h1t = _mm_res_ln_t(o, w_out.astype(BF16), h, ln_g[i, 0], ln_b[i, 0], TOK_BLOCK)
        ec = PEER_I1_PER_STEP * PEER_NKEYS
        vt = peer_v[i].astype(BF16).reshape(-1, ec, d).transpose(0, 2, 1)
        h = _peer(h1t, peer_w_q[i].T.astype(BF16), peer_keys1[i], peer_keys2[i], peer_u[i].astype(BF16), vt,
                  ln_g[i, 1], ln_b[i, 1], PEER_TOK_BLOCK)

    y_prompt = h[:np_rows].reshape(bp, tp_pad, d)[:, N_META:tp]
    y_sample = h[np_rows:n_real].reshape(bs, seq_s, d)
    hg, gd, ml, gl = outs['hgrn'], outs['gdn'], outs['mlstm'], outs['gla']
    return (y_prompt, y_sample, hg[0], gd[0], gd[1], ml[0], ml[1], ml[2], gl[0],
            hg[1], gd[2], gd[3], ml[3], ml[4], ml[5], gl[1])
```

```python
import functools

import jax
import jax.numpy as jnp
from jax import lax
from jax.experimental import pallas as pl
from jax.experimental.pallas import tpu as pltpu

F32 = jnp.float32
BF16 = jnp.bfloat16
HI = lax.Precision.HIGHEST

D_MODEL = 1024
DEPTH = 4
N_META = 16
C = 16
CONV_W = 4
NORM_EPS = 1e-6
LN_EPS = 1e-5
DN_ALPHA = (2 * DEPTH) ** 0.25
GLA_GATE_NORM = 16.0
LANE = 128
PEER_HEADS = 8
PEER_NKEYS = 128
PEER_TOPK = 16
PEER_HALF = 128
PEER_I1_PER_STEP = 8
TOK_BLOCK = 512
PEER_TOK_BLOCK = 512
PEER_LANE_BLOCK = 256
MM_TOK_BLOCK = 512
TOK_MULTIPLE = 1536
VMEM_LIMIT = 48 * 1024 * 1024
PEER_VMEM_LIMIT = 56 * 1024 * 1024


def _cparams(sem):
    return pltpu.CompilerParams(dimension_semantics=sem, vmem_limit_bytes=VMEM_LIMIT)


def _iota(shape, dim):
    return lax.broadcasted_iota(jnp.int32, shape, dim)


def _dot_nt(a, b, precision=None):
    return lax.dot_general(a, b, (((1,), (1,)), ((), ())), precision=precision, preferred_element_type=F32)


def _dot_tn(a, b, precision=None):
    return lax.dot_general(a, b, (((0,), (0,)), ((), ())), precision=precision, preferred_element_type=F32)


def _dot(a, b, precision=None):
    return jnp.dot(a, b, precision=precision, preferred_element_type=F32)


def _split(x):
    hi = x.astype(BF16)
    return hi, (x - hi.astype(F32)).astype(BF16)


def _dot3(a, b, nt=False):
    (ah, al), (bh, bl) = a, b
    d = _dot_nt if nt else _dot
    return d(ah, bh) + d(ah, bl) + d(al, bh)


def _cumsum_rows(x):
    tri = (_iota((C, C), 0) >= _iota((C, C), 1)).astype(F32)
    return _dot(tri, x, HI)


def _rows_to_cols(x):
    eye = (_iota((LANE, LANE), 0) == _iota((LANE, LANE), 1)).astype(F32)
    return _dot_nt(eye, x, HI)


def _mm_body(x_ref, w_ref, o_ref):
    o_ref[...] = _dot(x_ref[...].astype(BF16), w_ref[...])


def _mm(x, w, tm, tn):
    m, k = x.shape
    n = w.shape[1]
    return pl.pallas_call(
        _mm_body, grid=(m // tm, n // tn),
        in_specs=[pl.BlockSpec((tm, k), lambda i, j: (i, 0)), pl.BlockSpec((k, tn), lambda i, j: (0, j))],
        out_specs=pl.BlockSpec((tm, tn), lambda i, j: (i, j)),
        out_shape=jax.ShapeDtypeStruct((m, n), F32),
        compiler_params=_cparams(("parallel", "parallel")), name="mm")(x, w)


def _mm_ln_body(o_ref, w_ref, h_ref, g_ref, b_ref, out_ref):
    y = _dot(o_ref[...].astype(BF16), w_ref[...])
    z = DN_ALPHA * h_ref[...] + y
    zc = z - jnp.mean(z, -1, keepdims=True)
    var = jnp.mean(zc * zc, -1, keepdims=True)
    out_ref[...] = (zc * lax.rsqrt(var + LN_EPS) * g_ref[...] + b_ref[...]).T


def _mm_res_ln_t(o, w, h, g, b, tm):
    n, k = o.shape
    d = w.shape[1]
    return pl.pallas_call(
        _mm_ln_body, grid=(n // tm,),
        in_specs=[pl.BlockSpec((tm, k), lambda i: (i, 0)), pl.BlockSpec((k, d), lambda i: (0, 0)),
                  pl.BlockSpec((tm, d), lambda i: (i, 0)), pl.BlockSpec((1, d), lambda i: (0, 0)),
                  pl.BlockSpec((1, d), lambda i: (0, 0))],
        out_specs=pl.BlockSpec((d, tm), lambda i: (0, i)),
        out_shape=jax.ShapeDtypeStruct((d, n), F32),
        compiler_params=_cparams(("parallel",)), name="mm_res_ln")(o, w, h, g.reshape(1, d), b.reshape(1, d))


def _valid_rows(t, r0, tb, t_valid, nt):
    if t_valid == nt * tb:
        return None
    return (t * tb + r0 + _iota((C, 1), 0)) < t_valid


def _gla_body(*refs, nseq, hgrn, nh, dk, dv, tb, t_valid, nt):
    proj_refs = refs[:nseq]
    s0_ref, ng_ref, p1_ref, p2_ref, o_ref, sout_ref, st_scr = refs[nseq:]
    t = pl.program_id(1)

    @pl.when(t == 0)
    def _():
        for sq in range(nseq):
            for h in range(nh):
                st_scr[sq, h] = s0_ref[sq, h].T

    kw = nh * dk
    vw = nh * dv
    nr = nh * C
    ri = _iota((nr, 1), 0)
    cj = _iota((1, nr), 1)
    rhead = ri // C
    tri = ((rhead == cj // C) & (ri % C >= cj % C)).astype(F32)
    row3 = _iota((1, C, 1), 1)

    def stack(ref, rows, col0, width):
        return jnp.concatenate([ref[rows, col0 + h * width:col0 + (h + 1) * width] for h in range(nh)], axis=0)

    prep = {}
    for sq, proj_ref in enumerate(proj_refs):
        for ci in range(tb // C):
            rows = slice(ci * C, (ci + 1) * C)
            if hgrn:
                q = jax.nn.silu(stack(proj_ref, rows, 0, dk))
                lb = jnp.concatenate([jnp.broadcast_to(p1_ref[:, h * dk:(h + 1) * dk], (C, dk)) for h in range(nh)],
                                     axis=0)
                fg = lb + (1.0 - lb) * jax.nn.sigmoid(stack(proj_ref, rows, kw, dk))
                k = 1.0 - fg
                g = jnp.log(fg)
            else:
                r = proj_ref[rows, 2 * kw + 2 * vw:2 * kw + 2 * vw + LANE]
                logf_all = jax.nn.log_sigmoid(_dot(r, p1_ref[...], HI) + p2_ref[...]) / GLA_GATE_NORM
                q = stack(proj_ref, rows, 0, dk) * dk ** -0.5
                k = stack(proj_ref, rows, kw, dk)
                g = jnp.concatenate([logf_all[:, h * dk:(h + 1) * dk] for h in range(nh)], axis=0)
            v = stack(proj_ref, rows, 2 * kw, dv)
            if t_valid != nt * tb:
                valid = t * tb + ci * C + ri % C < t_valid
                k = jnp.where(valid, k, 0.0)
                g = jnp.where(valid, g, 0.0)
            b = _dot(tri, g, HI)
            b3, q3, k3, v3 = (x.reshape(nh, C, x.shape[-1]) for x in (b, q, k, v))
            o3 = jnp.zeros((nh, C, dv), F32)
            for s in range(C):
                d = jnp.exp(jnp.where(row3 >= s, b3 - b3[:, s:s + 1, :], -jnp.inf))
                col = jnp.sum(q3 * (k3[:, s:s + 1, :] * d), axis=-1, keepdims=True)
                o3 = o3 + col * v3[:, s:s + 1, :]
            b_end = b3[:, C - 1:C, :]
            prep[sq, ci] = dict(o=o3.reshape(nr, dv), qe=q * jnp.exp(b), kd=(k3 * jnp.exp(b_end - b3)).reshape(nr, dk),
                                dec=jnp.exp(b_end), v=v)

    sts = [st_scr[sq] for sq in range(nseq)]
    for ci in range(tb // C):
        rows = slice(ci * C, (ci + 1) * C)
        for sq in range(nseq):
            p = prep[sq, ci]
            inter = _dot_nt(p['qe'], sts[sq].reshape(nh * dv, dk))
            o = p['o'] + _tree(jnp.add, [jnp.where(rhead == h, inter[:, h * dv:(h + 1) * dv], 0.0)
                                         for h in range(nh)])
            v_wide = jnp.concatenate([jnp.where(rhead == h, p['v'], 0.0) for h in range(nh)], axis=1)
            sts[sq] = sts[sq] * p['dec'] + _dot_tn(v_wide, p['kd']).reshape(nh, dv, dk)
            o = o * lax.rsqrt(jnp.mean(o * o, -1, keepdims=True) + NORM_EPS) * ng_ref[...]
            for h in range(nh):
                gate = proj_refs[sq][rows, 2 * kw + vw + h * dv:2 * kw + vw + (h + 1) * dv]
                o_ref[sq, rows, h * dv:(h + 1) * dv] = o[h * C:(h + 1) * C, :] * jax.nn.silu(gate)
    for sq in range(nseq):
        st_scr[sq] = sts[sq]

    @pl.when(t == nt - 1)
    def _():
        for sq in range(nseq):
            for h in range(nh):
                sout_ref[sq, h] = st_scr[sq, h].T


def _seqs_per_step(nb, chunks):
    return max(n for n in (8, 4, 2, 1) if nb % n == 0 and n * chunks <= 8)


def _proj_specs(nseq, nt, tb, wp):
    return [pl.BlockSpec((tb, wp), functools.partial(lambda b, t, sq: ((b * nseq + sq) * nt + t, 0), sq=sq))
            for sq in range(nseq)]


def _gla_scan(proj, s0, ng, p1, p2, *, hgrn, nb, nt, tb, t_valid):
    _, nh, dk, dv = s0.shape
    wp = proj.shape[1]
    nseq = _seqs_per_step(nb, tb // C)
    body = functools.partial(_gla_body, nseq=nseq, hgrn=hgrn, nh=nh, dk=dk, dv=dv, tb=tb, t_valid=t_valid, nt=nt)
    const = lambda b, t: (0, 0)
    o, s = pl.pallas_call(
        body, grid=(nb // nseq, nt),
        in_specs=_proj_specs(nseq, nt, tb, wp) + [
            pl.BlockSpec((nseq, nh, dk, dv), lambda b, t: (b, 0, 0, 0)),
            pl.BlockSpec(ng.shape, const), pl.BlockSpec(p1.shape, const), pl.BlockSpec(p2.shape, const)],
        out_specs=[pl.BlockSpec((nseq, tb, nh * dv), lambda b, t: (b, t, 0)),
                   pl.BlockSpec((nseq, nh, dk, dv), lambda b, t: (b, 0, 0, 0))],
        out_shape=[jax.ShapeDtypeStruct((nb, nt * tb, nh * dv), F32), jax.ShapeDtypeStruct(s0.shape, F32)],
        scratch_shapes=[pltpu.VMEM((nseq, nh, dv, dk), F32)],
        compiler_params=_cparams(("parallel", "arbitrary")), name="hgrn_scan" if hgrn else "gla_scan",
    )(*([proj] * nseq), s0, ng, p1, p2)
    return o.reshape(nb * nt * tb, nh * dv), s


def _gdn_body(*refs, nseq, nh, dk, dv, tb, t_valid, nt):
    proj_refs = refs[:nseq]
    (s0_ref, conv0_ref, cw_ref, ng_ref, alog_ref, dtb_ref, o_ref, sout_ref, convout_ref,
     st_scr, cbuf, ybuf) = refs[nseq:]
    t = pl.program_id(1)
    cwid = 2 * nh * dk + nh * dv
    pre = 8 - (CONV_W - 1)

    @pl.when(t == 0)
    def _():
        for sq in range(nseq):
            for h in range(nh):
                st_scr[sq, :, h * dv:(h + 1) * dv] = s0_ref[sq, h]
            cbuf[sq, 0:8, :] = jnp.zeros((8, cwid), F32)
            cbuf[sq, pre:8, :] = conv0_ref[sq]

    for sq in range(nseq):
        cbuf[sq, 8:8 + tb, :] = proj_refs[sq][:, 0:cwid]
        y = cbuf[sq, pre:pre + tb, :] * cw_ref[0:1, :]
        for j in range(1, CONV_W):
            y = y + cbuf[sq, pre + j:pre + j + tb, :] * cw_ref[j:j + 1, :]
        ybuf[sq] = jax.nn.silu(y)

    nr = nh * C
    ri = _iota((nr, 1), 0)
    cj = _iota((1, nr), 1)
    rhead = ri // C
    same = rhead == cj // C
    incl = same & (ri % C >= cj % C)
    strict = same & (ri % C > cj % C)
    eye = (ri == cj).astype(F32)

    def stack(ref, rows, base, width):
        return jnp.concatenate([ref[rows, base + h * width:base + (h + 1) * width] for h in range(nh)], axis=0)

    def own_block(x):
        return _tree(jnp.add, [jnp.where(rhead == h, x[:, h * dv:(h + 1) * dv], 0.0) for h in range(nh)])

    prep = []
    for sq, ci in [(sq, ci) for sq in range(nseq) for ci in range(tb // C)]:
        rows = slice(ci * C, (ci + 1) * C)
        valid = _valid_rows(t, ci * C, tb, t_valid, nt)
        small = proj_refs[sq][rows, cwid + nh * dv:cwid + nh * dv + LANE]
        g_all = -jnp.exp(alog_ref[...]) * jax.nn.softplus(small + dtb_ref[...])
        beta_all = jax.nn.sigmoid(small)
        if valid is not None:
            g_all = jnp.where(valid, g_all, 0.0)
            beta_all = jnp.where(valid, beta_all, 0.0)
        b_all = _cumsum_rows(g_all)
        b_t = _rows_to_cols(b_all)
        bcol = jnp.concatenate([b_all[:, h:h + 1] for h in range(nh)], axis=0)
        brow = jnp.concatenate([b_t[h:h + 1, :] for h in range(nh)], axis=1)
        beta = jnp.concatenate([beta_all[:, nh + h:nh + h + 1] for h in range(nh)], axis=0)
        bend = jnp.concatenate([jnp.broadcast_to(b_all[C - 1:C, h:h + 1], (C, 1)) for h in range(nh)], axis=0)
        dec_row = jnp.concatenate([jnp.broadcast_to(jnp.exp(b_all[C - 1:C, h:h + 1]), (1, dv)) for h in range(nh)],
                                  axis=1)
        q = stack(ybuf.at[sq], rows, 0, dk)
        k = stack(ybuf.at[sq], rows, nh * dk, dk)
        v = stack(ybuf.at[sq], rows, 2 * nh * dk, dv)
        q = q * lax.rsqrt(jnp.sum(q * q, -1, keepdims=True) + NORM_EPS) * dk ** -0.5
        k = k * lax.rsqrt(jnp.sum(k * k, -1, keepdims=True) + NORM_EPS)
        if valid is not None:
            valid_st = jnp.concatenate([valid] * nh, axis=0)
            q = jnp.where(valid_st, q, 0.0)
            k = jnp.where(valid_st, k, 0.0)
            v = jnp.where(valid_st, v, 0.0)
        decay = jnp.exp(jnp.where(incl, bcol - brow, -jnp.inf))
        eb = jnp.exp(bcol)
        ks = _split(k)
        prep.append(dict(a=jnp.where(strict, _dot3(ks, ks, nt=True) * decay * beta, 0.0), bv=beta * v,
                         bk=(beta * eb) * k, att=_dot_nt(q, k) * decay, qe=q * eb, kd=k * jnp.exp(bend - bcol),
                         dec_row=dec_row, sq=sq, ci=ci))
    for p in prep:
        a_s = _split(p['a'])
        p['a2'] = _dot3(a_s, a_s)
    for p in prep:
        a2_s = _split(p['a2'])
        p['a4'] = _dot3(a2_s, a2_s)
        p['lo'] = _dot3(_split(eye - p['a']), _split(eye + p['a2']))
    for p in prep:
        a4_s = _split(p['a4'])
        p['a8'] = _dot3(a4_s, a4_s)
    for p in prep:
        p['hi'] = _dot3(_split(eye + p['a4']), _split(eye + p['a8']))
    for p in prep:
        p['tinv'] = _split(_dot3(_split(p['lo']), _split(p['hi'])))
    for p in prep:
        p['u'] = _dot3(p['tinv'], _split(p['bv']))
        p['wk'] = _dot3(p['tinv'], _split(p['bk']))

    s_cat = [st_scr[sq] for sq in range(nseq)]
    for p in sorted(prep, key=lambda p: (p['ci'], p['sq'])):
        sq = p['sq']
        rows = slice(p['ci'] * C, (p['ci'] + 1) * C)
        v_new = p['u'] - own_block(_dot(p['wk'], s_cat[sq]))
        o = _dot(p['att'], v_new) + own_block(_dot(p['qe'], s_cat[sq]))
        v_wide = jnp.concatenate([jnp.where(rhead == h, v_new, 0.0) for h in range(nh)], axis=1)
        s_cat[sq] = s_cat[sq] * p['dec_row'] + _dot_tn(p['kd'], v_wide)
        o = o * lax.rsqrt(jnp.mean(o * o, -1, keepdims=True) + NORM_EPS) * ng_ref[...]
        for h in range(nh):
            gate = proj_refs[sq][rows, cwid + h * dv:cwid + (h + 1) * dv]
            o_ref[sq, rows, h * dv:(h + 1) * dv] = o[h * C:(h + 1) * C, :] * jax.nn.silu(gate)
    for sq in range(nseq):
        st_scr[sq] = s_cat[sq]

    @pl.when(t < nt - 1)
    def _():
        for sq in range(nseq):
            cbuf[sq, pre:8, :] = cbuf[sq, pre + tb:8 + tb, :]

    @pl.when(t == nt - 1)
    def _():
        last = t_valid - (nt - 1) * tb
        for sq in range(nseq):
            for h in range(nh):
                sout_ref[sq, h] = st_scr[sq, :, h * dv:(h + 1) * dv]
            convout_ref[sq] = cbuf[sq, pre + last:8 + last, :]


def _gdn_scan(proj, s0, conv0, cw, ng, alog, dtb, *, nb, nt, tb, t_valid):
    _, nh, dk, dv = s0.shape
    wp = proj.shape[1]
    cwid = 2 * nh * dk + nh * dv
    nseq = _seqs_per_step(nb, tb // C)
    body = functools.partial(_gdn_body, nseq=nseq, nh=nh, dk=dk, dv=dv, tb=tb, t_valid=t_valid, nt=nt)
    const = lambda b, t: (0, 0)
    o, s, conv = pl.pallas_call(
        body, grid=(nb // nseq, nt),
        in_specs=_proj_specs(nseq, nt, tb, wp) + [
            pl.BlockSpec((nseq, nh, dk, dv), lambda b, t: (b, 0, 0, 0)),
            pl.BlockSpec((nseq, CONV_W - 1, cwid), lambda b, t: (b, 0, 0)),
            pl.BlockSpec(cw.shape, const), pl.BlockSpec(ng.shape, const),
            pl.BlockSpec(alog.shape, const), pl.BlockSpec(dtb.shape, const)],
        out_specs=[pl.BlockSpec((nseq, tb, nh * dv), lambda b, t: (b, t, 0)),
                   pl.BlockSpec((nseq, nh, dk, dv), lambda b, t: (b, 0, 0, 0)),
                   pl.BlockSpec((nseq, CONV_W - 1, cwid), lambda b, t: (b, 0, 0))],
        out_shape=[jax.ShapeDtypeStruct((nb, nt * tb, nh * dv), F32), jax.ShapeDtypeStruct(s0.shape, F32),
                   jax.ShapeDtypeStruct(conv0.shape, F32)],
        scratch_shapes=[pltpu.VMEM((nseq, dk, nh * dv), F32), pltpu.VMEM((nseq, tb + 8, cwid), F32),
                        pltpu.VMEM((nseq, tb, cwid), F32)],
        compiler_params=_cparams(("parallel", "arbitrary")), name="gdn_scan",
    )(*([proj] * nseq), s0, conv0, cw, ng, alog, dtb)
    return o.reshape(nb * nt * tb, nh * dv), s, conv


def _mlstm_body(*refs, nseq, nh, dk, dv, tb, t_valid, nt):
    proj_refs = refs[:nseq]
    (c0_ref, n0_ref, m0_ref, ng_ref, bi_ref, bf_ref, o_ref, cout_ref, nout_ref, mout_ref,
     ct_scr, n_scr, m_scr) = refs[nseq:]
    t = pl.program_id(1)

    @pl.when(t == 0)
    def _():
        for sq in range(nseq):
            for h in range(nh):
                ct_scr[sq, :, h * dv:(h + 1) * dv] = c0_ref[sq, h]
                n_scr[sq, h] = n0_ref[sq, h:h + 1, :]
                m_scr[sq, h] = jnp.broadcast_to(m0_ref[sq, :, h:h + 1], (1, LANE))

    base = 2 * nh * dk + 2 * nh * dv
    nr = nh * C
    ri = _iota((nr, 1), 0)
    cj = _iota((1, nr), 1)
    rhead = ri // C
    incl = (rhead == cj // C) & (ri % C >= cj % C)

    def stack(ref, rows, col0, width):
        return jnp.concatenate([ref[rows, col0 + h * width:col0 + (h + 1) * width] for h in range(nh)], axis=0)

    def per_head_col(vals):
        return jnp.concatenate([jnp.broadcast_to(x, (C, 1)) for x in vals], axis=0)

    prep = {}
    for sq, ci in [(sq, ci) for sq in range(nseq) for ci in range(tb // C)]:
        proj_ref = proj_refs[sq]
        rows = slice(ci * C, (ci + 1) * C)
        valid = _valid_rows(t, ci * C, tb, t_valid, nt)
        small = proj_ref[rows, base:base + LANE]
        li_all = small + bi_ref[...]
        lf_all = jax.nn.log_sigmoid(small + bf_ref[...])
        if valid is not None:
            lf_all = jnp.where(valid, lf_all, 0.0)
        b_all = _cumsum_rows(lf_all)
        b_t = _rows_to_cols(b_all)
        li_t = _rows_to_cols(li_all)
        bcol = jnp.concatenate([b_all[:, nh + h:nh + h + 1] for h in range(nh)], axis=0)
        licol = jnp.concatenate([li_all[:, h:h + 1] for h in range(nh)], axis=0)
        xrow = jnp.concatenate([b_t[nh + h:nh + h + 1, :] - li_t[h:h + 1, :] for h in range(nh)], axis=1)
        if valid is not None:
            xrow = jnp.where(t * tb + ci * C + cj % C < t_valid, xrow, jnp.inf)
            licol = jnp.where(t * tb + ci * C + ri % C < t_valid, licol, -jnp.inf)
        q = stack(proj_ref, rows, 0, dk)
        k = stack(proj_ref, rows, nh * dk, dk) * dk ** -0.5
        v = stack(proj_ref, rows, 2 * nh * dk, dv)
        dmat = jnp.where(incl, bcol - xrow, -jnp.inf)
        prep[sq, ci] = dict(q=q, k=k, v=v, bcol=bcol, licol=licol, dmat=dmat, qk=_dot_nt(q, k),
                            rowmax=jnp.max(dmat, axis=1, keepdims=True),
                            b_end=[b_all[C - 1:C, nh + h:nh + h + 1] for h in range(nh)])

    c_cat = [ct_scr[sq] for sq in range(nseq)]
    n_rows = [[n_scr[sq, h] for h in range(nh)] for sq in range(nseq)]
    m_vals = [[m_scr[sq, h][:, 0:1] for h in range(nh)] for sq in range(nseq)]
    for ci, sq in [(ci, sq) for ci in range(tb // C) for sq in range(nseq)]:
        p = prep[sq, ci]
        rows = slice(ci * C, (ci + 1) * C)
        q, k, v, bcol = p['q'], p['k'], p['v'], p['bcol']
        m_inter = bcol + per_head_col(m_vals[sq])
        m_t = jnp.maximum(m_inter, p['rowmax'])
        a_int = jnp.exp(m_inter - m_t)
        qk = p['qk'] * jnp.exp(p['dmat'] - m_t)
        qc = _dot(q, c_cat[sq])
        own = _tree(jnp.add, [jnp.where(rhead == h, qc[:, h * dv:(h + 1) * dv], 0.0) for h in range(nh)])
        num = _dot(qk, v) + a_int * own
        n_st = jnp.concatenate([jnp.broadcast_to(n_rows[sq][h], (C, dk)) for h in range(nh)], axis=0)
        den = jnp.sum(qk, axis=1, keepdims=True) + a_int * jnp.sum(q * n_st, axis=1, keepdims=True)
        hid = num / jnp.maximum(jnp.abs(den), jnp.exp(-m_t))
        last = [slice(h * C + C - 1, h * C + C) for h in range(nh)]
        m_vals[sq] = [m_t[r, :] for r in last]
        a_end = [a_int[r, :] for r in last]
        wk = jnp.exp(per_head_col(p['b_end']) - bcol + p['licol'] - per_head_col(m_vals[sq])) * k
        v_wide = jnp.concatenate([jnp.where(rhead == h, v, 0.0) for h in range(nh)], axis=1)
        a_row = jnp.concatenate([jnp.broadcast_to(a, (1, dv)) for a in a_end], axis=1)
        c_cat[sq] = c_cat[sq] * a_row + _dot_tn(wk, v_wide)
        n_rows[sq] = [a_end[h] * n_rows[sq][h] + jnp.sum(wk[h * C:(h + 1) * C, :], axis=0, keepdims=True)
                      for h in range(nh)]
        hc = hid - jnp.mean(hid, -1, keepdims=True)
        hn = hc * lax.rsqrt(jnp.mean(hc * hc, -1, keepdims=True) + NORM_EPS)
        for h in range(nh):
            og = jax.nn.sigmoid(
                proj_refs[sq][rows, 2 * nh * dk + nh * dv + h * dv:2 * nh * dk + nh * dv + (h + 1) * dv])
            o_ref[sq, rows, h * dv:(h + 1) * dv] = og * (hn[h * C:(h + 1) * C, :] * ng_ref[:, h * dv:(h + 1) * dv])
    for sq in range(nseq):
        ct_scr[sq] = c_cat[sq]
        for h in range(nh):
            n_scr[sq, h] = n_rows[sq][h]
            m_scr[sq, h] = jnp.broadcast_to(m_vals[sq][h], (1, LANE))

    @pl.when(t == nt - 1)
    def _():
        lane = _iota((1, nh), 1)
        for sq in range(nseq):
            mrow = jnp.zeros((1, nh), F32)
            for h in range(nh):
                cout_ref[sq, h] = ct_scr[sq, :, h * dv:(h + 1) * dv]
                nout_ref[sq, h:h + 1, :] = n_scr[sq, h]
                mrow = jnp.where(lane == h, m_scr[sq, h][:, 0:1], mrow)
            mout_ref[sq] = mrow


def _mlstm_scan(proj, c0, n0, m0, ng, bi, bf, *, nb, nt, tb, t_valid):
    _, nh, dk, dv = c0.shape
    wp = proj.shape[1]
    nseq = _seqs_per_step(nb, tb // C)
    body = functools.partial(_mlstm_body, nseq=nseq, nh=nh, dk=dk, dv=dv, tb=tb, t_valid=t_valid, nt=nt)
    const = lambda b, t: (0, 0)
    m0 = m0.reshape(nb, 1, nh)
    o, c, n, m = pl.pallas_call(
        body, grid=(nb // nseq, nt),
        in_specs=_proj_specs(nseq, nt, tb, wp) + [
            pl.BlockSpec((nseq, nh, dk, dv), lambda b, t: (b, 0, 0, 0)),
            pl.BlockSpec((nseq, nh, dk), lambda b, t: (b, 0, 0)),
            pl.BlockSpec((nseq, 1, nh), lambda b, t: (b, 0, 0)),
            pl.BlockSpec(ng.shape, const), pl.BlockSpec(bi.shape, const), pl.BlockSpec(bf.shape, const)],
        out_specs=[pl.BlockSpec((nseq, tb, nh * dv), lambda b, t: (b, t, 0)),
                   pl.BlockSpec((nseq, nh, dk, dv), lambda b, t: (b, 0, 0, 0)),
                   pl.BlockSpec((nseq, nh, dk), lambda b, t: (b, 0, 0)),
                   pl.BlockSpec((nseq, 1, nh), lambda b, t: (b, 0, 0))],
        out_shape=[jax.ShapeDtypeStruct((nb, nt * tb, nh * dv), F32), jax.ShapeDtypeStruct(c0.shape, F32),
                   jax.ShapeDtypeStruct(n0.shape, F32), jax.ShapeDtypeStruct((nb, 1, nh), F32)],
        scratch_shapes=[pltpu.VMEM((nseq, dk, nh * dv), F32), pltpu.VMEM((nseq, nh, 1, dk), F32),
                        pltpu.VMEM((nseq, nh, 1, LANE), F32)],
        compiler_params=_cparams(("parallel", "arbitrary")), name="mlstm_scan",
    )(*([proj] * nseq), c0, n0, m0, ng, bi, bf)
    return o.reshape(nb * nt * tb, nh * dv), c, n, m


SUB = 8


def _tree(op, xs):
    xs = list(xs)
    while len(xs) > 1:
        xs = [op(xs[i], xs[i + 1]) if i + 1 < len(xs) else xs[i] for i in range(0, len(xs), 2)]
    return xs[0]


def _all_sublanes(op, m):
    for shift in (4, 2, 1):
        m = op(m, pltpu.roll(m, shift, axis=0))
    return m


def _sorting_network(n):
    def merge(lo, hi, r):
        step = r * 2
        if step < hi - lo:
            yield from merge(lo, hi, step)
            yield from merge(lo + r, hi, step)
            yield from [(i, i + r) for i in range(lo + r, hi - r, step)]
        else:
            yield (lo, lo + r)

    def sort(lo, hi):
        if hi - lo >= 1:
            mid = lo + (hi - lo) // 2
            yield from sort(lo, mid)
            yield from sort(mid + 1, hi)
            yield from merge(lo, hi, 1)

    return list(sort(0, n - 1))


def _sort_tiles_desc(tiles):
    tiles = list(tiles)
    for i, j in _sorting_network(len(tiles)):
        tiles[i], tiles[j] = jnp.maximum(tiles[i], tiles[j]), jnp.minimum(tiles[i], tiles[j])
    return tiles


def _merge_top(lists, singles=None):
    lists = list(lists)
    vals = []
    for j in range(PEER_TOPK):
        head = lists[0] if singles is None else jnp.maximum(lists[0], singles)
        m = _all_sublanes(jnp.maximum, head)
        vals.append(m)
        if j == PEER_TOPK - 1:
            break
        hit = lists[0] == m
        for i in range(min(len(lists) - 1, PEER_TOPK - 1 - j)):
            lists[i] = jnp.where(hit, lists[i + 1], lists[i])
        if singles is not None:
            singles = jnp.where(singles == m, -jnp.inf, singles)
    return vals


def _route_tile(q_scr, k1_ref, k2_ref, r2_ref, c1_ref, a1_ref, a2_ref, h, lanes):
    ntile = PEER_NKEYS // SUB
    sub = _iota((SUB, LANE), 0)
    q1 = q_scr[pl.ds(pl.multiple_of(h * 2 * PEER_HALF, PEER_HALF), PEER_HALF), lanes]
    q2 = q_scr[pl.ds(pl.multiple_of(h * 2 * PEER_HALF + PEER_HALF, PEER_HALF), PEER_HALF), lanes]
    s1 = _dot(k1_ref[...], q1, HI)
    s2 = _dot(k2_ref[...], q2, HI)
    t1 = [s1[k * SUB:(k + 1) * SUB, :] for k in range(ntile)]
    t2 = [s2[k * SUB:(k + 1) * SUB, :] for k in range(ntile)]
    v1 = _merge_top(_sort_tiles_desc(t1))
    v2 = _merge_top(_sort_tiles_desc(t2))
    v2lo, v2hi = v2[SUB - 1], v2[2 * SUB - 1]
    for b in range(SUB - 2, -1, -1):
        v2lo = jnp.where(sub == b, v2[b], v2lo)
        v2hi = jnp.where(sub == b, v2[SUB + b], v2hi)
    cand_lo = [jnp.where(sub < min(PEER_TOPK // (a + 1), SUB), v1[a] + v2lo, -jnp.inf) for a in range(PEER_TOPK)]
    cand_hi = v1[0] + v2hi
    tau = _merge_top(cand_lo, cand_hi)[PEER_TOPK - 1]
    top = v1[0] + v2[0]
    z = _all_sublanes(jnp.add, _tree(jnp.add, [jnp.where(cd >= tau, jnp.exp(cd - top), 0.0)
                                               for cd in cand_lo + [cand_hi]]))
    inv_z = 1.0 / z
    pairs = [_all_sublanes(jnp.add, jnp.where(cd >= tau, 1.0, 0.0)) for cd in cand_lo]
    pairs[0] = pairs[0] + _all_sublanes(jnp.add, jnp.where(cand_hi >= tau, 1.0, 0.0))
    rank2, c1 = [], []
    for k in range(ntile):
        r = jnp.full((SUB, LANE), float(PEER_TOPK), F32)
        cnt = jnp.zeros((SUB, LANE), F32)
        for j in range(PEER_TOPK - 1, -1, -1):
            r = jnp.where(t2[k] >= v2[j], float(j), r)
            cnt = jnp.where(t1[k] == v1[j], pairs[j], cnt)
        rank2.append(r)
        c1.append(cnt)
    r2_ref[h, :, lanes] = jnp.concatenate(rank2, axis=0).astype(BF16)
    c1_ref[h, :, lanes] = jnp.concatenate(c1, axis=0)
    a1_ref[h, :, lanes] = jnp.concatenate([jnp.exp(t - v1[0]) for t in t1], axis=0)
    a2_ref[h, :, lanes] = jnp.concatenate([jnp.exp(t - v2[0]) * inv_z for t in t2], axis=0).astype(BF16)


def _peer_body(xm_ref, xr_ref, wq_ref, k1_ref, k2_ref, u_ref, vt_ref, g_ref, b_ref, out_ref,
               acc, xbf, p_scr, q_scr, r2_scr, c1_scr, a1_scr, a2_scr):
    r = pl.program_id(0)
    c = pl.program_id(1)
    tm = xbf.shape[1]

    @pl.when(c == 0)
    def _():
        xbf[...] = xm_ref[...].astype(BF16)
        acc[...] = jnp.zeros(acc.shape, F32)
        p_scr[...] = jnp.zeros(p_scr.shape, BF16)
        q_scr[...] = _dot(wq_ref[...], xr_ref[...].astype(BF16))

    @pl.when((c == 0) & (r == 0))
    def _():
        for scr in (r2_scr, c1_scr, a1_scr, a2_scr):
            scr[...] = jnp.zeros(scr.shape, scr.dtype)

    ntile = tm // LANE
    units = pl.num_programs(1) - 1
    tiles_per_unit = PEER_HEADS * ntile // units
    unit = (c + units - 1) % units
    h_route = unit // (ntile // tiles_per_unit)
    lane0 = (unit % (ntile // tiles_per_unit)) * tiles_per_unit * LANE
    wr = r % 2
    for sb in range(tiles_per_unit):
        _route_tile(q_scr, k1_ref, k2_ref, r2_scr.at[wr], c1_scr.at[wr], a1_scr.at[wr], a2_scr.at[wr], h_route,
                    pl.ds(pl.multiple_of(lane0 + sb * LANE, LANE), LANE))

    rd = (r + 1) % 2
    i1_base = jnp.minimum(c, pl.num_programs(1) - 2) * PEER_I1_PER_STEP
    rows16 = 2 * SUB
    acc[...] += _dot(vt_ref[0], p_scr[...])
    hid = _dot(u_ref[...], xbf[...])
    act = (0.5 * hid * (1.0 + lax.erf(hid * 0.5 ** 0.5))).astype(BF16)
    for l0 in range(0, tm, PEER_LANE_BLOCK):
        ls = slice(l0, l0 + PEER_LANE_BLOCK)
        for j in range(PEER_I1_PER_STEP):
            row = pl.ds(i1_base + j, 1)
            nkb = PEER_NKEYS // rows16
            w = [None] * nkb
            for h in range(PEER_HEADS):
                c1b = jnp.broadcast_to(c1_scr[rd, h, row, ls], (rows16, PEER_LANE_BLOCK)).astype(BF16)
                a1b = jnp.broadcast_to(a1_scr[rd, h, row, ls], (rows16, PEER_LANE_BLOCK)).astype(BF16)
                for kb in range(nkb):
                    ks = slice(kb * rows16, (kb + 1) * rows16)
                    term = jnp.where(r2_scr[rd, h, ks, ls] < c1b, a2_scr[rd, h, ks, ls] * a1b, 0.0)
                    w[kb] = term if w[kb] is None else w[kb] + term
            for kb in range(nkb):
                es = slice(j * PEER_NKEYS + kb * rows16, j * PEER_NKEYS + (kb + 1) * rows16)
                p_scr[es, ls] = w[kb] * act[es, ls]

    @pl.when(c == pl.num_programs(1) - 1)
    def _():
        z = DN_ALPHA * xm_ref[...] + acc[...]
        zc = z - jnp.mean(z, 0, keepdims=True)
        var = jnp.mean(zc * zc, 0, keepdims=True)
        out_ref[...] = (zc * lax.rsqrt(var + LN_EPS) * g_ref[...] + b_ref[...]).T


def _peer(xt, wq_t, k1, k2, u, vt, g, b, tm):
    d, n = xt.shape
    nb = n // tm
    ec = PEER_I1_PER_STEP * PEER_NKEYS
    nc = u.shape[0] // ec
    ntile = tm // LANE
    assert (PEER_HEADS * ntile) % nc == 0 and ntile % (PEER_HEADS * ntile // nc) == 0, "routing units per row"
    const = lambda r, c: (0, 0)
    rshape = (2, PEER_HEADS, PEER_NKEYS, tm)
    return pl.pallas_call(
        _peer_body, grid=(nb + 1, nc + 1),
        in_specs=[pl.BlockSpec((d, tm), lambda r, c: (0, jnp.maximum(r - 1, 0))),
                  pl.BlockSpec((d, tm), lambda r, c: (0, jnp.minimum(r, nb - 1))),
                  pl.BlockSpec(wq_t.shape, const), pl.BlockSpec(k1.shape, const), pl.BlockSpec(k2.shape, const),
                  pl.BlockSpec((ec, d), lambda r, c: (jnp.minimum(c, nc - 1), 0)),
                  pl.BlockSpec((1, d, ec), lambda r, c: (jnp.maximum(c - 1, 0), 0, 0)),
                  pl.BlockSpec((d, 1), const), pl.BlockSpec((d, 1), const)],
        out_specs=pl.BlockSpec((tm, d), lambda r, c: (jnp.maximum(r - 1, 0), 0)),
        out_shape=jax.ShapeDtypeStruct((n, d), F32),
        scratch_shapes=[pltpu.VMEM((d, tm), F32), pltpu.VMEM((d, tm), BF16), pltpu.VMEM((ec, tm), BF16),
                        pltpu.VMEM((wq_t.shape[0], tm), F32), pltpu.VMEM(rshape, BF16), pltpu.VMEM(rshape, F32),
                        pltpu.VMEM(rshape, F32), pltpu.VMEM(rshape, BF16)],
        compiler_params=pltpu.CompilerParams(dimension_semantics=("arbitrary", "arbitrary"),
                                             vmem_limit_bytes=PEER_VMEM_LIMIT), name="peer",
    )(xt, xt, wq_t, k1, k2, u, vt, g.reshape(d, 1), b.reshape(d, 1))


def _pad_cols(w, width):
    return jnp.pad(w, ((0, 0), (0, width - w.shape[1])))


def _lane_row(*parts):
    v = jnp.concatenate([p.astype(F32).reshape(-1) for p in parts])
    return jnp.pad(v, (0, LANE - v.shape[0])).reshape(1, LANE)


def _pick_tb(t_pad):
    for tb in (48, 32, 16):
        if t_pad % tb == 0:
            return tb
    raise ValueError(t_pad)


def kernel(x_prompt, x_sample, state_hgrn_S, state_gdn_S, state_gdn_conv, state_mlstm_C, state_mlstm_n, state_mlstm_m, state_gla_S, meta_tokens, hgrn_w_in, hgrn_lb, hgrn_norm_g, hgrn_w_out, gdn_w_in, gdn_conv_w, gdn_a_log, gdn_dt_bias, gdn_norm_g, gdn_w_out, mlstm_w_in, mlstm_b_i, mlstm_b_f, mlstm_norm_g, mlstm_w_out, gla_w_in, gla_w_gate, gla_b_gate, gla_norm_g, gla_w_out, peer_w_q, peer_keys1, peer_keys2, peer_u, peer_v, ln_g, ln_b):
    bp, seq, d = x_prompt.shape
    bs, seq_s, _ = x_sample.shape
    assert d == D_MODEL and len(state_hgrn_S) == 1 and len(state_gdn_S) == 1
    assert len(state_mlstm_C) == 1 and len(state_gla_S) == 1
    tp = N_META + seq
    tp_pad = -(-tp // C) * C
    ts_pad = -(-seq_s // C) * C
    tbp = _pick_tb(tp_pad)
    tbs = _pick_tb(ts_pad)
    np_rows = bp * tp_pad
    n_real = np_rows + bs * seq_s

    meta = jnp.broadcast_to(meta_tokens.astype(F32)[None], (bp, N_META, d))
    hp = jnp.concatenate([meta, x_prompt], axis=1)
    hp = jnp.pad(hp, ((0, 0), (0, tp_pad - tp), (0, 0))).reshape(np_rows, d)
    h = jnp.concatenate([hp, x_sample.reshape(bs * seq_s, d)], axis=0)
    n = -(-n_real // TOK_MULTIPLE) * TOK_MULTIPLE
    h = jnp.pad(h, ((0, n - n_real), (0, 0)))
    tm_mm = MM_TOK_BLOCK

    def split_cols(w, sizes):
        out, o = [], 0
        for s in sizes:
            out.append(w[:, o:o + s])
            o += s
        return out

    def run_mixer(scan, proj, states_p, states_s, **kw):
        res_p = scan(proj, *states_p, nb=bp, nt=tp_pad // tbp, tb=tbp, t_valid=tp, **kw)
        ps = proj[np_rows:n_real].reshape(bs, seq_s, -1)
        ps = jnp.pad(ps, ((0, 0), (0, ts_pad - seq_s), (0, 0))).reshape(bs * ts_pad, -1)
        res_s = scan(ps, *states_s, nb=bs, nt=ts_pad // tbs, tb=tbs, t_valid=seq_s, **kw)
        o_s = res_s[0].reshape(bs, ts_pad, -1)[:, :seq_s].reshape(bs * seq_s, -1)
        o = jnp.concatenate([res_p[0], o_s], axis=0)
        o = jnp.pad(o, ((0, n - n_real), (0, 0)))
        return o, res_p[1:], res_s[1:]

    lb_all = jnp.cumsum(jax.nn.softmax(hgrn_lb.astype(F32), axis=0), axis=0)
    outs = {}
    for i in range(DEPTH):
        mix = i % 4
        if mix == 0:
            w_in = hgrn_w_in[0].astype(BF16)
            proj = _mm(h, w_in, tm_mm, w_in.shape[1])
            nh, dk, dv = state_hgrn_S.shape[2:]
            zero = jnp.zeros((bp, nh, dk, dv), F32)
            scan = functools.partial(_gla_scan, ng=hgrn_norm_g[0].reshape(1, dv), p1=lb_all[i].reshape(1, nh * dk),
                                     p2=jnp.zeros((1, LANE), F32), hgrn=True)
            o, (sp,), (ss,) = run_mixer(scan, proj, (zero,), (state_hgrn_S[0],))
            outs['hgrn'] = (sp[None], ss[None])
            w_out = hgrn_w_out[0]
        elif mix == 1:
            nh, dk, dv = state_gdn_S.shape[2:]
            cwid = 2 * nh * dk + nh * dv
            qkv, a, b, g = split_cols(gdn_w_in[0], [cwid, nh, nh, nh * dv])
            w_in = jnp.concatenate([qkv, g, _pad_cols(jnp.concatenate([a, b], axis=1), LANE)], axis=1).astype(BF16)
            proj = _mm(h, w_in, tm_mm, w_in.shape[1])
            scan = functools.partial(_gdn_scan, cw=gdn_conv_w[0], ng=gdn_norm_g[0].reshape(1, dv),
                                     alog=_lane_row(gdn_a_log[0]), dtb=_lane_row(gdn_dt_bias[0]))
            o, (sp, cp), (ss, cs) = run_mixer(
                scan, proj,
                (jnp.zeros((bp, nh, dk, dv), F32), jnp.zeros((bp, CONV_W - 1, cwid), F32)),
                (state_gdn_S[0], state_gdn_conv[0]))
            outs['gdn'] = (sp[None], cp[None], ss[None], cs[None])
            w_out = gdn_w_out[0]
        elif mix == 2:
            nh, dk, dv = state_mlstm_C.shape[2:]
            q, k, v, ig, fg, og = split_cols(mlstm_w_in[0], [nh * dk, nh * dk, nh * dv, nh, nh, nh * dv])
            w_in = jnp.concatenate([q, k, v, og, _pad_cols(jnp.concatenate([ig, fg], axis=1), LANE)],
                                   axis=1).astype(BF16)
            proj = _mm(h, w_in, tm_mm, w_in.shape[1])
            zeros_nh = jnp.zeros((nh,), F32)
            scan = functools.partial(_mlstm_scan, ng=mlstm_norm_g[0].reshape(1, nh * dv),
                                     bi=_lane_row(mlstm_b_i[0]), bf=_lane_row(zeros_nh, mlstm_b_f[0]))
            o, (cp, npp, mp), (cs, ns, ms) = run_mixer(
                scan, proj,
                (jnp.zeros((bp, nh, dk, dv), F32), jnp.zeros((bp, nh, dk), F32), jnp.zeros((bp, nh), F32)),
                (state_mlstm_C[0], state_mlstm_n[0], state_mlstm_m[0]))
            outs['mlstm'] = (cp[None], npp[None], mp.reshape(1, bp, nh), cs[None], ns[None], ms.reshape(1, bs, nh))
            w_out = mlstm_w_out[0]
        else:
            nh, dk, dv = state_gla_S.shape[2:]
            rank = gla_w_gate.shape[1]
            q, k, v, g, r = split_cols(gla_w_in[0], [nh * dk, nh * dk, nh * dv, nh * dv, rank])
            w_in = jnp.concatenate([q, k, v, g, _pad_cols(r, LANE)], axis=1).astype(BF16)
            proj = _mm(h, w_in, tm_mm, w_in.shape[1])
            wg = jnp.pad(gla_w_gate[0].astype(F32), ((0, LANE - rank), (0, 0)))
            scan = functools.partial(_gla_scan, ng=gla_norm_g[0].reshape(1, dv), p1=wg,
                                     p2=gla_b_gate[0].reshape(1, nh * dk), hgrn=False)
            o, (sp,), (ss,) = run_mixer(scan, proj, (jnp.zeros((bp, nh, dk, dv), F32),), (state_gla_S[0],))
            outs['gla'] = (sp[None], ss[None])
            w_out = gla_w_out[0]

        h1t = _mm_res_ln_t(o, w_out.astype(BF16), h, ln_g[i, 0], ln_b[i, 0], TOK_BLOCK)
        ec = PEER_I1_PER_STEP * PEER_NKEYS
        vt = peer_v[i].astype(BF16).reshape(-1, ec, d).transpose(0, 2, 1)
        h = _peer(h1t, peer_w_q[i].T.astype(BF16), peer_keys1[i], peer_keys2[i], peer_u[i].astype(BF16), vt,
                  ln_g[i, 1], ln_b[i, 1], PEER_TOK_BLOCK)

    y_prompt = h[:np_rows].reshape(bp, tp_pad, d)[:, N_META:tp]
    y_sample = h[np_rows:n_real].reshape(bs, seq_s, d)
    hg, gd, ml, gl = outs['hgrn'], outs['gdn'], outs['mlstm'], outs['gla']
    return (y_prompt, y_sample, hg[0], gd[0], gd[1], ml[0], ml[1], ml[2], gl[0],
            hg[1], gd[2], gd[3], ml[3], ml[4], ml[5], gl[1])
```

```python
import functools

import jax
import jax.numpy as jnp
from jax import lax
from jax.experimental import pallas as pl
from jax.experimental.pallas import tpu as pltpu

F32 = jnp.float32
BF16 = jnp.bfloat16
HI = lax.Precision.HIGHEST

D_MODEL = 1024
DEPTH = 4
N_META = 16
C = 16
CONV_W = 4
NORM_EPS = 1e-6
LN_EPS = 1e-5
DN_ALPHA = (2 * DEPTH) ** 0.25
GLA_GATE_NORM = 16.0
LANE = 128
PEER_HEADS = 8
PEER_NKEYS = 128
PEER_TOPK = 16
PEER_HALF = 128
PEER_I1_PER_STEP = 8
TOK_BLOCK = 512
PEER_TOK_BLOCK = 512
PEER_LANE_BLOCK = 256
MM_TOK_BLOCK = 512
TOK_MULTIPLE = 1536
VMEM_LIMIT = 48 * 1024 * 1024
PEER_VMEM_LIMIT = 56 * 1024 * 1024


def _cparams(sem):
    return pltpu.CompilerParams(dimension_semantics=sem, vmem_limit_bytes=VMEM_LIMIT)


def _iota(shape, dim):
    return lax.broadcasted_iota(jnp.int32, shape, dim)


def _dot_nt(a, b, precision=None):
    return lax.dot_general(a, b, (((1,), (1,)), ((), ())), precision=precision, preferred_element_type=F32)


def _dot_tn(a, b, precision=None):
    return lax.dot_general(a, b, (((0,), (0,)), ((), ())), precision=precision, preferred_element_type=F32)


def _dot(a, b, precision=None):
    return jnp.dot(a, b, precision=precision, preferred_element_type=F32)


def _split(x):
    hi = x.astype(BF16)
    return hi, (x - hi.astype(F32)).astype(BF16)


def _dot3(a, b, nt=False):
    (ah, al), (bh, bl) = a, b
    d = _dot_nt if nt else _dot
    return d(ah, bh) + d(ah, bl) + d(al, bh)


def _cumsum_rows(x):
    tri = (_iota((C, C), 0) >= _iota((C, C), 1)).astype(F32)
    return _dot(tri, x, HI)


def _rows_to_cols(x):
    eye = (_iota((LANE, LANE), 0) == _iota((LANE, LANE), 1)).astype(F32)
    return _dot_nt(eye, x, HI)


def _mm_body(x_ref, w_ref, o_ref):
    o_ref[...] = _dot(x_ref[...].astype(BF16), w_ref[...])


def _mm(x, w, tm, tn):
    m, k = x.shape
    n = w.shape[1]
    return pl.pallas_call(
        _mm_body, grid=(m // tm, n // tn),
        in_specs=[pl.BlockSpec((tm, k), lambda i, j: (i, 0)), pl.BlockSpec((k, tn), lambda i, j: (0, j))],
        out_specs=pl.BlockSpec((tm, tn), lambda i, j: (i, j)),
        out_shape=jax.ShapeDtypeStruct((m, n), F32),
        compiler_params=_cparams(("parallel", "parallel")), name="mm")(x, w)


def _mm_ln_body(o_ref, w_ref, h_ref, g_ref, b_ref, out_ref):
    y = _dot(o_ref[...].astype(BF16), w_ref[...])
    z = DN_ALPHA * h_ref[...] + y
    zc = z - jnp.mean(z, -1, keepdims=True)
    var = jnp.mean(zc * zc, -1, keepdims=True)
    out_ref[...] = (zc * lax.rsqrt(var + LN_EPS) * g_ref[...] + b_ref[...]).T


def _mm_res_ln_t(o, w, h, g, b, tm):
    n, k = o.shape
    d = w.shape[1]
    return pl.pallas_call(
        _mm_ln_body, grid=(n // tm,),
        in_specs=[pl.BlockSpec((tm, k), lambda i: (i, 0)), pl.BlockSpec((k, d), lambda i: (0, 0)),
                  pl.BlockSpec((tm, d), lambda i: (i, 0)), pl.BlockSpec((1, d), lambda i: (0, 0)),
                  pl.BlockSpec((1, d), lambda i: (0, 0))],
        out_specs=pl.BlockSpec((d, tm), lambda i: (0, i)),
        out_shape=jax.ShapeDtypeStruct((d, n), F32),
        compiler_params=_cparams(("parallel",)), name="mm_res_ln")(o, w, h, g.reshape(1, d), b.reshape(1, d))


def _valid_rows(t, r0, tb, t_valid, nt):
    if t_valid == nt * tb:
        return None
    return (t * tb + r0 + _iota((C, 1), 0)) < t_valid


def _gla_body(*refs, nseq, hgrn, nh, dk, dv, tb, t_valid, nt):
    proj_refs = refs[:nseq]
    s0_ref, ng_ref, p1_ref, p2_ref, o_ref, sout_ref, st_scr = refs[nseq:]
    t = pl.program_id(1)

    @pl.when(t == 0)
    def _():
        for sq in range(nseq):
            for h in range(nh):
                st_scr[sq, h] = s0_ref[sq, h].T

    kw = nh * dk
    vw = nh * dv
    nr = nh * C
    ri = _iota((nr, 1), 0)
    cj = _iota((1, nr), 1)
    rhead = ri // C
    tri = ((rhead == cj // C) & (ri % C >= cj % C)).astype(F32)
    row3 = _iota((1, C, 1), 1)

    def stack(ref, rows, col0, width):
        return jnp.concatenate([ref[rows, col0 + h * width:col0 + (h + 1) * width] for h in range(nh)], axis=0)

    prep = {}
    for sq, proj_ref in enumerate(proj_refs):
        for ci in range(tb // C):
            rows = slice(ci * C, (ci + 1) * C)
            if hgrn:
                q = jax.nn.silu(stack(proj_ref, rows, 0, dk))
                lb = jnp.concatenate([jnp.broadcast_to(p1_ref[:, h * dk:(h + 1) * dk], (C, dk)) for h in range(nh)],
                                     axis=0)
                fg = lb + (1.0 - lb) * jax.nn.sigmoid(stack(proj_ref, rows, kw, dk))
                k = 1.0 - fg
                g = jnp.log(fg)
            else:
                r = proj_ref[rows, 2 * kw + 2 * vw:2 * kw + 2 * vw + LANE]
                logf_all = jax.nn.log_sigmoid(_dot(r, p1_ref[...], HI) + p2_ref[...]) / GLA_GATE_NORM
                q = stack(proj_ref, rows, 0, dk) * dk ** -0.5
                k = stack(proj_ref, rows, kw, dk)
                g = jnp.concatenate([logf_all[:, h * dk:(h + 1) * dk] for h in range(nh)], axis=0)
            v = stack(proj_ref, rows, 2 * kw, dv)
            if t_valid != nt * tb:
                valid = t * tb + ci * C + ri % C < t_valid
                k = jnp.where(valid, k, 0.0)
                g = jnp.where(valid, g, 0.0)
            b = _dot(tri, g, HI)
            b3, q3, k3, v3 = (x.reshape(nh, C, x.shape[-1]) for x in (b, q, k, v))
            o3 = jnp.zeros((nh, C, dv), F32)
            for s in range(C):
                d = jnp.exp(jnp.where(row3 >= s, b3 - b3[:, s:s + 1, :], -jnp.inf))
                col = jnp.sum(q3 * (k3[:, s:s + 1, :] * d), axis=-1, keepdims=True)
                o3 = o3 + col * v3[:, s:s + 1, :]
            b_end = b3[:, C - 1:C, :]
            prep[sq, ci] = dict(o=o3.reshape(nr, dv), qe=q * jnp.exp(b), kd=(k3 * jnp.exp(b_end - b3)).reshape(nr, dk),
                                dec=jnp.exp(b_end), v=v)

    sts = [st_scr[sq] for sq in range(nseq)]
    for ci in range(tb // C):
        rows = slice(ci * C, (ci + 1) * C)
        for sq in range(nseq):
            p = prep[sq, ci]
            inter = _dot_nt(p['qe'], sts[sq].reshape(nh * dv, dk))
            o = p['o'] + _tree(jnp.add, [jnp.where(rhead == h, inter[:, h * dv:(h + 1) * dv], 0.0)
                                         for h in range(nh)])
            v_wide = jnp.concatenate([jnp.where(rhead == h, p['v'], 0.0) for h in range(nh)], axis=1)
            sts[sq] = sts[sq] * p['dec'] + _dot_tn(v_wide, p['kd']).reshape(nh, dv, dk)
            o = o * lax.rsqrt(jnp.mean(o * o, -1, keepdims=True) + NORM_EPS) * ng_ref[...]
            for h in range(nh):
                gate = proj_refs[sq][rows, 2 * kw + vw + h * dv:2 * kw + vw + (h + 1) * dv]
                o_ref[sq, rows, h * dv:(h + 1) * dv] = o[h * C:(h + 1) * C, :] * jax.nn.silu(gate)
    for sq in range(nseq):
        st_scr[sq] = sts[sq]

    @pl.when(t == nt - 1)
    def _():
        for sq in range(nseq):
            for h in range(nh):
                sout_ref[sq, h] = st_scr[sq, h].T


def _seqs_per_step(nb, chunks):
    return max(n for n in (8, 4, 2, 1) if nb % n == 0 and n * chunks <= 8)


def _proj_specs(nseq, nt, tb, wp):
    return [pl.BlockSpec((tb, wp), functools.partial(lambda b, t, sq: ((b * nseq + sq) * nt + t, 0), sq=sq))
            for sq in range(nseq)]


def _gla_scan(proj, s0, ng, p1, p2, *, hgrn, nb, nt, tb, t_valid):
    _, nh, dk, dv = s0.shape
    wp = proj.shape[1]
    nseq = _seqs_per_step(nb, tb // C)
    body = functools.partial(_gla_body, nseq=nseq, hgrn=hgrn, nh=nh, dk=dk, dv=dv, tb=tb, t_valid=t_valid, nt=nt)
    const = lambda b, t: (0, 0)
    o, s = pl.pallas_call(
        body, grid=(nb // nseq, nt),
        in_specs=_proj_specs(nseq, nt, tb, wp) + [
            pl.BlockSpec((nseq, nh, dk, dv), lambda b, t: (b, 0, 0, 0)),
            pl.BlockSpec(ng.shape, const), pl.BlockSpec(p1.shape, const), pl.BlockSpec(p2.shape, const)],
        out_specs=[pl.BlockSpec((nseq, tb, nh * dv), lambda b, t: (b, t, 0)),
                   pl.BlockSpec((nseq, nh, dk, dv), lambda b, t: (b, 0, 0, 0))],
        out_shape=[jax.ShapeDtypeStruct((nb, nt * tb, nh * dv), F32), jax.ShapeDtypeStruct(s0.shape, F32)],
        scratch_shapes=[pltpu.VMEM((nseq, nh, dv, dk), F32)],
        compiler_params=_cparams(("parallel", "arbitrary")), name="hgrn_scan" if hgrn else "gla_scan",
    )(*([proj] * nseq), s0, ng, p1, p2)
    return o.reshape(nb * nt * tb, nh * dv), s


def _gdn_body(*refs, nseq, nh, dk, dv, tb, t_valid, nt):
    proj_refs = refs[:nseq]
    (s0_ref, conv0_ref, cw_ref, ng_ref, alog_ref, dtb_ref, o_ref, sout_ref, convout_ref,
     st_scr, cbuf, ybuf) = refs[nseq:]
    t = pl.program_id(1)
    cwid = 2 * nh * dk + nh * dv
    pre = 8 - (CONV_W - 1)

    @pl.when(t == 0)
    def _():
        for sq in range(nseq):
            for h in range(nh):
                st_scr[sq, :, h * dv:(h + 1) * dv] = s0_ref[sq, h]
            cbuf[sq, 0:8, :] = jnp.zeros((8, cwid), F32)
            cbuf[sq, pre:8, :] = conv0_ref[sq]

    for sq in range(nseq):
        cbuf[sq, 8:8 + tb, :] = proj_refs[sq][:, 0:cwid]
        y = cbuf[sq, pre:pre + tb, :] * cw_ref[0:1, :]
        for j in range(1, CONV_W):
            y = y + cbuf[sq, pre + j:pre + j + tb, :] * cw_ref[j:j + 1, :]
        ybuf[sq] = jax.nn.silu(y)

    nr = nh * C
    ri = _iota((nr, 1), 0)
    cj = _iota((1, nr), 1)
    rhead = ri // C
    same = rhead == cj // C
    incl = same & (ri % C >= cj % C)
    strict = same & (ri % C > cj % C)
    eye = (ri == cj).astype(F32)

    def stack(ref, rows, base, width):
        return jnp.concatenate([ref[rows, base + h * width:base + (h + 1) * width] for h in range(nh)], axis=0)

    def own_block(x):
        return _tree(jnp.add, [jnp.where(rhead == h, x[:, h * dv:(h + 1) * dv], 0.0) for h in range(nh)])

    prep = []
    for sq, ci in [(sq, ci) for sq in range(nseq) for ci in range(tb // C)]:
        rows = slice(ci * C, (ci + 1) * C)
        valid = _valid_rows(t, ci * C, tb, t_valid, nt)
        small = proj_refs[sq][rows, cwid + nh * dv:cwid + nh * dv + LANE]
        g_all = -jnp.exp(alog_ref[...]) * jax.nn.softplus(small + dtb_ref[...])
        beta_all = jax.nn.sigmoid(small)
        if valid is not None:
            g_all = jnp.where(valid, g_all, 0.0)
            beta_all = jnp.where(valid, beta_all, 0.0)
        b_all = _cumsum_rows(g_all)
        b_t = _rows_to_cols(b_all)
        bcol = jnp.concatenate([b_all[:, h:h + 1] for h in range(nh)], axis=0)
        brow = jnp.concatenate([b_t[h:h + 1, :] for h in range(nh)], axis=1)
        beta = jnp.concatenate([beta_all[:, nh + h:nh + h + 1] for h in range(nh)], axis=0)
        bend = jnp.concatenate([jnp.broadcast_to(b_all[C - 1:C, h:h + 1], (C, 1)) for h in range(nh)], axis=0)
        dec_row = jnp.concatenate([jnp.broadcast_to(jnp.exp(b_all[C - 1:C, h:h + 1]), (1, dv)) for h in range(nh)],
                                  axis=1)
        q = stack(ybuf.at[sq], rows, 0, dk)
        k = stack(ybuf.at[sq], rows, nh * dk, dk)
        v = stack(ybuf.at[sq], rows, 2 * nh * dk, dv)
        q = q * lax.rsqrt(jnp.sum(q * q, -1, keepdims=True) + NORM_EPS) * dk ** -0.5
        k = k * lax.rsqrt(jnp.sum(k * k, -1, keepdims=True) + NORM_EPS)
        if valid is not None:
            valid_st = jnp.concatenate([valid] * nh, axis=0)
            q = jnp.where(valid_st, q, 0.0)
            k = jnp.where(valid_st, k, 0.0)
            v = jnp.where(valid_st, v, 0.0)
        decay = jnp.exp(jnp.where(incl, bcol - brow, -jnp.inf))
        eb = jnp.exp(bcol)
        ks = _split(k)
        prep.append(dict(a=jnp.where(strict, _dot3(ks, ks, nt=True) * decay * beta, 0.0), bv=beta * v,
                         bk=(beta * eb) * k, att=_dot_nt(q, k) * decay, qe=q * eb, kd=k * jnp.exp(bend - bcol),
                         dec_row=dec_row, sq=sq, ci=ci))
    for p in prep:
        a_s = _split(p['a'])
        p['a2'] = _dot3(a_s, a_s)
    for p in prep:
        a2_s = _split(p['a2'])
        p['a4'] = _dot3(a2_s, a2_s)
        p['lo'] = _dot3(_split(eye - p['a']), _split(eye + p['a2']))
    for p in prep:
        a4_s = _split(p['a4'])
        p['a8'] = _dot3(a4_s, a4_s)
    for p in prep:
        p['hi'] = _dot3(_split(eye + p['a4']), _split(eye + p['a8']))
    for p in prep:
        p['tinv'] = _split(_dot3(_split(p['lo']), _split(p['hi'])))
    for p in prep:
        p['u'] = _dot3(p['tinv'], _split(p['bv']))
        p['wk'] = _dot3(p['tinv'], _split(p['bk']))

    s_cat = [st_scr[sq] for sq in range(nseq)]
    for p in sorted(prep, key=lambda p: (p['ci'], p['sq'])):
        sq = p['sq']
        rows = slice(p['ci'] * C, (p['ci'] + 1) * C)
        v_new = p['u'] - own_block(_dot(p['wk'], s_cat[sq]))
        o = _dot(p['att'], v_new) + own_block(_dot(p['qe'], s_cat[sq]))
        v_wide = jnp.concatenate([jnp.where(rhead == h, v_new, 0.0) for h in range(nh)], axis=1)
        s_cat[sq] = s_cat[sq] * p['dec_row'] + _dot_tn(p['kd'], v_wide)
        o = o * lax.rsqrt(jnp.mean(o * o, -1, keepdims=True) + NORM_EPS) * ng_ref[...]
        for h in range(nh):
            gate = proj_refs[sq][rows, cwid + h * dv:cwid + (h + 1) * dv]
            o_ref[sq, rows, h * dv:(h + 1) * dv] = o[h * C:(h + 1) * C, :] * jax.nn.silu(gate)
    for sq in range(nseq):
        st_scr[sq] = s_cat[sq]

    @pl.when(t < nt - 1)
    def _():
        for sq in range(nseq):
            cbuf[sq, pre:8, :] = cbuf[sq, pre + tb:8 + tb, :]

    @pl.when(t == nt - 1)
    def _():
        last = t_valid - (nt - 1) * tb
        for sq in range(nseq):
            for h in range(nh):
                sout_ref[sq, h] = st_scr[sq, :, h * dv:(h + 1) * dv]
            convout_ref[sq] = cbuf[sq, pre + last:8 + last, :]


def _gdn_scan(proj, s0, conv0, cw, ng, alog, dtb, *, nb, nt, tb, t_valid):
    _, nh, dk, dv = s0.shape
    wp = proj.shape[1]
    cwid = 2 * nh * dk + nh * dv
    nseq = _seqs_per_step(nb, tb // C)
    body = functools.partial(_gdn_body, nseq=nseq, nh=nh, dk=dk, dv=dv, tb=tb, t_valid=t_valid, nt=nt)
    const = lambda b, t: (0, 0)
    o, s, conv = pl.pallas_call(
        body, grid=(nb // nseq, nt),
        in_specs=_proj_specs(nseq, nt, tb, wp) + [
            pl.BlockSpec((nseq, nh, dk, dv), lambda b, t: (b, 0, 0, 0)),
            pl.BlockSpec((nseq, CONV_W - 1, cwid), lambda b, t: (b, 0, 0)),
            pl.BlockSpec(cw.shape, const), pl.BlockSpec(ng.shape, const),
            pl.BlockSpec(alog.shape, const), pl.BlockSpec(dtb.shape, const)],
        out_specs=[pl.BlockSpec((nseq, tb, nh * dv), lambda b, t: (b, t, 0)),
                   pl.BlockSpec((nseq, nh, dk, dv), lambda b, t: (b, 0, 0, 0)),
                   pl.BlockSpec((nseq, CONV_W - 1, cwid), lambda b, t: (b, 0, 0))],
        out_shape=[jax.ShapeDtypeStruct((nb, nt * tb, nh * dv), F32), jax.ShapeDtypeStruct(s0.shape, F32),
                   jax.ShapeDtypeStruct(conv0.shape, F32)],
        scratch_shapes=[pltpu.VMEM((nseq, dk, nh * dv), F32), pltpu.VMEM((nseq, tb + 8, cwid), F32),
                        pltpu.VMEM((nseq, tb, cwid), F32)],
        compiler_params=_cparams(("parallel", "arbitrary")), name="gdn_scan",
    )(*([proj] * nseq), s0, conv0, cw, ng, alog, dtb)
    return o.reshape(nb * nt * tb, nh * dv), s, conv


def _mlstm_body(*refs, nseq, nh, dk, dv, tb, t_valid, nt):
    proj_refs = refs[:nseq]
    (c0_ref, n0_ref, m0_ref, ng_ref, bi_ref, bf_ref, o_ref, cout_ref, nout_ref, mout_ref,
     ct_scr, n_scr, m_scr) = refs[nseq:]
    t = pl.program_id(1)

    @pl.when(t == 0)
    def _():
        for sq in range(nseq):
            for h in range(nh):
                ct_scr[sq, :, h * dv:(h + 1) * dv] = c0_ref[sq, h]
                n_scr[sq, h] = n0_ref[sq, h:h + 1, :]
                m_scr[sq, h] = jnp.broadcast_to(m0_ref[sq, :, h:h + 1], (1, LANE))

    base = 2 * nh * dk + 2 * nh * dv
    nr = nh * C
    ri = _iota((nr, 1), 0)
    cj = _iota((1, nr), 1)
    rhead = ri // C
    incl = (rhead == cj // C) & (ri % C >= cj % C)

    def stack(ref, rows, col0, width):
        return jnp.concatenate([ref[rows, col0 + h * width:col0 + (h + 1) * width] for h in range(nh)], axis=0)

    def per_head_col(vals):
        return jnp.concatenate([jnp.broadcast_to(x, (C, 1)) for x in vals], axis=0)

    prep = {}
    for sq, ci in [(sq, ci) for sq in range(nseq) for ci in range(tb // C)]:
        proj_ref = proj_refs[sq]
        rows = slice(ci * C, (ci + 1) * C)
        valid = _valid_rows(t, ci * C, tb, t_valid, nt)
        small = proj_ref[rows, base:base + LANE]
        li_all = small + bi_ref[...]
        lf_all = jax.nn.log_sigmoid(small + bf_ref[...])
        if valid is not None:
            lf_all = jnp.where(valid, lf_all, 0.0)
        b_all = _cumsum_rows(lf_all)
        b_t = _rows_to_cols(b_all)
        li_t = _rows_to_cols(li_all)
        bcol = jnp.concatenate([b_all[:, nh + h:nh + h + 1] for h in range(nh)], axis=0)
        licol = jnp.concatenate([li_all[:, h:h + 1] for h in range(nh)], axis=0)
        xrow = jnp.concatenate([b_t[nh + h:nh + h + 1, :] - li_t[h:h + 1, :] for h in range(nh)], axis=1)
        if valid is not None:
            xrow = jnp.where(t * tb + ci * C + cj % C < t_valid, xrow, jnp.inf)
            licol = jnp.where(t * tb + ci * C + ri % C < t_valid, licol, -jnp.inf)
        q = stack(proj_ref, rows, 0, dk)
        k = stack(proj_ref, rows, nh * dk, dk) * dk ** -0.5
        v = stack(proj_ref, rows, 2 * nh * dk, dv)
        dmat = jnp.where(incl, bcol - xrow, -jnp.inf)
        prep[sq, ci] = dict(q=q, k=k, v=v, bcol=bcol, licol=licol, dmat=dmat, qk=_dot_nt(q, k),
                            rowmax=jnp.max(dmat, axis=1, keepdims=True),
                            b_end=[b_all[C - 1:C, nh + h:nh + h + 1] for h in range(nh)])

    c_cat = [ct_scr[sq] for sq in range(nseq)]
    n_rows = [[n_scr[sq, h] for h in range(nh)] for sq in range(nseq)]
    m_vals = [[m_scr[sq, h][:, 0:1] for h in range(nh)] for sq in range(nseq)]
    for ci, sq in [(ci, sq) for ci in range(tb // C) for sq in range(nseq)]:
        p = prep[sq, ci]
        rows = slice(ci * C, (ci + 1) * C)
        q, k, v, bcol = p['q'], p['k'], p['v'], p['bcol']
        m_inter = bcol + per_head_col(m_vals[sq])
        m_t = jnp.maximum(m_inter, p['rowmax'])
        a_int = jnp.exp(m_inter - m_t)
        qk = p['qk'] * jnp.exp(p['dmat'] - m_t)
        qc = _dot(q, c_cat[sq])
        own = _tree(jnp.add, [jnp.where(rhead == h, qc[:, h * dv:(h + 1) * dv], 0.0) for h in range(nh)])
        num = _dot(qk, v) + a_int * own
        n_st = jnp.concatenate([jnp.broadcast_to(n_rows[sq][h], (C, dk)) for h in range(nh)], axis=0)
        den = jnp.sum(qk, axis=1, keepdims=True) + a_int * jnp.sum(q * n_st, axis=1, keepdims=True)
        hid = num / jnp.maximum(jnp.abs(den), jnp.exp(-m_t))
        last = [slice(h * C + C - 1, h * C + C) for h in range(nh)]
        m_vals[sq] = [m_t[r, :] for r in last]
        a_end = [a_int[r, :] for r in last]
        wk = jnp.exp(per_head_col(p['b_end']) - bcol + p['licol'] - per_head_col(m_vals[sq])) * k
        v_wide = jnp.concatenate([jnp.where(rhead == h, v, 0.0) for h in range(nh)], axis=1)
        a_row = jnp.concatenate([jnp.broadcast_to(a, (1, dv)) for a in a_end], axis=1)
        c_cat[sq] = c_cat[sq] * a_row + _dot_tn(wk, v_wide)
        n_rows[sq] = [a_end[h] * n_rows[sq][h] + jnp.sum(wk[h * C:(h + 1) * C, :], axis=0, keepdims=True)
                      for h in range(nh)]
        hc = hid - jnp.mean(hid, -1, keepdims=True)
        hn = hc * lax.rsqrt(jnp.mean(hc * hc, -1, keepdims=True) + NORM_EPS)
        for h in range(nh):
            og = jax.nn.sigmoid(
                proj_refs[sq][rows, 2 * nh * dk + nh * dv + h * dv:2 * nh * dk + nh * dv + (h + 1) * dv])
            o_ref[sq, rows, h * dv:(h + 1) * dv] = og * (hn[h * C:(h + 1) * C, :] * ng_ref[:, h * dv:(h + 1) * dv])
    for sq in range(nseq):
        ct_scr[sq] = c_cat[sq]
        for h in range(nh):
            n_scr[sq, h] = n_rows[sq][h]
            m_scr[sq, h] = jnp.broadcast_to(m_vals[sq][h], (1, LANE))

    @pl.when(t == nt - 1)
    def _():
        lane = _iota((1, nh), 1)
        for sq in range(nseq):
            mrow = jnp.zeros((1, nh), F32)
            for h in range(nh):
                cout_ref[sq, h] = ct_scr[sq, :, h * dv:(h + 1) * dv]
                nout_ref[sq, h:h + 1, :] = n_scr[sq, h]
                mrow = jnp.where(lane == h, m_scr[sq, h][:, 0:1], mrow)
            mout_ref[sq] = mrow


def _mlstm_scan(proj, c0, n0, m0, ng, bi, bf, *, nb, nt, tb, t_valid):
    _, nh, dk, dv = c0.shape
    wp = proj.shape[1]
    nseq = _seqs_per_step(nb, tb // C)
    body = functools.partial(_mlstm_body, nseq=nseq, nh=nh, dk=dk, dv=dv, tb=tb, t_valid=t_valid, nt=nt)
    const = lambda b, t: (0, 0)
    m0 = m0.reshape(nb, 1, nh)
    o, c, n, m = pl.pallas_call(
        body, grid=(nb // nseq, nt),
        in_specs=_proj_specs(nseq, nt, tb, wp) + [
            pl.BlockSpec((nseq, nh, dk, dv), lambda b, t: (b, 0, 0, 0)),
            pl.BlockSpec((nseq, nh, dk), lambda b, t: (b, 0, 0)),
            pl.BlockSpec((nseq, 1, nh), lambda b, t: (b, 0, 0)),
            pl.BlockSpec(ng.shape, const), pl.BlockSpec(bi.shape, const), pl.BlockSpec(bf.shape, const)],
        out_specs=[pl.BlockSpec((nseq, tb, nh * dv), lambda b, t: (b, t, 0)),
                   pl.BlockSpec((nseq, nh, dk, dv), lambda b, t: (b, 0, 0, 0)),
                   pl.BlockSpec((nseq, nh, dk), lambda b, t: (b, 0, 0)),
                   pl.BlockSpec((nseq, 1, nh), lambda b, t: (b, 0, 0))],
        out_shape=[jax.ShapeDtypeStruct((nb, nt * tb, nh * dv), F32), jax.ShapeDtypeStruct(c0.shape, F32),
                   jax.ShapeDtypeStruct(n0.shape, F32), jax.ShapeDtypeStruct((nb, 1, nh), F32)],
        scratch_shapes=[pltpu.VMEM((nseq, dk, nh * dv), F32), pltpu.VMEM((nseq, nh, 1, dk), F32),
                        pltpu.VMEM((nseq, nh, 1, LANE), F32)],
        compiler_params=_cparams(("parallel", "arbitrary")), name="mlstm_scan",
    )(*([proj] * nseq), c0, n0, m0, ng, bi, bf)
    return o.reshape(nb * nt * tb, nh * dv), c, n, m


SUB = 8


def _tree(op, xs):
    xs = list(xs)
    while len(xs) > 1:
        xs = [op(xs[i], xs[i + 1]) if i + 1 < len(xs) else xs[i] for i in range(0, len(xs), 2)]
    return xs[0]


def _all_sublanes(op, m):
    for shift in (4, 2, 1):
        m = op(m, pltpu.roll(m, shift, axis=0))
    return m


def _sorting_network(n):
    def merge(lo, hi, r):
        step = r * 2
        if step < hi - lo:
            yield from merge(lo, hi, step)
            yield from merge(lo + r, hi, step)
            yield from [(i, i + r) for i in range(lo + r, hi - r, step)]
        else:
            yield (lo, lo + r)

    def sort(lo, hi):
        if hi - lo >= 1:
            mid = lo + (hi - lo) // 2
            yield from sort(lo, mid)
            yield from sort(mid + 1, hi)
            yield from merge(lo, hi, 1)

    return list(sort(0, n - 1))


def _sort_tiles_desc(tiles):
    tiles = list(tiles)
    for i, j in _sorting_network(len(tiles)):
        tiles[i], tiles[j] = jnp.maximum(tiles[i], tiles[j]), jnp.minimum(tiles[i], tiles[j])
    return tiles


def _merge_top(lists, singles=None):
    lists = list(lists)
    vals = []
    for j in range(PEER_TOPK):
        head = lists[0] if singles is None else jnp.maximum(lists[0], singles)
        m = _all_sublanes(jnp.maximum, head)
        vals.append(m)
        if j == PEER_TOPK - 1:
            break
        hit = lists[0] == m
        for i in range(min(len(lists) - 1, PEER_TOPK - 1 - j)):
            lists[i] = jnp.where(hit, lists[i + 1], lists[i])
        if singles is not None:
            singles = jnp.where(singles == m, -jnp.inf, singles)
    return vals


def _route_tile(q_scr, k1_ref, k2_ref, r2_ref, c1_ref, a1_ref, a2_ref, h, lanes):
    ntile = PEER_NKEYS // SUB
    sub = _iota((SUB, LANE), 0)
    q1 = q_scr[pl.ds(pl.multiple_of(h * 2 * PEER_HALF, PEER_HALF), PEER_HALF), lanes]
    q2 = q_scr[pl.ds(pl.multiple_of(h * 2 * PEER_HALF + PEER_HALF, PEER_HALF), PEER_HALF), lanes]
    s1 = _dot(k1_ref[...], q1)
    s2 = _dot(k2_ref[...], q2)
    t1 = [s1[k * SUB:(k + 1) * SUB, :] for k in range(ntile)]
    t2 = [s2[k * SUB:(k + 1) * SUB, :] for k in range(ntile)]
    v1 = _merge_top(_sort_tiles_desc(t1))
    v2 = _merge_top(_sort_tiles_desc(t2))
    v2lo, v2hi = v2[SUB - 1], v2[2 * SUB - 1]
    for b in range(SUB - 2, -1, -1):
        v2lo = jnp.where(sub == b, v2[b], v2lo)
        v2hi = jnp.where(sub == b, v2[SUB + b], v2hi)
    cand_lo = [jnp.where(sub < min(PEER_TOPK // (a + 1), SUB), v1[a] + v2lo, -jnp.inf) for a in range(PEER_TOPK)]
    cand_hi = v1[0] + v2hi
    tau = _merge_top(cand_lo, cand_hi)[PEER_TOPK - 1]
    top = v1[0] + v2[0]
    z = _all_sublanes(jnp.add, _tree(jnp.add, [jnp.where(cd >= tau, jnp.exp(cd - top), 0.0)
                                               for cd in cand_lo + [cand_hi]]))
    inv_z = 1.0 / z
    pairs = [_all_sublanes(jnp.add, jnp.where(cd >= tau, 1.0, 0.0)) for cd in cand_lo]
    pairs[0] = pairs[0] + _all_sublanes(jnp.add, jnp.where(cand_hi >= tau, 1.0, 0.0))
    rank2, c1 = [], []
    for k in range(ntile):
        r = jnp.full((SUB, LANE), float(PEER_TOPK), F32)
        cnt = jnp.zeros((SUB, LANE), F32)
        for j in range(PEER_TOPK - 1, -1, -1):
            r = jnp.where(t2[k] >= v2[j], float(j), r)
            cnt = jnp.where(t1[k] == v1[j], pairs[j], cnt)
        rank2.append(r)
        c1.append(cnt)
    r2_ref[h, :, lanes] = jnp.concatenate(rank2, axis=0).astype(BF16)
    c1_ref[h, :, lanes] = jnp.concatenate(c1, axis=0)
    a1_ref[h, :, lanes] = jnp.concatenate([jnp.exp(t - v1[0]) for t in t1], axis=0)
    a2_ref[h, :, lanes] = jnp.concatenate([jnp.exp(t - v2[0]) * inv_z for t in t2], axis=0).astype(BF16)


def _peer_body(xm_ref, xr_ref, wq_ref, k1_ref, k2_ref, u_ref, vt_ref, g_ref, b_ref, out_ref,
               acc, xbf, p_scr, q_scr, r2_scr, c1_scr, a1_scr, a2_scr):
    r = pl.program_id(0)
    c = pl.program_id(1)
    tm = xbf.shape[1]

    @pl.when(c == 0)
    def _():
        xbf[...] = xm_ref[...].astype(BF16)
        acc[...] = jnp.zeros(acc.shape, F32)
        p_scr[...] = jnp.zeros(p_scr.shape, BF16)
        q_scr[...] = _dot(wq_ref[...], xr_ref[...].astype(BF16))

    @pl.when((c == 0) & (r == 0))
    def _():
        for scr in (r2_scr, c1_scr, a1_scr, a2_scr):
            scr[...] = jnp.zeros(scr.shape, scr.dtype)

    ntile = tm // LANE
    units = pl.num_programs(1) - 1
    tiles_per_unit = PEER_HEADS * ntile // units
    unit = (c + units - 1) % units
    h_route = unit // (ntile // tiles_per_unit)
    lane0 = (unit % (ntile // tiles_per_unit)) * tiles_per_unit * LANE
    wr = r % 2
    for sb in range(tiles_per_unit):
        _route_tile(q_scr, k1_ref, k2_ref, r2_scr.at[wr], c1_scr.at[wr], a1_scr.at[wr], a2_scr.at[wr], h_route,
                    pl.ds(pl.multiple_of(lane0 + sb * LANE, LANE), LANE))

    rd = (r + 1) % 2
    i1_base = jnp.minimum(c, pl.num_programs(1) - 2) * PEER_I1_PER_STEP
    rows16 = 2 * SUB
    acc[...] += _dot(vt_ref[0], p_scr[...])
    hid = _dot(u_ref[...], xbf[...])
    act = (0.5 * hid * (1.0 + lax.erf(hid * 0.5 ** 0.5))).astype(BF16)
    for l0 in range(0, tm, PEER_LANE_BLOCK):
        ls = slice(l0, l0 + PEER_LANE_BLOCK)
        for j in range(PEER_I1_PER_STEP):
            row = pl.ds(i1_base + j, 1)
            nkb = PEER_NKEYS // rows16
            w = [None] * nkb
            for h in range(PEER_HEADS):
                c1b = jnp.broadcast_to(c1_scr[rd, h, row, ls], (rows16, PEER_LANE_BLOCK)).astype(BF16)
                a1b = jnp.broadcast_to(a1_scr[rd, h, row, ls], (rows16, PEER_LANE_BLOCK)).astype(BF16)
                for kb in range(nkb):
                    ks = slice(kb * rows16, (kb + 1) * rows16)
                    term = jnp.where(r2_scr[rd, h, ks, ls] < c1b, a2_scr[rd, h, ks, ls] * a1b, 0.0)
                    w[kb] = term if w[kb] is None else w[kb] + term
            for kb in range(nkb):
                es = slice(j * PEER_NKEYS + kb * rows16, j * PEER_NKEYS + (kb + 1) * rows16)
                p_scr[es, ls] = w[kb] * act[es, ls]

    @pl.when(c == pl.num_programs(1) - 1)
    def _():
        z = DN_ALPHA * xm_ref[...] + acc[...]
        zc = z - jnp.mean(z, 0, keepdims=True)
        var = jnp.mean(zc * zc, 0, keepdims=True)
        out_ref[...] = (zc * lax.rsqrt(var + LN_EPS) * g_ref[...] + b_ref[...]).T


def _peer(xt, wq_t, k1, k2, u, vt, g, b, tm):
    d, n = xt.shape
    nb = n // tm
    ec = PEER_I1_PER_STEP * PEER_NKEYS
    nc = u.shape[0] // ec
    ntile = tm // LANE
    assert (PEER_HEADS * ntile) % nc == 0 and ntile % (PEER_HEADS * ntile // nc) == 0, "routing units per row"
    const = lambda r, c: (0, 0)
    rshape = (2, PEER_HEADS, PEER_NKEYS, tm)
    return pl.pallas_call(
        _peer_body, grid=(nb + 1, nc + 1),
        in_specs=[pl.BlockSpec((d, tm), lambda r, c: (0, jnp.maximum(r - 1, 0))),
                  pl.BlockSpec((d, tm), lambda r, c: (0, jnp.minimum(r, nb - 1))),
                  pl.BlockSpec(wq_t.shape, const), pl.BlockSpec(k1.shape, const), pl.BlockSpec(k2.shape, const),
                  pl.BlockSpec((ec, d), lambda r, c: (jnp.minimum(c, nc - 1), 0)),
                  pl.BlockSpec((1, d, ec), lambda r, c: (jnp.maximum(c - 1, 0), 0, 0)),
                  pl.BlockSpec((d, 1), const), pl.BlockSpec((d, 1), const)],
        out_specs=pl.BlockSpec((tm, d), lambda r, c: (jnp.maximum(r - 1, 0), 0)),
        out_shape=jax.ShapeDtypeStruct((n, d), F32),
        scratch_shapes=[pltpu.VMEM((d, tm), F32), pltpu.VMEM((d, tm), BF16), pltpu.VMEM((ec, tm), BF16),
                        pltpu.VMEM((wq_t.shape[0], tm), F32), pltpu.VMEM(rshape, BF16), pltpu.VMEM(rshape, F32),
                        pltpu.VMEM(rshape, F32), pltpu.VMEM(rshape, BF16)],
        compiler_params=pltpu.CompilerParams(dimension_semantics=("arbitrary", "arbitrary"),
                                             vmem_limit_bytes=PEER_VMEM_LIMIT), name="peer",
    )(xt, xt, wq_t, k1, k2, u, vt, g.reshape(d, 1), b.reshape(d, 1))


def _pad_cols(w, width):
    return jnp.pad(w, ((0, 0), (0, width - w.shape[1])))


def _lane_row(*parts):
    v = jnp.concatenate([p.astype(F32).reshape(-1) for p in parts])
    return jnp.pad(v, (0, LANE - v.shape[0])).reshape(1, LANE)


def _pick_tb(t_pad):
    for tb in (48, 32, 16):
        if t_pad % tb == 0:
            return tb
    raise ValueError(t_pad)


def kernel(x_prompt, x_sample, state_hgrn_S, state_gdn_S, state_gdn_conv, state_mlstm_C, state_mlstm_n, state_mlstm_m, state_gla_S, meta_tokens, hgrn_w_in, hgrn_lb, hgrn_norm_g, hgrn_w_out, gdn_w_in, gdn_conv_w, gdn_a_log, gdn_dt_bias, gdn_norm_g, gdn_w_out, mlstm_w_in, mlstm_b_i, mlstm_b_f, mlstm_norm_g, mlstm_w_out, gla_w_in, gla_w_gate, gla_b_gate, gla_norm_g, gla_w_out, peer_w_q, peer_keys1, peer_keys2, peer_u, peer_v, ln_g, ln_b):
    bp, seq, d = x_prompt.shape
    bs, seq_s, _ = x_sample.shape
    assert d == D_MODEL and len(state_hgrn_S) == 1 and len(state_gdn_S) == 1
    assert len(state_mlstm_C) == 1 and len(state_gla_S) == 1
    tp = N_META + seq
    tp_pad = -(-tp // C) * C
    ts_pad = -(-seq_s // C) * C
    tbp = _pick_tb(tp_pad)
    tbs = _pick_tb(ts_pad)
    np_rows = bp * tp_pad
    n_real = np_rows + bs * seq_s

    meta = jnp.broadcast_to(meta_tokens.astype(F32)[None], (bp, N_META, d))
    hp = jnp.concatenate([meta, x_prompt], axis=1)
    hp = jnp.pad(hp, ((0, 0), (0, tp_pad - tp), (0, 0))).reshape(np_rows, d)
    h = jnp.concatenate([hp, x_sample.reshape(bs * seq_s, d)], axis=0)
    n = -(-n_real // TOK_MULTIPLE) * TOK_MULTIPLE
    h = jnp.pad(h, ((0, n - n_real), (0, 0)))
    tm_mm = MM_TOK_BLOCK

    def split_cols(w, sizes):
        out, o = [], 0
        for s in sizes:
            out.append(w[:, o:o + s])
            o += s
        return out

    def run_mixer(scan, proj, states_p, states_s, **kw):
        res_p = scan(proj, *states_p, nb=bp, nt=tp_pad // tbp, tb=tbp, t_valid=tp, **kw)
        ps = proj[np_rows:n_real].reshape(bs, seq_s, -1)
        ps = jnp.pad(ps, ((0, 0), (0, ts_pad - seq_s), (0, 0))).reshape(bs * ts_pad, -1)
        res_s = scan(ps, *states_s, nb=bs, nt=ts_pad // tbs, tb=tbs, t_valid=seq_s, **kw)
        o_s = res_s[0].reshape(bs, ts_pad, -1)[:, :seq_s].reshape(bs * seq_s, -1)
        o = jnp.concatenate([res_p[0], o_s], axis=0)
        o = jnp.pad(o, ((0, n - n_real), (0, 0)))
        return o, res_p[1:], res_s[1:]

    lb_all = jnp.cumsum(jax.nn.softmax(hgrn_lb.astype(F32), axis=0), axis=0)
    outs = {}
    for i in range(DEPTH):
        mix = i % 4
        if mix == 0:
            w_in = hgrn_w_in[0].astype(BF16)
            proj = _mm(h, w_in, tm_mm, w_in.shape[1])
            nh, dk, dv = state_hgrn_S.shape[2:]
            zero = jnp.zeros((bp, nh, dk, dv), F32)
            scan = functools.partial(_gla_scan, ng=hgrn_norm_g[0].reshape(1, dv), p1=lb_all[i].reshape(1, nh * dk),
                                     p2=jnp.zeros((1, LANE), F32), hgrn=True)
            o, (sp,), (ss,) = run_mixer(scan, proj, (zero,), (state_hgrn_S[0],))
            outs['hgrn'] = (sp[None], ss[None])
            w_out = hgrn_w_out[0]
        elif mix == 1:
            nh, dk, dv = state_gdn_S.shape[2:]
            cwid = 2 * nh * dk + nh * dv
            qkv, a, b, g = split_cols(gdn_w_in[0], [cwid, nh, nh, nh * dv])
            w_in = jnp.concatenate([qkv, g, _pad_cols(jnp.concatenate([a, b], axis=1), LANE)], axis=1).astype(BF16)
            proj = _mm(h, w_in, tm_mm, w_in.shape[1])
            scan = functools.partial(_gdn_scan, cw=gdn_conv_w[0], ng=gdn_norm_g[0].reshape(1, dv),
                                     alog=_lane_row(gdn_a_log[0]), dtb=_lane_row(gdn_dt_bias[0]))
            o, (sp, cp), (ss, cs) = run_mixer(
                scan, proj,
                (jnp.zeros((bp, nh, dk, dv), F32), jnp.zeros((bp, CONV_W - 1, cwid), F32)),
                (state_gdn_S[0], state_gdn_conv[0]))
            outs['gdn'] = (sp[None], cp[None], ss[None], cs[None])
            w_out = gdn_w_out[0]
        elif mix == 2:
            nh, dk, dv = state_mlstm_C.shape[2:]
            q, k, v, ig, fg, og = split_cols(mlstm_w_in[0], [nh * dk, nh * dk, nh * dv, nh, nh, nh * dv])
            w_in = jnp.concatenate([q, k, v, og, _pad_cols(jnp.concatenate([ig, fg], axis=1), LANE)],
                                   axis=1).astype(BF16)
            proj = _mm(h, w_in, tm_mm, w_in.shape[1])
            zeros_nh = jnp.zeros((nh,), F32)
            scan = functools.partial(_mlstm_scan, ng=mlstm_norm_g[0].reshape(1, nh * dv),
                                     bi=_lane_row(mlstm_b_i[0]), bf=_lane_row(zeros_nh, mlstm_b_f[0]))
            o, (cp, npp, mp), (cs, ns, ms) = run_mixer(
                scan, proj,
                (jnp.zeros((bp, nh, dk, dv), F32), jnp.zeros((bp, nh, dk), F32), jnp.zeros((bp, nh), F32)),
                (state_mlstm_C[0], state_mlstm_n[0], state_mlstm_m[0]))
            outs['mlstm'] = (cp[None], npp[None], mp.reshape(1, bp, nh), cs[None], ns[None], ms.reshape(1, bs, nh))
            w_out = mlstm_w_out[0]
        else:
            nh, dk, dv = state_gla_S.shape[2:]
            rank = gla_w_gate.shape[1]
            q, k, v, g, r = split_cols(gla_w_in[0], [nh * dk, nh * dk, nh * dv, nh * dv, rank])
            w_in = jnp.concatenate([q, k, v, g, _pad_cols(r, LANE)], axis=1).astype(BF16)
            proj = _mm(h, w_in, tm_mm, w_in.shape[1])
            wg = jnp.pad(gla_w_gate[0].astype(F32), ((0, LANE - rank), (0, 0)))
            scan = functools.partial(_gla_scan, ng=gla_norm_g[0].reshape(1, dv), p1=wg,
                                     p2=gla_b_gate[0].reshape(1, nh * dk), hgrn=False)
            o, (sp,), (ss,) = run_mixer(scan, proj, (jnp.zeros((bp, nh, dk, dv), F32),), (state_gla_S[0],))
            outs['gla'] = (sp[None], ss[None])
            w_out = gla_w_out[0]

        h1t = _mm_res_ln_t(o, w_out.astype(BF16), h, ln_g[i, 0], ln_b[i, 0], TOK_BLOCK)
        ec = PEER_I1_PER_STEP * PEER_NKEYS
        vt = peer_v[i].astype(BF16).reshape(-1, ec, d).transpose(0, 2, 1)
        h = _peer(h1t, peer_w_q[i].T.astype(BF16), peer_keys1[i], peer_keys2[i], peer_u[i].astype(BF16), vt,
                  ln_g[i, 1], ln_b[i, 1], PEER_TOK_BLOCK)

    y_prompt = h[:np_rows].reshape(bp, tp_pad, d)[:, N_META:tp]
    y_sample = h[np_rows:n_real].reshape(bs, seq_s, d)
    hg, gd, ml, gl = outs['hgrn'], outs['gdn'], outs['mlstm'], outs['gla']
    return (y_prompt, y_sample, hg[0], gd[0], gd[1], ml[0], ml[1], ml[2], gl[0],
            hg[1], gd[2], gd[3], ml[3], ml[4], ml[5], gl[1])
```
